```python
import jax
import jax.numpy as jnp
from jax import lax
import numpy as np

D_MODEL = 1024
BATCH = 32
SEQ = 2048
DEPTH = 2

GM_W = 512
GM_GROUPS = 4
GM_CHUNK = 128
ML_H = 4
ML_DH = 128
ML_W = ML_H * ML_DH
ML_CHUNK = 64
CONV_K = 4
NSA_H = 8
NSA_G = 2
NSA_R = NSA_H // NSA_G
NSA_DH = 64
NSA_W = NSA_H * NSA_DH
NSA_KV = NSA_G * NSA_DH
NSA_NB = 3
CMP_BLOCK = 32
CMP_STRIDE = 16
SEL_BLOCK = 64
TOP_N = 8
WINDOW = 512
NSA_QC = 32
N_BRANCH = 3
D_FF = 4 * D_MODEL
EPS = 1e-6
NEG = -1e30
BIG = 1e4
F32 = jnp.float32
SPLIT_SIZES = (GM_W, GM_W, ML_W, ML_W, ML_W, ML_W, ML_H, ML_H, NSA_W, NSA_KV, NSA_KV, NSA_KV, NSA_KV, NSA_KV, NSA_KV, NSA_H * NSA_NB, D_MODEL, D_MODEL, D_MODEL)
P_TOTAL = sum(SPLIT_SIZES)

kernel_name = 'hybrid_gmlp_mlstm_nsa_block'


def rms_norm(x, g):
    x32 = x.astype(F32)
    y = x32 * lax.rsqrt(jnp.mean(x32 * x32, axis=-1, keepdims=True) + EPS)
    return (y * g.astype(F32)).astype(x.dtype)


def layer_norm(x, g, b):
    x32 = x.astype(F32)
    mu = jnp.mean(x32, axis=-1, keepdims=True)
    var = jnp.mean(jnp.square(x32 - mu), axis=-1, keepdims=True)
    return ((x32 - mu) * lax.rsqrt(var + EPS) * g.astype(F32) + b.astype(F32)).astype(x.dtype)


def causal_conv(x, w, b):
    S = x.shape[1]
    xp = jnp.pad(x, ((0, 0), (CONV_K - 1, 0), (0, 0)))
    y = b
    for j in range(CONV_K):
        y = y + xp[:, j:j + S] * w[j]
    return y


def gmlp_mixer(u_pre, v_pre, ln_g, ln_b, ws, bs):
    B, S, _ = u_pre.shape
    u = jax.nn.gelu(u_pre)
    v = layer_norm(jax.nn.gelu(v_pre), ln_g, ln_b)
    dg = GM_W // GM_GROUPS
    v = v.reshape(B, S // GM_CHUNK, GM_CHUNK, GM_GROUPS, dg)
    w = ws * jnp.tril(jnp.ones((GM_CHUNK, GM_CHUNK), ws.dtype))
    mixed = jnp.einsum('gts,bcsgd->bctgd', w, v) + bs.T[:, :, None]
    return u * mixed.reshape(B, S, GM_W)


def mlstm_chunk_step(carry, xs):
    c_mat, n_vec, m_prev = carry
    q, k, v, ig, lf = xs
    L = q.shape[2]
    tril = jnp.tril(jnp.ones((L, L), dtype=bool))
    b = jnp.cumsum(lf, axis=-1)
    a = b + m_prev[..., None]
    d = jnp.where(tril, b[..., :, None] - b[..., None, :] + ig[..., None, :], -jnp.inf)
    m = jnp.maximum(a, jnp.max(d, axis=-1))
    w_inter = jnp.exp(a - m)
    s = jnp.einsum('bhtd,bhsd->bhts', q, k) * jnp.exp(d - m[..., None])
    num = jnp.einsum('bhts,bhse->bhte', s, v) + w_inter[..., None] * jnp.einsum('bhed,bhtd->bhte', c_mat, q)
    den = jnp.sum(s, axis=-1) + w_inter * jnp.einsum('bhd,bhtd->bht', n_vec, q)
    h = num / jnp.maximum(jnp.abs(den), jnp.exp(-m))[..., None]
    m_last = m[..., -1]
    w_prev = jnp.exp(a[..., -1] - m_last)
    w_s = jnp.exp(b[..., -1:] - b + ig - m_last[..., None])
    c_new = w_prev[..., None, None] * c_mat + jnp.einsum('bhs,bhse,bhsd->bhed', w_s, v, k)
    n_new = w_prev[..., None] * n_vec + jnp.einsum('bhs,bhsd->bhd', w_s, k)
    return (c_new, n_new, m_last), h


def mlstm_mixer(q, k, v, o_pre, i_pre, f_pre, conv_w, conv_b, gate_b, norm_g):
    B, S, _ = v.shape
    dtype = v.dtype
    qk = jax.nn.silu(causal_conv(jnp.concatenate([q, k], axis=-1), conv_w, conv_b))
    q, k = jnp.split(qk, 2, axis=-1)
    nc = S // ML_CHUNK

    def heads(t):
        return t.astype(F32).reshape(B, nc, ML_CHUNK, ML_H, ML_DH).transpose(1, 0, 3, 2, 4)

    def gate_heads(t):
        return t.reshape(B, nc, ML_CHUNK, ML_H).transpose(1, 0, 3, 2)

    qh = heads(q)
    kh = heads(k) * (ML_DH ** -0.5)
    vh = heads(v)
    ig = gate_heads((i_pre + gate_b[:ML_H]).astype(F32))
    lf = gate_heads(jax.nn.log_sigmoid((f_pre + gate_b[ML_H:]).astype(F32)))
    init = (jnp.zeros((B, ML_H, ML_DH, ML_DH), F32), jnp.zeros((B, ML_H, ML_DH), F32), jnp.zeros((B, ML_H), F32))
    _, h = lax.scan(mlstm_chunk_step, init, (qh, kh, vh, ig, lf))
    h = h.transpose(1, 0, 3, 2, 4).reshape(B, S, ML_H, ML_DH)
    mu = jnp.mean(h, axis=-1, keepdims=True)
    var = jnp.mean(jnp.square(h - mu), axis=-1, keepdims=True)
    hn = (h - mu) * lax.rsqrt(var + EPS) * norm_g.astype(F32).reshape(ML_H, ML_DH)
    return (jax.nn.sigmoid(o_pre.astype(F32)) * hn.reshape(B, S, ML_W)).astype(dtype)


def compress_blocks(kv, pe, w1, w2):
    B, S, G, dh = kv.shape
    n_cmp = (S - CMP_BLOCK) // CMP_STRIDE + 1
    idx = jnp.arange(n_cmp)[:, None] * CMP_STRIDE + jnp.arange(CMP_BLOCK)[None, :]
    blocks = kv[:, idx] + pe[:, None, :]
    blocks = blocks.transpose(0, 3, 1, 2, 4).reshape(B, G, n_cmp, CMP_BLOCK * dh)
    return jax.nn.gelu(blocks @ w1) @ w2


def masked_softmax(s, valid, axis=-1):
    p = jax.nn.softmax(jnp.where(valid, s, NEG), axis=axis)
    return jnp.where(valid, p, 0.0)


def nsa_mixer(q, kc, vc, ks, vs, kw, vw, gates, pe_k, pe_v, phi_k1, phi_k2, phi_v1, phi_v2):
    B, S, _ = q.shape
    dtype = q.dtype
    qh = q.reshape(B, S, NSA_G, NSA_R, NSA_DH).transpose(0, 2, 3, 1, 4)

    def kv_heads(t):
        return t.reshape(B, S, NSA_G, NSA_DH)

    k_cmp = compress_blocks(kv_heads(kc), pe_k, phi_k1, phi_k2)
    v_cmp = compress_blocks(kv_heads(vc), pe_v, phi_v1, phi_v2).astype(F32)
    n_cmp = k_cmp.shape[2]
    cmp_start = jnp.arange(n_cmp) * CMP_STRIDE
    cmp_end = cmp_start + CMP_BLOCK - 1
    cmp_center = cmp_start.astype(F32) + (CMP_BLOCK - 1) * 0.5
    n_sel = S // SEL_BLOCK
    top_n = min(TOP_N, n_sel)
    sel = jnp.arange(n_sel)
    overlap = ((cmp_start[:, None] <= sel[None, :] * SEL_BLOCK + SEL_BLOCK - 1) & (cmp_end[:, None] >= sel[None, :] * SEL_BLOCK)).astype(F32)
    k_sel = kv_heads(ks).transpose(0, 2, 1, 3).reshape(B, NSA_G, n_sel, SEL_BLOCK, NSA_DH)
    v_sel = kv_heads(vs).transpose(0, 2, 1, 3).reshape(B, NSA_G, n_sel, SEL_BLOCK, NSA_DH)
    pad = ((0, 0), (0, 0), (WINDOW, 0), (0, 0))
    k_win = jnp.pad(kv_heads(kw).transpose(0, 2, 1, 3), pad)
    v_win = jnp.pad(kv_heads(vw).transpose(0, 2, 1, 3), pad)
    g = jax.nn.sigmoid(gates.astype(F32)).reshape(B, S, NSA_G, NSA_R, NSA_NB).transpose(0, 2, 3, 1, 4)
    slopes = (2.0 ** (-8.0 * (jnp.arange(NSA_H, dtype=F32) + 1.0) / NSA_H)).reshape(NSA_G, NSA_R)
    scale = NSA_DH ** -0.5
    bi = jnp.arange(B)[:, None, None, None]
    gi = jnp.arange(NSA_G)[None, :, None, None]

    def query_chunk(qi):
        t0 = qi * NSA_QC
        qb = lax.dynamic_slice_in_dim(qh, t0, NSA_QC, axis=3)
        t = t0 + jnp.arange(NSA_QC)
        tf = t.astype(F32)
        sc = jnp.einsum('bgrqd,bgcd->bgrqc', qb, k_cmp).astype(F32) * scale - slopes[:, :, None, None] * (tf[:, None] - cmp_center[None, :])
        p_c = masked_softmax(sc, cmp_end[None, :] <= t[:, None])
        o_cmp = jnp.einsum('bgrqc,bgcd->bgrqd', p_c, v_cmp)
        imp = jnp.einsum('bgrqc,cj->bgqj', p_c, overlap)
        jt = (t // SEL_BLOCK)[:, None]
        forced = (sel == 0) | (sel == jt) | (sel == jt - 1)
        imp = jnp.where(sel > jt, NEG, jnp.where(forced, BIG, imp))
        _, idx = lax.top_k(imp, top_n)
        kg = k_sel[bi, gi, idx]
        vg = v_sel[bi, gi, idx].astype(F32)
        s_pos = idx[..., None] * SEL_BLOCK + jnp.arange(SEL_BLOCK)
        dist_s = (tf[:, None, None] - s_pos.astype(F32))[:, :, None]
        ss = jnp.einsum('bgrqd,bgqnld->bgrqnl', qb, kg).astype(F32) * scale - slopes[:, :, None, None, None] * dist_s
        p_s = masked_softmax(ss, (s_pos <= t[:, None, None])[:, :, None], axis=(-2, -1))
        o_sel = jnp.einsum('bgrqnl,bgqnld->bgrqd', p_s, vg)
        kwb = lax.dynamic_slice_in_dim(k_win, t0, WINDOW + NSA_QC, axis=2)
        vwb = lax.dynamic_slice_in_dim(v_win, t0, WINDOW + NSA_QC, axis=2).astype(F32)
        pos = t0 - WINDOW + jnp.arange(WINDOW + NSA_QC)
        sw = jnp.einsum('bgrqd,bgkd->bgrqk', qb, kwb).astype(F32) * scale - slopes[:, :, None, None] * (tf[:, None] - pos.astype(F32)[None, :])
        valid_w = (pos[None, :] >= 0) & (pos[None, :] <= t[:, None]) & (t[:, None] - pos[None, :] < WINDOW)
        o_win = jnp.einsum('bgrqk,bgkd->bgrqd', masked_softmax(sw, valid_w), vwb)
        gb = lax.dynamic_slice_in_dim(g, t0, NSA_QC, axis=3)
        return gb[..., 0:1] * o_cmp + gb[..., 1:2] * o_sel + gb[..., 2:3] * o_win

    out = lax.map(query_chunk, jnp.arange(S // NSA_QC))
    return out.transpose(1, 0, 4, 2, 3, 5).reshape(B, S, NSA_W).astype(dtype)


def hybrid_mixer(h, w_in, gm_ln_g, gm_ln_b, gm_ws, gm_bs, ml_conv_w, ml_conv_b, ml_gate_b, ml_norm_g,
                 nsa_pe_k, nsa_pe_v, nsa_phi_k1, nsa_phi_k2, nsa_phi_v1, nsa_phi_v2, w_up_a, w_up_b, w_up_c, w_out):
    z = h @ w_in
    (gu, gv, mq, mk, mv, mo, mi, mf, nq, nkc, nvc, nks, nvs, nkw, nvw, ngate, ga, gbr, gc) = jnp.split(
        z, np.cumsum(SPLIT_SIZES)[:-1].tolist(), axis=-1)
    y_a = gmlp_mixer(gu, gv, gm_ln_g, gm_ln_b, gm_ws, gm_bs) @ w_up_a
    y_b = mlstm_mixer(mq, mk, mv, mo, mi, mf, ml_conv_w, ml_conv_b, ml_gate_b, ml_norm_g) @ w_up_b
    y_c = nsa_mixer(nq, nkc, nvc, nks, nvs, nkw, nvw, ngate, nsa_pe_k, nsa_pe_v,
                    nsa_phi_k1, nsa_phi_k2, nsa_phi_v1, nsa_phi_v2) @ w_up_c
    merged = jax.nn.sigmoid(ga) * y_a + jax.nn.sigmoid(gbr) * y_b + jax.nn.sigmoid(gc) * y_c
    return merged @ w_out


def setup_inputs(seed: int = 0) -> dict:
    key = jax.random.key(seed)
    ks = jax.random.split(key, 40)
    L = DEPTH

    def nrm(k, shape, scale):
        return scale * jax.random.normal(k, shape, F32)

    ml_gate_b = jnp.concatenate([nrm(ks[12], (L, ML_H), 0.1),
                                 jnp.linspace(3.0, 6.0, ML_H, dtype=F32)[None, :] + nrm(ks[13], (L, ML_H), 0.1)], axis=-1)
    return {
        'x': nrm(ks[0], (BATCH, SEQ, D_MODEL), 1.0),
        'c': nrm(ks[1], (BATCH, D_MODEL), 1.0),
        'g_norm1': 1.0 + nrm(ks[2], (L, D_MODEL), 0.02),
        'g_norm2': 1.0 + nrm(ks[3], (L, D_MODEL), 0.02),
        'w_ada': nrm(ks[4], (L, D_MODEL, 6 * D_MODEL), 0.3 * D_MODEL ** -0.5),
        'b_ada': nrm(ks[5], (L, 6 * D_MODEL), 0.02),
        'w_in': nrm(ks[6], (L, D_MODEL, P_TOTAL), D_MODEL ** -0.5),
        'gm_ln_g': 1.0 + nrm(ks[7], (L, GM_W), 0.02),
        'gm_ln_b': nrm(ks[8], (L, GM_W), 0.02),
        'gm_ws': nrm(ks[9], (L, GM_GROUPS, GM_CHUNK, GM_CHUNK), GM_CHUNK ** -0.5),
        'gm_bs': 1.0 + nrm(ks[10], (L, GM_GROUPS, GM_CHUNK), 0.1),
        'ml_conv_w': nrm(ks[11], (L, CONV_K, 2 * ML_W), CONV_K ** -0.5),
        'ml_conv_b': nrm(ks[14], (L, 2 * ML_W), 0.02),
        'ml_gate_b': ml_gate_b,
        'ml_norm_g': 1.0 + nrm(ks[15], (L, ML_W), 0.02),
        'nsa_pe_k': nrm(ks[16], (L, CMP_BLOCK, NSA_DH), 0.1),
        'nsa_pe_v': nrm(ks[17], (L, CMP_BLOCK, NSA_DH), 0.1),
        'nsa_phi_k1': nrm(ks[18], (L, CMP_BLOCK * NSA_DH, NSA_DH), (CMP_BLOCK * NSA_DH) ** -0.5),
        'nsa_phi_k2': nrm(ks[19], (L, NSA_DH, NSA_DH), NSA_DH ** -0.5),
        'nsa_phi_v1': nrm(ks[20], (L, CMP_BLOCK * NSA_DH, NSA_DH), (CMP_BLOCK * NSA_DH) ** -0.5),
        'nsa_phi_v2': nrm(ks[21], (L, NSA_DH, NSA_DH), NSA_DH ** -0.5),
        'w_up_a': nrm(ks[22], (L, GM_W, D_MODEL), GM_W ** -0.5),
        'w_up_b': nrm(ks[23], (L, ML_W, D_MODEL), ML_W ** -0.5),
        'w_up_c': nrm(ks[24], (L, NSA_W, D_MODEL), NSA_W ** -0.5),
        'w_out': nrm(ks[25], (L, D_MODEL, D_MODEL), D_MODEL ** -0.5),
        'w_mlp1': nrm(ks[26], (L, D_MODEL, D_FF), D_MODEL ** -0.5),
        'w_mlp2': nrm(ks[27], (L, D_FF, D_MODEL), D_FF ** -0.5),
        'g_final': 1.0 + nrm(ks[28], (D_MODEL,), 0.02),
    }


def reference(x, c, g_norm1, g_norm2, w_ada, b_ada, w_in, gm_ln_g, gm_ln_b, gm_ws, gm_bs,
              ml_conv_w, ml_conv_b, ml_gate_b, ml_norm_g, nsa_pe_k, nsa_pe_v, nsa_phi_k1, nsa_phi_k2,
              nsa_phi_v1, nsa_phi_v2, w_up_a, w_up_b, w_up_c, w_out, w_mlp1, w_mlp2, g_final):
    cond = jax.nn.silu(c)
    for l in range(DEPTH):
        mod = (cond @ w_ada[l] + b_ada[l])[:, None, :]
        sh1, sc1, gt1, sh2, sc2, gt2 = jnp.split(mod, 6, axis=-1)
        h = rms_norm(x, g_norm1[l]) * (1.0 + sc1) + sh1
        x = x + gt1 * hybrid_mixer(h, w_in[l], gm_ln_g[l], gm_ln_b[l], gm_ws[l], gm_bs[l],
                                   ml_conv_w[l], ml_conv_b[l], ml_gate_b[l], ml_norm_g[l],
                                   nsa_pe_k[l], nsa_pe_v[l], nsa_phi_k1[l], nsa_phi_k2[l],
                                   nsa_phi_v1[l], nsa_phi_v2[l], w_up_a[l], w_up_b[l], w_up_c[l], w_out[l])
        h = rms_norm(x, g_norm2[l]) * (1.0 + sc2) + sh2
        x = x + gt2 * (jnp.square(jax.nn.relu(h @ w_mlp1[l])) @ w_mlp2[l])
    return rms_norm(x, g_final)
```

```python
import functools

import numpy as np
import jax
import jax.numpy as jnp
from jax import lax
from jax.experimental import pallas as pl
from jax.experimental.pallas import tpu as pltpu

F32 = jnp.float32
BF16 = jnp.bfloat16

D_MODEL = 1024
GM_W = 512
GM_GROUPS = 4
GM_CHUNK = 128
ML_H = 4
ML_DH = 128
ML_W = ML_H * ML_DH
CONV_K = 4
NSA_H = 8
NSA_G = 2
NSA_R = NSA_H // NSA_G
NSA_DH = 64
NSA_W = NSA_H * NSA_DH
NSA_KV = NSA_G * NSA_DH
NSA_NB = 3
CMP_BLOCK = 32
CMP_STRIDE = 16
SEL_BLOCK = 64
TOP_N = 8
WINDOW = 512
D_FF = 4 * D_MODEL
EPS = 1e-6
NEG = -1e30
BIG = 1e4
SPLIT_SIZES = (GM_W, GM_W, ML_W, ML_W, ML_W, ML_W, ML_H, ML_H, NSA_W, NSA_KV, NSA_KV, NSA_KV, NSA_KV,
               NSA_KV, NSA_KV, NSA_H * NSA_NB, D_MODEL, D_MODEL, D_MODEL)
SPLIT_NAMES = ('gu', 'gv', 'mq', 'mk', 'mv', 'mo', 'mi', 'mf', 'nq', 'nkc', 'nvc', 'nks', 'nvs', 'nkw', 'nvw',
               'ngate', 'ga', 'gbr', 'gc')
_OFFS = dict(zip(SPLIT_NAMES, np.concatenate([[0], np.cumsum(SPLIT_SIZES)[:-1]]).tolist()))
_SIZE = dict(zip(SPLIT_NAMES, SPLIT_SIZES))

LANES = 128
VMEM_LIMIT = 56 * 1024 * 1024

ZM_GA, ZM_GBR, ZM_GC = 0, 1024, 2048
ZM_KV = 3072
ZM_GU, ZM_GV = 4096, 4608
ZM_MQ, ZM_MK, ZM_MV, ZM_MO = 5120, 5632, 6144, 6656
ZM_NQ = 7168
ZM_W = 7680
ZS_GATE = 0
ZS_MI = 24
ZS_MF = 28
ZS_W = 128

TM_PROJ = 1024
TN_PROJ = 1536
TS_GMLP = 512
ML_CHUNK = 128
CONV_BLK = 256
CONV_HALO = 16
TQ = 256
TK = 256
TM_MERGE = 512
TM_MLP = 512
FF_CHUNK = 1024


def _dot(a, b):
    return jnp.dot(a, b, preferred_element_type=F32)


def _dot_nt(a, b):
    return lax.dot_general(a, b, (((1,), (1,)), ((), ())), preferred_element_type=F32)


def _dot_tn(a, b):
    return lax.dot_general(a, b, (((0,), (0,)), ((), ())), preferred_element_type=F32)


def _split3(x):
    x1 = x.astype(BF16)
    r1 = x - x1.astype(F32)
    x2 = r1.astype(BF16)
    x3 = (r1 - x2.astype(F32)).astype(BF16)
    return x1, x2, x3


def _cparams(sem):
    return pltpu.CompilerParams(dimension_semantics=sem, vmem_limit_bytes=VMEM_LIMIT)


def _ada_kernel(c_ref, w_ref, b_ref, o_ref):
    c = c_ref[...]
    cond = c * jax.nn.sigmoid(c)
    c1, c2, c3 = _split3(cond)
    w1, w2, w3 = _split3(w_ref[0])
    acc = _dot(c1, w1) + (_dot(c1, w2) + _dot(c2, w1)) + (_dot(c1, w3) + _dot(c2, w2) + _dot(c3, w1))
    o_ref[0] = acc + b_ref[0]


def _ada(c, w_ada, b_ada):
    L, D, N = w_ada.shape
    B = c.shape[0]
    tn = 1536
    return pl.pallas_call(
        _ada_kernel,
        grid=(L, N // tn),
        in_specs=[pl.BlockSpec((B, D), lambda l, j: (0, 0)),
                  pl.BlockSpec((1, D, tn), lambda l, j: (l, 0, j)),
                  pl.BlockSpec((1, 1, tn), lambda l, j: (l, 0, j))],
        out_specs=pl.BlockSpec((1, B, tn), lambda l, j: (l, 0, j)),
        out_shape=jax.ShapeDtypeStruct((L, B, N), F32),
        compiler_params=_cparams(("arbitrary", "arbitrary")),
        name="ada_mod",
    )(c, w_ada, b_ada.reshape(L, 1, N))


def _modulated_norm(x, g, sc, sh):
    y = x * lax.rsqrt(jnp.mean(x * x, axis=-1, keepdims=True) + EPS)
    return (y * g) * (1.0 + sc) + sh


def _inproj_kernel(x_ref, g_ref, sc_ref, sh_ref, wm_ref, wc_ref, ws_ref, zm_ref, zc_ref, zs_ref, h_ref):
    @pl.when(pl.program_id(1) == 0)
    def _():
        h = _modulated_norm(x_ref[...], g_ref[...], sc_ref[0], sh_ref[0]).astype(BF16)
        h_ref[...] = h
        zc_ref[...] = _dot(h, wc_ref[...])
        zs_ref[...] = _dot(h, ws_ref[...])

    zm_ref[...] = _dot(h_ref[...], wm_ref[...]).astype(BF16)


def _inproj(x2, g, sc, sh, wm, wc, ws, S):
    M, D = x2.shape
    tm, tn = min(TM_PROJ, S), TN_PROJ
    per_b = S // tm
    return pl.pallas_call(
        _inproj_kernel,
        grid=(M // tm, ZM_W // tn),
        in_specs=[pl.BlockSpec((tm, D), lambda i, j: (i, 0)),
                  pl.BlockSpec((1, D), lambda i, j: (0, 0)),
                  pl.BlockSpec((1, 1, D), lambda i, j: (i // per_b, 0, 0)),
                  pl.BlockSpec((1, 1, D), lambda i, j: (i // per_b, 0, 0)),
                  pl.BlockSpec((D, tn), lambda i, j: (0, j)),
                  pl.BlockSpec((D, 2 * NSA_KV), lambda i, j: (0, 0)),
                  pl.BlockSpec((D, ZS_W), lambda i, j: (0, 0))],
        out_specs=[pl.BlockSpec((tm, tn), lambda i, j: (i, j)),
                   pl.BlockSpec((tm, 2 * NSA_KV), lambda i, j: (i, 0)),
                   pl.BlockSpec((tm, ZS_W), lambda i, j: (i, 0))],
        out_shape=[jax.ShapeDtypeStruct((M, ZM_W), BF16),
                   jax.ShapeDtypeStruct((M, 2 * NSA_KV), F32),
                   jax.ShapeDtypeStruct((M, ZS_W), F32)],
        scratch_shapes=[pltpu.VMEM((tm, D), BF16)],
        compiler_params=_cparams(("arbitrary", "arbitrary")),
        name="in_proj",
    )(x2, g, sc, sh, wm, wc, ws)


def _gmlp_kernel(u_ref, v_ref, lng_ref, lnb_ref, ws_ref, bst_ref, o_ref):
    ts = u_ref.shape[0]
    dg = GM_W // GM_GROUPS
    row = lax.broadcasted_iota(jnp.int32, (GM_CHUNK, GM_CHUNK), 0)
    col = lax.broadcasted_iota(jnp.int32, (GM_CHUNK, GM_CHUNK), 1)
    ws = [jnp.where(row >= col, ws_ref[g], 0.0).astype(BF16) for g in range(GM_GROUPS)]
    lng = lng_ref[...]
    lnb = lnb_ref[...]
    for c in range(ts // GM_CHUNK):
        r0 = c * GM_CHUNK
        u = jax.nn.gelu(u_ref[r0:r0 + GM_CHUNK, :].astype(F32))
        v = jax.nn.gelu(v_ref[r0:r0 + GM_CHUNK, :].astype(F32))
        mu = jnp.mean(v, axis=-1, keepdims=True)
        var = jnp.mean(jnp.square(v - mu), axis=-1, keepdims=True)
        vb = ((v - mu) * lax.rsqrt(var + EPS) * lng + lnb).astype(BF16)
        for g in range(GM_GROUPS):
            mixed = _dot(ws[g], vb[:, g * dg:(g + 1) * dg]) + bst_ref[:, g:g + 1]
            o_ref[r0:r0 + GM_CHUNK, g * dg:(g + 1) * dg] = (u[:, g * dg:(g + 1) * dg] * mixed).astype(BF16)


def _gmlp(zm, ln_g, ln_b, ws, bs, S):
    M = zm.shape[0]
    ts = min(TS_GMLP, S)
    return pl.pallas_call(
        _gmlp_kernel,
        grid=(M // ts,),
        in_specs=[pl.BlockSpec((ts, GM_W), lambda i: (i, ZM_GU // GM_W)),
                  pl.BlockSpec((ts, GM_W), lambda i: (i, ZM_GV // GM_W)),
                  pl.BlockSpec((1, GM_W), lambda i: (0, 0)),
                  pl.BlockSpec((1, GM_W), lambda i: (0, 0)),
                  pl.BlockSpec((GM_GROUPS, GM_CHUNK, GM_CHUNK), lambda i: (0, 0, 0)),
                  pl.BlockSpec((GM_CHUNK, GM_GROUPS), lambda i: (0, 0))],
        out_specs=pl.BlockSpec((ts, GM_W), lambda i: (i, 0)),
        out_shape=jax.ShapeDtypeStruct((M, GM_W), BF16),
        compiler_params=_cparams(("arbitrary",)),
        name="gmlp_mixer",
    )(zm, zm, ln_g.reshape(1, GM_W), ln_b.reshape(1, GM_W), ws, bs.T)


def _log_sigmoid(x):
    return jnp.minimum(x, 0.0) - jnp.log1p(jnp.exp(-jnp.abs(x)))


def _conv_silu(x_ext, w, b):
    n = x_ext.shape[0] - CONV_HALO
    y = b
    for j in range(CONV_K):
        sh = CONV_K - 1 - j
        xs = x_ext if sh == 0 else pltpu.roll(x_ext, sh, axis=0)
        y = y + xs[CONV_HALO:CONV_HALO + n] * w[j:j + 1]
    return y * jax.nn.sigmoid(y)


def _mlstm_kernel(q_ref, k_ref, v_ref, o_ref, zs_ref, cw_ref, cb_ref, gb_ref, ng_ref, out_ref,
                  qc_ref, kc_ref, c_ref):
    S = q_ref.shape[0]
    L = min(ML_CHUNK, S)
    blk = min(CONV_BLK, S)

    for src, dst, c0, scale in ((q_ref, qc_ref, 0, None), (k_ref, kc_ref, ML_W, ML_DH ** -0.5)):
        w = cw_ref[:, c0:c0 + ML_W]
        b = cb_ref[:, c0:c0 + ML_W]

        def conv_block(x_ext, r0, dst=dst, w=w, b=b, scale=scale):
            y = _conv_silu(x_ext, w, b)
            if scale is not None:
                y = y * scale
            dst[pl.ds(r0, blk), :] = y.astype(BF16)

        first = jnp.concatenate([jnp.zeros((CONV_HALO, ML_W), F32), src[0:blk, :].astype(F32)], axis=0)
        conv_block(first, 0)

        def conv_body(i, carry, src=src, conv_block=conv_block):
            r0 = pl.multiple_of(i * blk, blk)
            conv_block(src[pl.ds(r0 - CONV_HALO, blk + CONV_HALO), :].astype(F32), r0)
            return carry

        lax.fori_loop(1, S // blk, conv_body, 0)

    c_ref[...] = jnp.zeros_like(c_ref)
    row = lax.broadcasted_iota(jnp.int32, (L, L), 0)
    col = lax.broadcasted_iota(jnp.int32, (L, L), 1)
    tril = row >= col
    tril_b = jnp.where(tril, 1.0, 0.0).astype(BF16)
    triu_b = jnp.where(row <= col, 1.0, 0.0).astype(BF16)
    lane = lax.broadcasted_iota(jnp.int32, (L, ML_DH), 1)
    ones_col = jnp.where(lane == 0, 1.0, 0.0).astype(BF16)
    gb = gb_ref[...]

    def chunk_body(c, m_prevs):
        t0 = pl.multiple_of(c * L, L)
        gi = zs_ref[pl.ds(t0, L), :] + gb
        lf_cols = _log_sigmoid(gi)
        gi_t = gi.T
        ig_rows = gi_t[ZS_MI:ZS_MI + ML_H]
        lf_rows = _log_sigmoid(gi_t[ZS_MI:ZS_MI + 2 * ML_H])
        c1, c2, c3 = _split3(lf_cols)
        b_cols = _dot(tril_b, c1) + _dot(tril_b, c2) + _dot(tril_b, c3)
        r1, r2, r3 = _split3(lf_rows)
        b_rows = _dot(r1, triu_b) + _dot(r2, triu_b) + _dot(r3, triu_b)
        m_news = []
        for h in range(ML_H):
            hs = slice(h * ML_DH, (h + 1) * ML_DH)
            bc = b_cols[:, ZS_MF + h:ZS_MF + h + 1]
            igc = gi[:, ZS_MI + h:ZS_MI + h + 1]
            br = b_rows[ML_H + h:ML_H + h + 1, :]
            igr = ig_rows[h:h + 1, :]
            m_prev = m_prevs[h]
            a_col = bc + m_prev
            dmat = jnp.where(tril, bc - br + igr, NEG)
            m_col = jnp.maximum(a_col, jnp.max(dmat, axis=1, keepdims=True))
            w_inter = jnp.exp(a_col - m_col)
            q = qc_ref[pl.ds(t0, L), hs]
            k = kc_ref[pl.ds(t0, L), hs]
            v = v_ref[pl.ds(t0, L), hs]
            v_aug = jnp.concatenate([v, ones_col], axis=1)
            s = _dot_nt(q, k) * jnp.exp(dmat - m_col)
            cmat = c_ref[h]
            num_aug = _dot(s.astype(BF16), v_aug) + w_inter * _dot(q, cmat.astype(BF16))
            num = num_aug[:, :ML_DH]
            den = num_aug[:, ML_DH:ML_DH + 1]
            hval = num / jnp.maximum(jnp.abs(den), jnp.exp(-m_col))
            m_last = m_col[L - 1:L, :]
            w_prev = jnp.exp(a_col[L - 1:L, :] - m_last)
            w_s = jnp.exp(bc[L - 1:L, :] - bc + igc - m_last)
            vw = (w_s * v_aug.astype(F32)).astype(BF16)
            c_ref[h] = w_prev * cmat + _dot_tn(k, vw)
            m_news.append(m_last)
            mu = jnp.mean(hval, axis=-1, keepdims=True)
            var = jnp.mean(jnp.square(hval - mu), axis=-1, keepdims=True)
            hn = (hval - mu) * lax.rsqrt(var + EPS) * ng_ref[:, hs]
            og = jax.nn.sigmoid(o_ref[pl.ds(t0, L), hs].astype(F32))
            out_ref[pl.ds(t0, L), hs] = (og * hn).astype(BF16)
        return tuple(m_news)

    lax.fori_loop(0, S // L, chunk_body, tuple(jnp.zeros((1, 1), F32) for _ in range(ML_H)))


def _mlstm(zm, zs, conv_w, conv_b, gate_b, norm_g, S):
    M = zm.shape[0]
    gb_row = jnp.zeros((1, ZS_W), F32).at[0, ZS_MI:ZS_MI + 2 * ML_H].set(gate_b)

    def col(off):
        return pl.BlockSpec((S, ML_W), lambda b: (b, off // ML_W))

    return pl.pallas_call(
        _mlstm_kernel,
        grid=(M // S,),
        in_specs=[col(ZM_MQ), col(ZM_MK), col(ZM_MV), col(ZM_MO),
                  pl.BlockSpec((S, ZS_W), lambda b: (b, 0)),
                  pl.BlockSpec((CONV_K, 2 * ML_W), lambda b: (0, 0)),
                  pl.BlockSpec((1, 2 * ML_W), lambda b: (0, 0)),
                  pl.BlockSpec((1, ZS_W), lambda b: (0, 0)),
                  pl.BlockSpec((1, ML_W), lambda b: (0, 0))],
        out_specs=pl.BlockSpec((S, ML_W), lambda b: (b, 0)),
        out_shape=jax.ShapeDtypeStruct((M, ML_W), BF16),
        scratch_shapes=[pltpu.VMEM((S, ML_W), BF16), pltpu.VMEM((S, ML_W), BF16),
                        pltpu.VMEM((ML_H, ML_DH, 2 * ML_DH), F32)],
        compiler_params=_cparams(("arbitrary",)),
        name="mlstm_mixer",
    )(zm, zm, zm, zm, zs, conv_w, conv_b.reshape(1, 2 * ML_W), gb_row, norm_g.reshape(1, ML_W))


def _compress_kernel(x_ref, pe_ref, w1_ref, w2_ref, o_ref):
    n = x_ref.shape[0]
    cw = 2 * NSA_KV
    half = CMP_BLOCK // 2
    acc_a = jnp.zeros((n, cw), F32)
    acc_b = jnp.zeros((n, cw), F32)
    for j in range(half):
        xj = x_ref[:, j * cw:(j + 1) * cw]
        acc_a = acc_a + _dot((xj + pe_ref[j:j + 1, :]).astype(BF16), w1_ref[j])
        acc_b = acc_b + _dot((xj + pe_ref[half + j:half + j + 1, :]).astype(BF16), w1_ref[half + j])
    pre = acc_a + pltpu.roll(acc_b, n - 1, axis=0)
    o_ref[0] = _dot(jax.nn.gelu(pre).astype(BF16), w2_ref[...]).astype(BF16)


def _blockdiag(blocks):
    n = len(blocks)
    rows = []
    for i, blk in enumerate(blocks):
        rows.append(jnp.concatenate([blk if j == i else jnp.zeros((blk.shape[0], blocks[j].shape[1]), blk.dtype)
                                     for j in range(n)], axis=1))
    return jnp.concatenate(rows, axis=0)


def _compress(zc, pe_k, pe_v, phi_k1, phi_k2, phi_v1, phi_v2, S):
    M = zc.shape[0]
    B = M // S
    n = S // CMP_STRIDE
    cw = 2 * NSA_KV
    x = zc.reshape(B * n, CMP_STRIDE * cw)
    pe = jnp.concatenate([pe_k, pe_k, pe_v, pe_v], axis=1)
    k1 = phi_k1.reshape(CMP_BLOCK, NSA_DH, NSA_DH)
    v1 = phi_v1.reshape(CMP_BLOCK, NSA_DH, NSA_DH)
    w1 = jnp.stack([_blockdiag([k1[j], k1[j], v1[j], v1[j]]) for j in range(CMP_BLOCK)]).astype(BF16)
    k2 = jnp.concatenate([phi_k2, phi_k2], axis=1)
    v2 = jnp.concatenate([phi_v2, phi_v2], axis=1)
    w2 = _blockdiag([k2, k2, v2, v2]).astype(BF16)
    return pl.pallas_call(
        _compress_kernel,
        grid=(B,),
        in_specs=[pl.BlockSpec((n, CMP_STRIDE * cw), lambda b: (b, 0)),
                  pl.BlockSpec((CMP_BLOCK, cw), lambda b: (0, 0)),
                  pl.BlockSpec((CMP_BLOCK, cw, cw), lambda b: (0, 0, 0)),
                  pl.BlockSpec((cw, 2 * cw), lambda b: (0, 0))],
        out_specs=pl.BlockSpec((1, n, 2 * cw), lambda b: (b, 0, 0)),
        out_shape=jax.ShapeDtypeStruct((B, n, 2 * cw), BF16),
        compiler_params=_cparams(("arbitrary",)),
        name="nsa_compress",
    )(x, pe, w1, w2)


def _nsa_kernel(q_ref, cmp_ref, kv_ref, zs_ref, ovt_ref, out_ref,
                qe_ref, ocmp_ref, owin_ref, sel_ref, m_ref, l_ref, acc_ref):
    S = kv_ref.shape[0]
    tq = q_ref.shape[0]
    ncmp = cmp_ref.shape[1]
    nsel = S // SEL_BLOCK
    wspan = min(WINDOW + tq, S)
    qi = pl.program_id(1)
    t0 = qi * tq
    t_col = t0 + lax.broadcasted_iota(jnp.int32, (tq, 1), 0)
    tf_col = t_col.astype(F32)
    lane = lax.broadcasted_iota(jnp.int32, (1, LANES), 1)
    lo_half = lane < NSA_DH

    for h in range(NSA_H):
        pair = q_ref[:, (h // 2) * LANES:(h // 2 + 1) * LANES]
        keep = lo_half if h % 2 == 0 else jnp.logical_not(lo_half)
        qe_ref[h] = jnp.where(keep, pair, jnp.zeros_like(pair)) * (NSA_DH ** -0.5)

    c_row = lax.broadcasted_iota(jnp.int32, (1, ncmp), 1)
    cmp_end = c_row * CMP_STRIDE + (CMP_BLOCK - 1)
    cmp_center = (c_row * CMP_STRIDE).astype(F32) + (CMP_BLOCK - 1) * 0.5
    valid_c = cmp_end <= t_col
    dist_c = tf_col - cmp_center

    w_start = pl.multiple_of(jnp.maximum(t0 + tq - wspan, 0), tq)
    pos_w = w_start + lax.broadcasted_iota(jnp.int32, (1, wspan), 1)
    valid_w = jnp.logical_and(pos_w <= t_col, t_col - pos_w < WINDOW)
    dist_w = tf_col - pos_w.astype(F32)

    j_col = lax.broadcasted_iota(jnp.int32, (nsel, 1), 0)
    t_row = t0 + lax.broadcasted_iota(jnp.int32, (1, tq), 1)
    jt = t_row >> 6
    forced = jnp.logical_or(j_col == 0, jnp.logical_or(j_col == jt, j_col == jt - 1))
    future = j_col > jt

    for g in range(NSA_G):
        kc = cmp_ref[0, :, g * LANES:(g + 1) * LANES]
        vc = cmp_ref[0, :, (NSA_G + g) * LANES:(NSA_G + g + 1) * LANES]
        kw = kv_ref[pl.ds(w_start, wspan), (4 + g) * LANES:(5 + g) * LANES]
        vw = kv_ref[pl.ds(w_start, wspan), (6 + g) * LANES:(7 + g) * LANES]
        psum = jnp.zeros((tq, ncmp), F32)
        for r in range(NSA_R):
            h = g * NSA_R + r
            slope = 2.0 ** (-8.0 * (h + 1.0) / NSA_H)
            qe = qe_ref[h]
            sc = jnp.where(valid_c, _dot_nt(qe, kc) - slope * dist_c, NEG)
            e = jnp.where(valid_c, jnp.exp(sc - jnp.max(sc, axis=1, keepdims=True)), 0.0)
            den = jnp.sum(e, axis=1, keepdims=True)
            p = e / jnp.where(den > 0.0, den, 1.0)
            psum = psum + p
            ocmp_ref[h] = _dot(p.astype(BF16), vc)
            sw = jnp.where(valid_w, _dot_nt(qe, kw) - slope * dist_w, NEG)
            e = jnp.exp(sw - jnp.max(sw, axis=1, keepdims=True))
            p = e / jnp.sum(e, axis=1, keepdims=True)
            owin_ref[h] = _dot(p.astype(BF16), vw)

        p_hi = psum.astype(BF16)
        p_lo = (psum - p_hi.astype(F32)).astype(BF16)
        imp_t = _dot_nt(ovt_ref[...], p_hi) + _dot_nt(ovt_ref[...], p_lo)
        val = jnp.where(future, NEG, jnp.where(forced, BIG, imp_t[:nsel]))
        rank = jnp.zeros((nsel, tq), F32)
        for i in range(nsel):
            vi = val[i:i + 1, :]
            tie = jnp.where(vi == val, 1.0, 0.0) * jnp.where(j_col > i, 1.0, 0.0)
            rank = rank + jnp.where(vi > val, 1.0, 0.0) + tie
        sel_t = jnp.where(rank < float(min(TOP_N, nsel)), 1.0, 0.0)
        if nsel < LANES:
            sel_t = jnp.concatenate([sel_t, jnp.zeros((LANES - nsel, tq), F32)], axis=0)
        sel_ref[g] = sel_t.T.astype(BF16)

    m_ref[...] = jnp.full(m_ref.shape, NEG, F32)
    l_ref[...] = jnp.zeros(l_ref.shape, F32)
    acc_ref[...] = jnp.zeros(acc_ref.shape, F32)
    e_row = lax.broadcasted_iota(jnp.int32, (LANES, TK), 0)
    e_col = lax.broadcasted_iota(jnp.int32, (LANES, TK), 1) >> 6

    def sel_body(kb, carry):
        k0 = pl.multiple_of(kb * TK, TK)
        pos = k0 + lax.broadcasted_iota(jnp.int32, (1, TK), 1)
        causal = pos <= t_col
        dist = tf_col - pos.astype(F32)
        expand = jnp.where(e_row == e_col + kb * (TK // SEL_BLOCK), 1.0, 0.0).astype(BF16)
        for g in range(NSA_G):
            chosen = _dot(sel_ref[g], expand) > 0.5
            valid = jnp.logical_and(chosen, causal)
            ks = kv_ref[pl.ds(k0, TK), g * LANES:(g + 1) * LANES]
            vs = kv_ref[pl.ds(k0, TK), (2 + g) * LANES:(3 + g) * LANES]
            for r in range(NSA_R):
                h = g * NSA_R + r
                slope = 2.0 ** (-8.0 * (h + 1.0) / NSA_H)
                ss = jnp.where(valid, _dot_nt(qe_ref[h], ks) - slope * dist, NEG)
                m_old = m_ref[h]
                m_new = jnp.maximum(m_old, jnp.max(ss, axis=1, keepdims=True))
                alpha = jnp.exp(m_old - m_new)
                p = jnp.where(valid, jnp.exp(ss - m_new[:, 0:1]), 0.0)
                l_ref[h] = alpha * l_ref[h] + jnp.sum(p, axis=1, keepdims=True)
                acc_ref[h] = alpha * acc_ref[h] + _dot(p.astype(BF16), vs)
                m_ref[h] = m_new
        return carry

    lax.fori_loop(0, qi + 1, sel_body, 0)

    gates = jax.nn.sigmoid(zs_ref[...])
    for hp in range(NSA_H // 2):
        comb = []
        for e in range(2):
            h = 2 * hp + e
            c0 = ZS_GATE + NSA_NB * h
            o_sel = acc_ref[h] / l_ref[h]
            comb.append(gates[:, c0:c0 + 1] * ocmp_ref[h] + gates[:, c0 + 1:c0 + 2] * o_sel
                        + gates[:, c0 + 2:c0 + 3] * owin_ref[h])
        out_ref[:, hp * LANES:(hp + 1) * LANES] = jnp.where(lo_half, comb[0], comb[1]).astype(BF16)


def _overlap_t(S):
    n_cmp_pad = S // CMP_STRIDE
    c = np.arange(n_cmp_pad)[None, :]
    j = np.arange(LANES)[:, None]
    ov = (c * CMP_STRIDE <= j * SEL_BLOCK + SEL_BLOCK - 1) & (c * CMP_STRIDE + CMP_BLOCK - 1 >= j * SEL_BLOCK)
    ov &= (j < S // SEL_BLOCK) & (c < (S - CMP_BLOCK) // CMP_STRIDE + 1)
    return jnp.asarray(ov.astype(np.float32), dtype=BF16)


def _nsa(zm, zs, cmp, S):
    M = zm.shape[0]
    B = M // S
    tq = min(TQ, S)
    nq = S // tq
    n = S // CMP_STRIDE
    head = pltpu.VMEM((NSA_H, tq, LANES), F32)
    return pl.pallas_call(
        _nsa_kernel,
        grid=(B, nq),
        in_specs=[pl.BlockSpec((tq, NSA_W), lambda b, i: (b * nq + i, ZM_NQ // NSA_W)),
                  pl.BlockSpec((1, n, 4 * LANES), lambda b, i: (b, 0, 0)),
                  pl.BlockSpec((S, 8 * LANES), lambda b, i: (b, ZM_KV // (8 * LANES))),
                  pl.BlockSpec((tq, ZS_W), lambda b, i: (b * nq + i, 0)),
                  pl.BlockSpec((LANES, n), lambda b, i: (0, 0))],
        out_specs=pl.BlockSpec((tq, NSA_W), lambda b, i: (b * nq + i, 0)),
        out_shape=jax.ShapeDtypeStruct((M, NSA_W), BF16),
        scratch_shapes=[pltpu.VMEM((NSA_H, tq, LANES), BF16), head, head,
                        pltpu.VMEM((NSA_G, tq, LANES), BF16), head, head, head],
        compiler_params=_cparams(("arbitrary", "arbitrary")),
        name="nsa_attention",
    )(zm, cmp, zm, zs, _overlap_t(S))


def _merge_kernel(a_ref, b_ref, c_ref, ga_ref, gb_ref, gc_ref, x_ref, gt_ref, wa_ref, wb_ref, wc_ref, wo_ref,
                  o_ref):
    merged = (jax.nn.sigmoid(ga_ref[...].astype(F32)) * _dot(a_ref[...], wa_ref[...])
              + jax.nn.sigmoid(gb_ref[...].astype(F32)) * _dot(b_ref[...], wb_ref[...])
              + jax.nn.sigmoid(gc_ref[...].astype(F32)) * _dot(c_ref[...], wc_ref[...]))
    o_ref[...] = x_ref[...] + gt_ref[0] * _dot(merged.astype(BF16), wo_ref[...])


def _merge(ya, yb, yc, zm, x2, gt, wa, wb, wc, wo, S):
    M, D = x2.shape
    tm = min(TM_MERGE, S)
    per_b = S // tm

    def rows(w, jcol=0):
        return pl.BlockSpec((tm, w), lambda i: (i, jcol))

    def full(shape):
        return pl.BlockSpec(shape, lambda i: (0,) * len(shape))

    return pl.pallas_call(
        _merge_kernel,
        grid=(M // tm,),
        in_specs=[rows(GM_W), rows(ML_W), rows(NSA_W),
                  rows(D, ZM_GA // D), rows(D, ZM_GBR // D), rows(D, ZM_GC // D),
                  rows(D), pl.BlockSpec((1, 1, D), lambda i: (i // per_b, 0, 0)),
                  full((GM_W, D)), full((ML_W, D)), full((NSA_W, D)), full((D, D))],
        out_specs=rows(D),
        out_shape=jax.ShapeDtypeStruct((M, D), F32),
        compiler_params=_cparams(("arbitrary",)),
        name="merge_out",
    )(ya, yb, yc, zm, zm, zm, x2, gt, wa, wb, wc, wo)


def _mlp_kernel(x_ref, g_ref, sc_ref, sh_ref, gt_ref, w1_ref, w2_ref, gf_ref, o_ref, acc_ref, *, final_norm):
    x = x_ref[...]
    h = _modulated_norm(x, g_ref[...], sc_ref[0], sh_ref[0]).astype(BF16)
    for c in range(D_FF // FF_CHUNK):
        mid = jnp.square(jnp.maximum(_dot(h, w1_ref[:, c * FF_CHUNK:(c + 1) * FF_CHUNK]), 0.0)).astype(BF16)
        upd = _dot(mid, w2_ref[c * FF_CHUNK:(c + 1) * FF_CHUNK, :])
        if c == 0:
            acc_ref[...] = upd
        else:
            acc_ref[...] += upd
    y = x + gt_ref[0] * acc_ref[...]
    if final_norm:
        y = (y * lax.rsqrt(jnp.mean(y * y, axis=-1, keepdims=True) + EPS)) * gf_ref[...]
    o_ref[...] = y


def _mlp(x2, g, sc, sh, gt, w1, w2, g_final, S, final_norm):
    M, D = x2.shape
    tm = min(TM_MLP, S)
    per_b = S // tm
    mod = pl.BlockSpec((1, 1, D), lambda i: (i // per_b, 0, 0))
    return pl.pallas_call(
        functools.partial(_mlp_kernel, final_norm=final_norm),
        grid=(M // tm,),
        in_specs=[pl.BlockSpec((tm, D), lambda i: (i, 0)),
                  pl.BlockSpec((1, D), lambda i: (0, 0)),
                  mod, mod, mod,
                  pl.BlockSpec((D, D_FF), lambda i: (0, 0)),
                  pl.BlockSpec((D_FF, D), lambda i: (0, 0)),
                  pl.BlockSpec((1, D), lambda i: (0, 0))],
        out_specs=pl.BlockSpec((tm, D), lambda i: (i, 0)),
        out_shape=jax.ShapeDtypeStruct((M, D), F32),
        scratch_shapes=[pltpu.VMEM((tm, D), F32)],
        compiler_params=_cparams(("arbitrary",)),
        name="relu2_mlp",
    )(x2, g, sc, sh, gt, w1, w2, g_final)


def _w_cols(w, name):
    return w[:, _OFFS[name]:_OFFS[name] + _SIZE[name]]


def _dup_heads(w):
    parts = []
    for g in range(NSA_G):
        blk = w[:, g * NSA_DH:(g + 1) * NSA_DH]
        parts += [blk, blk]
    return jnp.concatenate(parts, axis=1)


def _proj_weights(w):
    D = w.shape[0]
    kv = [_dup_heads(_w_cols(w, n)) for n in ('nks', 'nvs', 'nkw', 'nvw')]
    wm = jnp.concatenate([_w_cols(w, 'ga'), _w_cols(w, 'gbr'), _w_cols(w, 'gc')] + kv
                         + [_w_cols(w, n) for n in ('gu', 'gv', 'mq', 'mk', 'mv', 'mo', 'nq')], axis=1)
    wc = jnp.concatenate([_w_cols(w, 'nkc'), _w_cols(w, 'nvc')], axis=1)
    ws = jnp.concatenate([_w_cols(w, 'ngate'), _w_cols(w, 'mi'), _w_cols(w, 'mf'),
                          jnp.zeros((D, ZS_W - NSA_H * NSA_NB - 2 * ML_H), w.dtype)], axis=1)
    return wm.astype(BF16), wc.astype(BF16), ws.astype(BF16)


def kernel(x, c, g_norm1, g_norm2, w_ada, b_ada, w_in, gm_ln_g, gm_ln_b, gm_ws, gm_bs, ml_conv_w, ml_conv_b,
           ml_gate_b, ml_norm_g, nsa_pe_k, nsa_pe_v, nsa_phi_k1, nsa_phi_k2, nsa_phi_v1, nsa_phi_v2,
           w_up_a, w_up_b, w_up_c, w_out, w_mlp1, w_mlp2, g_final):
    B, S, D = x.shape
    depth = w_in.shape[0]
    M = B * S
    mod = _ada(c, w_ada, b_ada)
    x2 = x.reshape(M, D)
    for l in range(depth):
        sh1, sc1, gt1, sh2, sc2, gt2 = [mod[l, :, i * D:(i + 1) * D].reshape(B, 1, D) for i in range(6)]
        wm, wc, ws = _proj_weights(w_in[l])
        zm, zc, zs = _inproj(x2, g_norm1[l].reshape(1, D), sc1, sh1, wm, wc, ws, S)
        ya = _gmlp(zm, gm_ln_g[l], gm_ln_b[l], gm_ws[l], gm_bs[l], S)
        yb = _mlstm(zm, zs, ml_conv_w[l], ml_conv_b[l], ml_gate_b[l], ml_norm_g[l], S)
        cmp = _compress(zc, nsa_pe_k[l], nsa_pe_v[l], nsa_phi_k1[l], nsa_phi_k2[l], nsa_phi_v1[l],
                        nsa_phi_v2[l], S)
        yc = _nsa(zm, zs, cmp, S)
        x2 = _merge(ya, yb, yc, zm, x2, gt1, w_up_a[l].astype(BF16), w_up_b[l].astype(BF16),
                    w_up_c[l].astype(BF16), w_out[l].astype(BF16), S)
        x2 = _mlp(x2, g_norm2[l].reshape(1, D), sc2, sh2, gt2, w_mlp1[l].astype(BF16), w_mlp2[l].astype(BF16),
                  g_final.reshape(1, D), S, final_norm=(l == depth - 1))
    return x2.reshape(B, S, D)
```

```python
import functools

import numpy as np
import jax
import jax.numpy as jnp
from jax import lax
from jax.experimental import pallas as pl
from jax.experimental.pallas import tpu as pltpu

F32 = jnp.float32
BF16 = jnp.bfloat16

D_MODEL = 1024
GM_W = 512
GM_GROUPS = 4
GM_CHUNK = 128
ML_H = 4
ML_DH = 128
ML_W = ML_H * ML_DH
CONV_K = 4
NSA_H = 8
NSA_G = 2
NSA_R = NSA_H // NSA_G
NSA_DH = 64
NSA_W = NSA_H * NSA_DH
NSA_KV = NSA_G * NSA_DH
NSA_NB = 3
CMP_BLOCK = 32
CMP_STRIDE = 16
SEL_BLOCK = 64
TOP_N = 8
WINDOW = 512
D_FF = 4 * D_MODEL
EPS = 1e-6
NEG = -1e30
BIG = 1e4
SPLIT_SIZES = (GM_W, GM_W, ML_W, ML_W, ML_W, ML_W, ML_H, ML_H, NSA_W, NSA_KV, NSA_KV, NSA_KV, NSA_KV,
               NSA_KV, NSA_KV, NSA_H * NSA_NB, D_MODEL, D_MODEL, D_MODEL)
SPLIT_NAMES = ('gu', 'gv', 'mq', 'mk', 'mv', 'mo', 'mi', 'mf', 'nq', 'nkc', 'nvc', 'nks', 'nvs', 'nkw', 'nvw',
               'ngate', 'ga', 'gbr', 'gc')
_OFFS = dict(zip(SPLIT_NAMES, np.concatenate([[0], np.cumsum(SPLIT_SIZES)[:-1]]).tolist()))
_SIZE = dict(zip(SPLIT_NAMES, SPLIT_SIZES))

LANES = 128
VMEM_LIMIT = 56 * 1024 * 1024

ZM_GA, ZM_GBR, ZM_GC = 0, 1024, 2048
ZM_GU, ZM_GV = 3072, 3584
ZM_MQ, ZM_MK, ZM_MV, ZM_MO = 4096, 4608, 5120, 5632
ZM_NQ = 6144
ZM_KK = 6656
ZM_VV = 7168
ZM_W = 7424
KK_W = 4 * LANES
VV_W = 2 * LANES
BR_SEL, BR_WIN = 0, 1
FEAT_SEL = 0
FEAT_HI = 32
FEAT_LO = 33
ZS_GATE = 0
ZS_MI = 24
ZS_MF = 28
ZS_W = 128

TM_PROJ = 1024
TN_PROJ = 3712
TS_GMLP = 512
ML_CHUNK = 128
CONV_BLK = 256
CONV_HALO = 16
TQ = 256
TK = 256
TM_MERGE = 512
TM_MLP = 512
FF_CHUNK = 1024


def _dot(a, b):
    return jnp.dot(a, b, preferred_element_type=F32)


def _dot_nt(a, b):
    return lax.dot_general(a, b, (((1,), (1,)), ((), ())), preferred_element_type=F32)


def _dot_tn(a, b):
    return lax.dot_general(a, b, (((0,), (0,)), ((), ())), preferred_element_type=F32)


def _split3(x):
    x1 = x.astype(BF16)
    r1 = x - x1.astype(F32)
    x2 = r1.astype(BF16)
    x3 = (r1 - x2.astype(F32)).astype(BF16)
    return x1, x2, x3


def _cparams(sem):
    return pltpu.CompilerParams(dimension_semantics=sem, vmem_limit_bytes=VMEM_LIMIT)


def _ada_kernel(c_ref, w_ref, b_ref, o_ref):
    c = c_ref[...]
    cond = c * jax.nn.sigmoid(c)
    c1, c2, c3 = _split3(cond)
    w1, w2, w3 = _split3(w_ref[0])
    acc = _dot(c1, w1) + (_dot(c1, w2) + _dot(c2, w1)) + (_dot(c1, w3) + _dot(c2, w2) + _dot(c3, w1))
    o_ref[0] = acc + b_ref[0]


def _ada(c, w_ada, b_ada):
    L, D, N = w_ada.shape
    B = c.shape[0]
    tn = 1536
    return pl.pallas_call(
        _ada_kernel,
        grid=(L, N // tn),
        in_specs=[pl.BlockSpec((B, D), lambda l, j: (0, 0)),
                  pl.BlockSpec((1, D, tn), lambda l, j: (l, 0, j)),
                  pl.BlockSpec((1, 1, tn), lambda l, j: (l, 0, j))],
        out_specs=pl.BlockSpec((1, B, tn), lambda l, j: (l, 0, j)),
        out_shape=jax.ShapeDtypeStruct((L, B, N), F32),
        compiler_params=_cparams(("arbitrary", "arbitrary")),
        name="ada_mod",
    )(c, w_ada, b_ada.reshape(L, 1, N))


def _modulated_norm(x, g, sc, sh):
    y = x * lax.rsqrt(jnp.mean(x * x, axis=-1, keepdims=True) + EPS)
    return (y * g) * (1.0 + sc) + sh


def _inproj_kernel(x_ref, g_ref, sc_ref, sh_ref, wm_ref, wc_ref, ws_ref, zm_ref, zc_ref, zs_ref, h_ref):
    @pl.when(pl.program_id(1) == 0)
    def _():
        h = _modulated_norm(x_ref[...], g_ref[...], sc_ref[0], sh_ref[0]).astype(BF16)
        h_ref[...] = h
        zc_ref[...] = _dot(h, wc_ref[...])
        zs_ref[...] = _dot(h, ws_ref[...])

    zm_ref[...] = _dot(h_ref[...], wm_ref[...]).astype(BF16)


def _inproj(x2, g, sc, sh, wm, wc, ws, S):
    M, D = x2.shape
    tm, tn = min(TM_PROJ, S), TN_PROJ
    per_b = S // tm
    return pl.pallas_call(
        _inproj_kernel,
        grid=(M // tm, ZM_W // tn),
        in_specs=[pl.BlockSpec((tm, D), lambda i, j: (i, 0)),
                  pl.BlockSpec((1, D), lambda i, j: (0, 0)),
                  pl.BlockSpec((1, 1, D), lambda i, j: (i // per_b, 0, 0)),
                  pl.BlockSpec((1, 1, D), lambda i, j: (i // per_b, 0, 0)),
                  pl.BlockSpec((D, tn), lambda i, j: (0, j)),
                  pl.BlockSpec((D, 2 * NSA_KV), lambda i, j: (0, 0)),
                  pl.BlockSpec((D, ZS_W), lambda i, j: (0, 0))],
        out_specs=[pl.BlockSpec((tm, tn), lambda i, j: (i, j)),
                   pl.BlockSpec((tm, 2 * NSA_KV), lambda i, j: (i, 0)),
                   pl.BlockSpec((tm, ZS_W), lambda i, j: (i, 0))],
        out_shape=[jax.ShapeDtypeStruct((M, ZM_W), BF16),
                   jax.ShapeDtypeStruct((M, 2 * NSA_KV), F32),
                   jax.ShapeDtypeStruct((M, ZS_W), F32)],
        scratch_shapes=[pltpu.VMEM((tm, D), BF16)],
        compiler_params=_cparams(("arbitrary", "arbitrary")),
        name="in_proj",
    )(x2, g, sc, sh, wm, wc, ws)


def _gmlp_kernel(u_ref, v_ref, lng_ref, lnb_ref, ws_ref, bst_ref, o_ref):
    ts = u_ref.shape[0]
    dg = GM_W // GM_GROUPS
    row = lax.broadcasted_iota(jnp.int32, (GM_CHUNK, GM_CHUNK), 0)
    col = lax.broadcasted_iota(jnp.int32, (GM_CHUNK, GM_CHUNK), 1)
    ws = [jnp.where(row >= col, ws_ref[g], 0.0).astype(BF16) for g in range(GM_GROUPS)]
    lng = lng_ref[...]
    lnb = lnb_ref[...]
    for c in range(ts // GM_CHUNK):
        r0 = c * GM_CHUNK
        u = jax.nn.gelu(u_ref[r0:r0 + GM_CHUNK, :].astype(F32))
        v = jax.nn.gelu(v_ref[r0:r0 + GM_CHUNK, :].astype(F32))
        mu = jnp.mean(v, axis=-1, keepdims=True)
        var = jnp.mean(jnp.square(v - mu), axis=-1, keepdims=True)
        vb = ((v - mu) * lax.rsqrt(var + EPS) * lng + lnb).astype(BF16)
        for g in range(GM_GROUPS):
            mixed = _dot(ws[g], vb[:, g * dg:(g + 1) * dg]) + bst_ref[:, g:g + 1]
            o_ref[r0:r0 + GM_CHUNK, g * dg:(g + 1) * dg] = (u[:, g * dg:(g + 1) * dg] * mixed).astype(BF16)


def _gmlp(zm, ln_g, ln_b, ws, bs, S):
    M = zm.shape[0]
    ts = min(TS_GMLP, S)
    return pl.pallas_call(
        _gmlp_kernel,
        grid=(M // ts,),
        in_specs=[pl.BlockSpec((ts, GM_W), lambda i: (i, ZM_GU // GM_W)),
                  pl.BlockSpec((ts, GM_W), lambda i: (i, ZM_GV // GM_W)),
                  pl.BlockSpec((1, GM_W), lambda i: (0, 0)),
                  pl.BlockSpec((1, GM_W), lambda i: (0, 0)),
                  pl.BlockSpec((GM_GROUPS, GM_CHUNK, GM_CHUNK), lambda i: (0, 0, 0)),
                  pl.BlockSpec((GM_CHUNK, GM_GROUPS), lambda i: (0, 0))],
        out_specs=pl.BlockSpec((ts, GM_W), lambda i: (i, 0)),
        out_shape=jax.ShapeDtypeStruct((M, GM_W), BF16),
        compiler_params=_cparams(("arbitrary",)),
        name="gmlp_mixer",
    )(zm, zm, ln_g.reshape(1, GM_W), ln_b.reshape(1, GM_W), ws, bs.T)


def _log_sigmoid(x):
    return jnp.minimum(x, 0.0) - jnp.log1p(jnp.exp(-jnp.abs(x)))


def _conv_silu(x_ext, w, b):
    n = x_ext.shape[0] - CONV_HALO
    y = b
    for j in range(CONV_K):
        sh = CONV_K - 1 - j
        xs = x_ext if sh == 0 else pltpu.roll(x_ext, sh, axis=0)
        y = y + xs[CONV_HALO:CONV_HALO + n] * w[j:j + 1]
    return y * jax.nn.sigmoid(y)


def _mlstm_kernel(q_ref, k_ref, v_ref, o_ref, zs_ref, cw_ref, cb_ref, gb_ref, ng_ref, out_ref,
                  qc_ref, kc_ref, c_ref):
    S = q_ref.shape[0]
    L = min(ML_CHUNK, S)
    blk = min(CONV_BLK, S)

    for src, dst, c0, scale in ((q_ref, qc_ref, 0, None), (k_ref, kc_ref, ML_W, ML_DH ** -0.5)):
        w = cw_ref[:, c0:c0 + ML_W]
        b = cb_ref[:, c0:c0 + ML_W]

        def conv_block(x_ext, r0, dst=dst, w=w, b=b, scale=scale):
            y = _conv_silu(x_ext, w, b)
            if scale is not None:
                y = y * scale
            dst[pl.ds(r0, blk), :] = y.astype(BF16)

        first = jnp.concatenate([jnp.zeros((CONV_HALO, ML_W), F32), src[0:blk, :].astype(F32)], axis=0)
        conv_block(first, 0)

        def conv_body(i, carry, src=src, conv_block=conv_block):
            r0 = pl.multiple_of(i * blk, blk)
            conv_block(src[pl.ds(r0 - CONV_HALO, blk + CONV_HALO), :].astype(F32), r0)
            return carry

        lax.fori_loop(1, S // blk, conv_body, 0)

    c_ref[...] = jnp.zeros_like(c_ref)
    row = lax.broadcasted_iota(jnp.int32, (L, L), 0)
    col = lax.broadcasted_iota(jnp.int32, (L, L), 1)
    tril = row >= col
    tril_b = jnp.where(tril, 1.0, 0.0).astype(BF16)
    triu_b = jnp.where(row <= col, 1.0, 0.0).astype(BF16)
    lane = lax.broadcasted_iota(jnp.int32, (L, ML_DH), 1)
    ones_col = jnp.where(lane == 0, 1.0, 0.0).astype(BF16)
    gb = gb_ref[...]

    def chunk_body(c, m_prevs):
        t0 = pl.multiple_of(c * L, L)
        gi = zs_ref[pl.ds(t0, L), :] + gb
        lf_cols = _log_sigmoid(gi)
        gi_t = gi.T
        ig_rows = gi_t[ZS_MI:ZS_MI + ML_H]
        lf_rows = _log_sigmoid(gi_t[ZS_MI:ZS_MI + 2 * ML_H])
        c1, c2, c3 = _split3(lf_cols)
        b_cols = _dot(tril_b, c1) + _dot(tril_b, c2) + _dot(tril_b, c3)
        r1, r2, r3 = _split3(lf_rows)
        b_rows = _dot(r1, triu_b) + _dot(r2, triu_b) + _dot(r3, triu_b)
        m_news = []
        for h in range(ML_H):
            hs = slice(h * ML_DH, (h + 1) * ML_DH)
            bc = b_cols[:, ZS_MF + h:ZS_MF + h + 1]
            igc = gi[:, ZS_MI + h:ZS_MI + h + 1]
            br = b_rows[ML_H + h:ML_H + h + 1, :]
            igr = ig_rows[h:h + 1, :]
            m_prev = m_prevs[h]
            a_col = bc + m_prev
            dmat = jnp.where(tril, bc - br + igr, NEG)
            m_col = jnp.maximum(a_col, jnp.max(dmat, axis=1, keepdims=True))
            w_inter = jnp.exp(a_col - m_col)
            q = qc_ref[pl.ds(t0, L), hs]
            k = kc_ref[pl.ds(t0, L), hs]
            v = v_ref[pl.ds(t0, L), hs]
            v_aug = jnp.concatenate([v, ones_col], axis=1)
            s = _dot_nt(q, k) * jnp.exp(dmat - m_col)
            cmat = c_ref[h]
            num_aug = _dot(s.astype(BF16), v_aug) + w_inter * _dot(q, cmat.astype(BF16))
            num = num_aug[:, :ML_DH]
            den = num_aug[:, ML_DH:ML_DH + 1]
            hval = num / jnp.maximum(jnp.abs(den), jnp.exp(-m_col))
            m_last = m_col[L - 1:L, :]
            w_prev = jnp.exp(a_col[L - 1:L, :] - m_last)
            w_s = jnp.exp(bc[L - 1:L, :] - bc + igc - m_last)
            vw = (w_s * v_aug.astype(F32)).astype(BF16)
            c_ref[h] = w_prev * cmat + _dot_tn(k, vw)
            m_news.append(m_last)
            mu = jnp.mean(hval, axis=-1, keepdims=True)
            var = jnp.mean(jnp.square(hval - mu), axis=-1, keepdims=True)
            hn = (hval - mu) * lax.rsqrt(var + EPS) * ng_ref[:, hs]
            og = jax.nn.sigmoid(o_ref[pl.ds(t0, L), hs].astype(F32))
            out_ref[pl.ds(t0, L), hs] = (og * hn).astype(BF16)
        return tuple(m_news)

    lax.fori_loop(0, S // L, chunk_body, tuple(jnp.zeros((1, 1), F32) for _ in range(ML_H)))


def _mlstm(zm, zs, conv_w, conv_b, gate_b, norm_g, S):
    M = zm.shape[0]
    gb_row = jnp.zeros((1, ZS_W), F32).at[0, ZS_MI:ZS_MI + 2 * ML_H].set(gate_b)

    def col(off):
        return pl.BlockSpec((S, ML_W), lambda b: (b, off // ML_W))

    return pl.pallas_call(
        _mlstm_kernel,
        grid=(M // S,),
        in_specs=[col(ZM_MQ), col(ZM_MK), col(ZM_MV), col(ZM_MO),
                  pl.BlockSpec((S, ZS_W), lambda b: (b, 0)),
                  pl.BlockSpec((CONV_K, 2 * ML_W), lambda b: (0, 0)),
                  pl.BlockSpec((1, 2 * ML_W), lambda b: (0, 0)),
                  pl.BlockSpec((1, ZS_W), lambda b: (0, 0)),
                  pl.BlockSpec((1, ML_W), lambda b: (0, 0))],
        out_specs=pl.BlockSpec((S, ML_W), lambda b: (b, 0)),
        out_shape=jax.ShapeDtypeStruct((M, ML_W), BF16),
        scratch_shapes=[pltpu.VMEM((S, ML_W), BF16), pltpu.VMEM((S, ML_W), BF16),
                        pltpu.VMEM((ML_H, ML_DH, 2 * ML_DH), F32)],
        compiler_params=_cparams(("arbitrary",)),
        name="mlstm_mixer",
    )(zm, zm, zm, zm, zs, conv_w, conv_b.reshape(1, 2 * ML_W), gb_row, norm_g.reshape(1, ML_W))


def _nsa_prep_kernel(x_ref, pe_ref, w1_ref, w2_ref, kk_ref, vv_ref, feat_ref, featc_ref,
                     kaug_ref, vt_ref, kcaug_ref, vct_ref):
    n = x_ref.shape[0]
    S = kk_ref.shape[0]
    cw = 2 * NSA_KV
    half = CMP_BLOCK // 2
    acc_a = jnp.zeros((n, cw), F32)
    acc_b = jnp.zeros((n, cw), F32)
    for j in range(half):
        xj = x_ref[:, j * cw:(j + 1) * cw]
        acc_a = acc_a + _dot((xj + pe_ref[j:j + 1, :]).astype(BF16), w1_ref[j])
        acc_b = acc_b + _dot((xj + pe_ref[half + j:half + j + 1, :]).astype(BF16), w1_ref[half + j])
    pre = acc_a + pltpu.roll(acc_b, n - 1, axis=0)
    cmp = _dot(jax.nn.gelu(pre).astype(BF16), w2_ref[...])

    lane = lax.broadcasted_iota(jnp.int32, (1, LANES), 1)
    keep = [jnp.where(lane < NSA_DH, 1.0, 0.0).astype(BF16), jnp.where(lane >= NSA_DH, 1.0, 0.0).astype(BF16)]
    for g in range(NSA_G):
        kd = cmp[:, g * LANES:(g + 1) * LANES].astype(BF16)
        for e in range(2):
            kcaug_ref[0, g * 2 + e] = kd * keep[e] + featc_ref[e]
    vct_ref[0] = cmp[:, NSA_G * LANES:(NSA_G + 1) * LANES].T.astype(BF16)
    tb = min(TK, S)
    for br in range(2):
        for g in range(NSA_G):
            kd = kk_ref[:, (br * NSA_G + g) * LANES:(br * NSA_G + g + 1) * LANES]
            for e in range(2):
                kaug_ref[0, (br * NSA_G + g) * 2 + e] = kd * keep[e] + feat_ref[br * 2 + e]
        for c in range(S // tb):
            v = vv_ref[c * tb:(c + 1) * tb, br * LANES:(br + 1) * LANES].astype(F32)
            vt_ref[0, br, :, c * tb:(c + 1) * tb] = v.T.astype(BF16)


def _blockdiag(blocks):
    n = len(blocks)
    rows = []
    for i, blk in enumerate(blocks):
        rows.append(jnp.concatenate([blk if j == i else jnp.zeros((blk.shape[0], blocks[j].shape[1]), blk.dtype)
                                     for j in range(n)], axis=1))
    return jnp.concatenate(rows, axis=0)


def _key_features(S):
    pos = np.arange(S)
    f = np.zeros((4, S, LANES), np.float32)
    for e in range(2):
        base = NSA_DH * (1 - e)
        f[e, pos, base + FEAT_SEL + pos // SEL_BLOCK] = 1.0
        for br in range(2):
            f[br * 2 + e, :, base + FEAT_HI] = pos // 64
            f[br * 2 + e, :, base + FEAT_LO] = pos % 64
    return jnp.asarray(f, dtype=BF16)


def _cmp_features(S):
    c = np.arange(S // CMP_STRIDE)
    hi = (c * CMP_STRIDE) // 64
    lo = c * CMP_STRIDE + (CMP_BLOCK - 1) * 0.5 - 64 * hi
    f = np.zeros((2, c.size, LANES), np.float32)
    for e in range(2):
        base = NSA_DH * (1 - e)
        f[e, :, base + FEAT_HI] = hi
        f[e, :, base + FEAT_LO] = lo
    return jnp.asarray(f, dtype=BF16)


def _nsa_prep(zc, zm, pe_k, pe_v, phi_k1, phi_k2, phi_v1, phi_v2, S):
    M = zc.shape[0]
    B = M // S
    n = S // CMP_STRIDE
    cw = 2 * NSA_KV
    assert S // SEL_BLOCK <= FEAT_HI and S <= 64 * 64
    x = zc.reshape(B * n, CMP_STRIDE * cw)
    pe = jnp.concatenate([pe_k, pe_k, pe_v, pe_v], axis=1)
    k1 = phi_k1.reshape(CMP_BLOCK, NSA_DH, NSA_DH)
    v1 = phi_v1.reshape(CMP_BLOCK, NSA_DH, NSA_DH)
    w1 = jnp.stack([_blockdiag([k1[j], k1[j], v1[j], v1[j]]) for j in range(CMP_BLOCK)]).astype(BF16)
    k2 = jnp.concatenate([phi_k2, phi_k2], axis=1)
    w2 = _blockdiag([k2, k2, phi_v2, phi_v2]).astype(BF16)

    def full(shape):
        return pl.BlockSpec(shape, lambda b: (0,) * len(shape))

    return pl.pallas_call(
        _nsa_prep_kernel,
        grid=(B,),
        in_specs=[pl.BlockSpec((n, CMP_STRIDE * cw), lambda b: (b, 0)),
                  full((CMP_BLOCK, cw)), full((CMP_BLOCK, cw, cw)), full((cw, 3 * LANES)),
                  pl.BlockSpec((S, KK_W), lambda b: (b, ZM_KK // KK_W)),
                  pl.BlockSpec((S, VV_W), lambda b: (b, ZM_VV // VV_W)),
                  full((4, S, LANES)), full((2, n, LANES))],
        out_specs=[pl.BlockSpec((1, 8, S, LANES), lambda b: (b, 0, 0, 0)),
                   pl.BlockSpec((1, 2, LANES, S), lambda b: (b, 0, 0, 0)),
                   pl.BlockSpec((1, 4, n, LANES), lambda b: (b, 0, 0, 0)),
                   pl.BlockSpec((1, LANES, n), lambda b: (b, 0, 0))],
        out_shape=[jax.ShapeDtypeStruct((B, 8, S, LANES), BF16),
                   jax.ShapeDtypeStruct((B, 2, LANES, S), BF16),
                   jax.ShapeDtypeStruct((B, 4, n, LANES), BF16),
                   jax.ShapeDtypeStruct((B, LANES, n), BF16)],
        compiler_params=_cparams(("arbitrary",)),
        name="nsa_prep",
    )(x, pe, w1, w2, zm, zm, _key_features(S), _cmp_features(S))


def _nsa_kernel(q_ref, kaug_ref, vt_ref, kcaug_ref, vct_ref, zs_ref, ovt_ref, out_ref,
                qa_ref, ocmp_ref, m_ref, l_ref, acc_ref, bias_ref):
    S = kaug_ref.shape[2]
    tq = q_ref.shape[0]
    ncmp = kcaug_ref.shape[2]
    nsel = S // SEL_BLOCK
    qi = pl.program_id(1)
    t0 = pl.multiple_of(qi * tq, tq)
    t_row = t0 + lax.broadcasted_iota(jnp.int32, (1, tq), 1)
    lane = lax.broadcasted_iota(jnp.int32, (1, LANES), 1)
    half_f = [jnp.where(lane < NSA_DH, 1.0, 0.0), jnp.where(lane >= NSA_DH, 1.0, 0.0)]

    def make_query(h, block_penalty):
        e = h % 2
        base = NSA_DH * (1 - e)
        slope = 2.0 ** (-8.0 * (h + 1.0) / NSA_H)
        pair = q_ref[:, (h // 2) * LANES:(h // 2 + 1) * LANES].astype(F32)
        qa = pair * (half_f[e] * (NSA_DH ** -0.5))
        qa = qa + jnp.where(lane == base + FEAT_HI, slope * 64.0, jnp.where(lane == base + FEAT_LO, slope, 0.0))
        if block_penalty is not None:
            qa = qa + block_penalty * half_f[1 - e]
        qa_ref[h] = qa.astype(BF16)

    for h in range(NSA_H):
        make_query(h, None)

    ki = lax.broadcasted_iota(jnp.int32, (TK, tq), 0)
    qu = lax.broadcasted_iota(jnp.int32, (TK, tq), 1)
    bias_ref[0] = jnp.where(ki <= qu, 0.0, NEG)
    bias_ref[1] = jnp.where(ki > qu, 0.0, NEG)

    c_col = lax.broadcasted_iota(jnp.int32, (ncmp, 1), 0)
    valid_c = c_col * CMP_STRIDE + (CMP_BLOCK - 1) <= t_row
    j_col = lax.broadcasted_iota(jnp.int32, (nsel, 1), 0)
    jt = t_row >> 6
    forced = jnp.logical_or(j_col == 0, jnp.logical_or(j_col == jt, j_col == jt - 1))
    future = j_col > jt

    for g in range(NSA_G):
        psum = jnp.zeros((ncmp, tq), F32)
        for r in range(NSA_R):
            h = g * NSA_R + r
            sc = jnp.where(valid_c, _dot_nt(kcaug_ref[0, g * 2 + h % 2], qa_ref[h]), NEG)
            e = jnp.where(valid_c, jnp.exp(sc - jnp.max(sc, axis=0, keepdims=True)), 0.0)
            den = jnp.sum(e, axis=0, keepdims=True)
            p = e * (1.0 / jnp.where(den > 0.0, den, 1.0))
            psum = psum + p
            ocmp_ref[h] = _dot(vct_ref[0, g * NSA_DH:(g + 1) * NSA_DH, :], p.astype(BF16))

        p_hi = psum.astype(BF16)
        p_lo = (psum - p_hi.astype(F32)).astype(BF16)
        imp = _dot(ovt_ref[...], p_hi) + _dot(ovt_ref[...], p_lo)
        val = jnp.where(future, NEG, jnp.where(forced, BIG, imp))
        rank = jnp.zeros((nsel, tq), F32)
        for i in range(nsel):
            vi = val[i:i + 1, :]
            tie = jnp.where(vi == val, 1.0, 0.0) * jnp.where(j_col > i, 1.0, 0.0)
            rank = rank + jnp.where(vi > val, 1.0, 0.0) + tie
        penalty = jnp.where(rank < float(min(TOP_N, nsel)), 0.0, NEG)
        pad = jnp.zeros((NSA_DH - nsel, tq), F32)
        penalty_t = jnp.concatenate([penalty, pad, penalty, pad], axis=0).T
        for r in range(NSA_R):
            make_query(g * NSA_R + r, penalty_t)

    m_ref[...] = jnp.full(m_ref.shape, NEG, F32)
    l_ref[...] = jnp.zeros(l_ref.shape, F32)
    acc_ref[...] = jnp.zeros(acc_ref.shape, F32)

    def attend(br, k0, bias):
        def scores(h):
            return _dot_nt(kaug_ref[0, (br * NSA_G + h // NSA_R) * 2 + h % 2, pl.ds(k0, TK), :], qa_ref[h])

        s_next = scores(0)
        for h in range(NSA_H):
            s_t = s_next
            if h + 1 < NSA_H:
                s_next = scores(h + 1)
            g = h // NSA_R
            slot = br * NSA_H + h
            if bias is not None:
                s_t = s_t + bias_ref[bias]
            m_old = m_ref[slot]
            m_new = jnp.maximum(m_old, jnp.max(s_t, axis=0, keepdims=True))
            alpha = jnp.exp(m_old - m_new)
            p = jnp.exp(s_t - m_new)
            l_ref[slot] = alpha * l_ref[slot] + jnp.sum(p, axis=0, keepdims=True)
            v_t = vt_ref[0, br, g * NSA_DH:(g + 1) * NSA_DH, pl.ds(k0, TK)]
            acc_ref[slot] = alpha * acc_ref[slot] + _dot(v_t, p.astype(BF16))
            m_ref[slot] = m_new

    attend(BR_WIN, t0, 0)
    n_back = WINDOW // TK
    for d in range(1, n_back + 1):
        @pl.when(qi >= d)
        def _(d=d):
            attend(BR_WIN, pl.multiple_of(t0 - d * TK, TK), 1 if d == n_back else None)

    def sel_body(kb, carry):
        attend(BR_SEL, pl.multiple_of(kb * TK, TK), None)
        return carry

    lax.fori_loop(0, qi, sel_body, 0)
    attend(BR_SEL, t0, 0)

    gates_t = jax.nn.sigmoid(zs_ref[...]).T
    for hp in range(NSA_H // 2):
        comb = []
        for e in range(2):
            h = 2 * hp + e
            c0 = ZS_GATE + NSA_NB * h
            o_sel = acc_ref[BR_SEL * NSA_H + h] * (1.0 / l_ref[BR_SEL * NSA_H + h])
            o_win = acc_ref[BR_WIN * NSA_H + h] * (1.0 / l_ref[BR_WIN * NSA_H + h])
            comb.append(gates_t[c0:c0 + 1] * ocmp_ref[h] + gates_t[c0 + 1:c0 + 2] * o_sel
                        + gates_t[c0 + 2:c0 + 3] * o_win)
        out_ref[:, hp * LANES:(hp + 1) * LANES] = jnp.concatenate(comb, axis=0).T.astype(BF16)


def _overlap_t(S):
    c = np.arange(S // CMP_STRIDE)[None, :]
    j = np.arange(S // SEL_BLOCK)[:, None]
    ov = (c * CMP_STRIDE <= j * SEL_BLOCK + SEL_BLOCK - 1) & (c * CMP_STRIDE + CMP_BLOCK - 1 >= j * SEL_BLOCK)
    ov &= c < (S - CMP_BLOCK) // CMP_STRIDE + 1
    return jnp.asarray(ov.astype(np.float32), dtype=BF16)


def _nsa(zm, zs, kaug, vt, kcaug, vct, S):
    M = zm.shape[0]
    B = M // S
    tq = min(TQ, S)
    assert tq == TK and WINDOW % TK == 0 and S % TK == 0
    nq = S // tq
    n = S // CMP_STRIDE
    nsel = S // SEL_BLOCK
    return pl.pallas_call(
        _nsa_kernel,
        grid=(B, nq),
        in_specs=[pl.BlockSpec((tq, NSA_W), lambda b, i: (b * nq + i, ZM_NQ // NSA_W)),
                  pl.BlockSpec((1, 8, S, LANES), lambda b, i: (b, 0, 0, 0)),
                  pl.BlockSpec((1, 2, LANES, S), lambda b, i: (b, 0, 0, 0)),
                  pl.BlockSpec((1, 4, n, LANES), lambda b, i: (b, 0, 0, 0)),
                  pl.BlockSpec((1, LANES, n), lambda b, i: (b, 0, 0)),
                  pl.BlockSpec((tq, ZS_W), lambda b, i: (b * nq + i, 0)),
                  pl.BlockSpec((nsel, n), lambda b, i: (0, 0))],
        out_specs=pl.BlockSpec((tq, NSA_W), lambda b, i: (b * nq + i, 0)),
        out_shape=jax.ShapeDtypeStruct((M, NSA_W), BF16),
        scratch_shapes=[pltpu.VMEM((NSA_H, tq, LANES), BF16),
                        pltpu.VMEM((NSA_H, NSA_DH, tq), F32),
                        pltpu.VMEM((2 * NSA_H, 1, tq), F32),
                        pltpu.VMEM((2 * NSA_H, 1, tq), F32),
                        pltpu.VMEM((2 * NSA_H, NSA_DH, tq), F32),
                        pltpu.VMEM((2, TK, tq), F32)],
        compiler_params=_cparams(("arbitrary", "arbitrary")),
        name="nsa_attention",
    )(zm, kaug, vt, kcaug, vct, zs, _overlap_t(S))


def _merge_kernel(a_ref, b_ref, c_ref, ga_ref, gb_ref, gc_ref, x_ref, gt_ref, wa_ref, wb_ref, wc_ref, wo_ref,
                  o_ref):
    merged = (jax.nn.sigmoid(ga_ref[...].astype(F32)) * _dot(a_ref[...], wa_ref[...])
              + jax.nn.sigmoid(gb_ref[...].astype(F32)) * _dot(b_ref[...], wb_ref[...])
              + jax.nn.sigmoid(gc_ref[...].astype(F32)) * _dot(c_ref[...], wc_ref[...]))
    o_ref[...] = x_ref[...] + gt_ref[0] * _dot(merged.astype(BF16), wo_ref[...])


def _merge(ya, yb, yc, zm, x2, gt, wa, wb, wc, wo, S):
    M, D = x2.shape
    tm = min(TM_MERGE, S)
    per_b = S // tm

    def rows(w, jcol=0):
        return pl.BlockSpec((tm, w), lambda i: (i, jcol))

    def full(shape):
        return pl.BlockSpec(shape, lambda i: (0,) * len(shape))

    return pl.pallas_call(
        _merge_kernel,
        grid=(M // tm,),
        in_specs=[rows(GM_W), rows(ML_W), rows(NSA_W),
                  rows(D, ZM_GA // D), rows(D, ZM_GBR // D), rows(D, ZM_GC // D),
                  rows(D), pl.BlockSpec((1, 1, D), lambda i: (i // per_b, 0, 0)),
                  full((GM_W, D)), full((ML_W, D)), full((NSA_W, D)), full((D, D))],
        out_specs=rows(D),
        out_shape=jax.ShapeDtypeStruct((M, D), F32),
        compiler_params=_cparams(("arbitrary",)),
        name="merge_out",
    )(ya, yb, yc, zm, zm, zm, x2, gt, wa, wb, wc, wo)


def _mlp_kernel(x_ref, g_ref, sc_ref, sh_ref, gt_ref, w1_ref, w2_ref, gf_ref, o_ref, acc_ref, *, final_norm):
    x = x_ref[...]
    h = _modulated_norm(x, g_ref[...], sc_ref[0], sh_ref[0]).astype(BF16)
    for c in range(D_FF // FF_CHUNK):
        mid = jnp.square(jnp.maximum(_dot(h, w1_ref[:, c * FF_CHUNK:(c + 1) * FF_CHUNK]), 0.0)).astype(BF16)
        upd = _dot(mid, w2_ref[c * FF_CHUNK:(c + 1) * FF_CHUNK, :])
        if c == 0:
            acc_ref[...] = upd
        else:
            acc_ref[...] += upd
    y = x + gt_ref[0] * acc_ref[...]
    if final_norm:
        y = (y * lax.rsqrt(jnp.mean(y * y, axis=-1, keepdims=True) + EPS)) * gf_ref[...]
    o_ref[...] = y


def _mlp(x2, g, sc, sh, gt, w1, w2, g_final, S, final_norm):
    M, D = x2.shape
    tm = min(TM_MLP, S)
    per_b = S // tm
    mod = pl.BlockSpec((1, 1, D), lambda i: (i // per_b, 0, 0))
    return pl.pallas_call(
        functools.partial(_mlp_kernel, final_norm=final_norm),
        grid=(M // tm,),
        in_specs=[pl.BlockSpec((tm, D), lambda i: (i, 0)),
                  pl.BlockSpec((1, D), lambda i: (0, 0)),
                  mod, mod, mod,
                  pl.BlockSpec((D, D_FF), lambda i: (0, 0)),
                  pl.BlockSpec((D_FF, D), lambda i: (0, 0)),
                  pl.BlockSpec((1, D), lambda i: (0, 0))],
        out_specs=pl.BlockSpec((tm, D), lambda i: (i, 0)),
        out_shape=jax.ShapeDtypeStruct((M, D), F32),
        scratch_shapes=[pltpu.VMEM((tm, D), F32)],
        compiler_params=_cparams(("arbitrary",)),
        name="relu2_mlp",
    )(x2, g, sc, sh, gt, w1, w2, g_final)


def _w_cols(w, name):
    return w[:, _OFFS[name]:_OFFS[name] + _SIZE[name]]


def _dup_heads(w):
    parts = []
    for g in range(NSA_G):
        blk = w[:, g * NSA_DH:(g + 1) * NSA_DH]
        parts += [blk, blk]
    return jnp.concatenate(parts, axis=1)


def _proj_weights(w):
    D = w.shape[0]
    wm = jnp.concatenate([_w_cols(w, n) for n in ('ga', 'gbr', 'gc', 'gu', 'gv', 'mq', 'mk', 'mv', 'mo', 'nq')]
                         + [_dup_heads(_w_cols(w, 'nks')), _dup_heads(_w_cols(w, 'nkw')),
                            _w_cols(w, 'nvs'), _w_cols(w, 'nvw')], axis=1)
    wc = jnp.concatenate([_w_cols(w, 'nkc'), _w_cols(w, 'nvc')], axis=1)
    ws = jnp.concatenate([_w_cols(w, 'ngate'), _w_cols(w, 'mi'), _w_cols(w, 'mf'),
                          jnp.zeros((D, ZS_W - NSA_H * NSA_NB - 2 * ML_H), w.dtype)], axis=1)
    return wm.astype(BF16), wc.astype(BF16), ws.astype(BF16)


def kernel(x, c, g_norm1, g_norm2, w_ada, b_ada, w_in, gm_ln_g, gm_ln_b, gm_ws, gm_bs, ml_conv_w, ml_conv_b,
           ml_gate_b, ml_norm_g, nsa_pe_k, nsa_pe_v, nsa_phi_k1, nsa_phi_k2, nsa_phi_v1, nsa_phi_v2,
           w_up_a, w_up_b, w_up_c, w_out, w_mlp1, w_mlp2, g_final):
    B, S, D = x.shape
    depth = w_in.shape[0]
    M = B * S
    mod = _ada(c, w_ada, b_ada)
    x2 = x.reshape(M, D)
    for l in range(depth):
        sh1, sc1, gt1, sh2, sc2, gt2 = [mod[l, :, i * D:(i + 1) * D].reshape(B, 1, D) for i in range(6)]
        wm, wc, ws = _proj_weights(w_in[l])
        zm, zc, zs = _inproj(x2, g_norm1[l].reshape(1, D), sc1, sh1, wm, wc, ws, S)
        ya = _gmlp(zm, gm_ln_g[l], gm_ln_b[l], gm_ws[l], gm_bs[l], S)
        yb = _mlstm(zm, zs, ml_conv_w[l], ml_conv_b[l], ml_gate_b[l], ml_norm_g[l], S)
        kaug, vt, kcaug, vct = _nsa_prep(zc, zm, nsa_pe_k[l], nsa_pe_v[l], nsa_phi_k1[l], nsa_phi_k2[l],
                                         nsa_phi_v1[l], nsa_phi_v2[l], S)
        yc = _nsa(zm, zs, kaug, vt, kcaug, vct, S)
        x2 = _merge(ya, yb, yc, zm, x2, gt1, w_up_a[l].astype(BF16), w_up_b[l].astype(BF16),
                    w_up_c[l].astype(BF16), w_out[l].astype(BF16), S)
        x2 = _mlp(x2, g_norm2[l].reshape(1, D), sc2, sh2, gt2, w_mlp1[l].astype(BF16), w_mlp2[l].astype(BF16),
                  g_final.reshape(1, D), S, final_norm=(l == depth - 1))
    return x2.reshape(B, S, D)
```

```python
import functools

import numpy as np
import jax
import jax.numpy as jnp
from jax import lax
from jax.experimental import pallas as pl
from jax.experimental.pallas import tpu as pltpu

F32 = jnp.float32
BF16 = jnp.bfloat16

D_MODEL = 1024
GM_W = 512
GM_GROUPS = 4
GM_CHUNK = 128
ML_H = 4
ML_DH = 128
ML_W = ML_H * ML_DH
CONV_K = 4
NSA_H = 8
NSA_G = 2
NSA_R = NSA_H // NSA_G
NSA_DH = 64
NSA_W = NSA_H * NSA_DH
NSA_KV = NSA_G * NSA_DH
NSA_NB = 3
CMP_BLOCK = 32
CMP_STRIDE = 16
SEL_BLOCK = 64
TOP_N = 8
WINDOW = 512
D_FF = 4 * D_MODEL
EPS = 1e-6
NEG = -1e30
BIG = 1e4
SPLIT_SIZES = (GM_W, GM_W, ML_W, ML_W, ML_W, ML_W, ML_H, ML_H, NSA_W, NSA_KV, NSA_KV, NSA_KV, NSA_KV,
               NSA_KV, NSA_KV, NSA_H * NSA_NB, D_MODEL, D_MODEL, D_MODEL)
SPLIT_NAMES = ('gu', 'gv', 'mq', 'mk', 'mv', 'mo', 'mi', 'mf', 'nq', 'nkc', 'nvc', 'nks', 'nvs', 'nkw', 'nvw',
               'ngate', 'ga', 'gbr', 'gc')
_OFFS = dict(zip(SPLIT_NAMES, np.concatenate([[0], np.cumsum(SPLIT_SIZES)[:-1]]).tolist()))
_SIZE = dict(zip(SPLIT_NAMES, SPLIT_SIZES))

LANES = 128
VMEM_LIMIT = 56 * 1024 * 1024

ZM_GA, ZM_GBR, ZM_GC = 0, 1024, 2048
ZM_GU, ZM_GV = 3072, 3584
ZM_MQ, ZM_MK, ZM_MV, ZM_MO = 4096, 4608, 5120, 5632
ZM_NQ = 6144
ZM_KK = 6656
ZM_VV = 7168
ZM_W = 7424
KK_W = 4 * LANES
VV_W = 2 * LANES
BR_SEL, BR_WIN = 0, 1
MASK_CAUSAL, MASK_WINDOW_EDGE = 0, 1
FEAT_SEL = 0
FEAT_HI = 32
FEAT_LO = 33
ZS_GATE = 0
ZS_MI = 24
ZS_MF = 28
ZS_W = 128

TM_PROJ = 1024
TN_PROJ = 3712
TS_GMLP = 512
ML_CHUNK = 128
CONV_BLK = 256
CONV_HALO = 16
TQ = 256
TK = 256
TM_MERGE = 512
TM_MLP = 512
FF_CHUNK = 1024


def _dot(a, b):
    return jnp.dot(a, b, preferred_element_type=F32)


def _dot_nt(a, b):
    return lax.dot_general(a, b, (((1,), (1,)), ((), ())), preferred_element_type=F32)


def _dot_tn(a, b):
    return lax.dot_general(a, b, (((0,), (0,)), ((), ())), preferred_element_type=F32)


def _split3(x):
    x1 = x.astype(BF16)
    r1 = x - x1.astype(F32)
    x2 = r1.astype(BF16)
    x3 = (r1 - x2.astype(F32)).astype(BF16)
    return x1, x2, x3


def _cparams(sem):
    return pltpu.CompilerParams(dimension_semantics=sem, vmem_limit_bytes=VMEM_LIMIT)


def _ada_kernel(c_ref, w_ref, b_ref, o_ref):
    c = c_ref[...]
    cond = c * jax.nn.sigmoid(c)
    c1, c2, c3 = _split3(cond)
    w1, w2, w3 = _split3(w_ref[0])
    acc = _dot(c1, w1) + (_dot(c1, w2) + _dot(c2, w1)) + (_dot(c1, w3) + _dot(c2, w2) + _dot(c3, w1))
    o_ref[0] = acc + b_ref[0]


def _ada(c, w_ada, b_ada):
    L, D, N = w_ada.shape
    B = c.shape[0]
    tn = 1536
    return pl.pallas_call(
        _ada_kernel,
        grid=(L, N // tn),
        in_specs=[pl.BlockSpec((B, D), lambda l, j: (0, 0)),
                  pl.BlockSpec((1, D, tn), lambda l, j: (l, 0, j)),
                  pl.BlockSpec((1, 1, tn), lambda l, j: (l, 0, j))],
        out_specs=pl.BlockSpec((1, B, tn), lambda l, j: (l, 0, j)),
        out_shape=jax.ShapeDtypeStruct((L, B, N), F32),
        compiler_params=_cparams(("arbitrary", "arbitrary")),
        name="ada_mod",
    )(c, w_ada, b_ada.reshape(L, 1, N))


def _modulated_norm(x, g, sc, sh):
    y = x * lax.rsqrt(jnp.mean(x * x, axis=-1, keepdims=True) + EPS)
    return (y * g) * (1.0 + sc) + sh


def _inproj_kernel(x_ref, g_ref, sc_ref, sh_ref, wm_ref, wc_ref, ws_ref, zm_ref, zc_ref, zs_ref, h_ref):
    @pl.when(pl.program_id(1) == 0)
    def _():
        h = _modulated_norm(x_ref[...], g_ref[...], sc_ref[0], sh_ref[0]).astype(BF16)
        h_ref[...] = h
        zc_ref[...] = _dot(h, wc_ref[...])
        zs_ref[...] = _dot(h, ws_ref[...])

    zm_ref[...] = _dot(h_ref[...], wm_ref[...]).astype(BF16)


def _inproj(x2, g, sc, sh, wm, wc, ws, S):
    M, D = x2.shape
    tm, tn = min(TM_PROJ, S), TN_PROJ
    per_b = S // tm
    return pl.pallas_call(
        _inproj_kernel,
        grid=(M // tm, ZM_W // tn),
        in_specs=[pl.BlockSpec((tm, D), lambda i, j: (i, 0)),
                  pl.BlockSpec((1, D), lambda i, j: (0, 0)),
                  pl.BlockSpec((1, 1, D), lambda i, j: (i // per_b, 0, 0)),
                  pl.BlockSpec((1, 1, D), lambda i, j: (i // per_b, 0, 0)),
                  pl.BlockSpec((D, tn), lambda i, j: (0, j)),
                  pl.BlockSpec((D, 2 * NSA_KV), lambda i, j: (0, 0)),
                  pl.BlockSpec((D, ZS_W), lambda i, j: (0, 0))],
        out_specs=[pl.BlockSpec((tm, tn), lambda i, j: (i, j)),
                   pl.BlockSpec((tm, 2 * NSA_KV), lambda i, j: (i, 0)),
                   pl.BlockSpec((tm, ZS_W), lambda i, j: (i, 0))],
        out_shape=[jax.ShapeDtypeStruct((M, ZM_W), BF16),
                   jax.ShapeDtypeStruct((M, 2 * NSA_KV), F32),
                   jax.ShapeDtypeStruct((M, ZS_W), F32)],
        scratch_shapes=[pltpu.VMEM((tm, D), BF16)],
        compiler_params=_cparams(("arbitrary", "arbitrary")),
        name="in_proj",
    )(x2, g, sc, sh, wm, wc, ws)


def _gmlp_kernel(u_ref, v_ref, lng_ref, lnb_ref, ws_ref, bst_ref, o_ref):
    ts = u_ref.shape[0]
    dg = GM_W // GM_GROUPS
    row = lax.broadcasted_iota(jnp.int32, (GM_CHUNK, GM_CHUNK), 0)
    col = lax.broadcasted_iota(jnp.int32, (GM_CHUNK, GM_CHUNK), 1)
    ws = [jnp.where(row >= col, ws_ref[g], 0.0).astype(BF16) for g in range(GM_GROUPS)]
    lng = lng_ref[...]
    lnb = lnb_ref[...]
    for c in range(ts // GM_CHUNK):
        r0 = c * GM_CHUNK
        u = jax.nn.gelu(u_ref[r0:r0 + GM_CHUNK, :].astype(F32))
        v = jax.nn.gelu(v_ref[r0:r0 + GM_CHUNK, :].astype(F32))
        mu = jnp.mean(v, axis=-1, keepdims=True)
        var = jnp.mean(jnp.square(v - mu), axis=-1, keepdims=True)
        vb = ((v - mu) * lax.rsqrt(var + EPS) * lng + lnb).astype(BF16)
        for g in range(GM_GROUPS):
            mixed = _dot(ws[g], vb[:, g * dg:(g + 1) * dg]) + bst_ref[:, g:g + 1]
            o_ref[r0:r0 + GM_CHUNK, g * dg:(g + 1) * dg] = (u[:, g * dg:(g + 1) * dg] * mixed).astype(BF16)


def _gmlp(zm, ln_g, ln_b, ws, bs, S):
    M = zm.shape[0]
    ts = min(TS_GMLP, S)
    return pl.pallas_call(
        _gmlp_kernel,
        grid=(M // ts,),
        in_specs=[pl.BlockSpec((ts, GM_W), lambda i: (i, ZM_GU // GM_W)),
                  pl.BlockSpec((ts, GM_W), lambda i: (i, ZM_GV // GM_W)),
                  pl.BlockSpec((1, GM_W), lambda i: (0, 0)),
                  pl.BlockSpec((1, GM_W), lambda i: (0, 0)),
                  pl.BlockSpec((GM_GROUPS, GM_CHUNK, GM_CHUNK), lambda i: (0, 0, 0)),
                  pl.BlockSpec((GM_CHUNK, GM_GROUPS), lambda i: (0, 0))],
        out_specs=pl.BlockSpec((ts, GM_W), lambda i: (i, 0)),
        out_shape=jax.ShapeDtypeStruct((M, GM_W), BF16),
        compiler_params=_cparams(("arbitrary",)),
        name="gmlp_mixer",
    )(zm, zm, ln_g.reshape(1, GM_W), ln_b.reshape(1, GM_W), ws, bs.T)


def _log_sigmoid(x):
    return jnp.minimum(x, 0.0) - jnp.log1p(jnp.exp(-jnp.abs(x)))


def _conv_silu(x_ext, w, b):
    n = x_ext.shape[0] - CONV_HALO
    y = b
    for j in range(CONV_K):
        sh = CONV_K - 1 - j
        xs = x_ext if sh == 0 else pltpu.roll(x_ext, sh, axis=0)
        y = y + xs[CONV_HALO:CONV_HALO + n] * w[j:j + 1]
    return y * jax.nn.sigmoid(y)


def _mlstm_kernel(q_ref, k_ref, v_ref, o_ref, zs_ref, cw_ref, cb_ref, gb_ref, ng_ref, out_ref,
                  qc_ref, kc_ref, c_ref):
    S = q_ref.shape[0]
    L = min(ML_CHUNK, S)
    blk = min(CONV_BLK, S)

    for src, dst, c0, scale in ((q_ref, qc_ref, 0, None), (k_ref, kc_ref, ML_W, ML_DH ** -0.5)):
        w = cw_ref[:, c0:c0 + ML_W]
        b = cb_ref[:, c0:c0 + ML_W]

        def conv_block(x_ext, r0, dst=dst, w=w, b=b, scale=scale):
            y = _conv_silu(x_ext, w, b)
            if scale is not None:
                y = y * scale
            dst[pl.ds(r0, blk), :] = y.astype(BF16)

        first = jnp.concatenate([jnp.zeros((CONV_HALO, ML_W), F32), src[0:blk, :].astype(F32)], axis=0)
        conv_block(first, 0)

        def conv_body(i, carry, src=src, conv_block=conv_block):
            r0 = pl.multiple_of(i * blk, blk)
            conv_block(src[pl.ds(r0 - CONV_HALO, blk + CONV_HALO), :].astype(F32), r0)
            return carry

        lax.fori_loop(1, S // blk, conv_body, 0)

    c_ref[...] = jnp.zeros_like(c_ref)
    row = lax.broadcasted_iota(jnp.int32, (L, L), 0)
    col = lax.broadcasted_iota(jnp.int32, (L, L), 1)
    tril = row >= col
    tril_b = jnp.where(tril, 1.0, 0.0).astype(BF16)
    triu_b = jnp.where(row <= col, 1.0, 0.0).astype(BF16)
    lane = lax.broadcasted_iota(jnp.int32, (L, ML_DH), 1)
    ones_col = jnp.where(lane == 0, 1.0, 0.0).astype(BF16)
    gb = gb_ref[...]

    def chunk_body(c, m_prevs):
        t0 = pl.multiple_of(c * L, L)
        gi = zs_ref[pl.ds(t0, L), :] + gb
        lf_cols = _log_sigmoid(gi)
        gi_t = gi.T
        ig_rows = gi_t[ZS_MI:ZS_MI + ML_H]
        lf_rows = _log_sigmoid(gi_t[ZS_MI:ZS_MI + 2 * ML_H])
        c1, c2, c3 = _split3(lf_cols)
        b_cols = _dot(tril_b, c1) + _dot(tril_b, c2) + _dot(tril_b, c3)
        r1, r2, r3 = _split3(lf_rows)
        b_rows = _dot(r1, triu_b) + _dot(r2, triu_b) + _dot(r3, triu_b)
        m_news = []
        for h in range(ML_H):
            hs = slice(h * ML_DH, (h + 1) * ML_DH)
            bc = b_cols[:, ZS_MF + h:ZS_MF + h + 1]
            igc = gi[:, ZS_MI + h:ZS_MI + h + 1]
            br = b_rows[ML_H + h:ML_H + h + 1, :]
            igr = ig_rows[h:h + 1, :]
            m_prev = m_prevs[h]
            a_col = bc + m_prev
            dmat = jnp.where(tril, bc - br + igr, NEG)
            m_col = jnp.maximum(a_col, jnp.max(dmat, axis=1, keepdims=True))
            w_inter = jnp.exp(a_col - m_col)
            q = qc_ref[pl.ds(t0, L), hs]
            k = kc_ref[pl.ds(t0, L), hs]
            v = v_ref[pl.ds(t0, L), hs]
            v_aug = jnp.concatenate([v, ones_col], axis=1)
            s = _dot_nt(q, k) * jnp.exp(dmat - m_col)
            cmat = c_ref[h]
            num_aug = _dot(s.astype(BF16), v_aug) + w_inter * _dot(q, cmat.astype(BF16))
            num = num_aug[:, :ML_DH]
            den = num_aug[:, ML_DH:ML_DH + 1]
            hval = num / jnp.maximum(jnp.abs(den), jnp.exp(-m_col))
            m_last = m_col[L - 1:L, :]
            w_prev = jnp.exp(a_col[L - 1:L, :] - m_last)
            w_s = jnp.exp(bc[L - 1:L, :] - bc + igc - m_last)
            vw = (w_s * v_aug.astype(F32)).astype(BF16)
            c_ref[h] = w_prev * cmat + _dot_tn(k, vw)
            m_news.append(m_last)
            mu = jnp.mean(hval, axis=-1, keepdims=True)
            var = jnp.mean(jnp.square(hval - mu), axis=-1, keepdims=True)
            hn = (hval - mu) * lax.rsqrt(var + EPS) * ng_ref[:, hs]
            og = jax.nn.sigmoid(o_ref[pl.ds(t0, L), hs].astype(F32))
            out_ref[pl.ds(t0, L), hs] = (og * hn).astype(BF16)
        return tuple(m_news)

    lax.fori_loop(0, S // L, chunk_body, tuple(jnp.zeros((1, 1), F32) for _ in range(ML_H)))


def _mlstm(zm, zs, conv_w, conv_b, gate_b, norm_g, S):
    M = zm.shape[0]
    gb_row = jnp.zeros((1, ZS_W), F32).at[0, ZS_MI:ZS_MI + 2 * ML_H].set(gate_b)

    def col(off):
        return pl.BlockSpec((S, ML_W), lambda b: (b, off // ML_W))

    return pl.pallas_call(
        _mlstm_kernel,
        grid=(M // S,),
        in_specs=[col(ZM_MQ), col(ZM_MK), col(ZM_MV), col(ZM_MO),
                  pl.BlockSpec((S, ZS_W), lambda b: (b, 0)),
                  pl.BlockSpec((CONV_K, 2 * ML_W), lambda b: (0, 0)),
                  pl.BlockSpec((1, 2 * ML_W), lambda b: (0, 0)),
                  pl.BlockSpec((1, ZS_W), lambda b: (0, 0)),
                  pl.BlockSpec((1, ML_W), lambda b: (0, 0))],
        out_specs=pl.BlockSpec((S, ML_W), lambda b: (b, 0)),
        out_shape=jax.ShapeDtypeStruct((M, ML_W), BF16),
        scratch_shapes=[pltpu.VMEM((S, ML_W), BF16), pltpu.VMEM((S, ML_W), BF16),
                        pltpu.VMEM((ML_H, ML_DH, 2 * ML_DH), F32)],
        compiler_params=_cparams(("arbitrary",)),
        name="mlstm_mixer",
    )(zm, zm, zm, zm, zs, conv_w, conv_b.reshape(1, 2 * ML_W), gb_row, norm_g.reshape(1, ML_W))


def _nsa_prep_kernel(x_ref, pe_ref, w1_ref, w2_ref, kk_ref, vv_ref, feat_ref, featc_ref,
                     kaug_ref, vt_ref, kcaug_ref, vct_ref):
    n = x_ref.shape[0]
    S = kk_ref.shape[0]
    cw = 2 * NSA_KV
    half = CMP_BLOCK // 2
    acc_a = jnp.zeros((n, cw), F32)
    acc_b = jnp.zeros((n, cw), F32)
    for j in range(half):
        xj = x_ref[:, j * cw:(j + 1) * cw]
        acc_a = acc_a + _dot((xj + pe_ref[j:j + 1, :]).astype(BF16), w1_ref[j])
        acc_b = acc_b + _dot((xj + pe_ref[half + j:half + j + 1, :]).astype(BF16), w1_ref[half + j])
    pre = acc_a + pltpu.roll(acc_b, n - 1, axis=0)
    cmp = _dot(jax.nn.gelu(pre).astype(BF16), w2_ref[...])

    lane = lax.broadcasted_iota(jnp.int32, (1, LANES), 1)
    keep = [jnp.where(lane < NSA_DH, 1.0, 0.0).astype(BF16), jnp.where(lane >= NSA_DH, 1.0, 0.0).astype(BF16)]
    for g in range(NSA_G):
        kd = cmp[:, g * LANES:(g + 1) * LANES].astype(BF16)
        for e in range(2):
            kcaug_ref[0, g * 2 + e] = kd * keep[e] + featc_ref[e]
    vct_ref[0] = cmp[:, NSA_G * LANES:(NSA_G + 1) * LANES].T.astype(BF16)
    tb = min(TK, S)
    for br in range(2):
        for g in range(NSA_G):
            kd = kk_ref[:, (br * NSA_G + g) * LANES:(br * NSA_G + g + 1) * LANES]
            for e in range(2):
                kaug_ref[0, (br * NSA_G + g) * 2 + e] = kd * keep[e] + feat_ref[br * 2 + e]
        for c in range(S // tb):
            v = vv_ref[c * tb:(c + 1) * tb, br * LANES:(br + 1) * LANES].astype(F32)
            vt_ref[0, br, :, c * tb:(c + 1) * tb] = v.T.astype(BF16)


def _blockdiag(blocks):
    n = len(blocks)
    rows = []
    for i, blk in enumerate(blocks):
        rows.append(jnp.concatenate([blk if j == i else jnp.zeros((blk.shape[0], blocks[j].shape[1]), blk.dtype)
                                     for j in range(n)], axis=1))
    return jnp.concatenate(rows, axis=0)


def _key_features(S):
    pos = np.arange(S)
    f = np.zeros((4, S, LANES), np.float32)
    for e in range(2):
        base = NSA_DH * (1 - e)
        f[e, pos, base + FEAT_SEL + pos // SEL_BLOCK] = 1.0
        for br in range(2):
            f[br * 2 + e, :, base + FEAT_HI] = pos // 64
            f[br * 2 + e, :, base + FEAT_LO] = pos % 64
    return jnp.asarray(f, dtype=BF16)


def _cmp_features(S):
    c = np.arange(S // CMP_STRIDE)
    hi = (c * CMP_STRIDE) // 64
    lo = c * CMP_STRIDE + (CMP_BLOCK - 1) * 0.5 - 64 * hi
    f = np.zeros((2, c.size, LANES), np.float32)
    for e in range(2):
        base = NSA_DH * (1 - e)
        f[e, :, base + FEAT_HI] = hi
        f[e, :, base + FEAT_LO] = lo
    return jnp.asarray(f, dtype=BF16)


def _nsa_prep(zc, zm, pe_k, pe_v, phi_k1, phi_k2, phi_v1, phi_v2, S):
    M = zc.shape[0]
    B = M // S
    n = S // CMP_STRIDE
    cw = 2 * NSA_KV
    assert S // SEL_BLOCK <= FEAT_HI and S <= 64 * 64
    x = zc.reshape(B * n, CMP_STRIDE * cw)
    pe = jnp.concatenate([pe_k, pe_k, pe_v, pe_v], axis=1)
    k1 = phi_k1.reshape(CMP_BLOCK, NSA_DH, NSA_DH)
    v1 = phi_v1.reshape(CMP_BLOCK, NSA_DH, NSA_DH)
    w1 = jnp.stack([_blockdiag([k1[j], k1[j], v1[j], v1[j]]) for j in range(CMP_BLOCK)]).astype(BF16)
    k2 = jnp.concatenate([phi_k2, phi_k2], axis=1)
    w2 = _blockdiag([k2, k2, phi_v2, phi_v2]).astype(BF16)

    def full(shape):
        return pl.BlockSpec(shape, lambda b: (0,) * len(shape))

    return pl.pallas_call(
        _nsa_prep_kernel,
        grid=(B,),
        in_specs=[pl.BlockSpec((n, CMP_STRIDE * cw), lambda b: (b, 0)),
                  full((CMP_BLOCK, cw)), full((CMP_BLOCK, cw, cw)), full((cw, 3 * LANES)),
                  pl.BlockSpec((S, KK_W), lambda b: (b, ZM_KK // KK_W)),
                  pl.BlockSpec((S, VV_W), lambda b: (b, ZM_VV // VV_W)),
                  full((4, S, LANES)), full((2, n, LANES))],
        out_specs=[pl.BlockSpec((1, 8, S, LANES), lambda b: (b, 0, 0, 0)),
                   pl.BlockSpec((1, 2, LANES, S), lambda b: (b, 0, 0, 0)),
                   pl.BlockSpec((1, 4, n, LANES), lambda b: (b, 0, 0, 0)),
                   pl.BlockSpec((1, LANES, n), lambda b: (b, 0, 0))],
        out_shape=[jax.ShapeDtypeStruct((B, 8, S, LANES), BF16),
                   jax.ShapeDtypeStruct((B, 2, LANES, S), BF16),
                   jax.ShapeDtypeStruct((B, 4, n, LANES), BF16),
                   jax.ShapeDtypeStruct((B, LANES, n), BF16)],
        compiler_params=_cparams(("arbitrary",)),
        name="nsa_prep",
    )(x, pe, w1, w2, zm, zm, _key_features(S), _cmp_features(S))


def _nsa_kernel(q_ref, kaug_ref, vt_ref, kcaug_ref, vct_ref, zs_ref, ovt_ref, out_ref,
                qat_ref, ocmp_ref, m_ref, l_ref, acc_ref):
    S = kaug_ref.shape[2]
    tq = q_ref.shape[0]
    ncmp = kcaug_ref.shape[2]
    nsel = S // SEL_BLOCK
    n_pairs = NSA_H // 2
    qi = pl.program_id(1)
    t0 = pl.multiple_of(qi * tq, tq)
    t_row = t0 + lax.broadcasted_iota(jnp.int32, (1, tq), 1)
    lane = lax.broadcasted_iota(jnp.int32, (1, LANES), 1)
    half_f = [jnp.where(lane < NSA_DH, 1.0, 0.0), jnp.where(lane >= NSA_DH, 1.0, 0.0)]

    def make_query(h, block_penalty):
        r = h % NSA_R
        e = r % 2
        base = NSA_DH * (1 - e)
        slope = 2.0 ** (-8.0 * (h + 1.0) / NSA_H)
        lanes2 = q_ref[:, (h // 2) * LANES:(h // 2 + 1) * LANES].astype(F32)
        qa = lanes2 * (half_f[e] * (NSA_DH ** -0.5))
        qa = qa + jnp.where(lane == base + FEAT_HI, slope * 64.0, jnp.where(lane == base + FEAT_LO, slope, 0.0))
        if block_penalty is not None:
            qa = qa + block_penalty * half_f[1 - e]
        qat_ref[(h // NSA_R) * 2 + e, :, (r // 2) * tq:(r // 2 + 1) * tq] = qa.T.astype(BF16)

    for h in range(NSA_H):
        make_query(h, None)

    ki = lax.broadcasted_iota(jnp.int32, (TK, 2 * tq), 0)
    qu = lax.broadcasted_iota(jnp.int32, (TK, 2 * tq), 1)
    qu = jnp.where(qu >= tq, qu - tq, qu)
    tile_valid = {MASK_CAUSAL: ki <= qu,
                  MASK_WINDOW_EDGE: ki > qu}

    c_col = lax.broadcasted_iota(jnp.int32, (ncmp, 1), 0)
    t_row2 = jnp.concatenate([t_row, t_row], axis=1)
    valid_c = c_col * CMP_STRIDE + (CMP_BLOCK - 1) <= t_row2
    j_col = lax.broadcasted_iota(jnp.int32, (nsel, 1), 0)
    jt = t_row >> 6
    forced = jnp.logical_or(j_col == 0, jnp.logical_or(j_col == jt, j_col == jt - 1))
    future = j_col > jt

    for g in range(NSA_G):
        psum = jnp.zeros((ncmp, tq), F32)
        for e in range(2):
            pair = g * 2 + e
            sc = jnp.where(valid_c, _dot(kcaug_ref[0, pair], qat_ref[pair]), NEG)
            ex = jnp.where(valid_c, jnp.exp(sc - jnp.max(sc, axis=0, keepdims=True)), 0.0)
            den = jnp.sum(ex, axis=0, keepdims=True)
            p = ex * (1.0 / jnp.where(den > 0.0, den, 1.0))
            psum = psum + p[:, :tq] + p[:, tq:]
            ocmp_ref[pair] = _dot(vct_ref[0, g * NSA_DH:(g + 1) * NSA_DH, :], p.astype(BF16))

        p_hi = psum.astype(BF16)
        p_lo = (psum - p_hi.astype(F32)).astype(BF16)
        imp = _dot(ovt_ref[...], p_hi) + _dot(ovt_ref[...], p_lo)
        val = jnp.where(future, NEG, jnp.where(forced, BIG, imp))
        rank = jnp.zeros((nsel, tq), F32)
        for i in range(nsel):
            vi = val[i:i + 1, :]
            tie = jnp.where(vi == val, 1.0, 0.0) * jnp.where(j_col > i, 1.0, 0.0)
            rank = rank + jnp.where(vi > val, 1.0, 0.0) + tie
        penalty = jnp.where(rank < float(min(TOP_N, nsel)), 0.0, NEG)
        pad = jnp.zeros((NSA_DH - nsel, tq), F32)
        penalty_t = jnp.concatenate([penalty, pad, penalty, pad], axis=0).T
        for r in range(NSA_R):
            make_query(g * NSA_R + r, penalty_t)

    m_ref[...] = jnp.full(m_ref.shape, NEG, F32)
    l_ref[...] = jnp.zeros(l_ref.shape, F32)
    acc_ref[...] = jnp.zeros(acc_ref.shape, F32)

    def attend(br, k0, mask):
        def scores(pair):
            return _dot(kaug_ref[0, br * n_pairs + pair, pl.ds(k0, TK), :], qat_ref[pair])

        s_next = scores(0)
        for pair in range(n_pairs):
            s_t = s_next
            if pair + 1 < n_pairs:
                s_next = scores(pair + 1)
            g = pair // 2
            slot = br * n_pairs + pair
            if mask is not None:
                s_t = jnp.where(tile_valid[mask], s_t, NEG)
            m_old = m_ref[slot]
            m_new = jnp.maximum(m_old, jnp.max(s_t, axis=0, keepdims=True))
            alpha = jnp.exp(m_old - m_new)
            p = jnp.exp(s_t - m_new)
            l_ref[slot] = alpha * l_ref[slot] + jnp.sum(p, axis=0, keepdims=True)
            v_t = vt_ref[0, br, g * NSA_DH:(g + 1) * NSA_DH, pl.ds(k0, TK)]
            acc_ref[slot] = alpha * acc_ref[slot] + _dot(v_t, p.astype(BF16))
            m_ref[slot] = m_new

    attend(BR_WIN, t0, MASK_CAUSAL)
    n_back = WINDOW // TK
    for d in range(1, n_back + 1):
        @pl.when(qi >= d)
        def _(d=d):
            attend(BR_WIN, pl.multiple_of(t0 - d * TK, TK), MASK_WINDOW_EDGE if d == n_back else None)

    def sel_body(kb, carry):
        attend(BR_SEL, pl.multiple_of(kb * TK, TK), None)
        return carry

    lax.fori_loop(0, qi, sel_body, 0)
    attend(BR_SEL, t0, MASK_CAUSAL)

    gates_t = jax.nn.sigmoid(zs_ref[...]).T
    for hp in range(NSA_H // 2):
        comb = []
        for e in range(2):
            h = 2 * hp + e
            c0 = ZS_GATE + NSA_NB * h
            pair = (h // NSA_R) * 2 + e
            cols = slice(((h % NSA_R) // 2) * tq, ((h % NSA_R) // 2 + 1) * tq)
            sel, win = BR_SEL * n_pairs + pair, BR_WIN * n_pairs + pair
            o_sel = acc_ref[sel, :, cols] * (1.0 / l_ref[sel, :, cols])
            o_win = acc_ref[win, :, cols] * (1.0 / l_ref[win, :, cols])
            comb.append(gates_t[c0:c0 + 1] * ocmp_ref[pair, :, cols] + gates_t[c0 + 1:c0 + 2] * o_sel
                        + gates_t[c0 + 2:c0 + 3] * o_win)
        out_ref[:, hp * LANES:(hp + 1) * LANES] = jnp.concatenate(comb, axis=0).T.astype(BF16)


def _overlap_t(S):
    c = np.arange(S // CMP_STRIDE)[None, :]
    j = np.arange(S // SEL_BLOCK)[:, None]
    ov = (c * CMP_STRIDE <= j * SEL_BLOCK + SEL_BLOCK - 1) & (c * CMP_STRIDE + CMP_BLOCK - 1 >= j * SEL_BLOCK)
    ov &= c < (S - CMP_BLOCK) // CMP_STRIDE + 1
    return jnp.asarray(ov.astype(np.float32), dtype=BF16)


def _nsa(zm, zs, kaug, vt, kcaug, vct, S):
    M = zm.shape[0]
    B = M // S
    tq = min(TQ, S)
    assert tq == TK and WINDOW % TK == 0 and S % TK == 0
    nq = S // tq
    n = S // CMP_STRIDE
    nsel = S // SEL_BLOCK
    return pl.pallas_call(
        _nsa_kernel,
        grid=(B, nq),
        in_specs=[pl.BlockSpec((tq, NSA_W), lambda b, i: (b * nq + i, ZM_NQ // NSA_W)),
                  pl.BlockSpec((1, 8, S, LANES), lambda b, i: (b, 0, 0, 0)),
                  pl.BlockSpec((1, 2, LANES, S), lambda b, i: (b, 0, 0, 0)),
                  pl.BlockSpec((1, 4, n, LANES), lambda b, i: (b, 0, 0, 0)),
                  pl.BlockSpec((1, LANES, n), lambda b, i: (b, 0, 0)),
                  pl.BlockSpec((tq, ZS_W), lambda b, i: (b * nq + i, 0)),
                  pl.BlockSpec((nsel, n), lambda b, i: (0, 0))],
        out_specs=pl.BlockSpec((tq, NSA_W), lambda b, i: (b * nq + i, 0)),
        out_shape=jax.ShapeDtypeStruct((M, NSA_W), BF16),
        scratch_shapes=[pltpu.VMEM((NSA_H // 2, LANES, 2 * tq), BF16),
                        pltpu.VMEM((NSA_H // 2, NSA_DH, 2 * tq), F32),
                        pltpu.VMEM((NSA_H, 1, 2 * tq), F32),
                        pltpu.VMEM((NSA_H, 1, 2 * tq), F32),
                        pltpu.VMEM((NSA_H, NSA_DH, 2 * tq), F32)],
        compiler_params=_cparams(("arbitrary", "arbitrary")),
        name="nsa_attention",
    )(zm, kaug, vt, kcaug, vct, zs, _overlap_t(S))


def _merge_kernel(a_ref, b_ref, c_ref, ga_ref, gb_ref, gc_ref, x_ref, gt_ref, wa_ref, wb_ref, wc_ref, wo_ref,
                  o_ref):
    merged = (jax.nn.sigmoid(ga_ref[...].astype(F32)) * _dot(a_ref[...], wa_ref[...])
              + jax.nn.sigmoid(gb_ref[...].astype(F32)) * _dot(b_ref[...], wb_ref[...])
              + jax.nn.sigmoid(gc_ref[...].astype(F32)) * _dot(c_ref[...], wc_ref[...]))
    o_ref[...] = x_ref[...] + gt_ref[0] * _dot(merged.astype(BF16), wo_ref[...])


def _merge(ya, yb, yc, zm, x2, gt, wa, wb, wc, wo, S):
    M, D = x2.shape
    tm = min(TM_MERGE, S)
    per_b = S // tm

    def rows(w, jcol=0):
        return pl.BlockSpec((tm, w), lambda i: (i, jcol))

    def full(shape):
        return pl.BlockSpec(shape, lambda i: (0,) * len(shape))

    return pl.pallas_call(
        _merge_kernel,
        grid=(M // tm,),
        in_specs=[rows(GM_W), rows(ML_W), rows(NSA_W),
                  rows(D, ZM_GA // D), rows(D, ZM_GBR // D), rows(D, ZM_GC // D),
                  rows(D), pl.BlockSpec((1, 1, D), lambda i: (i // per_b, 0, 0)),
                  full((GM_W, D)), full((ML_W, D)), full((NSA_W, D)), full((D, D))],
        out_specs=rows(D),
        out_shape=jax.ShapeDtypeStruct((M, D), F32),
        compiler_params=_cparams(("arbitrary",)),
        name="merge_out",
    )(ya, yb, yc, zm, zm, zm, x2, gt, wa, wb, wc, wo)


def _mlp_kernel(x_ref, g_ref, sc_ref, sh_ref, gt_ref, w1_ref, w2_ref, gf_ref, o_ref, acc_ref, *, final_norm):
    x = x_ref[...]
    h = _modulated_norm(x, g_ref[...], sc_ref[0], sh_ref[0]).astype(BF16)
    for c in range(D_FF // FF_CHUNK):
        mid = jnp.square(jnp.maximum(_dot(h, w1_ref[:, c * FF_CHUNK:(c + 1) * FF_CHUNK]), 0.0)).astype(BF16)
        upd = _dot(mid, w2_ref[c * FF_CHUNK:(c + 1) * FF_CHUNK, :])
        if c == 0:
            acc_ref[...] = upd
        else:
            acc_ref[...] += upd
    y = x + gt_ref[0] * acc_ref[...]
    if final_norm:
        y = (y * lax.rsqrt(jnp.mean(y * y, axis=-1, keepdims=True) + EPS)) * gf_ref[...]
    o_ref[...] = y


def _mlp(x2, g, sc, sh, gt, w1, w2, g_final, S, final_norm):
    M, D = x2.shape
    tm = min(TM_MLP, S)
    per_b = S // tm
    mod = pl.BlockSpec((1, 1, D), lambda i: (i // per_b, 0, 0))
    return pl.pallas_call(
        functools.partial(_mlp_kernel, final_norm=final_norm),
        grid=(M // tm,),
        in_specs=[pl.BlockSpec((tm, D), lambda i: (i, 0)),
                  pl.BlockSpec((1, D), lambda i: (0, 0)),
                  mod, mod, mod,
                  pl.BlockSpec((D, D_FF), lambda i: (0, 0)),
                  pl.BlockSpec((D_FF, D), lambda i: (0, 0)),
                  pl.BlockSpec((1, D), lambda i: (0, 0))],
        out_specs=pl.BlockSpec((tm, D), lambda i: (i, 0)),
        out_shape=jax.ShapeDtypeStruct((M, D), F32),
        scratch_shapes=[pltpu.VMEM((tm, D), F32)],
        compiler_params=_cparams(("arbitrary",)),
        name="relu2_mlp",
    )(x2, g, sc, sh, gt, w1, w2, g_final)


def _w_cols(w, name):
    return w[:, _OFFS[name]:_OFFS[name] + _SIZE[name]]


def _dup_heads(w):
    parts = []
    for g in range(NSA_G):
        blk = w[:, g * NSA_DH:(g + 1) * NSA_DH]
        parts += [blk, blk]
    return jnp.concatenate(parts, axis=1)


def _proj_weights(w):
    D = w.shape[0]
    wm = jnp.concatenate([_w_cols(w, n) for n in ('ga', 'gbr', 'gc', 'gu', 'gv', 'mq', 'mk', 'mv', 'mo', 'nq')]
                         + [_dup_heads(_w_cols(w, 'nks')), _dup_heads(_w_cols(w, 'nkw')),
                            _w_cols(w, 'nvs'), _w_cols(w, 'nvw')], axis=1)
    wc = jnp.concatenate([_w_cols(w, 'nkc'), _w_cols(w, 'nvc')], axis=1)
    ws = jnp.concatenate([_w_cols(w, 'ngate'), _w_cols(w, 'mi'), _w_cols(w, 'mf'),
                          jnp.zeros((D, ZS_W - NSA_H * NSA_NB - 2 * ML_H), w.dtype)], axis=1)
    return wm.astype(BF16), wc.astype(BF16), ws.astype(BF16)


def kernel(x, c, g_norm1, g_norm2, w_ada, b_ada, w_in, gm_ln_g, gm_ln_b, gm_ws, gm_bs, ml_conv_w, ml_conv_b,
           ml_gate_b, ml_norm_g, nsa_pe_k, nsa_pe_v, nsa_phi_k1, nsa_phi_k2, nsa_phi_v1, nsa_phi_v2,
           w_up_a, w_up_b, w_up_c, w_out, w_mlp1, w_mlp2, g_final):
    B, S, D = x.shape
    depth = w_in.shape[0]
    M = B * S
    mod = _ada(c, w_ada, b_ada)
    x2 = x.reshape(M, D)
    for l in range(depth):
        sh1, sc1, gt1, sh2, sc2, gt2 = [mod[l, :, i * D:(i + 1) * D].reshape(B, 1, D) for i in range(6)]
        wm, wc, ws = _proj_weights(w_in[l])
        zm, zc, zs = _inproj(x2, g_norm1[l].reshape(1, D), sc1, sh1, wm, wc, ws, S)
        ya = _gmlp(zm, gm_ln_g[l], gm_ln_b[l], gm_ws[l], gm_bs[l], S)
        yb = _mlstm(zm, zs, ml_conv_w[l], ml_conv_b[l], ml_gate_b[l], ml_norm_g[l], S)
        kaug, vt, kcaug, vct = _nsa_prep(zc, zm, nsa_pe_k[l], nsa_pe_v[l], nsa_phi_k1[l], nsa_phi_k2[l],
                                         nsa_phi_v1[l], nsa_phi_v2[l], S)
        yc = _nsa(zm, zs, kaug, vt, kcaug, vct, S)
        x2 = _merge(ya, yb, yc, zm, x2, gt1, w_up_a[l].astype(BF16), w_up_b[l].astype(BF16),
                    w_up_c[l].astype(BF16), w_out[l].astype(BF16), S)
        x2 = _mlp(x2, g_norm2[l].reshape(1, D), sc2, sh2, gt2, w_mlp1[l].astype(BF16), w_mlp2[l].astype(BF16),
                  g_final.reshape(1, D), S, final_norm=(l == depth - 1))
    return x2.reshape(B, S, D)
```

```python
import functools

import numpy as np
import jax
import jax.numpy as jnp
from jax import lax
from jax.experimental import pallas as pl
from jax.experimental.pallas import tpu as pltpu

F32 = jnp.float32
BF16 = jnp.bfloat16

D_MODEL = 1024
GM_W = 512
GM_GROUPS = 4
GM_CHUNK = 128
ML_H = 4
ML_DH = 128
ML_W = ML_H * ML_DH
CONV_K = 4
NSA_H = 8
NSA_G = 2
NSA_R = NSA_H // NSA_G
NSA_DH = 64
NSA_W = NSA_H * NSA_DH
NSA_KV = NSA_G * NSA_DH
NSA_NB = 3
CMP_BLOCK = 32
CMP_STRIDE = 16
SEL_BLOCK = 64
TOP_N = 8
WINDOW = 512
D_FF = 4 * D_MODEL
EPS = 1e-6
NEG = -1e30
BIG = 1e4
SPLIT_SIZES = (GM_W, GM_W, ML_W, ML_W, ML_W, ML_W, ML_H, ML_H, NSA_W, NSA_KV, NSA_KV, NSA_KV, NSA_KV,
               NSA_KV, NSA_KV, NSA_H * NSA_NB, D_MODEL, D_MODEL, D_MODEL)
SPLIT_NAMES = ('gu', 'gv', 'mq', 'mk', 'mv', 'mo', 'mi', 'mf', 'nq', 'nkc', 'nvc', 'nks', 'nvs', 'nkw', 'nvw',
               'ngate', 'ga', 'gbr', 'gc')
_OFFS = dict(zip(SPLIT_NAMES, np.concatenate([[0], np.cumsum(SPLIT_SIZES)[:-1]]).tolist()))
_SIZE = dict(zip(SPLIT_NAMES, SPLIT_SIZES))

LANES = 128
VMEM_LIMIT = 56 * 1024 * 1024

ZM_GA, ZM_GBR, ZM_GC = 0, 1024, 2048
ZM_GU, ZM_GV = 3072, 3584
ZM_MQ, ZM_MK, ZM_MV, ZM_MO = 4096, 4608, 5120, 5632
ZM_NQ = 6144
ZM_KK = 6656
ZM_VV = 7168
ZM_W = 7424
KK_W = 4 * LANES
VV_W = 2 * LANES
BR_SEL, BR_WIN = 0, 1
MASK_CAUSAL, MASK_WINDOW_EDGE = 0, 1
FEAT_SEL = 0
FEAT_HI = 32
FEAT_LO = 33
ZS_GATE = 0
ZS_MI = 24
ZS_MF = 28
ZS_W = 128

TM_PROJ = 1024
TN_PROJ = 3712
TS_GMLP = 512
ML_CHUNK = 128
CONV_BLK = 256
CONV_HALO = 16
TQ = 256
TK = 256
SCORE_LOOKAHEAD = 2
VT_ROWS = NSA_DH + 16
TM_MERGE = 512
TM_MLP = 512
FF_CHUNK = 1024


def _dot(a, b):
    return jnp.dot(a, b, preferred_element_type=F32)


def _split3(x):
    x1 = x.astype(BF16)
    r1 = x - x1.astype(F32)
    x2 = r1.astype(BF16)
    x3 = (r1 - x2.astype(F32)).astype(BF16)
    return x1, x2, x3


def _cparams(sem):
    return pltpu.CompilerParams(dimension_semantics=sem, vmem_limit_bytes=VMEM_LIMIT)


def _ada_kernel(c_ref, w_ref, b_ref, o_ref):
    c = c_ref[...]
    cond = c * jax.nn.sigmoid(c)
    c1, c2, c3 = _split3(cond)
    w1, w2, w3 = _split3(w_ref[0])
    acc = _dot(c1, w1) + (_dot(c1, w2) + _dot(c2, w1)) + (_dot(c1, w3) + _dot(c2, w2) + _dot(c3, w1))
    o_ref[0] = acc + b_ref[0]


def _ada(c, w_ada, b_ada):
    L, D, N = w_ada.shape
    B = c.shape[0]
    tn = 1536
    return pl.pallas_call(
        _ada_kernel,
        grid=(L, N // tn),
        in_specs=[pl.BlockSpec((B, D), lambda l, j: (0, 0)),
                  pl.BlockSpec((1, D, tn), lambda l, j: (l, 0, j)),
                  pl.BlockSpec((1, 1, tn), lambda l, j: (l, 0, j))],
        out_specs=pl.BlockSpec((1, B, tn), lambda l, j: (l, 0, j)),
        out_shape=jax.ShapeDtypeStruct((L, B, N), F32),
        compiler_params=_cparams(("arbitrary", "arbitrary")),
        name="ada_mod",
    )(c, w_ada, b_ada.reshape(L, 1, N))


def _modulated_norm(x, g, sc, sh):
    y = x * lax.rsqrt(jnp.mean(x * x, axis=-1, keepdims=True) + EPS)
    return (y * g) * (1.0 + sc) + sh


def _inproj_kernel(x_ref, g_ref, sc_ref, sh_ref, wm_ref, wc_ref, ws_ref, zm_ref, zc_ref, zs_ref, h_ref):
    @pl.when(pl.program_id(1) == 0)
    def _():
        h = _modulated_norm(x_ref[...], g_ref[...], sc_ref[0], sh_ref[0]).astype(BF16)
        h_ref[...] = h
        zc_ref[...] = _dot(h, wc_ref[...])
        zs_ref[...] = _dot(h, ws_ref[...])

    zm_ref[...] = _dot(h_ref[...], wm_ref[...]).astype(BF16)


def _inproj(x2, g, sc, sh, wm, wc, ws, S):
    M, D = x2.shape
    tm, tn = min(TM_PROJ, S), TN_PROJ
    per_b = S // tm
    return pl.pallas_call(
        _inproj_kernel,
        grid=(M // tm, ZM_W // tn),
        in_specs=[pl.BlockSpec((tm, D), lambda i, j: (i, 0)),
                  pl.BlockSpec((1, D), lambda i, j: (0, 0)),
                  pl.BlockSpec((1, 1, D), lambda i, j: (i // per_b, 0, 0)),
                  pl.BlockSpec((1, 1, D), lambda i, j: (i // per_b, 0, 0)),
                  pl.BlockSpec((D, tn), lambda i, j: (0, j)),
                  pl.BlockSpec((D, 2 * NSA_KV), lambda i, j: (0, 0)),
                  pl.BlockSpec((D, ZS_W), lambda i, j: (0, 0))],
        out_specs=[pl.BlockSpec((tm, tn), lambda i, j: (i, j)),
                   pl.BlockSpec((tm, 2 * NSA_KV), lambda i, j: (i, 0)),
                   pl.BlockSpec((tm, ZS_W), lambda i, j: (i, 0))],
        out_shape=[jax.ShapeDtypeStruct((M, ZM_W), BF16),
                   jax.ShapeDtypeStruct((M, 2 * NSA_KV), F32),
                   jax.ShapeDtypeStruct((M, ZS_W), F32)],
        scratch_shapes=[pltpu.VMEM((tm, D), BF16)],
        compiler_params=_cparams(("arbitrary", "arbitrary")),
        name="in_proj",
    )(x2, g, sc, sh, wm, wc, ws)


def _gmlp_kernel(u_ref, v_ref, lng_ref, lnb_ref, ws_ref, bst_ref, o_ref):
    ts = u_ref.shape[0]
    dg = GM_W // GM_GROUPS
    row = lax.broadcasted_iota(jnp.int32, (GM_CHUNK, GM_CHUNK), 0)
    col = lax.broadcasted_iota(jnp.int32, (GM_CHUNK, GM_CHUNK), 1)
    ws = [jnp.where(row >= col, ws_ref[g], 0.0).astype(BF16) for g in range(GM_GROUPS)]
    lng = lng_ref[...]
    lnb = lnb_ref[...]
    for c in range(ts // GM_CHUNK):
        r0 = c * GM_CHUNK
        u = jax.nn.gelu(u_ref[r0:r0 + GM_CHUNK, :].astype(F32))
        v = jax.nn.gelu(v_ref[r0:r0 + GM_CHUNK, :].astype(F32))
        mu = jnp.mean(v, axis=-1, keepdims=True)
        var = jnp.mean(jnp.square(v - mu), axis=-1, keepdims=True)
        vb = ((v - mu) * lax.rsqrt(var + EPS) * lng + lnb).astype(BF16)
        for g in range(GM_GROUPS):
            mixed = _dot(ws[g], vb[:, g * dg:(g + 1) * dg]) + bst_ref[:, g:g + 1]
            o_ref[r0:r0 + GM_CHUNK, g * dg:(g + 1) * dg] = (u[:, g * dg:(g + 1) * dg] * mixed).astype(BF16)


def _gmlp(zm, ln_g, ln_b, ws, bs, S):
    M = zm.shape[0]
    ts = min(TS_GMLP, S)
    return pl.pallas_call(
        _gmlp_kernel,
        grid=(M // ts,),
        in_specs=[pl.BlockSpec((ts, GM_W), lambda i: (i, ZM_GU // GM_W)),
                  pl.BlockSpec((ts, GM_W), lambda i: (i, ZM_GV // GM_W)),
                  pl.BlockSpec((1, GM_W), lambda i: (0, 0)),
                  pl.BlockSpec((1, GM_W), lambda i: (0, 0)),
                  pl.BlockSpec((GM_GROUPS, GM_CHUNK, GM_CHUNK), lambda i: (0, 0, 0)),
                  pl.BlockSpec((GM_CHUNK, GM_GROUPS), lambda i: (0, 0))],
        out_specs=pl.BlockSpec((ts, GM_W), lambda i: (i, 0)),
        out_shape=jax.ShapeDtypeStruct((M, GM_W), BF16),
        compiler_params=_cparams(("arbitrary",)),
        name="gmlp_mixer",
    )(zm, zm, ln_g.reshape(1, GM_W), ln_b.reshape(1, GM_W), ws, bs.T)


def _log_sigmoid(x):
    return jnp.minimum(x, 0.0) - jnp.log1p(jnp.exp(-jnp.abs(x)))


def _conv_silu(x_ext, w, b):
    n = x_ext.shape[0] - CONV_HALO
    y = b
    for j in range(CONV_K):
        sh = CONV_K - 1 - j
        xs = x_ext if sh == 0 else pltpu.roll(x_ext, sh, axis=0)
        y = y + xs[CONV_HALO:CONV_HALO + n] * w[j:j + 1]
    return y * jax.nn.sigmoid(y)


def _mlstm_kernel(q_ref, k_ref, v_ref, o_ref, zs_ref, cw_ref, cb_ref, gb_ref, ng_ref, out_ref,
                  qc_ref, kt_ref, c_ref):
    S = q_ref.shape[0]
    L = min(ML_CHUNK, S)
    blk = min(CONV_BLK, S)

    for src, c0, is_key in ((q_ref, 0, False), (k_ref, ML_W, True)):
        w = cw_ref[:, c0:c0 + ML_W]
        b = cb_ref[:, c0:c0 + ML_W]

        def conv_block(x_ext, r0, w=w, b=b, is_key=is_key):
            y = _conv_silu(x_ext, w, b)
            if is_key:
                y = y * (ML_DH ** -0.5)
                for h in range(ML_H):
                    kt_ref[h * ML_DH:(h + 1) * ML_DH, pl.ds(r0, blk)] = y[:, h * ML_DH:(h + 1) * ML_DH].T.astype(BF16)
            else:
                qc_ref[pl.ds(r0, blk), :] = y.astype(BF16)

        first = jnp.concatenate([jnp.zeros((CONV_HALO, ML_W), F32), src[0:blk, :].astype(F32)], axis=0)
        conv_block(first, 0)

        def conv_body(i, carry, src=src, conv_block=conv_block):
            r0 = pl.multiple_of(i * blk, blk)
            conv_block(src[pl.ds(r0 - CONV_HALO, blk + CONV_HALO), :].astype(F32), r0)
            return carry

        lax.fori_loop(1, S // blk, conv_body, 0)

    c_ref[...] = jnp.zeros_like(c_ref)
    row = lax.broadcasted_iota(jnp.int32, (L, L), 0)
    col = lax.broadcasted_iota(jnp.int32, (L, L), 1)
    tril = row >= col
    tril_b = jnp.where(tril, 1.0, 0.0).astype(BF16)
    triu_b = jnp.where(row <= col, 1.0, 0.0).astype(BF16)
    lane = lax.broadcasted_iota(jnp.int32, (L, ML_DH), 1)
    ones_col = jnp.where(lane == 0, 1.0, 0.0).astype(BF16)
    gb = gb_ref[...]

    def chunk_body(c, m_prevs):
        t0 = pl.multiple_of(c * L, L)
        gi = zs_ref[pl.ds(t0, L), :] + gb
        lf_cols = _log_sigmoid(gi)
        gi_t = gi.T
        ig_rows = gi_t[ZS_MI:ZS_MI + ML_H]
        lf_rows = _log_sigmoid(gi_t[ZS_MI:ZS_MI + 2 * ML_H])
        c1, c2, c3 = _split3(lf_cols)
        b_cols = _dot(tril_b, c1) + _dot(tril_b, c2) + _dot(tril_b, c3)
        r1, r2, r3 = _split3(lf_rows)
        b_rows = _dot(r1, triu_b) + _dot(r2, triu_b) + _dot(r3, triu_b)
        heads = range(ML_H)
        hs = [slice(h * ML_DH, (h + 1) * ML_DH) for h in heads]
        q = [qc_ref[pl.ds(t0, L), hs[h]] for h in heads]
        k_t = [kt_ref[hs[h], pl.ds(t0, L)] for h in heads]
        cmat = [c_ref[h] for h in heads]
        qk = [_dot(q[h], k_t[h]) for h in heads]
        qc = [_dot(q[h], cmat[h].astype(BF16)) for h in heads]
        v_aug, dmat, m_col, w_inter, vw, w_prev, m_last = [], [], [], [], [], [], []
        for h in heads:
            bc = b_cols[:, ZS_MF + h:ZS_MF + h + 1]
            igc = gi[:, ZS_MI + h:ZS_MI + h + 1]
            br = b_rows[ML_H + h:ML_H + h + 1, :]
            igr = ig_rows[h:h + 1, :]
            a_col = bc + m_prevs[h]
            dmat.append(jnp.where(tril, bc - br + igr, NEG))
            m_col.append(jnp.maximum(a_col, jnp.max(dmat[h], axis=1, keepdims=True)))
            w_inter.append(jnp.exp(a_col - m_col[h]))
            m_last.append(m_col[h][L - 1:L, :])
            w_prev.append(jnp.exp(a_col[L - 1:L, :] - m_last[h]))
            w_s = jnp.exp(bc[L - 1:L, :] - bc + igc - m_last[h])
            v_aug.append(jnp.concatenate([v_ref[pl.ds(t0, L), hs[h]], ones_col], axis=1))
            vw.append((w_s * v_aug[h].astype(F32)).astype(BF16))
        for h in heads:
            c_ref[h] = w_prev[h] * cmat[h] + _dot(k_t[h], vw[h])
        num_aug = [_dot((qk[h] * jnp.exp(dmat[h] - m_col[h])).astype(BF16), v_aug[h]) + w_inter[h] * qc[h]
                   for h in heads]
        for h in heads:
            num = num_aug[h][:, :ML_DH]
            den = num_aug[h][:, ML_DH:ML_DH + 1]
            hval = num / jnp.maximum(jnp.abs(den), jnp.exp(-m_col[h]))
            mu = jnp.mean(hval, axis=-1, keepdims=True)
            var = jnp.mean(jnp.square(hval - mu), axis=-1, keepdims=True)
            hn = (hval - mu) * lax.rsqrt(var + EPS) * ng_ref[:, hs[h]]
            og = jax.nn.sigmoid(o_ref[pl.ds(t0, L), hs[h]].astype(F32))
            out_ref[pl.ds(t0, L), hs[h]] = (og * hn).astype(BF16)
        return tuple(m_last)

    lax.fori_loop(0, S // L, chunk_body, tuple(jnp.zeros((1, 1), F32) for _ in range(ML_H)))


def _mlstm(zm, zs, conv_w, conv_b, gate_b, norm_g, S):
    M = zm.shape[0]
    gb_row = jnp.zeros((1, ZS_W), F32).at[0, ZS_MI:ZS_MI + 2 * ML_H].set(gate_b)

    def col(off):
        return pl.BlockSpec((S, ML_W), lambda b: (b, off // ML_W))

    return pl.pallas_call(
        _mlstm_kernel,
        grid=(M // S,),
        in_specs=[col(ZM_MQ), col(ZM_MK), col(ZM_MV), col(ZM_MO),
                  pl.BlockSpec((S, ZS_W), lambda b: (b, 0)),
                  pl.BlockSpec((CONV_K, 2 * ML_W), lambda b: (0, 0)),
                  pl.BlockSpec((1, 2 * ML_W), lambda b: (0, 0)),
                  pl.BlockSpec((1, ZS_W), lambda b: (0, 0)),
                  pl.BlockSpec((1, ML_W), lambda b: (0, 0))],
        out_specs=pl.BlockSpec((S, ML_W), lambda b: (b, 0)),
        out_shape=jax.ShapeDtypeStruct((M, ML_W), BF16),
        scratch_shapes=[pltpu.VMEM((S, ML_W), BF16), pltpu.VMEM((ML_W, S), BF16),
                        pltpu.VMEM((ML_H, ML_DH, 2 * ML_DH), F32)],
        compiler_params=_cparams(("arbitrary",)),
        name="mlstm_mixer",
    )(zm, zm, zm, zm, zs, conv_w, conv_b.reshape(1, 2 * ML_W), gb_row, norm_g.reshape(1, ML_W))


def _nsa_prep_kernel(x_ref, pe_ref, w1_ref, w2_ref, kk_ref, vv_ref, feat_ref, featc_ref,
                     kaug_ref, vt_ref, kcaug_ref, vct_ref):
    n = x_ref.shape[0]
    S = kk_ref.shape[0]
    cw = 2 * NSA_KV
    half = CMP_BLOCK // 2
    acc_a = jnp.zeros((n, cw), F32)
    acc_b = jnp.zeros((n, cw), F32)
    for j in range(half):
        xj = x_ref[:, j * cw:(j + 1) * cw]
        acc_a = acc_a + _dot((xj + pe_ref[j:j + 1, :]).astype(BF16), w1_ref[j])
        acc_b = acc_b + _dot((xj + pe_ref[half + j:half + j + 1, :]).astype(BF16), w1_ref[half + j])
    pre = acc_a + pltpu.roll(acc_b, n - 1, axis=0)
    cmp = _dot(jax.nn.gelu(pre).astype(BF16), w2_ref[...])

    lane = lax.broadcasted_iota(jnp.int32, (1, LANES), 1)
    keep = [jnp.where(lane < NSA_DH, 1.0, 0.0).astype(BF16), jnp.where(lane >= NSA_DH, 1.0, 0.0).astype(BF16)]
    for g in range(NSA_G):
        kd = cmp[:, g * LANES:(g + 1) * LANES].astype(BF16)
        for e in range(2):
            kcaug_ref[0, g * 2 + e] = kd * keep[e] + featc_ref[e]
    vct_ref[0] = cmp[:, NSA_G * LANES:(NSA_G + 1) * LANES].T.astype(BF16)
    tb = min(TK, S)
    for br in range(2):
        for g in range(NSA_G):
            kd = kk_ref[:, (br * NSA_G + g) * LANES:(br * NSA_G + g + 1) * LANES]
            for e in range(2):
                kaug_ref[0, (br * NSA_G + g) * 2 + e] = kd * keep[e] + feat_ref[br * 2 + e]
        for c in range(S // tb):
            v_t = vv_ref[c * tb:(c + 1) * tb, br * LANES:(br + 1) * LANES].astype(F32).T.astype(BF16)
            for g in range(NSA_G):
                vt_ref[0, br, g, 0:NSA_DH, c * tb:(c + 1) * tb] = v_t[g * NSA_DH:(g + 1) * NSA_DH]
                vt_ref[0, br, g, NSA_DH:VT_ROWS, c * tb:(c + 1) * tb] = jnp.ones((VT_ROWS - NSA_DH, tb), BF16)


def _blockdiag(blocks):
    n = len(blocks)
    rows = []
    for i, blk in enumerate(blocks):
        rows.append(jnp.concatenate([blk if j == i else jnp.zeros((blk.shape[0], blocks[j].shape[1]), blk.dtype)
                                     for j in range(n)], axis=1))
    return jnp.concatenate(rows, axis=0)


def _key_features(S):
    pos = np.arange(S)
    f = np.zeros((4, S, LANES), np.float32)
    for e in range(2):
        base = NSA_DH * (1 - e)
        f[e, pos, base + FEAT_SEL + pos // SEL_BLOCK] = 1.0
        for br in range(2):
            f[br * 2 + e, :, base + FEAT_HI] = pos // 64
            f[br * 2 + e, :, base + FEAT_LO] = pos % 64
    return jnp.asarray(f, dtype=BF16)


def _cmp_features(S):
    c = np.arange(S // CMP_STRIDE)
    hi = (c * CMP_STRIDE) // 64
    lo = c * CMP_STRIDE + (CMP_BLOCK - 1) * 0.5 - 64 * hi
    f = np.zeros((2, c.size, LANES), np.float32)
    for e in range(2):
        base = NSA_DH * (1 - e)
        f[e, :, base + FEAT_HI] = hi
        f[e, :, base + FEAT_LO] = lo
    return jnp.asarray(f, dtype=BF16)


def _nsa_prep(zc, zm, pe_k, pe_v, phi_k1, phi_k2, phi_v1, phi_v2, S):
    M = zc.shape[0]
    B = M // S
    n = S // CMP_STRIDE
    cw = 2 * NSA_KV
    assert S // SEL_BLOCK <= FEAT_HI and S <= 64 * 64
    x = zc.reshape(B * n, CMP_STRIDE * cw)
    pe = jnp.concatenate([pe_k, pe_k, pe_v, pe_v], axis=1)
    k1 = phi_k1.reshape(CMP_BLOCK, NSA_DH, NSA_DH)
    v1 = phi_v1.reshape(CMP_BLOCK, NSA_DH, NSA_DH)
    w1 = jnp.stack([_blockdiag([k1[j], k1[j], v1[j], v1[j]]) for j in range(CMP_BLOCK)]).astype(BF16)
    k2 = jnp.concatenate([phi_k2, phi_k2], axis=1)
    w2 = _blockdiag([k2, k2, phi_v2, phi_v2]).astype(BF16)

    def full(shape):
        return pl.BlockSpec(shape, lambda b: (0,) * len(shape))

    return pl.pallas_call(
        _nsa_prep_kernel,
        grid=(B,),
        in_specs=[pl.BlockSpec((n, CMP_STRIDE * cw), lambda b: (b, 0)),
                  full((CMP_BLOCK, cw)), full((CMP_BLOCK, cw, cw)), full((cw, 3 * LANES)),
                  pl.BlockSpec((S, KK_W), lambda b: (b, ZM_KK // KK_W)),
                  pl.BlockSpec((S, VV_W), lambda b: (b, ZM_VV // VV_W)),
                  full((4, S, LANES)), full((2, n, LANES))],
        out_specs=[pl.BlockSpec((1, 8, S, LANES), lambda b: (b, 0, 0, 0)),
                   pl.BlockSpec((1, 2, NSA_G, VT_ROWS, S), lambda b: (b, 0, 0, 0, 0)),
                   pl.BlockSpec((1, 4, n, LANES), lambda b: (b, 0, 0, 0)),
                   pl.BlockSpec((1, LANES, n), lambda b: (b, 0, 0))],
        out_shape=[jax.ShapeDtypeStruct((B, 8, S, LANES), BF16),
                   jax.ShapeDtypeStruct((B, 2, NSA_G, VT_ROWS, S), BF16),
                   jax.ShapeDtypeStruct((B, 4, n, LANES), BF16),
                   jax.ShapeDtypeStruct((B, LANES, n), BF16)],
        compiler_params=_cparams(("arbitrary",)),
        name="nsa_prep",
    )(x, pe, w1, w2, zm, zm, _key_features(S), _cmp_features(S))


def _nsa_kernel(q_ref, kaug_ref, vt_ref, kcaug_ref, vct_ref, zs_ref, ovt_ref, out_ref,
                qat_ref, ocmp_ref, m_ref, acc_ref):
    S = kaug_ref.shape[2]
    tq = q_ref.shape[0]
    ncmp = kcaug_ref.shape[2]
    nsel = S // SEL_BLOCK
    n_pairs = NSA_H // 2
    qi = pl.program_id(1)
    t0 = pl.multiple_of(qi * tq, tq)
    t_row = t0 + lax.broadcasted_iota(jnp.int32, (1, tq), 1)
    lane = lax.broadcasted_iota(jnp.int32, (1, LANES), 1)
    half_f = [jnp.where(lane < NSA_DH, 1.0, 0.0), jnp.where(lane >= NSA_DH, 1.0, 0.0)]

    def make_query(h, block_penalty):
        r = h % NSA_R
        e = r % 2
        base = NSA_DH * (1 - e)
        slope = 2.0 ** (-8.0 * (h + 1.0) / NSA_H)
        lanes2 = q_ref[:, (h // 2) * LANES:(h // 2 + 1) * LANES].astype(F32)
        qa = lanes2 * (half_f[e] * (NSA_DH ** -0.5))
        qa = qa + jnp.where(lane == base + FEAT_HI, slope * 64.0, jnp.where(lane == base + FEAT_LO, slope, 0.0))
        if block_penalty is not None:
            qa = qa + block_penalty * half_f[1 - e]
        qat_ref[(h // NSA_R) * 2 + e, :, (r // 2) * tq:(r // 2 + 1) * tq] = qa.T.astype(BF16)

    for h in range(NSA_H):
        make_query(h, None)

    ki = lax.broadcasted_iota(jnp.int32, (TK, 2 * tq), 0)
    qu = lax.broadcasted_iota(jnp.int32, (TK, 2 * tq), 1)
    qu = jnp.where(qu >= tq, qu - tq, qu)
    tile_valid = {MASK_CAUSAL: ki <= qu,
                  MASK_WINDOW_EDGE: ki > qu}

    c_col = lax.broadcasted_iota(jnp.int32, (ncmp, 1), 0)
    t_row2 = jnp.concatenate([t_row, t_row], axis=1)
    valid_c = c_col * CMP_STRIDE + (CMP_BLOCK - 1) <= t_row2
    j_col = lax.broadcasted_iota(jnp.int32, (nsel, 1), 0)
    jt = t_row >> 6
    forced = jnp.logical_or(j_col == 0, jnp.logical_or(j_col == jt, j_col == jt - 1))
    future = j_col > jt

    for g in range(NSA_G):
        psum = jnp.zeros((ncmp, tq), F32)
        for e in range(2):
            pair = g * 2 + e
            sc = jnp.where(valid_c, _dot(kcaug_ref[0, pair], qat_ref[pair]), NEG)
            ex = jnp.where(valid_c, jnp.exp(sc - jnp.max(sc, axis=0, keepdims=True)), 0.0)
            den = jnp.sum(ex, axis=0, keepdims=True)
            p = ex * (1.0 / jnp.where(den > 0.0, den, 1.0))
            psum = psum + p[:, :tq] + p[:, tq:]
            ocmp_ref[pair] = _dot(vct_ref[0, g * NSA_DH:(g + 1) * NSA_DH, :], p.astype(BF16))

        p_hi = psum.astype(BF16)
        p_lo = (psum - p_hi.astype(F32)).astype(BF16)
        imp = _dot(ovt_ref[...], p_hi) + _dot(ovt_ref[...], p_lo)
        val = jnp.where(future, NEG, jnp.where(forced, BIG, imp))
        rank = jnp.zeros((nsel, tq), F32)
        for i in range(nsel):
            vi = val[i:i + 1, :]
            tie = jnp.where(vi == val, 1.0, 0.0) * jnp.where(j_col > i, 1.0, 0.0)
            rank = rank + jnp.where(vi > val, 1.0, 0.0) + tie
        penalty = jnp.where(rank < float(min(TOP_N, nsel)), 0.0, NEG)
        pad = jnp.zeros((NSA_DH - nsel, tq), F32)
        penalty_t = jnp.concatenate([penalty, pad, penalty, pad], axis=0).T
        for r in range(NSA_R):
            make_query(g * NSA_R + r, penalty_t)

    m_ref[...] = jnp.full(m_ref.shape, NEG, F32)
    acc_ref[...] = jnp.zeros(acc_ref.shape, F32)

    def attend(br, k0, mask):
        def scores(pair):
            return _dot(kaug_ref[0, br * n_pairs + pair, pl.ds(k0, TK), :], qat_ref[pair])

        s_queue = [scores(i) for i in range(SCORE_LOOKAHEAD)]
        for pair in range(n_pairs):
            s_t = s_queue.pop(0)
            if pair + SCORE_LOOKAHEAD < n_pairs:
                s_queue.append(scores(pair + SCORE_LOOKAHEAD))
            g = pair // 2
            slot = br * n_pairs + pair
            if mask is not None:
                s_t = jnp.where(tile_valid[mask], s_t, NEG)
            m_old = m_ref[slot]
            m_new = jnp.maximum(m_old, jnp.max(s_t, axis=0, keepdims=True))
            alpha = jnp.exp(m_old - m_new)
            p = jnp.exp(s_t - m_new).astype(BF16)
            acc_ref[slot] = alpha * acc_ref[slot] + _dot(vt_ref[0, br, g, :, pl.ds(k0, TK)], p)
            m_ref[slot] = m_new

    attend(BR_WIN, t0, MASK_CAUSAL)
    n_back = WINDOW // TK
    for d in range(1, n_back + 1):
        @pl.when(qi >= d)
        def _(d=d):
            attend(BR_WIN, pl.multiple_of(t0 - d * TK, TK), MASK_WINDOW_EDGE if d == n_back else None)

    def sel_body(kb, carry):
        attend(BR_SEL, pl.multiple_of(kb * TK, TK), None)
        return carry

    lax.fori_loop(0, qi, sel_body, 0)
    attend(BR_SEL, t0, MASK_CAUSAL)

    gates_t = jax.nn.sigmoid(zs_ref[...]).T
    for hp in range(NSA_H // 2):
        comb = []
        for e in range(2):
            h = 2 * hp + e
            c0 = ZS_GATE + NSA_NB * h
            pair = (h // NSA_R) * 2 + e
            cols = slice(((h % NSA_R) // 2) * tq, ((h % NSA_R) // 2 + 1) * tq)
            sel, win = BR_SEL * n_pairs + pair, BR_WIN * n_pairs + pair
            o_sel = acc_ref[sel, 0:NSA_DH, cols] * (1.0 / acc_ref[sel, NSA_DH:NSA_DH + 1, cols])
            o_win = acc_ref[win, 0:NSA_DH, cols] * (1.0 / acc_ref[win, NSA_DH:NSA_DH + 1, cols])
            comb.append(gates_t[c0:c0 + 1] * ocmp_ref[pair, :, cols] + gates_t[c0 + 1:c0 + 2] * o_sel
                        + gates_t[c0 + 2:c0 + 3] * o_win)
        out_ref[:, hp * LANES:(hp + 1) * LANES] = jnp.concatenate(comb, axis=0).T.astype(BF16)


def _overlap_t(S):
    c = np.arange(S // CMP_STRIDE)[None, :]
    j = np.arange(S // SEL_BLOCK)[:, None]
    ov = (c * CMP_STRIDE <= j * SEL_BLOCK + SEL_BLOCK - 1) & (c * CMP_STRIDE + CMP_BLOCK - 1 >= j * SEL_BLOCK)
    ov &= c < (S - CMP_BLOCK) // CMP_STRIDE + 1
    return jnp.asarray(ov.astype(np.float32), dtype=BF16)


def _nsa(zm, zs, kaug, vt, kcaug, vct, S):
    M = zm.shape[0]
    B = M // S
    tq = min(TQ, S)
    assert tq == TK and WINDOW % TK == 0 and S % TK == 0
    nq = S // tq
    n = S // CMP_STRIDE
    nsel = S // SEL_BLOCK
    return pl.pallas_call(
        _nsa_kernel,
        grid=(B, nq),
        in_specs=[pl.BlockSpec((tq, NSA_W), lambda b, i: (b * nq + i, ZM_NQ // NSA_W)),
                  pl.BlockSpec((1, 8, S, LANES), lambda b, i: (b, 0, 0, 0)),
                  pl.BlockSpec((1, 2, NSA_G, VT_ROWS, S), lambda b, i: (b, 0, 0, 0, 0)),
                  pl.BlockSpec((1, 4, n, LANES), lambda b, i: (b, 0, 0, 0)),
                  pl.BlockSpec((1, LANES, n), lambda b, i: (b, 0, 0)),
                  pl.BlockSpec((tq, ZS_W), lambda b, i: (b * nq + i, 0)),
                  pl.BlockSpec((nsel, n), lambda b, i: (0, 0))],
        out_specs=pl.BlockSpec((tq, NSA_W), lambda b, i: (b * nq + i, 0)),
        out_shape=jax.ShapeDtypeStruct((M, NSA_W), BF16),
        scratch_shapes=[pltpu.VMEM((NSA_H // 2, LANES, 2 * tq), BF16),
                        pltpu.VMEM((NSA_H // 2, NSA_DH, 2 * tq), F32),
                        pltpu.VMEM((NSA_H, 1, 2 * tq), F32),
                        pltpu.VMEM((NSA_H, VT_ROWS, 2 * tq), F32)],
        compiler_params=_cparams(("arbitrary", "arbitrary")),
        name="nsa_attention",
    )(zm, kaug, vt, kcaug, vct, zs, _overlap_t(S))


def _merge_kernel(a_ref, b_ref, c_ref, ga_ref, gb_ref, gc_ref, x_ref, gt_ref, wa_ref, wb_ref, wc_ref, wo_ref,
                  o_ref):
    merged = (jax.nn.sigmoid(ga_ref[...].astype(F32)) * _dot(a_ref[...], wa_ref[...])
              + jax.nn.sigmoid(gb_ref[...].astype(F32)) * _dot(b_ref[...], wb_ref[...])
              + jax.nn.sigmoid(gc_ref[...].astype(F32)) * _dot(c_ref[...], wc_ref[...]))
    o_ref[...] = x_ref[...] + gt_ref[0] * _dot(merged.astype(BF16), wo_ref[...])


def _merge(ya, yb, yc, zm, x2, gt, wa, wb, wc, wo, S):
    M, D = x2.shape
    tm = min(TM_MERGE, S)
    per_b = S // tm

    def rows(w, jcol=0):
        return pl.BlockSpec((tm, w), lambda i: (i, jcol))

    def full(shape):
        return pl.BlockSpec(shape, lambda i: (0,) * len(shape))

    return pl.pallas_call(
        _merge_kernel,
        grid=(M // tm,),
        in_specs=[rows(GM_W), rows(ML_W), rows(NSA_W),
                  rows(D, ZM_GA // D), rows(D, ZM_GBR // D), rows(D, ZM_GC // D),
                  rows(D), pl.BlockSpec((1, 1, D), lambda i: (i // per_b, 0, 0)),
                  full((GM_W, D)), full((ML_W, D)), full((NSA_W, D)), full((D, D))],
        out_specs=rows(D),
        out_shape=jax.ShapeDtypeStruct((M, D), F32),
        compiler_params=_cparams(("arbitrary",)),
        name="merge_out",
    )(ya, yb, yc, zm, zm, zm, x2, gt, wa, wb, wc, wo)


def _mlp_kernel(x_ref, g_ref, sc_ref, sh_ref, gt_ref, w1_ref, w2_ref, gf_ref, o_ref, acc_ref, *, final_norm):
    x = x_ref[...]
    h = _modulated_norm(x, g_ref[...], sc_ref[0], sh_ref[0]).astype(BF16)
    for c in range(D_FF // FF_CHUNK):
        mid = jnp.square(jnp.maximum(_dot(h, w1_ref[:, c * FF_CHUNK:(c + 1) * FF_CHUNK]), 0.0)).astype(BF16)
        upd = _dot(mid, w2_ref[c * FF_CHUNK:(c + 1) * FF_CHUNK, :])
        if c == 0:
            acc_ref[...] = upd
        else:
            acc_ref[...] += upd
    y = x + gt_ref[0] * acc_ref[...]
    if final_norm:
        y = (y * lax.rsqrt(jnp.mean(y * y, axis=-1, keepdims=True) + EPS)) * gf_ref[...]
    o_ref[...] = y


def _mlp(x2, g, sc, sh, gt, w1, w2, g_final, S, final_norm):
    M, D = x2.shape
    tm = min(TM_MLP, S)
    per_b = S // tm
    mod = pl.BlockSpec((1, 1, D), lambda i: (i // per_b, 0, 0))
    return pl.pallas_call(
        functools.partial(_mlp_kernel, final_norm=final_norm),
        grid=(M // tm,),
        in_specs=[pl.BlockSpec((tm, D), lambda i: (i, 0)),
                  pl.BlockSpec((1, D), lambda i: (0, 0)),
                  mod, mod, mod,
                  pl.BlockSpec((D, D_FF), lambda i: (0, 0)),
                  pl.BlockSpec((D_FF, D), lambda i: (0, 0)),
                  pl.BlockSpec((1, D), lambda i: (0, 0))],
        out_specs=pl.BlockSpec((tm, D), lambda i: (i, 0)),
        out_shape=jax.ShapeDtypeStruct((M, D), F32),
        scratch_shapes=[pltpu.VMEM((tm, D), F32)],
        compiler_params=_cparams(("arbitrary",)),
        name="relu2_mlp",
    )(x2, g, sc, sh, gt, w1, w2, g_final)


def _w_cols(w, name):
    return w[:, _OFFS[name]:_OFFS[name] + _SIZE[name]]


def _dup_heads(w):
    parts = []
    for g in range(NSA_G):
        blk = w[:, g * NSA_DH:(g + 1) * NSA_DH]
        parts += [blk, blk]
    return jnp.concatenate(parts, axis=1)


def _proj_weights(w):
    D = w.shape[0]
    wm = jnp.concatenate([_w_cols(w, n) for n in ('ga', 'gbr', 'gc', 'gu', 'gv', 'mq', 'mk', 'mv', 'mo', 'nq')]
                         + [_dup_heads(_w_cols(w, 'nks')), _dup_heads(_w_cols(w, 'nkw')),
                            _w_cols(w, 'nvs'), _w_cols(w, 'nvw')], axis=1)
    wc = jnp.concatenate([_w_cols(w, 'nkc'), _w_cols(w, 'nvc')], axis=1)
    ws = jnp.concatenate([_w_cols(w, 'ngate'), _w_cols(w, 'mi'), _w_cols(w, 'mf'),
                          jnp.zeros((D, ZS_W - NSA_H * NSA_NB - 2 * ML_H), w.dtype)], axis=1)
    return wm.astype(BF16), wc.astype(BF16), ws.astype(BF16)


def kernel(x, c, g_norm1, g_norm2, w_ada, b_ada, w_in, gm_ln_g, gm_ln_b, gm_ws, gm_bs, ml_conv_w, ml_conv_b,
           ml_gate_b, ml_norm_g, nsa_pe_k, nsa_pe_v, nsa_phi_k1, nsa_phi_k2, nsa_phi_v1, nsa_phi_v2,
           w_up_a, w_up_b, w_up_c, w_out, w_mlp1, w_mlp2, g_final):
    B, S, D = x.shape
    depth = w_in.shape[0]
    M = B * S
    mod = _ada(c, w_ada, b_ada)
    x2 = x.reshape(M, D)
    for l in range(depth):
        sh1, sc1, gt1, sh2, sc2, gt2 = [mod[l, :, i * D:(i + 1) * D].reshape(B, 1, D) for i in range(6)]
        wm, wc, ws = _proj_weights(w_in[l])
        zm, zc, zs = _inproj(x2, g_norm1[l].reshape(1, D), sc1, sh1, wm, wc, ws, S)
        ya = _gmlp(zm, gm_ln_g[l], gm_ln_b[l], gm_ws[l], gm_bs[l], S)
        yb = _mlstm(zm, zs, ml_conv_w[l], ml_conv_b[l], ml_gate_b[l], ml_norm_g[l], S)
        kaug, vt, kcaug, vct = _nsa_prep(zc, zm, nsa_pe_k[l], nsa_pe_v[l], nsa_phi_k1[l], nsa_phi_k2[l],
                                         nsa_phi_v1[l], nsa_phi_v2[l], S)
        yc = _nsa(zm, zs, kaug, vt, kcaug, vct, S)
        x2 = _merge(ya, yb, yc, zm, x2, gt1, w_up_a[l].astype(BF16), w_up_b[l].astype(BF16),
                    w_up_c[l].astype(BF16), w_out[l].astype(BF16), S)
        x2 = _mlp(x2, g_norm2[l].reshape(1, D), sc2, sh2, gt2, w_mlp1[l].astype(BF16), w_mlp2[l].astype(BF16),
                  g_final.reshape(1, D), S, final_norm=(l == depth - 1))
    return x2.reshape(B, S, D)
```

```python
import functools

import numpy as np
import jax
import jax.numpy as jnp
from jax import lax
from jax.experimental import pallas as pl
from jax.experimental.pallas import tpu as pltpu

F32 = jnp.float32
BF16 = jnp.bfloat16

D_MODEL = 1024
GM_W = 512
GM_GROUPS = 4
GM_CHUNK = 128
ML_H = 4
ML_DH = 128
ML_W = ML_H * ML_DH
CONV_K = 4
NSA_H = 8
NSA_G = 2
NSA_R = NSA_H // NSA_G
NSA_DH = 64
NSA_W = NSA_H * NSA_DH
NSA_KV = NSA_G * NSA_DH
NSA_NB = 3
CMP_BLOCK = 32
CMP_STRIDE = 16
SEL_BLOCK = 64
TOP_N = 8
WINDOW = 512
D_FF = 4 * D_MODEL
EPS = 1e-6
NEG = -1e30
BIG = 1e4
SPLIT_SIZES = (GM_W, GM_W, ML_W, ML_W, ML_W, ML_W, ML_H, ML_H, NSA_W, NSA_KV, NSA_KV, NSA_KV, NSA_KV,
               NSA_KV, NSA_KV, NSA_H * NSA_NB, D_MODEL, D_MODEL, D_MODEL)
SPLIT_NAMES = ('gu', 'gv', 'mq', 'mk', 'mv', 'mo', 'mi', 'mf', 'nq', 'nkc', 'nvc', 'nks', 'nvs', 'nkw', 'nvw',
               'ngate', 'ga', 'gbr', 'gc')
_OFFS = dict(zip(SPLIT_NAMES, np.concatenate([[0], np.cumsum(SPLIT_SIZES)[:-1]]).tolist()))
_SIZE = dict(zip(SPLIT_NAMES, SPLIT_SIZES))

LANES = 128
VMEM_LIMIT = 56 * 1024 * 1024

ZM_GA, ZM_GBR, ZM_GC = 0, 1024, 2048
ZM_GU, ZM_GV = 3072, 3584
ZM_MQ, ZM_MK, ZM_MV, ZM_MO = 4096, 4608, 5120, 5632
ZM_NQ = 6144
ZM_KK = 6656
ZM_VV = 7168
ZM_W = 7424
KK_W = 4 * LANES
VV_W = 2 * LANES
BR_SEL, BR_WIN = 0, 1
MASK_CAUSAL, MASK_WINDOW_EDGE = 0, 1
FEAT_SEL = 0
FEAT_HI = 32
FEAT_LO = 33
ZS_GATE = 0
ZS_MI = 24
ZS_MF = 28
ZS_W = 128

TM_PROJ = 1024
TN_PROJ = 3712
TS_GMLP = 512
ML_CHUNK = 128
CONV_BLK = 256
CONV_HALO = 16
TQ = 256
TK = 256
SCORE_LOOKAHEAD = 2
VT_ROWS = NSA_DH + 16
TM_MERGE = 512
TM_MLP = 512
FF_CHUNK = 1024


def _dot(a, b):
    return jnp.dot(a, b, preferred_element_type=F32)


def _split3(x):
    x1 = x.astype(BF16)
    r1 = x - x1.astype(F32)
    x2 = r1.astype(BF16)
    x3 = (r1 - x2.astype(F32)).astype(BF16)
    return x1, x2, x3


def _cparams(sem):
    return pltpu.CompilerParams(dimension_semantics=sem, vmem_limit_bytes=VMEM_LIMIT)


def _ada_kernel(c_ref, w_ref, b_ref, o_ref):
    c = c_ref[...]
    cond = c * jax.nn.sigmoid(c)
    c1, c2, c3 = _split3(cond)
    w1, w2, w3 = _split3(w_ref[0])
    acc = _dot(c1, w1) + (_dot(c1, w2) + _dot(c2, w1)) + (_dot(c1, w3) + _dot(c2, w2) + _dot(c3, w1))
    o_ref[0] = acc + b_ref[0]


def _ada(c, w_ada, b_ada):
    L, D, N = w_ada.shape
    B = c.shape[0]
    tn = 1536
    return pl.pallas_call(
        _ada_kernel,
        grid=(L, N // tn),
        in_specs=[pl.BlockSpec((B, D), lambda l, j: (0, 0)),
                  pl.BlockSpec((1, D, tn), lambda l, j: (l, 0, j)),
                  pl.BlockSpec((1, 1, tn), lambda l, j: (l, 0, j))],
        out_specs=pl.BlockSpec((1, B, tn), lambda l, j: (l, 0, j)),
        out_shape=jax.ShapeDtypeStruct((L, B, N), F32),
        compiler_params=_cparams(("arbitrary", "arbitrary")),
        name="ada_mod",
    )(c, w_ada, b_ada.reshape(L, 1, N))


def _modulated_norm(x, g, sc, sh):
    y = x * lax.rsqrt(jnp.mean(x * x, axis=-1, keepdims=True) + EPS)
    return (y * g) * (1.0 + sc) + sh


def _inproj_kernel(x_ref, g_ref, sc_ref, sh_ref, wm_ref, wc_ref, ws_ref, zm_ref, zc_ref, zs_ref, h_ref):
    @pl.when(pl.program_id(1) == 0)
    def _():
        h = _modulated_norm(x_ref[...], g_ref[...], sc_ref[0], sh_ref[0]).astype(BF16)
        h_ref[...] = h
        zc = _dot(h, wc_ref[...])
        for part in range(zc_ref.shape[0]):
            zc_ref[part] = zc[:, part * LANES:(part + 1) * LANES]
        zs_ref[...] = _dot(h, ws_ref[...])

    zm_ref[...] = _dot(h_ref[...], wm_ref[...]).astype(BF16)


def _inproj(x2, g, sc, sh, wm, wc, ws, S):
    M, D = x2.shape
    tm, tn = min(TM_PROJ, S), TN_PROJ
    per_b = S // tm
    return pl.pallas_call(
        _inproj_kernel,
        grid=(M // tm, ZM_W // tn),
        in_specs=[pl.BlockSpec((tm, D), lambda i, j: (i, 0)),
                  pl.BlockSpec((1, D), lambda i, j: (0, 0)),
                  pl.BlockSpec((1, 1, D), lambda i, j: (i // per_b, 0, 0)),
                  pl.BlockSpec((1, 1, D), lambda i, j: (i // per_b, 0, 0)),
                  pl.BlockSpec((D, tn), lambda i, j: (0, j)),
                  pl.BlockSpec((D, 2 * NSA_KV), lambda i, j: (0, 0)),
                  pl.BlockSpec((D, ZS_W), lambda i, j: (0, 0))],
        out_specs=[pl.BlockSpec((tm, tn), lambda i, j: (i, j)),
                   pl.BlockSpec((2 * NSA_KV // LANES, tm, LANES), lambda i, j: (0, i, 0)),
                   pl.BlockSpec((tm, ZS_W), lambda i, j: (i, 0))],
        out_shape=[jax.ShapeDtypeStruct((M, ZM_W), BF16),
                   jax.ShapeDtypeStruct((2 * NSA_KV // LANES, M, LANES), F32),
                   jax.ShapeDtypeStruct((M, ZS_W), F32)],
        scratch_shapes=[pltpu.VMEM((tm, D), BF16)],
        compiler_params=_cparams(("arbitrary", "arbitrary")),
        name="in_proj",
    )(x2, g, sc, sh, wm, wc, ws)


def _gmlp_kernel(u_ref, v_ref, lng_ref, lnb_ref, ws_ref, bst_ref, o_ref):
    ts = u_ref.shape[0]
    dg = GM_W // GM_GROUPS
    row = lax.broadcasted_iota(jnp.int32, (GM_CHUNK, GM_CHUNK), 0)
    col = lax.broadcasted_iota(jnp.int32, (GM_CHUNK, GM_CHUNK), 1)
    ws = [jnp.where(row >= col, ws_ref[g], 0.0).astype(BF16) for g in range(GM_GROUPS)]
    lng = lng_ref[...]
    lnb = lnb_ref[...]
    for c in range(ts // GM_CHUNK):
        r0 = c * GM_CHUNK
        u = jax.nn.gelu(u_ref[r0:r0 + GM_CHUNK, :].astype(F32))
        v = jax.nn.gelu(v_ref[r0:r0 + GM_CHUNK, :].astype(F32))
        mu = jnp.mean(v, axis=-1, keepdims=True)
        var = jnp.mean(jnp.square(v - mu), axis=-1, keepdims=True)
        vb = ((v - mu) * lax.rsqrt(var + EPS) * lng + lnb).astype(BF16)
        for g in range(GM_GROUPS):
            mixed = _dot(ws[g], vb[:, g * dg:(g + 1) * dg]) + bst_ref[:, g:g + 1]
            o_ref[r0:r0 + GM_CHUNK, g * dg:(g + 1) * dg] = (u[:, g * dg:(g + 1) * dg] * mixed).astype(BF16)


def _gmlp(zm, ln_g, ln_b, ws, bs, S):
    M = zm.shape[0]
    ts = min(TS_GMLP, S)
    return pl.pallas_call(
        _gmlp_kernel,
        grid=(M // ts,),
        in_specs=[pl.BlockSpec((ts, GM_W), lambda i: (i, ZM_GU // GM_W)),
                  pl.BlockSpec((ts, GM_W), lambda i: (i, ZM_GV // GM_W)),
                  pl.BlockSpec((1, GM_W), lambda i: (0, 0)),
                  pl.BlockSpec((1, GM_W), lambda i: (0, 0)),
                  pl.BlockSpec((GM_GROUPS, GM_CHUNK, GM_CHUNK), lambda i: (0, 0, 0)),
                  pl.BlockSpec((GM_CHUNK, GM_GROUPS), lambda i: (0, 0))],
        out_specs=pl.BlockSpec((ts, GM_W), lambda i: (i, 0)),
        out_shape=jax.ShapeDtypeStruct((M, GM_W), BF16),
        compiler_params=_cparams(("arbitrary",)),
        name="gmlp_mixer",
    )(zm, zm, ln_g.reshape(1, GM_W), ln_b.reshape(1, GM_W), ws, bs.T)


def _log_sigmoid(x):
    return jnp.minimum(x, 0.0) - jnp.log1p(jnp.exp(-jnp.abs(x)))


def _conv_silu(x_ext, w, b):
    n = x_ext.shape[0] - CONV_HALO
    y = b
    for j in range(CONV_K):
        sh = CONV_K - 1 - j
        xs = x_ext if sh == 0 else pltpu.roll(x_ext, sh, axis=0)
        y = y + xs[CONV_HALO:CONV_HALO + n] * w[j:j + 1]
    return y * jax.nn.sigmoid(y)


def _mlstm_kernel(q_ref, k_ref, v_ref, o_ref, zs_ref, cw_ref, cb_ref, gb_ref, ng_ref, out_ref,
                  qc_ref, kt_ref, c_ref):
    S = q_ref.shape[0]
    L = min(ML_CHUNK, S)
    blk = min(CONV_BLK, S)

    for src, c0, is_key in ((q_ref, 0, False), (k_ref, ML_W, True)):
        w = cw_ref[:, c0:c0 + ML_W]
        b = cb_ref[:, c0:c0 + ML_W]

        def conv_block(x_ext, r0, w=w, b=b, is_key=is_key):
            y = _conv_silu(x_ext, w, b)
            if is_key:
                y = y * (ML_DH ** -0.5)
                for h in range(ML_H):
                    kt_ref[h * ML_DH:(h + 1) * ML_DH, pl.ds(r0, blk)] = y[:, h * ML_DH:(h + 1) * ML_DH].T.astype(BF16)
            else:
                qc_ref[pl.ds(r0, blk), :] = y.astype(BF16)

        first = jnp.concatenate([jnp.zeros((CONV_HALO, ML_W), F32), src[0:blk, :].astype(F32)], axis=0)
        conv_block(first, 0)

        def conv_body(i, carry, src=src, conv_block=conv_block):
            r0 = pl.multiple_of(i * blk, blk)
            conv_block(src[pl.ds(r0 - CONV_HALO, blk + CONV_HALO), :].astype(F32), r0)
            return carry

        lax.fori_loop(1, S // blk, conv_body, 0)

    c_ref[...] = jnp.zeros_like(c_ref)
    row = lax.broadcasted_iota(jnp.int32, (L, L), 0)
    col = lax.broadcasted_iota(jnp.int32, (L, L), 1)
    tril = row >= col
    tril_b = jnp.where(tril, 1.0, 0.0).astype(BF16)
    triu_b = jnp.where(row <= col, 1.0, 0.0).astype(BF16)
    ones_blk = jnp.ones((L, ML_DH), BF16)
    time_row = lax.broadcasted_iota(jnp.int32, (L, ZS_W), 0)
    gb = gb_ref[...]

    def chunk_body(c, m_row):
        t0 = pl.multiple_of(c * L, L)
        gi = zs_ref[pl.ds(t0, L), :] + gb
        lf_cols = _log_sigmoid(gi)
        gi_t = gi.T
        lf_rows = _log_sigmoid(gi_t[ZS_MI:ZS_MI + 2 * ML_H])
        c1, c2, c3 = _split3(lf_cols)
        b_cols = _dot(tril_b, c1) + _dot(tril_b, c2) + _dot(tril_b, c3)
        r1, r2, r3 = _split3(lf_rows)
        b_rows = _dot(r1, triu_b) + _dot(r2, triu_b) + _dot(r3, triu_b)
        r_rows = gi_t[ZS_MI:ZS_MI + ML_H] - b_rows[ML_H:2 * ML_H]
        b_at_i = pltpu.roll(b_cols, ZS_W - (ZS_MF - ZS_MI), axis=1)
        u_cols = gi - b_at_i
        shift = 1
        while shift < L:
            u_cols = jnp.maximum(u_cols, jnp.where(time_row >= shift, pltpu.roll(u_cols, shift, axis=0), NEG))
            shift *= 2
        u_cols = jnp.maximum(u_cols, m_row)
        m_cols = b_at_i + u_cols
        u_last = u_cols[L - 1:L, :]
        w_prev_row = jnp.exp(m_row - u_last)
        heads = range(ML_H)
        hs = [slice(h * ML_DH, (h + 1) * ML_DH) for h in heads]
        q = [qc_ref[pl.ds(t0, L), hs[h]] for h in heads]
        k_t = [kt_ref[hs[h], pl.ds(t0, L)] for h in heads]
        cmat = [c_ref[h] for h in heads]
        qk = [_dot(q[h], k_t[h]) for h in heads]
        qc = [_dot(q[h], cmat[h].astype(BF16)) for h in heads]
        v_aug = [jnp.concatenate([v_ref[pl.ds(t0, L), hs[h]], ones_blk], axis=1) for h in heads]
        u_rep, gate = [], []
        for h in heads:
            lane_h = ZS_MI + h
            u_rep.append(jnp.broadcast_to(u_cols[:, lane_h:lane_h + 1], (L, ML_DH)))
            gate.append(jnp.exp(jnp.where(tril, r_rows[h:h + 1, :] - u_rep[h], NEG)))
            w_s = jnp.exp(r_rows[h:h + 1, :] - u_last[:, lane_h:lane_h + 1])
            k_w = (k_t[h].astype(F32) * w_s).astype(BF16)
            c_ref[h] = w_prev_row[:, lane_h:lane_h + 1] * cmat[h] + _dot(k_w, v_aug[h])
        for h in heads:
            lane_h = ZS_MI + h
            w_inter = jnp.exp(m_row[:, lane_h:lane_h + 1] - u_rep[h])
            num_aug = _dot((qk[h] * gate[h]).astype(BF16), v_aug[h]) + jnp.concatenate([w_inter, w_inter], 1) * qc[h]
            m_rep = jnp.broadcast_to(m_cols[:, lane_h:lane_h + 1], (L, ML_DH))
            hval = num_aug[:, :ML_DH] / jnp.maximum(jnp.abs(num_aug[:, ML_DH:]), jnp.exp(-m_rep))
            mu = jnp.mean(hval, axis=-1, keepdims=True)
            var = jnp.mean(jnp.square(hval - mu), axis=-1, keepdims=True)
            hn = (hval - mu) * lax.rsqrt(var + EPS) * ng_ref[:, hs[h]]
            og = jax.nn.sigmoid(o_ref[pl.ds(t0, L), hs[h]].astype(F32))
            out_ref[pl.ds(t0, L), hs[h]] = (og * hn).astype(BF16)
        return m_cols[L - 1:L, :]

    lax.fori_loop(0, S // L, chunk_body, jnp.zeros((1, ZS_W), F32))


def _mlstm(zm, zs, conv_w, conv_b, gate_b, norm_g, S):
    M = zm.shape[0]
    gb_row = jnp.zeros((1, ZS_W), F32).at[0, ZS_MI:ZS_MI + 2 * ML_H].set(gate_b)

    def col(off):
        return pl.BlockSpec((S, ML_W), lambda b: (b, off // ML_W))

    return pl.pallas_call(
        _mlstm_kernel,
        grid=(M // S,),
        in_specs=[col(ZM_MQ), col(ZM_MK), col(ZM_MV), col(ZM_MO),
                  pl.BlockSpec((S, ZS_W), lambda b: (b, 0)),
                  pl.BlockSpec((CONV_K, 2 * ML_W), lambda b: (0, 0)),
                  pl.BlockSpec((1, 2 * ML_W), lambda b: (0, 0)),
                  pl.BlockSpec((1, ZS_W), lambda b: (0, 0)),
                  pl.BlockSpec((1, ML_W), lambda b: (0, 0))],
        out_specs=pl.BlockSpec((S, ML_W), lambda b: (b, 0)),
        out_shape=jax.ShapeDtypeStruct((M, ML_W), BF16),
        scratch_shapes=[pltpu.VMEM((S, ML_W), BF16), pltpu.VMEM((ML_W, S), BF16),
                        pltpu.VMEM((ML_H, ML_DH, 2 * ML_DH), F32)],
        compiler_params=_cparams(("arbitrary",)),
        name="mlstm_mixer",
    )(zm, zm, zm, zm, zs, conv_w, conv_b.reshape(1, 2 * ML_W), gb_row, norm_g.reshape(1, ML_W))


def _nsa_prep_kernel(x_ref, pe_ref, w1_ref, w2_ref, kk_ref, vv_ref, feat_ref, featc_ref,
                     kaug_ref, vt_ref, kcaug_ref, vct_ref):
    S = kk_ref.shape[0]
    n = S // CMP_STRIDE
    cw = 2 * NSA_KV
    half = CMP_BLOCK // 2
    acc_a = jnp.zeros((n, cw), F32)
    acc_b = jnp.zeros((n, cw), F32)
    for j in range(half):
        xj = jnp.concatenate([x_ref[part, pl.ds(j, n, stride=CMP_STRIDE), :] for part in range(cw // LANES)],
                             axis=1)
        acc_a = acc_a + _dot((xj + pe_ref[j:j + 1, :]).astype(BF16), w1_ref[j])
        acc_b = acc_b + _dot((xj + pe_ref[half + j:half + j + 1, :]).astype(BF16), w1_ref[half + j])
    pre = acc_a + pltpu.roll(acc_b, n - 1, axis=0)
    cmp = _dot(jax.nn.gelu(pre).astype(BF16), w2_ref[...])

    lane = lax.broadcasted_iota(jnp.int32, (1, LANES), 1)
    keep = [jnp.where(lane < NSA_DH, 1.0, 0.0).astype(BF16), jnp.where(lane >= NSA_DH, 1.0, 0.0).astype(BF16)]
    for g in range(NSA_G):
        kd = cmp[:, g * LANES:(g + 1) * LANES].astype(BF16)
        for e in range(2):
            kcaug_ref[0, g * 2 + e] = kd * keep[e] + featc_ref[e]
    vct_ref[0] = cmp[:, NSA_G * LANES:(NSA_G + 1) * LANES].T.astype(BF16)
    tb = min(TK, S)
    for br in range(2):
        for g in range(NSA_G):
            kd = kk_ref[:, (br * NSA_G + g) * LANES:(br * NSA_G + g + 1) * LANES]
            for e in range(2):
                kaug_ref[0, (br * NSA_G + g) * 2 + e] = kd * keep[e] + feat_ref[br * 2 + e]
        for c in range(S // tb):
            v_t = vv_ref[c * tb:(c + 1) * tb, br * LANES:(br + 1) * LANES].astype(F32).T.astype(BF16)
            for g in range(NSA_G):
                vt_ref[0, br, g, 0:NSA_DH, c * tb:(c + 1) * tb] = v_t[g * NSA_DH:(g + 1) * NSA_DH]
                vt_ref[0, br, g, NSA_DH:VT_ROWS, c * tb:(c + 1) * tb] = jnp.ones((VT_ROWS - NSA_DH, tb), BF16)


def _blockdiag(blocks):
    n = len(blocks)
    rows = []
    for i, blk in enumerate(blocks):
        rows.append(jnp.concatenate([blk if j == i else jnp.zeros((blk.shape[0], blocks[j].shape[1]), blk.dtype)
                                     for j in range(n)], axis=1))
    return jnp.concatenate(rows, axis=0)


def _key_features(S):
    pos = np.arange(S)
    f = np.zeros((4, S, LANES), np.float32)
    for e in range(2):
        base = NSA_DH * (1 - e)
        f[e, pos, base + FEAT_SEL + pos // SEL_BLOCK] = 1.0
        for br in range(2):
            f[br * 2 + e, :, base + FEAT_HI] = pos // 64
            f[br * 2 + e, :, base + FEAT_LO] = pos % 64
    return jnp.asarray(f, dtype=BF16)


def _cmp_features(S):
    c = np.arange(S // CMP_STRIDE)
    hi = (c * CMP_STRIDE) // 64
    lo = c * CMP_STRIDE + (CMP_BLOCK - 1) * 0.5 - 64 * hi
    f = np.zeros((2, c.size, LANES), np.float32)
    for e in range(2):
        base = NSA_DH * (1 - e)
        f[e, :, base + FEAT_HI] = hi
        f[e, :, base + FEAT_LO] = lo
    return jnp.asarray(f, dtype=BF16)


def _nsa_prep(zc, zm, pe_k, pe_v, phi_k1, phi_k2, phi_v1, phi_v2, S):
    M = zc.shape[1]
    B = M // S
    n = S // CMP_STRIDE
    cw = 2 * NSA_KV
    assert S // SEL_BLOCK <= FEAT_HI and S <= 64 * 64
    pe = jnp.concatenate([pe_k, pe_k, pe_v, pe_v], axis=1)
    k1 = phi_k1.reshape(CMP_BLOCK, NSA_DH, NSA_DH)
    v1 = phi_v1.reshape(CMP_BLOCK, NSA_DH, NSA_DH)
    w1 = jnp.stack([_blockdiag([k1[j], k1[j], v1[j], v1[j]]) for j in range(CMP_BLOCK)]).astype(BF16)
    k2 = jnp.concatenate([phi_k2, phi_k2], axis=1)
    w2 = _blockdiag([k2, k2, phi_v2, phi_v2]).astype(BF16)

    def full(shape):
        return pl.BlockSpec(shape, lambda b: (0,) * len(shape))

    return pl.pallas_call(
        _nsa_prep_kernel,
        grid=(B,),
        in_specs=[pl.BlockSpec((cw // LANES, S, LANES), lambda b: (0, b, 0)),
                  full((CMP_BLOCK, cw)), full((CMP_BLOCK, cw, cw)), full((cw, 3 * LANES)),
                  pl.BlockSpec((S, KK_W), lambda b: (b, ZM_KK // KK_W)),
                  pl.BlockSpec((S, VV_W), lambda b: (b, ZM_VV // VV_W)),
                  full((4, S, LANES)), full((2, n, LANES))],
        out_specs=[pl.BlockSpec((1, 8, S, LANES), lambda b: (b, 0, 0, 0)),
                   pl.BlockSpec((1, 2, NSA_G, VT_ROWS, S), lambda b: (b, 0, 0, 0, 0)),
                   pl.BlockSpec((1, 4, n, LANES), lambda b: (b, 0, 0, 0)),
                   pl.BlockSpec((1, LANES, n), lambda b: (b, 0, 0))],
        out_shape=[jax.ShapeDtypeStruct((B, 8, S, LANES), BF16),
                   jax.ShapeDtypeStruct((B, 2, NSA_G, VT_ROWS, S), BF16),
                   jax.ShapeDtypeStruct((B, 4, n, LANES), BF16),
                   jax.ShapeDtypeStruct((B, LANES, n), BF16)],
        compiler_params=_cparams(("arbitrary",)),
        name="nsa_prep",
    )(zc, pe, w1, w2, zm, zm, _key_features(S), _cmp_features(S))


def _nsa_kernel(q_ref, kaug_ref, vt_ref, kcaug_ref, vct_ref, zs_ref, ovt_ref, out_ref,
                qat_ref, ocmp_ref, m_ref, acc_ref):
    S = kaug_ref.shape[2]
    tq = q_ref.shape[0]
    ncmp = kcaug_ref.shape[2]
    nsel = S // SEL_BLOCK
    n_pairs = NSA_H // 2
    qi = pl.program_id(1)
    t0 = pl.multiple_of(qi * tq, tq)
    t_row = t0 + lax.broadcasted_iota(jnp.int32, (1, tq), 1)

    for k in range(NSA_H // 2):
        q_t = (q_ref[:, k * LANES:(k + 1) * LANES].astype(F32) * (NSA_DH ** -0.5)).T.astype(BF16)
        for e in range(2):
            qat_ref[(k // 2) * 2 + e, e * NSA_DH:(e + 1) * NSA_DH, (k % 2) * tq:(k % 2 + 1) * tq] = (
                q_t[e * NSA_DH:(e + 1) * NSA_DH])
    feat_row = lax.broadcasted_iota(jnp.int32, (NSA_DH, 2 * tq), 0)
    second_head = lax.broadcasted_iota(jnp.int32, (NSA_DH, 2 * tq), 1) >= tq
    for pair in range(n_pairs):
        g, e = pair // 2, pair % 2
        slopes = [2.0 ** (-8.0 * (g * NSA_R + 2 * s + e + 1.0) / NSA_H) for s in range(2)]
        slope = jnp.where(second_head, slopes[1], slopes[0])
        feat = jnp.where(feat_row == FEAT_HI, slope * 64.0, jnp.where(feat_row == FEAT_LO, slope, 0.0))
        qat_ref[pair, NSA_DH * (1 - e):NSA_DH * (2 - e), :] = feat.astype(BF16)

    ki = lax.broadcasted_iota(jnp.int32, (TK, 2 * tq), 0)
    qu = lax.broadcasted_iota(jnp.int32, (TK, 2 * tq), 1)
    qu = jnp.where(qu >= tq, qu - tq, qu)
    tile_valid = {MASK_CAUSAL: ki <= qu,
                  MASK_WINDOW_EDGE: ki > qu}

    c_col = lax.broadcasted_iota(jnp.int32, (ncmp, 1), 0)
    t_row2 = jnp.concatenate([t_row, t_row], axis=1)
    valid_c = c_col * CMP_STRIDE + (CMP_BLOCK - 1) <= t_row2
    j_col = lax.broadcasted_iota(jnp.int32, (nsel, 1), 0)
    jt = t_row >> 6
    forced = jnp.logical_or(j_col == 0, jnp.logical_or(j_col == jt, j_col == jt - 1))
    future = j_col > jt

    for g in range(NSA_G):
        psum = jnp.zeros((ncmp, tq), F32)
        for e in range(2):
            pair = g * 2 + e
            sc = jnp.where(valid_c, _dot(kcaug_ref[0, pair], qat_ref[pair]), NEG)
            ex = jnp.where(valid_c, jnp.exp(sc - jnp.max(sc, axis=0, keepdims=True)), 0.0)
            den = jnp.sum(ex, axis=0, keepdims=True)
            p = ex * (1.0 / jnp.where(den > 0.0, den, 1.0))
            psum = psum + p[:, :tq] + p[:, tq:]
            ocmp_ref[pair] = _dot(vct_ref[0, g * NSA_DH:(g + 1) * NSA_DH, :], p.astype(BF16))

        p_hi = psum.astype(BF16)
        p_lo = (psum - p_hi.astype(F32)).astype(BF16)
        imp = _dot(ovt_ref[...], p_hi) + _dot(ovt_ref[...], p_lo)
        val = jnp.where(future, NEG, jnp.where(forced, BIG, imp))
        rank = jnp.zeros((nsel, tq), F32)
        for i in range(nsel):
            vi = val[i:i + 1, :]
            tie = jnp.where(vi == val, 1.0, 0.0) * jnp.where(j_col > i, 1.0, 0.0)
            rank = rank + jnp.where(vi > val, 1.0, 0.0) + tie
        penalty = jnp.where(rank < float(min(TOP_N, nsel)), 0.0, NEG).astype(BF16)
        for e in range(2):
            first = NSA_DH * (1 - e) + FEAT_SEL
            qat_ref[g * 2 + e, first:first + nsel, :] = jnp.concatenate([penalty, penalty], axis=1)

    m_ref[...] = jnp.full(m_ref.shape, NEG, F32)
    acc_ref[...] = jnp.zeros(acc_ref.shape, F32)

    def attend(br, k0, mask):
        def scores(pair):
            return _dot(kaug_ref[0, br * n_pairs + pair, pl.ds(k0, TK), :], qat_ref[pair])

        s_queue = [scores(i) for i in range(SCORE_LOOKAHEAD)]
        for pair in range(n_pairs):
            s_t = s_queue.pop(0)
            if pair + SCORE_LOOKAHEAD < n_pairs:
                s_queue.append(scores(pair + SCORE_LOOKAHEAD))
            g = pair // 2
            slot = br * n_pairs + pair
            if mask is not None:
                s_t = jnp.where(tile_valid[mask], s_t, NEG)
            m_old = m_ref[slot]
            m_new = jnp.maximum(m_old, jnp.max(s_t, axis=0, keepdims=True))
            alpha = jnp.exp(m_old - m_new)
            p = jnp.exp(s_t - m_new).astype(BF16)
            acc_ref[slot] = alpha * acc_ref[slot] + _dot(vt_ref[0, br, g, :, pl.ds(k0, TK)], p)
            m_ref[slot] = m_new

    attend(BR_WIN, t0, MASK_CAUSAL)
    n_back = WINDOW // TK
    for d in range(1, n_back + 1):
        @pl.when(qi >= d)
        def _(d=d):
            attend(BR_WIN, pl.multiple_of(t0 - d * TK, TK), MASK_WINDOW_EDGE if d == n_back else None)

    def sel_body(kb, carry):
        attend(BR_SEL, pl.multiple_of(kb * TK, TK), None)
        return carry

    lax.fori_loop(0, qi, sel_body, 0)
    attend(BR_SEL, t0, MASK_CAUSAL)

    gates_t = jax.nn.sigmoid(zs_ref[...]).T
    for hp in range(NSA_H // 2):
        comb = []
        for e in range(2):
            h = 2 * hp + e
            c0 = ZS_GATE + NSA_NB * h
            pair = (h // NSA_R) * 2 + e
            cols = slice(((h % NSA_R) // 2) * tq, ((h % NSA_R) // 2 + 1) * tq)
            sel, win = BR_SEL * n_pairs + pair, BR_WIN * n_pairs + pair
            o_sel = acc_ref[sel, 0:NSA_DH, cols] * (1.0 / acc_ref[sel, NSA_DH:NSA_DH + 1, cols])
            o_win = acc_ref[win, 0:NSA_DH, cols] * (1.0 / acc_ref[win, NSA_DH:NSA_DH + 1, cols])
            comb.append(gates_t[c0:c0 + 1] * ocmp_ref[pair, :, cols] + gates_t[c0 + 1:c0 + 2] * o_sel
                        + gates_t[c0 + 2:c0 + 3] * o_win)
        out_ref[:, hp * LANES:(hp + 1) * LANES] = jnp.concatenate(comb, axis=0).T.astype(BF16)


def _overlap_t(S):
    c = np.arange(S // CMP_STRIDE)[None, :]
    j = np.arange(S // SEL_BLOCK)[:, None]
    ov = (c * CMP_STRIDE <= j * SEL_BLOCK + SEL_BLOCK - 1) & (c * CMP_STRIDE + CMP_BLOCK - 1 >= j * SEL_BLOCK)
    ov &= c < (S - CMP_BLOCK) // CMP_STRIDE + 1
    return jnp.asarray(ov.astype(np.float32), dtype=BF16)


def _nsa(zm, zs, kaug, vt, kcaug, vct, S):
    M = zm.shape[0]
    B = M // S
    tq = min(TQ, S)
    assert tq == TK and WINDOW % TK == 0 and S % TK == 0
    nq = S // tq
    n = S // CMP_STRIDE
    nsel = S // SEL_BLOCK
    return pl.pallas_call(
        _nsa_kernel,
        grid=(B, nq),
        in_specs=[pl.BlockSpec((tq, NSA_W), lambda b, i: (b * nq + i, ZM_NQ // NSA_W)),
                  pl.BlockSpec((1, 8, S, LANES), lambda b, i: (b, 0, 0, 0)),
                  pl.BlockSpec((1, 2, NSA_G, VT_ROWS, S), lambda b, i: (b, 0, 0, 0, 0)),
                  pl.BlockSpec((1, 4, n, LANES), lambda b, i: (b, 0, 0, 0)),
                  pl.BlockSpec((1, LANES, n), lambda b, i: (b, 0, 0)),
                  pl.BlockSpec((tq, ZS_W), lambda b, i: (b * nq + i, 0)),
                  pl.BlockSpec((nsel, n), lambda b, i: (0, 0))],
        out_specs=pl.BlockSpec((tq, NSA_W), lambda b, i: (b * nq + i, 0)),
        out_shape=jax.ShapeDtypeStruct((M, NSA_W), BF16),
        scratch_shapes=[pltpu.VMEM((NSA_H // 2, LANES, 2 * tq), BF16),
                        pltpu.VMEM((NSA_H // 2, NSA_DH, 2 * tq), F32),
                        pltpu.VMEM((NSA_H, 1, 2 * tq), F32),
                        pltpu.VMEM((NSA_H, VT_ROWS, 2 * tq), F32)],
        compiler_params=_cparams(("arbitrary", "arbitrary")),
        name="nsa_attention",
    )(zm, kaug, vt, kcaug, vct, zs, _overlap_t(S))


def _merge_kernel(a_ref, b_ref, c_ref, ga_ref, gb_ref, gc_ref, x_ref, gt_ref, wa_ref, wb_ref, wc_ref, wo_ref,
                  o_ref):
    merged = (jax.nn.sigmoid(ga_ref[...].astype(F32)) * _dot(a_ref[...], wa_ref[...])
              + jax.nn.sigmoid(gb_ref[...].astype(F32)) * _dot(b_ref[...], wb_ref[...])
              + jax.nn.sigmoid(gc_ref[...].astype(F32)) * _dot(c_ref[...], wc_ref[...]))
    o_ref[...] = x_ref[...] + gt_ref[0] * _dot(merged.astype(BF16), wo_ref[...])


def _merge(ya, yb, yc, zm, x2, gt, wa, wb, wc, wo, S):
    M, D = x2.shape
    tm = min(TM_MERGE, S)
    per_b = S // tm

    def rows(w, jcol=0):
        return pl.BlockSpec((tm, w), lambda i: (i, jcol))

    def full(shape):
        return pl.BlockSpec(shape, lambda i: (0,) * len(shape))

    return pl.pallas_call(
        _merge_kernel,
        grid=(M // tm,),
        in_specs=[rows(GM_W), rows(ML_W), rows(NSA_W),
                  rows(D, ZM_GA // D), rows(D, ZM_GBR // D), rows(D, ZM_GC // D),
                  rows(D), pl.BlockSpec((1, 1, D), lambda i: (i // per_b, 0, 0)),
                  full((GM_W, D)), full((ML_W, D)), full((NSA_W, D)), full((D, D))],
        out_specs=rows(D),
        out_shape=jax.ShapeDtypeStruct((M, D), F32),
        compiler_params=_cparams(("arbitrary",)),
        name="merge_out",
    )(ya, yb, yc, zm, zm, zm, x2, gt, wa, wb, wc, wo)


def _mlp_kernel(x_ref, g_ref, sc_ref, sh_ref, gt_ref, w1_ref, w2_ref, gf_ref, o_ref, acc_ref, *, final_norm):
    x = x_ref[...]
    h = _modulated_norm(x, g_ref[...], sc_ref[0], sh_ref[0]).astype(BF16)
    for c in range(D_FF // FF_CHUNK):
        mid = jnp.square(jnp.maximum(_dot(h, w1_ref[:, c * FF_CHUNK:(c + 1) * FF_CHUNK]), 0.0)).astype(BF16)
        upd = _dot(mid, w2_ref[c * FF_CHUNK:(c + 1) * FF_CHUNK, :])
        if c == 0:
            acc_ref[...] = upd
        else:
            acc_ref[...] += upd
    y = x + gt_ref[0] * acc_ref[...]
    if final_norm:
        y = (y * lax.rsqrt(jnp.mean(y * y, axis=-1, keepdims=True) + EPS)) * gf_ref[...]
    o_ref[...] = y


def _mlp(x2, g, sc, sh, gt, w1, w2, g_final, S, final_norm):
    M, D = x2.shape
    tm = min(TM_MLP, S)
    per_b = S // tm
    mod = pl.BlockSpec((1, 1, D), lambda i: (i // per_b, 0, 0))
    return pl.pallas_call(
        functools.partial(_mlp_kernel, final_norm=final_norm),
        grid=(M // tm,),
        in_specs=[pl.BlockSpec((tm, D), lambda i: (i, 0)),
                  pl.BlockSpec((1, D), lambda i: (0, 0)),
                  mod, mod, mod,
                  pl.BlockSpec((D, D_FF), lambda i: (0, 0)),
                  pl.BlockSpec((D_FF, D), lambda i: (0, 0)),
                  pl.BlockSpec((1, D), lambda i: (0, 0))],
        out_specs=pl.BlockSpec((tm, D), lambda i: (i, 0)),
        out_shape=jax.ShapeDtypeStruct((M, D), F32),
        scratch_shapes=[pltpu.VMEM((tm, D), F32)],
        compiler_params=_cparams(("arbitrary",)),
        name="relu2_mlp",
    )(x2, g, sc, sh, gt, w1, w2, g_final)


def _w_cols(w, name):
    return w[:, _OFFS[name]:_OFFS[name] + _SIZE[name]]


def _dup_heads(w):
    parts = []
    for g in range(NSA_G):
        blk = w[:, g * NSA_DH:(g + 1) * NSA_DH]
        parts += [blk, blk]
    return jnp.concatenate(parts, axis=1)


def _proj_weights(w):
    D = w.shape[0]
    wm = jnp.concatenate([_w_cols(w, n) for n in ('ga', 'gbr', 'gc', 'gu', 'gv', 'mq', 'mk', 'mv', 'mo', 'nq')]
                         + [_dup_heads(_w_cols(w, 'nks')), _dup_heads(_w_cols(w, 'nkw')),
                            _w_cols(w, 'nvs'), _w_cols(w, 'nvw')], axis=1)
    wc = jnp.concatenate([_w_cols(w, 'nkc'), _w_cols(w, 'nvc')], axis=1)
    ws = jnp.concatenate([_w_cols(w, 'ngate'), _w_cols(w, 'mi'), _w_cols(w, 'mf'),
                          jnp.zeros((D, ZS_W - NSA_H * NSA_NB - 2 * ML_H), w.dtype)], axis=1)
    return wm.astype(BF16), wc.astype(BF16), ws.astype(BF16)


def kernel(x, c, g_norm1, g_norm2, w_ada, b_ada, w_in, gm_ln_g, gm_ln_b, gm_ws, gm_bs, ml_conv_w, ml_conv_b,
           ml_gate_b, ml_norm_g, nsa_pe_k, nsa_pe_v, nsa_phi_k1, nsa_phi_k2, nsa_phi_v1, nsa_phi_v2,
           w_up_a, w_up_b, w_up_c, w_out, w_mlp1, w_mlp2, g_final):
    B, S, D = x.shape
    depth = w_in.shape[0]
    M = B * S
    mod = _ada(c, w_ada, b_ada)
    x2 = x.reshape(M, D)
    for l in range(depth):
        sh1, sc1, gt1, sh2, sc2, gt2 = [mod[l, :, i * D:(i + 1) * D].reshape(B, 1, D) for i in range(6)]
        wm, wc, ws = _proj_weights(w_in[l])
        zm, zc, zs = _inproj(x2, g_norm1[l].reshape(1, D), sc1, sh1, wm, wc, ws, S)
        ya = _gmlp(zm, gm_ln_g[l], gm_ln_b[l], gm_ws[l], gm_bs[l], S)
        yb = _mlstm(zm, zs, ml_conv_w[l], ml_conv_b[l], ml_gate_b[l], ml_norm_g[l], S)
        kaug, vt, kcaug, vct = _nsa_prep(zc, zm, nsa_pe_k[l], nsa_pe_v[l], nsa_phi_k1[l], nsa_phi_k2[l],
                                         nsa_phi_v1[l], nsa_phi_v2[l], S)
        yc = _nsa(zm, zs, kaug, vt, kcaug, vct, S)
        x2 = _merge(ya, yb, yc, zm, x2, gt1, w_up_a[l].astype(BF16), w_up_b[l].astype(BF16),
                    w_up_c[l].astype(BF16), w_out[l].astype(BF16), S)
        x2 = _mlp(x2, g_norm2[l].reshape(1, D), sc2, sh2, gt2, w_mlp1[l].astype(BF16), w_mlp2[l].astype(BF16),
                  g_final.reshape(1, D), S, final_norm=(l == depth - 1))
    return x2.reshape(B, S, D)
```

```python
import functools

import numpy as np
import jax
import jax.numpy as jnp
from jax import lax
from jax.experimental import pallas as pl
from jax.experimental.pallas import tpu as pltpu

F32 = jnp.float32
BF16 = jnp.bfloat16

D_MODEL = 1024
GM_W = 512
GM_GROUPS = 4
GM_CHUNK = 128
ML_H = 4
ML_DH = 128
ML_W = ML_H * ML_DH
CONV_K = 4
NSA_H = 8
NSA_G = 2
NSA_R = NSA_H // NSA_G
NSA_DH = 64
NSA_W = NSA_H * NSA_DH
NSA_KV = NSA_G * NSA_DH
NSA_NB = 3
CMP_BLOCK = 32
CMP_STRIDE = 16
SEL_BLOCK = 64
TOP_N = 8
WINDOW = 512
D_FF = 4 * D_MODEL
EPS = 1e-6
NEG = -1e30
BIG = 1e4
TAKEN = -3e38
SPLIT_SIZES = (GM_W, GM_W, ML_W, ML_W, ML_W, ML_W, ML_H, ML_H, NSA_W, NSA_KV, NSA_KV, NSA_KV, NSA_KV,
               NSA_KV, NSA_KV, NSA_H * NSA_NB, D_MODEL, D_MODEL, D_MODEL)
SPLIT_NAMES = ('gu', 'gv', 'mq', 'mk', 'mv', 'mo', 'mi', 'mf', 'nq', 'nkc', 'nvc', 'nks', 'nvs', 'nkw', 'nvw',
               'ngate', 'ga', 'gbr', 'gc')
_OFFS = dict(zip(SPLIT_NAMES, np.concatenate([[0], np.cumsum(SPLIT_SIZES)[:-1]]).tolist()))
_SIZE = dict(zip(SPLIT_NAMES, SPLIT_SIZES))

LANES = 128
VMEM_LIMIT = 56 * 1024 * 1024

ZM_GA, ZM_GBR, ZM_GC = 0, 1024, 2048
ZM_GU, ZM_GV = 3072, 3584
ZM_MQ, ZM_MK, ZM_MV, ZM_MO = 4096, 4608, 5120, 5632
ZM_NQ = 6144
ZM_KK = 6656
ZM_VV = 7168
ZM_W = 7424
KK_W = 4 * LANES
VV_W = 2 * LANES
BR_SEL, BR_WIN = 0, 1
MASK_CAUSAL, MASK_WINDOW_EDGE = 0, 1
FEAT_SEL = 0
FEAT_HI = 32
FEAT_LO = 33
ZS_GATE = 0
ZS_MI = 24
ZS_MF = 28
ZS_W = 128

TM_PROJ = 1024
TN_PROJ = 3712
ML_CHUNK = 128
CONV_BLK = 256
CONV_HALO = 16
TQ = 256
TK = 256
SCORE_LOOKAHEAD = 2
VT_ROWS = NSA_DH + 16
TM_MERGE = 512
TM_MLP = 512
FF_CHUNK = 1024


def _dot(a, b):
    return jnp.dot(a, b, preferred_element_type=F32)


def _split3(x):
    x1 = x.astype(BF16)
    r1 = x - x1.astype(F32)
    x2 = r1.astype(BF16)
    x3 = (r1 - x2.astype(F32)).astype(BF16)
    return x1, x2, x3


def _cparams(sem):
    return pltpu.CompilerParams(dimension_semantics=sem, vmem_limit_bytes=VMEM_LIMIT)


def _ada_kernel(c_ref, w_ref, b_ref, o_ref):
    c = c_ref[...]
    cond = c * jax.nn.sigmoid(c)
    c1, c2, c3 = _split3(cond)
    w1, w2, w3 = _split3(w_ref[0])
    acc = _dot(c1, w1) + (_dot(c1, w2) + _dot(c2, w1)) + (_dot(c1, w3) + _dot(c2, w2) + _dot(c3, w1))
    o_ref[0] = acc + b_ref[0]


def _ada(c, w_ada, b_ada):
    L, D, N = w_ada.shape
    B = c.shape[0]
    tn = 1536
    return pl.pallas_call(
        _ada_kernel,
        grid=(L, N // tn),
        in_specs=[pl.BlockSpec((B, D), lambda l, j: (0, 0)),
                  pl.BlockSpec((1, D, tn), lambda l, j: (l, 0, j)),
                  pl.BlockSpec((1, 1, tn), lambda l, j: (l, 0, j))],
        out_specs=pl.BlockSpec((1, B, tn), lambda l, j: (l, 0, j)),
        out_shape=jax.ShapeDtypeStruct((L, B, N), F32),
        compiler_params=_cparams(("arbitrary", "arbitrary")),
        name="ada_mod",
    )(c, w_ada, b_ada.reshape(L, 1, N))


def _modulated_norm(x, g, sc, sh):
    y = x * lax.rsqrt(jnp.mean(x * x, axis=-1, keepdims=True) + EPS)
    return (y * g) * (1.0 + sc) + sh


def _inproj_kernel(x_ref, g_ref, sc_ref, sh_ref, wm_ref, wc_ref, ws_ref, zm_ref, zc_ref, zs_ref, h_ref):
    @pl.when(pl.program_id(1) == 0)
    def _():
        h = _modulated_norm(x_ref[...], g_ref[...], sc_ref[0], sh_ref[0]).astype(BF16)
        h_ref[...] = h
        zc = _dot(h, wc_ref[...])
        for part in range(zc_ref.shape[0]):
            zc_ref[part] = zc[:, part * LANES:(part + 1) * LANES]
        zs_ref[...] = _dot(h, ws_ref[...])

    zm_ref[...] = _dot(h_ref[...], wm_ref[...]).astype(BF16)


def _inproj(x2, g, sc, sh, wm, wc, ws, S):
    M, D = x2.shape
    tm, tn = min(TM_PROJ, S), TN_PROJ
    per_b = S // tm
    return pl.pallas_call(
        _inproj_kernel,
        grid=(M // tm, ZM_W // tn),
        in_specs=[pl.BlockSpec((tm, D), lambda i, j: (i, 0)),
                  pl.BlockSpec((1, D), lambda i, j: (0, 0)),
                  pl.BlockSpec((1, 1, D), lambda i, j: (i // per_b, 0, 0)),
                  pl.BlockSpec((1, 1, D), lambda i, j: (i // per_b, 0, 0)),
                  pl.BlockSpec((D, tn), lambda i, j: (0, j)),
                  pl.BlockSpec((D, 2 * NSA_KV), lambda i, j: (0, 0)),
                  pl.BlockSpec((D, ZS_W), lambda i, j: (0, 0))],
        out_specs=[pl.BlockSpec((tm, tn), lambda i, j: (i, j)),
                   pl.BlockSpec((2 * NSA_KV // LANES, tm, LANES), lambda i, j: (0, i, 0)),
                   pl.BlockSpec((tm, ZS_W), lambda i, j: (i, 0))],
        out_shape=[jax.ShapeDtypeStruct((M, ZM_W), BF16),
                   jax.ShapeDtypeStruct((2 * NSA_KV // LANES, M, LANES), F32),
                   jax.ShapeDtypeStruct((M, ZS_W), F32)],
        scratch_shapes=[pltpu.VMEM((tm, D), BF16)],
        compiler_params=_cparams(("arbitrary", "arbitrary")),
        name="in_proj",
    )(x2, g, sc, sh, wm, wc, ws)


def _gmlp_rows(u_ref, v_ref, lng_ref, lnb_ref, ws_ref, bst_ref, o_ref):
    ts = u_ref.shape[0]
    dg = GM_W // GM_GROUPS
    row = lax.broadcasted_iota(jnp.int32, (GM_CHUNK, GM_CHUNK), 0)
    col = lax.broadcasted_iota(jnp.int32, (GM_CHUNK, GM_CHUNK), 1)
    ws = [jnp.where(row >= col, ws_ref[g], 0.0).astype(BF16) for g in range(GM_GROUPS)]
    lng = lng_ref[...]
    lnb = lnb_ref[...]
    for c in range(ts // GM_CHUNK):
        r0 = c * GM_CHUNK
        u = jax.nn.gelu(u_ref[r0:r0 + GM_CHUNK, :].astype(F32))
        v = jax.nn.gelu(v_ref[r0:r0 + GM_CHUNK, :].astype(F32))
        mu = jnp.mean(v, axis=-1, keepdims=True)
        var = jnp.mean(jnp.square(v - mu), axis=-1, keepdims=True)
        vb = ((v - mu) * lax.rsqrt(var + EPS) * lng + lnb).astype(BF16)
        for g in range(GM_GROUPS):
            mixed = _dot(ws[g], vb[:, g * dg:(g + 1) * dg]) + bst_ref[:, g:g + 1]
            o_ref[r0:r0 + GM_CHUNK, g * dg:(g + 1) * dg] = (u[:, g * dg:(g + 1) * dg] * mixed).astype(BF16)


def _log_sigmoid(x):
    return jnp.minimum(x, 0.0) - jnp.log1p(jnp.exp(-jnp.abs(x)))


def _conv_silu(x_ext, w, b):
    n = x_ext.shape[0] - CONV_HALO
    y = b
    for j in range(CONV_K):
        sh = CONV_K - 1 - j
        xs = x_ext if sh == 0 else pltpu.roll(x_ext, sh, axis=0)
        y = y + xs[CONV_HALO:CONV_HALO + n] * w[j:j + 1]
    return y * jax.nn.sigmoid(y)


def _mlstm_kernel(q_ref, k_ref, v_ref, o_ref, zs_ref, cw_ref, cb_ref, gb_ref, ng_ref, out_ref,
                  qc_ref, kt_ref, c_ref):
    S = q_ref.shape[0]
    L = min(ML_CHUNK, S)
    blk = min(CONV_BLK, S)

    for src, c0, is_key in ((q_ref, 0, False), (k_ref, ML_W, True)):
        w = cw_ref[:, c0:c0 + ML_W]
        b = cb_ref[:, c0:c0 + ML_W]

        def conv_block(x_ext, r0, w=w, b=b, is_key=is_key):
            y = _conv_silu(x_ext, w, b)
            if is_key:
                y = y * (ML_DH ** -0.5)
                for h in range(ML_H):
                    kt_ref[h * ML_DH:(h + 1) * ML_DH, pl.ds(r0, blk)] = y[:, h * ML_DH:(h + 1) * ML_DH].T.astype(BF16)
            else:
                qc_ref[pl.ds(r0, blk), :] = y.astype(BF16)

        first = jnp.concatenate([jnp.zeros((CONV_HALO, ML_W), F32), src[0:blk, :].astype(F32)], axis=0)
        conv_block(first, 0)

        def conv_body(i, carry, src=src, conv_block=conv_block):
            r0 = pl.multiple_of(i * blk, blk)
            conv_block(src[pl.ds(r0 - CONV_HALO, blk + CONV_HALO), :].astype(F32), r0)
            return carry

        lax.fori_loop(1, S // blk, conv_body, 0)

    c_ref[...] = jnp.zeros_like(c_ref)
    row = lax.broadcasted_iota(jnp.int32, (L, L), 0)
    col = lax.broadcasted_iota(jnp.int32, (L, L), 1)
    tril = row >= col
    tril_b = jnp.where(tril, 1.0, 0.0).astype(BF16)
    triu_b = jnp.where(row <= col, 1.0, 0.0).astype(BF16)
    ones_blk = jnp.ones((L, ML_DH), BF16)
    time_row = lax.broadcasted_iota(jnp.int32, (L, ZS_W), 0)
    gb = gb_ref[...]

    def chunk_body(c, m_row):
        t0 = pl.multiple_of(c * L, L)
        gi = zs_ref[pl.ds(t0, L), :] + gb
        lf_cols = _log_sigmoid(gi)
        gi_t = gi.T
        lf_rows = _log_sigmoid(gi_t[ZS_MI:ZS_MI + 2 * ML_H])
        c1, c2, c3 = _split3(lf_cols)
        b_cols = _dot(tril_b, c1) + _dot(tril_b, c2) + _dot(tril_b, c3)
        r1, r2, r3 = _split3(lf_rows)
        b_rows = _dot(r1, triu_b) + _dot(r2, triu_b) + _dot(r3, triu_b)
        r_rows = gi_t[ZS_MI:ZS_MI + ML_H] - b_rows[ML_H:2 * ML_H]
        b_at_i = pltpu.roll(b_cols, ZS_W - (ZS_MF - ZS_MI), axis=1)
        u_cols = gi - b_at_i
        shift = 1
        while shift < L:
            u_cols = jnp.maximum(u_cols, jnp.where(time_row >= shift, pltpu.roll(u_cols, shift, axis=0), NEG))
            shift *= 2
        u_cols = jnp.maximum(u_cols, m_row)
        m_cols = b_at_i + u_cols
        u_last = u_cols[L - 1:L, :]
        w_prev_row = jnp.exp(m_row - u_last)
        heads = range(ML_H)
        hs = [slice(h * ML_DH, (h + 1) * ML_DH) for h in heads]
        q = [qc_ref[pl.ds(t0, L), hs[h]] for h in heads]
        k_t = [kt_ref[hs[h], pl.ds(t0, L)] for h in heads]
        cmat = [c_ref[h] for h in heads]
        qk = [_dot(q[h], k_t[h]) for h in heads]
        qc = [_dot(q[h], cmat[h].astype(BF16)) for h in heads]
        v_aug = [jnp.concatenate([v_ref[pl.ds(t0, L), hs[h]], ones_blk], axis=1) for h in heads]
        u_rep, gate = [], []
        for h in heads:
            lane_h = ZS_MI + h
            u_rep.append(jnp.broadcast_to(u_cols[:, lane_h:lane_h + 1], (L, ML_DH)))
            gate.append(jnp.exp(jnp.where(tril, r_rows[h:h + 1, :] - u_rep[h], NEG)))
            w_s = jnp.exp(r_rows[h:h + 1, :] - u_last[:, lane_h:lane_h + 1])
            k_w = (k_t[h].astype(F32) * w_s).astype(BF16)
            c_ref[h] = w_prev_row[:, lane_h:lane_h + 1] * cmat[h] + _dot(k_w, v_aug[h])
        for h in heads:
            lane_h = ZS_MI + h
            w_inter = jnp.exp(m_row[:, lane_h:lane_h + 1] - u_rep[h])
            num_aug = _dot((qk[h] * gate[h]).astype(BF16), v_aug[h]) + jnp.concatenate([w_inter, w_inter], 1) * qc[h]
            m_rep = jnp.broadcast_to(m_cols[:, lane_h:lane_h + 1], (L, ML_DH))
            hval = num_aug[:, :ML_DH] / jnp.maximum(jnp.abs(num_aug[:, ML_DH:]), jnp.exp(-m_rep))
            mu = jnp.mean(hval, axis=-1, keepdims=True)
            var = jnp.mean(jnp.square(hval - mu), axis=-1, keepdims=True)
            hn = (hval - mu) * lax.rsqrt(var + EPS) * ng_ref[:, hs[h]]
            og = jax.nn.sigmoid(o_ref[pl.ds(t0, L), hs[h]].astype(F32))
            out_ref[pl.ds(t0, L), hs[h]] = (og * hn).astype(BF16)
        return m_cols[L - 1:L, :]

    lax.fori_loop(0, S // L, chunk_body, jnp.zeros((1, ZS_W), F32))


def _mlstm(zm, zs, conv_w, conv_b, gate_b, norm_g, S):
    M = zm.shape[0]
    gb_row = jnp.zeros((1, ZS_W), F32).at[0, ZS_MI:ZS_MI + 2 * ML_H].set(gate_b)

    def col(off):
        return pl.BlockSpec((S, ML_W), lambda b: (b, off // ML_W))

    return pl.pallas_call(
        _mlstm_kernel,
        grid=(M // S,),
        in_specs=[col(ZM_MQ), col(ZM_MK), col(ZM_MV), col(ZM_MO),
                  pl.BlockSpec((S, ZS_W), lambda b: (b, 0)),
                  pl.BlockSpec((CONV_K, 2 * ML_W), lambda b: (0, 0)),
                  pl.BlockSpec((1, 2 * ML_W), lambda b: (0, 0)),
                  pl.BlockSpec((1, ZS_W), lambda b: (0, 0)),
                  pl.BlockSpec((1, ML_W), lambda b: (0, 0))],
        out_specs=pl.BlockSpec((S, ML_W), lambda b: (b, 0)),
        out_shape=jax.ShapeDtypeStruct((M, ML_W), BF16),
        scratch_shapes=[pltpu.VMEM((S, ML_W), BF16), pltpu.VMEM((ML_W, S), BF16),
                        pltpu.VMEM((ML_H, ML_DH, 2 * ML_DH), F32)],
        compiler_params=_cparams(("arbitrary",)),
        name="mlstm_mixer",
    )(zm, zm, zm, zm, zs, conv_w, conv_b.reshape(1, 2 * ML_W), gb_row, norm_g.reshape(1, ML_W))


def _nsa_prep_kernel(x_ref, pe_ref, w1_ref, w2_ref, kk_ref, vv_ref, feat_ref, featc_ref,
                     kaug_ref, vt_ref, kcaug_ref, vct_ref):
    S = kk_ref.shape[0]
    n = S // CMP_STRIDE
    cw = 2 * NSA_KV
    half = CMP_BLOCK // 2
    acc_a = jnp.zeros((n, cw), F32)
    acc_b = jnp.zeros((n, cw), F32)
    for j in range(half):
        xj = jnp.concatenate([x_ref[part, pl.ds(j, n, stride=CMP_STRIDE), :] for part in range(cw // LANES)],
                             axis=1)
        acc_a = acc_a + _dot((xj + pe_ref[j:j + 1, :]).astype(BF16), w1_ref[j])
        acc_b = acc_b + _dot((xj + pe_ref[half + j:half + j + 1, :]).astype(BF16), w1_ref[half + j])
    pre = acc_a + pltpu.roll(acc_b, n - 1, axis=0)
    cmp = _dot(jax.nn.gelu(pre).astype(BF16), w2_ref[...])

    lane = lax.broadcasted_iota(jnp.int32, (1, LANES), 1)
    keep = [jnp.where(lane < NSA_DH, 1.0, 0.0).astype(BF16), jnp.where(lane >= NSA_DH, 1.0, 0.0).astype(BF16)]
    for g in range(NSA_G):
        kd = cmp[:, g * LANES:(g + 1) * LANES].astype(BF16)
        for e in range(2):
            kcaug_ref[0, g * 2 + e] = kd * keep[e] + featc_ref[e]
    vct_ref[0] = cmp[:, NSA_G * LANES:(NSA_G + 1) * LANES].T.astype(BF16)
    tb = min(TK, S)
    for br in range(2):
        for g in range(NSA_G):
            kd = kk_ref[:, (br * NSA_G + g) * LANES:(br * NSA_G + g + 1) * LANES]
            for e in range(2):
                kaug_ref[0, (br * NSA_G + g) * 2 + e] = kd * keep[e] + feat_ref[br * 2 + e]
        for c in range(S // tb):
            v_t = vv_ref[c * tb:(c + 1) * tb, br * LANES:(br + 1) * LANES].astype(F32).T.astype(BF16)
            for g in range(NSA_G):
                vt_ref[0, br, g, 0:NSA_DH, c * tb:(c + 1) * tb] = v_t[g * NSA_DH:(g + 1) * NSA_DH]
                vt_ref[0, br, g, NSA_DH:VT_ROWS, c * tb:(c + 1) * tb] = jnp.ones((VT_ROWS - NSA_DH, tb), BF16)


def _blockdiag(blocks):
    n = len(blocks)
    rows = []
    for i, blk in enumerate(blocks):
        rows.append(jnp.concatenate([blk if j == i else jnp.zeros((blk.shape[0], blocks[j].shape[1]), blk.dtype)
                                     for j in range(n)], axis=1))
    return jnp.concatenate(rows, axis=0)


def _key_features(S):
    pos = np.arange(S)
    f = np.zeros((4, S, LANES), np.float32)
    for e in range(2):
        base = NSA_DH * (1 - e)
        f[e, pos, base + FEAT_SEL + pos // SEL_BLOCK] = 1.0
        for br in range(2):
            f[br * 2 + e, :, base + FEAT_HI] = pos // 64
            f[br * 2 + e, :, base + FEAT_LO] = pos % 64
    return jnp.asarray(f, dtype=BF16)


def _cmp_features(S):
    c = np.arange(S // CMP_STRIDE)
    hi = (c * CMP_STRIDE) // 64
    lo = c * CMP_STRIDE + (CMP_BLOCK - 1) * 0.5 - 64 * hi
    f = np.zeros((2, c.size, LANES), np.float32)
    for e in range(2):
        base = NSA_DH * (1 - e)
        f[e, :, base + FEAT_HI] = hi
        f[e, :, base + FEAT_LO] = lo
    return jnp.asarray(f, dtype=BF16)


def _nsa_prep(zc, zm, pe_k, pe_v, phi_k1, phi_k2, phi_v1, phi_v2, S):
    M = zc.shape[1]
    B = M // S
    n = S // CMP_STRIDE
    cw = 2 * NSA_KV
    assert S // SEL_BLOCK <= FEAT_HI and S <= 64 * 64
    pe = jnp.concatenate([pe_k, pe_k, pe_v, pe_v], axis=1)
    k1 = phi_k1.reshape(CMP_BLOCK, NSA_DH, NSA_DH)
    v1 = phi_v1.reshape(CMP_BLOCK, NSA_DH, NSA_DH)
    blocks = jnp.stack([k1, k1, v1, v1], axis=1)
    w1 = jnp.einsum('jaxy,ab->jaxby', blocks, jnp.eye(4, dtype=F32)).reshape(CMP_BLOCK, cw, cw).astype(BF16)
    k2 = jnp.concatenate([phi_k2, phi_k2], axis=1)
    w2 = _blockdiag([k2, k2, phi_v2, phi_v2]).astype(BF16)

    def full(shape):
        return pl.BlockSpec(shape, lambda b: (0,) * len(shape))

    return pl.pallas_call(
        _nsa_prep_kernel,
        grid=(B,),
        in_specs=[pl.BlockSpec((cw // LANES, S, LANES), lambda b: (0, b, 0)),
                  full((CMP_BLOCK, cw)), full((CMP_BLOCK, cw, cw)), full((cw, 3 * LANES)),
                  pl.BlockSpec((S, KK_W), lambda b: (b, ZM_KK // KK_W)),
                  pl.BlockSpec((S, VV_W), lambda b: (b, ZM_VV // VV_W)),
                  full((4, S, LANES)), full((2, n, LANES))],
        out_specs=[pl.BlockSpec((1, 8, S, LANES), lambda b: (b, 0, 0, 0)),
                   pl.BlockSpec((1, 2, NSA_G, VT_ROWS, S), lambda b: (b, 0, 0, 0, 0)),
                   pl.BlockSpec((1, 4, n, LANES), lambda b: (b, 0, 0, 0)),
                   pl.BlockSpec((1, LANES, n), lambda b: (b, 0, 0))],
        out_shape=[jax.ShapeDtypeStruct((B, 8, S, LANES), BF16),
                   jax.ShapeDtypeStruct((B, 2, NSA_G, VT_ROWS, S), BF16),
                   jax.ShapeDtypeStruct((B, 4, n, LANES), BF16),
                   jax.ShapeDtypeStruct((B, LANES, n), BF16)],
        compiler_params=_cparams(("arbitrary",)),
        name="nsa_prep",
    )(zc, pe, w1, w2, zm, zm, _key_features(S), _cmp_features(S))


def _nsa_kernel(q_ref, kaug_ref, vt_ref, kcaug_ref, vct_ref, zs_ref, ovt_ref, out_ref,
                qat_ref, ocmp_ref, m_ref, acc_ref):
    S = kaug_ref.shape[2]
    tq = q_ref.shape[0]
    ncmp = kcaug_ref.shape[2]
    nsel = S // SEL_BLOCK
    n_pairs = NSA_H // 2
    qi = pl.program_id(1)
    t0 = pl.multiple_of(qi * tq, tq)
    t_row = t0 + lax.broadcasted_iota(jnp.int32, (1, tq), 1)

    for k in range(NSA_H // 2):
        q_t = (q_ref[:, k * LANES:(k + 1) * LANES].astype(F32) * (NSA_DH ** -0.5)).T.astype(BF16)
        for e in range(2):
            qat_ref[(k // 2) * 2 + e, e * NSA_DH:(e + 1) * NSA_DH, (k % 2) * tq:(k % 2 + 1) * tq] = (
                q_t[e * NSA_DH:(e + 1) * NSA_DH])
    feat_row = lax.broadcasted_iota(jnp.int32, (NSA_DH, 2 * tq), 0)
    second_head = lax.broadcasted_iota(jnp.int32, (NSA_DH, 2 * tq), 1) >= tq
    for pair in range(n_pairs):
        g, e = pair // 2, pair % 2
        slopes = [2.0 ** (-8.0 * (g * NSA_R + 2 * s + e + 1.0) / NSA_H) for s in range(2)]
        slope = jnp.where(second_head, slopes[1], slopes[0])
        feat = jnp.where(feat_row == FEAT_HI, slope * 64.0, jnp.where(feat_row == FEAT_LO, slope, 0.0))
        qat_ref[pair, NSA_DH * (1 - e):NSA_DH * (2 - e), :] = feat.astype(BF16)

    ki = lax.broadcasted_iota(jnp.int32, (TK, 2 * tq), 0)
    qu = lax.broadcasted_iota(jnp.int32, (TK, 2 * tq), 1)
    qu = jnp.where(qu >= tq, qu - tq, qu)
    tile_valid = {MASK_CAUSAL: ki <= qu,
                  MASK_WINDOW_EDGE: ki > qu}

    c_col = lax.broadcasted_iota(jnp.int32, (ncmp, 1), 0)
    t_row2 = jnp.concatenate([t_row, t_row], axis=1)
    valid_c = c_col * CMP_STRIDE + (CMP_BLOCK - 1) <= t_row2
    j_col = lax.broadcasted_iota(jnp.int32, (nsel, 1), 0)
    j_f = j_col.astype(F32)
    jt = t_row >> 6
    forced = jnp.logical_or(j_col == 0, jnp.logical_or(j_col == jt, j_col == jt - 1))
    future = j_col > jt

    for g in range(NSA_G):
        psum = jnp.zeros((ncmp, tq), F32)
        for e in range(2):
            pair = g * 2 + e
            sc = jnp.where(valid_c, _dot(kcaug_ref[0, pair], qat_ref[pair]), NEG)
            ex = jnp.where(valid_c, jnp.exp(sc - jnp.max(sc, axis=0, keepdims=True)), 0.0)
            den = jnp.sum(ex, axis=0, keepdims=True)
            p = ex * (1.0 / jnp.where(den > 0.0, den, 1.0))
            psum = psum + p[:, :tq] + p[:, tq:]
            ocmp_ref[pair] = _dot(vct_ref[0, g * NSA_DH:(g + 1) * NSA_DH, :], p.astype(BF16))

        p_hi = psum.astype(BF16)
        p_lo = (psum - p_hi.astype(F32)).astype(BF16)
        imp = _dot(ovt_ref[...], p_hi) + _dot(ovt_ref[...], p_lo)
        val = jnp.where(future, NEG, jnp.where(forced, BIG, imp))
        penalty = jnp.full((nsel, tq), NEG, F32)
        for _ in range(min(TOP_N, nsel)):
            best = jnp.max(val, axis=0, keepdims=True)
            first = jnp.min(jnp.where(val == best, j_f, float(nsel)), axis=0, keepdims=True)
            hit = j_f == first
            penalty = jnp.where(hit, 0.0, penalty)
            val = jnp.where(hit, TAKEN, val)
        penalty = penalty.astype(BF16)
        for e in range(2):
            first = NSA_DH * (1 - e) + FEAT_SEL
            qat_ref[g * 2 + e, first:first + nsel, :] = jnp.concatenate([penalty, penalty], axis=1)

    m_ref[...] = jnp.full(m_ref.shape, NEG, F32)
    acc_ref[...] = jnp.zeros(acc_ref.shape, F32)

    def attend(br, k0, mask):
        def scores(pair):
            return _dot(kaug_ref[0, br * n_pairs + pair, pl.ds(k0, TK), :], qat_ref[pair])

        s_queue = [scores(i) for i in range(SCORE_LOOKAHEAD)]
        for pair in range(n_pairs):
            s_t = s_queue.pop(0)
            if pair + SCORE_LOOKAHEAD < n_pairs:
                s_queue.append(scores(pair + SCORE_LOOKAHEAD))
            g = pair // 2
            slot = br * n_pairs + pair
            if mask is not None:
                s_t = jnp.where(tile_valid[mask], s_t, NEG)
            m_old = m_ref[slot]
            m_new = jnp.maximum(m_old, jnp.max(s_t, axis=0, keepdims=True))
            alpha = jnp.exp(m_old - m_new)
            p = jnp.exp(s_t - m_new).astype(BF16)
            acc_ref[slot] = alpha * acc_ref[slot] + _dot(vt_ref[0, br, g, :, pl.ds(k0, TK)], p)
            m_ref[slot] = m_new

    attend(BR_WIN, t0, MASK_CAUSAL)
    n_back = WINDOW // TK
    for d in range(1, n_back + 1):
        @pl.when(qi >= d)
        def _(d=d):
            attend(BR_WIN, pl.multiple_of(t0 - d * TK, TK), MASK_WINDOW_EDGE if d == n_back else None)

    def sel_body(kb, carry):
        attend(BR_SEL, pl.multiple_of(kb * TK, TK), None)
        return carry

    lax.fori_loop(0, qi, sel_body, 0)
    attend(BR_SEL, t0, MASK_CAUSAL)

    gates_t = jax.nn.sigmoid(zs_ref[...]).T
    for hp in range(NSA_H // 2):
        comb = []
        for e in range(2):
            h = 2 * hp + e
            c0 = ZS_GATE + NSA_NB * h
            pair = (h // NSA_R) * 2 + e
            cols = slice(((h % NSA_R) // 2) * tq, ((h % NSA_R) // 2 + 1) * tq)
            sel, win = BR_SEL * n_pairs + pair, BR_WIN * n_pairs + pair
            o_sel = acc_ref[sel, 0:NSA_DH, cols] * (1.0 / acc_ref[sel, NSA_DH:NSA_DH + 1, cols])
            o_win = acc_ref[win, 0:NSA_DH, cols] * (1.0 / acc_ref[win, NSA_DH:NSA_DH + 1, cols])
            comb.append(gates_t[c0:c0 + 1] * ocmp_ref[pair, :, cols] + gates_t[c0 + 1:c0 + 2] * o_sel
                        + gates_t[c0 + 2:c0 + 3] * o_win)
        out_ref[:, hp * LANES:(hp + 1) * LANES] = jnp.concatenate(comb, axis=0).T.astype(BF16)


def _overlap_t(S):
    c = np.arange(S // CMP_STRIDE)[None, :]
    j = np.arange(S // SEL_BLOCK)[:, None]
    ov = (c * CMP_STRIDE <= j * SEL_BLOCK + SEL_BLOCK - 1) & (c * CMP_STRIDE + CMP_BLOCK - 1 >= j * SEL_BLOCK)
    ov &= c < (S - CMP_BLOCK) // CMP_STRIDE + 1
    return jnp.asarray(ov.astype(np.float32), dtype=BF16)


def _nsa(zm, zs, kaug, vt, kcaug, vct, S):
    M = zm.shape[0]
    B = M // S
    tq = min(TQ, S)
    assert tq == TK and WINDOW % TK == 0 and S % TK == 0
    nq = S // tq
    n = S // CMP_STRIDE
    nsel = S // SEL_BLOCK
    return pl.pallas_call(
        _nsa_kernel,
        grid=(B, nq),
        in_specs=[pl.BlockSpec((tq, NSA_W), lambda b, i: (b * nq + i, ZM_NQ // NSA_W)),
                  pl.BlockSpec((1, 8, S, LANES), lambda b, i: (b, 0, 0, 0)),
                  pl.BlockSpec((1, 2, NSA_G, VT_ROWS, S), lambda b, i: (b, 0, 0, 0, 0)),
                  pl.BlockSpec((1, 4, n, LANES), lambda b, i: (b, 0, 0, 0)),
                  pl.BlockSpec((1, LANES, n), lambda b, i: (b, 0, 0)),
                  pl.BlockSpec((tq, ZS_W), lambda b, i: (b * nq + i, 0)),
                  pl.BlockSpec((nsel, n), lambda b, i: (0, 0))],
        out_specs=pl.BlockSpec((tq, NSA_W), lambda b, i: (b * nq + i, 0)),
        out_shape=jax.ShapeDtypeStruct((M, NSA_W), BF16),
        scratch_shapes=[pltpu.VMEM((NSA_H // 2, LANES, 2 * tq), BF16),
                        pltpu.VMEM((NSA_H // 2, NSA_DH, 2 * tq), F32),
                        pltpu.VMEM((NSA_H, 1, 2 * tq), F32),
                        pltpu.VMEM((NSA_H, VT_ROWS, 2 * tq), F32)],
        compiler_params=_cparams(("arbitrary", "arbitrary")),
        name="nsa_attention",
    )(zm, kaug, vt, kcaug, vct, zs, _overlap_t(S))


def _merge_kernel(gu0_ref, gv0_ref, gu_ref, gv_ref, lng_ref, lnb_ref, ws_ref, bst_ref, b_ref, c_ref,
                  ga_ref, gb_ref, gc_ref, x_ref, gt_ref, wa_ref, wb_ref, wc_ref, wo_ref, o_ref, a_ref):
    i = pl.program_id(0)

    @pl.when(i == 0)
    def _():
        _gmlp_rows(gu0_ref, gv0_ref, lng_ref, lnb_ref, ws_ref, bst_ref, a_ref.at[0])

    a = a_ref[i % 2]
    merged = (jax.nn.sigmoid(ga_ref[...].astype(F32)) * _dot(a, wa_ref[...])
              + jax.nn.sigmoid(gb_ref[...].astype(F32)) * _dot(b_ref[...], wb_ref[...])
              + jax.nn.sigmoid(gc_ref[...].astype(F32)) * _dot(c_ref[...], wc_ref[...]))
    o_ref[...] = x_ref[...] + gt_ref[0] * _dot(merged.astype(BF16), wo_ref[...])
    _gmlp_rows(gu_ref, gv_ref, lng_ref, lnb_ref, ws_ref, bst_ref, a_ref.at[(i + 1) % 2])


def _merge(yb, yc, zm, x2, gt, ln_g, ln_b, ws, bs, wa, wb, wc, wo, S):
    M, D = x2.shape
    tm = min(TM_MERGE, S)
    per_b = S // tm

    def rows(w, jcol=0):
        return pl.BlockSpec((tm, w), lambda i: (i, jcol))

    def next_rows(jcol):
        return pl.BlockSpec((tm, GM_W), lambda i: (jnp.minimum(i + 1, M // tm - 1), jcol))

    def full(shape):
        return pl.BlockSpec(shape, lambda i: (0,) * len(shape))

    return pl.pallas_call(
        _merge_kernel,
        grid=(M // tm,),
        in_specs=[rows(GM_W, ZM_GU // GM_W), rows(GM_W, ZM_GV // GM_W),
                  next_rows(ZM_GU // GM_W), next_rows(ZM_GV // GM_W), full((1, GM_W)), full((1, GM_W)),
                  full((GM_GROUPS, GM_CHUNK, GM_CHUNK)), full((GM_CHUNK, GM_GROUPS)),
                  rows(ML_W), rows(NSA_W),
                  rows(D, ZM_GA // D), rows(D, ZM_GBR // D), rows(D, ZM_GC // D),
                  rows(D), pl.BlockSpec((1, 1, D), lambda i: (i // per_b, 0, 0)),
                  full((GM_W, D)), full((ML_W, D)), full((NSA_W, D)), full((D, D))],
        out_specs=rows(D),
        out_shape=jax.ShapeDtypeStruct((M, D), F32),
        scratch_shapes=[pltpu.VMEM((2, tm, GM_W), BF16)],
        compiler_params=_cparams(("arbitrary",)),
        name="gmlp_merge_out",
    )(zm, zm, zm, zm, ln_g.reshape(1, GM_W), ln_b.reshape(1, GM_W), ws, bs.T, yb, yc, zm, zm, zm, x2, gt, wa, wb, wc, wo)


def _mlp_kernel(x_ref, g_ref, sc_ref, sh_ref, gt_ref, w1_ref, w2_ref, gf_ref, o_ref, acc_ref, *, final_norm):
    x = x_ref[...]
    h = _modulated_norm(x, g_ref[...], sc_ref[0], sh_ref[0]).astype(BF16)
    for c in range(D_FF // FF_CHUNK):
        mid = jnp.square(jnp.maximum(_dot(h, w1_ref[:, c * FF_CHUNK:(c + 1) * FF_CHUNK]), 0.0)).astype(BF16)
        upd = _dot(mid, w2_ref[c * FF_CHUNK:(c + 1) * FF_CHUNK, :])
        if c == 0:
            acc_ref[...] = upd
        else:
            acc_ref[...] += upd
    y = x + gt_ref[0] * acc_ref[...]
    if final_norm:
        y = (y * lax.rsqrt(jnp.mean(y * y, axis=-1, keepdims=True) + EPS)) * gf_ref[...]
    o_ref[...] = y


def _mlp(x2, g, sc, sh, gt, w1, w2, g_final, S, final_norm):
    M, D = x2.shape
    tm = min(TM_MLP, S)
    per_b = S // tm
    mod = pl.BlockSpec((1, 1, D), lambda i: (i // per_b, 0, 0))
    return pl.pallas_call(
        functools.partial(_mlp_kernel, final_norm=final_norm),
        grid=(M // tm,),
        in_specs=[pl.BlockSpec((tm, D), lambda i: (i, 0)),
                  pl.BlockSpec((1, D), lambda i: (0, 0)),
                  mod, mod, mod,
                  pl.BlockSpec((D, D_FF), lambda i: (0, 0)),
                  pl.BlockSpec((D_FF, D), lambda i: (0, 0)),
                  pl.BlockSpec((1, D), lambda i: (0, 0))],
        out_specs=pl.BlockSpec((tm, D), lambda i: (i, 0)),
        out_shape=jax.ShapeDtypeStruct((M, D), F32),
        scratch_shapes=[pltpu.VMEM((tm, D), F32)],
        compiler_params=_cparams(("arbitrary",)),
        name="relu2_mlp",
    )(x2, g, sc, sh, gt, w1, w2, g_final)


def _w_cols(w, name):
    return w[:, _OFFS[name]:_OFFS[name] + _SIZE[name]]


def _dup_heads(w):
    parts = []
    for g in range(NSA_G):
        blk = w[:, g * NSA_DH:(g + 1) * NSA_DH]
        parts += [blk, blk]
    return jnp.concatenate(parts, axis=1)


def _proj_weights(w):
    D = w.shape[0]
    wm = jnp.concatenate([_w_cols(w, n) for n in ('ga', 'gbr', 'gc', 'gu', 'gv', 'mq', 'mk', 'mv', 'mo', 'nq')]
                         + [_dup_heads(_w_cols(w, 'nks')), _dup_heads(_w_cols(w, 'nkw')),
                            _w_cols(w, 'nvs'), _w_cols(w, 'nvw')], axis=1)
    wc = jnp.concatenate([_w_cols(w, 'nkc'), _w_cols(w, 'nvc')], axis=1)
    ws = jnp.concatenate([_w_cols(w, 'ngate'), _w_cols(w, 'mi'), _w_cols(w, 'mf'),
                          jnp.zeros((D, ZS_W - NSA_H * NSA_NB - 2 * ML_H), w.dtype)], axis=1)
    return wm.astype(BF16), wc.astype(BF16), ws.astype(BF16)


def kernel(x, c, g_norm1, g_norm2, w_ada, b_ada, w_in, gm_ln_g, gm_ln_b, gm_ws, gm_bs, ml_conv_w, ml_conv_b,
           ml_gate_b, ml_norm_g, nsa_pe_k, nsa_pe_v, nsa_phi_k1, nsa_phi_k2, nsa_phi_v1, nsa_phi_v2,
           w_up_a, w_up_b, w_up_c, w_out, w_mlp1, w_mlp2, g_final):
    B, S, D = x.shape
    depth = w_in.shape[0]
    M = B * S
    mod = _ada(c, w_ada, b_ada)
    x2 = x.reshape(M, D)
    for l in range(depth):
        sh1, sc1, gt1, sh2, sc2, gt2 = [mod[l, :, i * D:(i + 1) * D].reshape(B, 1, D) for i in range(6)]
        wm, wc, ws = _proj_weights(w_in[l])
        zm, zc, zs = _inproj(x2, g_norm1[l].reshape(1, D), sc1, sh1, wm, wc, ws, S)
        yb = _mlstm(zm, zs, ml_conv_w[l], ml_conv_b[l], ml_gate_b[l], ml_norm_g[l], S)
        kaug, vt, kcaug, vct = _nsa_prep(zc, zm, nsa_pe_k[l], nsa_pe_v[l], nsa_phi_k1[l], nsa_phi_k2[l],
                                         nsa_phi_v1[l], nsa_phi_v2[l], S)
        yc = _nsa(zm, zs, kaug, vt, kcaug, vct, S)
        x2 = _merge(yb, yc, zm, x2, gt1, gm_ln_g[l], gm_ln_b[l], gm_ws[l], gm_bs[l], w_up_a[l].astype(BF16),
                    w_up_b[l].astype(BF16), w_up_c[l].astype(BF16), w_out[l].astype(BF16), S)
        x2 = _mlp(x2, g_norm2[l].reshape(1, D), sc2, sh2, gt2, w_mlp1[l].astype(BF16), w_mlp2[l].astype(BF16),
                  g_final.reshape(1, D), S, final_norm=(l == depth - 1))
    return x2.reshape(B, S, D)
```

```python
import functools

import numpy as np
import jax
import jax.numpy as jnp
from jax import lax
from jax.experimental import pallas as pl
from jax.experimental.pallas import tpu as pltpu

F32 = jnp.float32
BF16 = jnp.bfloat16

D_MODEL = 1024
GM_W = 512
GM_GROUPS = 4
GM_CHUNK = 128
ML_H = 4
ML_DH = 128
ML_W = ML_H * ML_DH
CONV_K = 4
NSA_H = 8
NSA_G = 2
NSA_R = NSA_H // NSA_G
NSA_DH = 64
NSA_W = NSA_H * NSA_DH
NSA_KV = NSA_G * NSA_DH
NSA_NB = 3
CMP_BLOCK = 32
CMP_STRIDE = 16
SEL_BLOCK = 64
TOP_N = 8
WINDOW = 512
D_FF = 4 * D_MODEL
EPS = 1e-6
NEG = -1e30
BIG = 1e4
TAKEN = -3e38
SPLIT_SIZES = (GM_W, GM_W, ML_W, ML_W, ML_W, ML_W, ML_H, ML_H, NSA_W, NSA_KV, NSA_KV, NSA_KV, NSA_KV,
               NSA_KV, NSA_KV, NSA_H * NSA_NB, D_MODEL, D_MODEL, D_MODEL)
SPLIT_NAMES = ('gu', 'gv', 'mq', 'mk', 'mv', 'mo', 'mi', 'mf', 'nq', 'nkc', 'nvc', 'nks', 'nvs', 'nkw', 'nvw',
               'ngate', 'ga', 'gbr', 'gc')
_OFFS = dict(zip(SPLIT_NAMES, np.concatenate([[0], np.cumsum(SPLIT_SIZES)[:-1]]).tolist()))
_SIZE = dict(zip(SPLIT_NAMES, SPLIT_SIZES))

LANES = 128
VMEM_LIMIT = 56 * 1024 * 1024

ZM_GA, ZM_GBR, ZM_GC = 0, 1024, 2048
ZM_GU, ZM_GV = 3072, 3584
ZM_MQ, ZM_MK, ZM_MV, ZM_MO = 4096, 4608, 5120, 5632
ZM_NQ = 6144
ZM_KK = 6656
ZM_VV = 7168
ZM_W = 7424
KK_W = 4 * LANES
VV_W = 2 * LANES
BR_SEL, BR_WIN = 0, 1
MASK_CAUSAL, MASK_WINDOW_EDGE = 0, 1
FEAT_SEL = 0
FEAT_HI = 32
FEAT_LO = 33
ZS_GATE = 0
ZS_MI = 24
ZS_MF = 28
ZS_W = 128

TM_PROJ = 1024
TN_PROJ = 3712
ML_CHUNK = 128
ML_ROWS = 2
CONV_HALO = 16
TQ = 256
TK = 256
SCORE_LOOKAHEAD = 2
VT_ROWS = NSA_DH + 16
TM_MERGE = 512
TM_MLP = 512
FF_CHUNK = 1024


def _dot(a, b):
    return jnp.dot(a, b, preferred_element_type=F32)


def _split3(x):
    x1 = x.astype(BF16)
    r1 = x - x1.astype(F32)
    x2 = r1.astype(BF16)
    x3 = (r1 - x2.astype(F32)).astype(BF16)
    return x1, x2, x3


def _cparams(sem):
    return pltpu.CompilerParams(dimension_semantics=sem, vmem_limit_bytes=VMEM_LIMIT)


def _ada_kernel(c_ref, w_ref, b_ref, o_ref):
    c = c_ref[...]
    cond = c * jax.nn.sigmoid(c)
    c1, c2, c3 = _split3(cond)
    w1, w2, w3 = _split3(w_ref[0])
    acc = _dot(c1, w1) + (_dot(c1, w2) + _dot(c2, w1)) + (_dot(c1, w3) + _dot(c2, w2) + _dot(c3, w1))
    o_ref[0] = acc + b_ref[0]


def _ada(c, w_ada, b_ada):
    L, D, N = w_ada.shape
    B = c.shape[0]
    tn = 1536
    return pl.pallas_call(
        _ada_kernel,
        grid=(L, N // tn),
        in_specs=[pl.BlockSpec((B, D), lambda l, j: (0, 0)),
                  pl.BlockSpec((1, D, tn), lambda l, j: (l, 0, j)),
                  pl.BlockSpec((1, 1, tn), lambda l, j: (l, 0, j))],
        out_specs=pl.BlockSpec((1, B, tn), lambda l, j: (l, 0, j)),
        out_shape=jax.ShapeDtypeStruct((L, B, N), F32),
        compiler_params=_cparams(("arbitrary", "arbitrary")),
        name="ada_mod",
    )(c, w_ada, b_ada.reshape(L, 1, N))


def _modulated_norm(x, g, sc, sh):
    y = x * lax.rsqrt(jnp.mean(x * x, axis=-1, keepdims=True) + EPS)
    return (y * g) * (1.0 + sc) + sh


def _inproj_kernel(x_ref, g_ref, sc_ref, sh_ref, wm_ref, wc_ref, ws_ref, zm_ref, zc_ref, zs_ref, h_ref):
    @pl.when(pl.program_id(1) == 0)
    def _():
        h = _modulated_norm(x_ref[...], g_ref[...], sc_ref[0], sh_ref[0]).astype(BF16)
        h_ref[...] = h
        zc = _dot(h, wc_ref[...])
        for part in range(zc_ref.shape[0]):
            zc_ref[part] = zc[:, part * LANES:(part + 1) * LANES]
        zs_ref[...] = _dot(h, ws_ref[...])

    zm_ref[...] = _dot(h_ref[...], wm_ref[...]).astype(BF16)


def _inproj(x2, g, sc, sh, wm, wc, ws, S):
    M, D = x2.shape
    tm, tn = min(TM_PROJ, S), TN_PROJ
    per_b = S // tm
    return pl.pallas_call(
        _inproj_kernel,
        grid=(M // tm, ZM_W // tn),
        in_specs=[pl.BlockSpec((tm, D), lambda i, j: (i, 0)),
                  pl.BlockSpec((1, D), lambda i, j: (0, 0)),
                  pl.BlockSpec((1, 1, D), lambda i, j: (i // per_b, 0, 0)),
                  pl.BlockSpec((1, 1, D), lambda i, j: (i // per_b, 0, 0)),
                  pl.BlockSpec((D, tn), lambda i, j: (0, j)),
                  pl.BlockSpec((D, 2 * NSA_KV), lambda i, j: (0, 0)),
                  pl.BlockSpec((D, ZS_W), lambda i, j: (0, 0))],
        out_specs=[pl.BlockSpec((tm, tn), lambda i, j: (i, j)),
                   pl.BlockSpec((2 * NSA_KV // LANES, tm, LANES), lambda i, j: (0, i, 0)),
                   pl.BlockSpec((tm, ZS_W), lambda i, j: (i, 0))],
        out_shape=[jax.ShapeDtypeStruct((M, ZM_W), BF16),
                   jax.ShapeDtypeStruct((2 * NSA_KV // LANES, M, LANES), F32),
                   jax.ShapeDtypeStruct((M, ZS_W), F32)],
        scratch_shapes=[pltpu.VMEM((tm, D), BF16)],
        compiler_params=_cparams(("arbitrary", "arbitrary")),
        name="in_proj",
    )(x2, g, sc, sh, wm, wc, ws)


def _gmlp_rows(u_ref, v_ref, lng_ref, lnb_ref, ws_ref, bst_ref, o_ref):
    ts = u_ref.shape[0]
    dg = GM_W // GM_GROUPS
    row = lax.broadcasted_iota(jnp.int32, (GM_CHUNK, GM_CHUNK), 0)
    col = lax.broadcasted_iota(jnp.int32, (GM_CHUNK, GM_CHUNK), 1)
    ws = [jnp.where(row >= col, ws_ref[g], 0.0).astype(BF16) for g in range(GM_GROUPS)]
    lng = lng_ref[...]
    lnb = lnb_ref[...]
    for c in range(ts // GM_CHUNK):
        r0 = c * GM_CHUNK
        u = jax.nn.gelu(u_ref[r0:r0 + GM_CHUNK, :].astype(F32))
        v = jax.nn.gelu(v_ref[r0:r0 + GM_CHUNK, :].astype(F32))
        mu = jnp.mean(v, axis=-1, keepdims=True)
        var = jnp.mean(jnp.square(v - mu), axis=-1, keepdims=True)
        vb = ((v - mu) * lax.rsqrt(var + EPS) * lng + lnb).astype(BF16)
        for g in range(GM_GROUPS):
            mixed = _dot(ws[g], vb[:, g * dg:(g + 1) * dg]) + bst_ref[:, g:g + 1]
            o_ref[r0:r0 + GM_CHUNK, g * dg:(g + 1) * dg] = (u[:, g * dg:(g + 1) * dg] * mixed).astype(BF16)


def _log_sigmoid(x):
    return jnp.minimum(x, 0.0) - jnp.log1p(jnp.exp(-jnp.abs(x)))


def _conv_silu(x_ext, w, b):
    n = x_ext.shape[0] - CONV_HALO
    y = b
    for j in range(CONV_K):
        sh = CONV_K - 1 - j
        xs = x_ext if sh == 0 else pltpu.roll(x_ext, sh, axis=0)
        y = y + xs[CONV_HALO:CONV_HALO + n] * w[j:j + 1]
    return y * jax.nn.sigmoid(y)


def _mlstm_kernel(q_ref, k_ref, v_ref, o_ref, zs_ref, cw_ref, cb_ref, gb_ref, ng_ref, out_ref, c_ref, *, S):
    n_rows = q_ref.shape[0] // S
    L = min(ML_CHUNK, S)
    c_ref[...] = jnp.zeros_like(c_ref)
    row = lax.broadcasted_iota(jnp.int32, (L, L), 0)
    col = lax.broadcasted_iota(jnp.int32, (L, L), 1)
    tril = row >= col
    tril_b = jnp.where(tril, 1.0, 0.0).astype(BF16)
    triu_b = jnp.where(row <= col, 1.0, 0.0).astype(BF16)
    ones_blk = jnp.ones((L, ML_DH), BF16)
    time_row = lax.broadcasted_iota(jnp.int32, (L, ZS_W), 0)
    gb = gb_ref[...]
    hs = [slice(h * ML_DH, (h + 1) * ML_DH) for h in range(ML_H)]
    chains = [(r, h) for r in range(n_rows) for h in range(ML_H)]

    def conv_rows(src, base, t0, c, c0):
        cur = src[pl.ds(pl.multiple_of(base + t0, L), L), :].astype(F32)
        halo0 = pl.multiple_of(base + jnp.maximum(t0 - CONV_HALO, 0), CONV_HALO)
        halo = jnp.where(c > 0, src[pl.ds(halo0, CONV_HALO), :].astype(F32), 0.0)
        return _conv_silu(jnp.concatenate([halo, cur], axis=0), cw_ref[:, c0:c0 + ML_W], cb_ref[:, c0:c0 + ML_W])

    def chunk_body(c, m_rows):
        t0 = pl.multiple_of(c * L, L)
        q, k_t, r_rows, u_cols, m_cols, u_last, w_prev_row = {}, {}, [], [], [], [], []
        for r in range(n_rows):
            base = r * S
            rows = pl.ds(pl.multiple_of(base + t0, L), L)
            gi = zs_ref[rows, :] + gb
            lf_cols = _log_sigmoid(gi)
            gi_t = gi.T
            lf_rows = _log_sigmoid(gi_t[ZS_MI:ZS_MI + 2 * ML_H])
            c1, c2, c3 = _split3(lf_cols)
            b_cols = _dot(tril_b, c1) + _dot(tril_b, c2) + _dot(tril_b, c3)
            r1, r2, r3 = _split3(lf_rows)
            b_rows = _dot(r1, triu_b) + _dot(r2, triu_b) + _dot(r3, triu_b)
            r_rows.append(gi_t[ZS_MI:ZS_MI + ML_H] - b_rows[ML_H:2 * ML_H])
            b_at_i = pltpu.roll(b_cols, ZS_W - (ZS_MF - ZS_MI), axis=1)
            u = gi - b_at_i
            shift = 1
            while shift < L:
                u = jnp.maximum(u, jnp.where(time_row >= shift, pltpu.roll(u, shift, axis=0), NEG))
                shift *= 2
            u = jnp.maximum(u, m_rows[r])
            u_cols.append(u)
            m_cols.append(b_at_i + u)
            u_last.append(u[L - 1:L, :])
            w_prev_row.append(jnp.exp(m_rows[r] - u_last[r]))
            q_all = conv_rows(q_ref, base, t0, c, 0).astype(BF16)
            k_all = conv_rows(k_ref, base, t0, c, ML_W) * (ML_DH ** -0.5)
            for h in range(ML_H):
                q[r, h] = q_all[:, hs[h]]
                k_t[r, h] = k_all[:, hs[h]].T
        cmat = {ch: c_ref[ch[0] * ML_H + ch[1]] for ch in chains}
        qk = {ch: _dot(q[ch], k_t[ch].astype(BF16)) for ch in chains}
        qc = {ch: _dot(q[ch], cmat[ch].astype(BF16)) for ch in chains}
        v_aug = {(r, h): jnp.concatenate([v_ref[pl.ds(pl.multiple_of(r * S + t0, L), L), hs[h]], ones_blk], axis=1)
                 for r, h in chains}
        u_rep, gate = {}, {}
        for ch in chains:
            r, h = ch
            lane_h = ZS_MI + h
            u_rep[ch] = jnp.broadcast_to(u_cols[r][:, lane_h:lane_h + 1], (L, ML_DH))
            gate[ch] = jnp.exp(jnp.where(tril, r_rows[r][h:h + 1, :] - u_rep[ch], NEG))
            w_s = jnp.exp(r_rows[r][h:h + 1, :] - u_last[r][:, lane_h:lane_h + 1])
            k_w = (k_t[ch] * w_s).astype(BF16)
            c_ref[r * ML_H + h] = w_prev_row[r][:, lane_h:lane_h + 1] * cmat[ch] + _dot(k_w, v_aug[ch])
        for ch in chains:
            r, h = ch
            lane_h = ZS_MI + h
            rows = pl.ds(pl.multiple_of(r * S + t0, L), L)
            w_inter = jnp.exp(m_rows[r][:, lane_h:lane_h + 1] - u_rep[ch])
            num_aug = (_dot((qk[ch] * gate[ch]).astype(BF16), v_aug[ch])
                       + jnp.concatenate([w_inter, w_inter], axis=1) * qc[ch])
            m_rep = jnp.broadcast_to(m_cols[r][:, lane_h:lane_h + 1], (L, ML_DH))
            hval = num_aug[:, :ML_DH] / jnp.maximum(jnp.abs(num_aug[:, ML_DH:]), jnp.exp(-m_rep))
            mu = jnp.mean(hval, axis=-1, keepdims=True)
            var = jnp.mean(jnp.square(hval - mu), axis=-1, keepdims=True)
            hn = (hval - mu) * lax.rsqrt(var + EPS) * ng_ref[:, hs[h]]
            og = jax.nn.sigmoid(o_ref[rows, hs[h]].astype(F32))
            out_ref[rows, hs[h]] = (og * hn).astype(BF16)
        return tuple(m[L - 1:L, :] for m in m_cols)

    lax.fori_loop(0, S // L, chunk_body, tuple(jnp.zeros((1, ZS_W), F32) for _ in range(n_rows)))


def _mlstm(zm, zs, conv_w, conv_b, gate_b, norm_g, S):
    M = zm.shape[0]
    n_rows = ML_ROWS if (M // S) % ML_ROWS == 0 else 1
    gb_row = jnp.zeros((1, ZS_W), F32).at[0, ZS_MI:ZS_MI + 2 * ML_H].set(gate_b)

    def col(off):
        return pl.BlockSpec((n_rows * S, ML_W), lambda b: (b, off // ML_W))

    return pl.pallas_call(
        functools.partial(_mlstm_kernel, S=S),
        grid=(M // (n_rows * S),),
        in_specs=[col(ZM_MQ), col(ZM_MK), col(ZM_MV), col(ZM_MO),
                  pl.BlockSpec((n_rows * S, ZS_W), lambda b: (b, 0)),
                  pl.BlockSpec((CONV_K, 2 * ML_W), lambda b: (0, 0)),
                  pl.BlockSpec((1, 2 * ML_W), lambda b: (0, 0)),
                  pl.BlockSpec((1, ZS_W), lambda b: (0, 0)),
                  pl.BlockSpec((1, ML_W), lambda b: (0, 0))],
        out_specs=pl.BlockSpec((n_rows * S, ML_W), lambda b: (b, 0)),
        out_shape=jax.ShapeDtypeStruct((M, ML_W), BF16),
        scratch_shapes=[pltpu.VMEM((n_rows * ML_H, ML_DH, 2 * ML_DH), F32)],
        compiler_params=_cparams(("arbitrary",)),
        name="mlstm_mixer",
    )(zm, zm, zm, zm, zs, conv_w, conv_b.reshape(1, 2 * ML_W), gb_row, norm_g.reshape(1, ML_W))


def _nsa_prep_kernel(x_ref, pe_ref, w1_ref, w2_ref, kk_ref, vv_ref, feat_ref, featc_ref,
                     kaug_ref, vt_ref, kcaug_ref, vct_ref):
    S = kk_ref.shape[0]
    n = S // CMP_STRIDE
    cw = 2 * NSA_KV
    half = CMP_BLOCK // 2
    acc_a = jnp.zeros((n, cw), F32)
    acc_b = jnp.zeros((n, cw), F32)
    for j in range(half):
        xj = jnp.concatenate([x_ref[part, pl.ds(j, n, stride=CMP_STRIDE), :] for part in range(cw // LANES)],
                             axis=1)
        acc_a = acc_a + _dot((xj + pe_ref[j:j + 1, :]).astype(BF16), w1_ref[j])
        acc_b = acc_b + _dot((xj + pe_ref[half + j:half + j + 1, :]).astype(BF16), w1_ref[half + j])
    pre = acc_a + pltpu.roll(acc_b, n - 1, axis=0)
    cmp = _dot(jax.nn.gelu(pre).astype(BF16), w2_ref[...])

    lane = lax.broadcasted_iota(jnp.int32, (1, LANES), 1)
    keep = [jnp.where(lane < NSA_DH, 1.0, 0.0).astype(BF16), jnp.where(lane >= NSA_DH, 1.0, 0.0).astype(BF16)]
    for g in range(NSA_G):
        kd = cmp[:, g * LANES:(g + 1) * LANES].astype(BF16)
        for e in range(2):
            kcaug_ref[0, g * 2 + e] = kd * keep[e] + featc_ref[e]
    vct_ref[0] = cmp[:, NSA_G * LANES:(NSA_G + 1) * LANES].T.astype(BF16)
    tb = min(TK, S)
    for br in range(2):
        for g in range(NSA_G):
            kd = kk_ref[:, (br * NSA_G + g) * LANES:(br * NSA_G + g + 1) * LANES]
            for e in range(2):
                kaug_ref[0, (br * NSA_G + g) * 2 + e] = kd * keep[e] + feat_ref[br * 2 + e]
        for c in range(S // tb):
            v_t = vv_ref[c * tb:(c + 1) * tb, br * LANES:(br + 1) * LANES].astype(F32).T.astype(BF16)
            for g in range(NSA_G):
                vt_ref[0, br, g, 0:NSA_DH, c * tb:(c + 1) * tb] = v_t[g * NSA_DH:(g + 1) * NSA_DH]
                vt_ref[0, br, g, NSA_DH:VT_ROWS, c * tb:(c + 1) * tb] = jnp.ones((VT_ROWS - NSA_DH, tb), BF16)


def _blockdiag(blocks):
    n = len(blocks)
    rows = []
    for i, blk in enumerate(blocks):
        rows.append(jnp.concatenate([blk if j == i else jnp.zeros((blk.shape[0], blocks[j].shape[1]), blk.dtype)
                                     for j in range(n)], axis=1))
    return jnp.concatenate(rows, axis=0)


def _key_features(S):
    pos = np.arange(S)
    f = np.zeros((4, S, LANES), np.float32)
    for e in range(2):
        base = NSA_DH * (1 - e)
        f[e, pos, base + FEAT_SEL + pos // SEL_BLOCK] = 1.0
        for br in range(2):
            f[br * 2 + e, :, base + FEAT_HI] = pos // 64
            f[br * 2 + e, :, base + FEAT_LO] = pos % 64
    return jnp.asarray(f, dtype=BF16)


def _cmp_features(S):
    c = np.arange(S // CMP_STRIDE)
    hi = (c * CMP_STRIDE) // 64
    lo = c * CMP_STRIDE + (CMP_BLOCK - 1) * 0.5 - 64 * hi
    f = np.zeros((2, c.size, LANES), np.float32)
    for e in range(2):
        base = NSA_DH * (1 - e)
        f[e, :, base + FEAT_HI] = hi
        f[e, :, base + FEAT_LO] = lo
    return jnp.asarray(f, dtype=BF16)


def _nsa_prep(zc, zm, pe_k, pe_v, phi_k1, phi_k2, phi_v1, phi_v2, S):
    M = zc.shape[1]
    B = M // S
    n = S // CMP_STRIDE
    cw = 2 * NSA_KV
    assert S // SEL_BLOCK <= FEAT_HI and S <= 64 * 64
    pe = jnp.concatenate([pe_k, pe_k, pe_v, pe_v], axis=1)
    k1 = phi_k1.reshape(CMP_BLOCK, NSA_DH, NSA_DH)
    v1 = phi_v1.reshape(CMP_BLOCK, NSA_DH, NSA_DH)
    blocks = jnp.stack([k1, k1, v1, v1], axis=1)
    w1 = jnp.einsum('jaxy,ab->jaxby', blocks, jnp.eye(4, dtype=F32)).reshape(CMP_BLOCK, cw, cw).astype(BF16)
    k2 = jnp.concatenate([phi_k2, phi_k2], axis=1)
    w2 = _blockdiag([k2, k2, phi_v2, phi_v2]).astype(BF16)

    def full(shape):
        return pl.BlockSpec(shape, lambda b: (0,) * len(shape))

    return pl.pallas_call(
        _nsa_prep_kernel,
        grid=(B,),
        in_specs=[pl.BlockSpec((cw // LANES, S, LANES), lambda b: (0, b, 0)),
                  full((CMP_BLOCK, cw)), full((CMP_BLOCK, cw, cw)), full((cw, 3 * LANES)),
                  pl.BlockSpec((S, KK_W), lambda b: (b, ZM_KK // KK_W)),
                  pl.BlockSpec((S, VV_W), lambda b: (b, ZM_VV // VV_W)),
                  full((4, S, LANES)), full((2, n, LANES))],
        out_specs=[pl.BlockSpec((1, 8, S, LANES), lambda b: (b, 0, 0, 0)),
                   pl.BlockSpec((1, 2, NSA_G, VT_ROWS, S), lambda b: (b, 0, 0, 0, 0)),
                   pl.BlockSpec((1, 4, n, LANES), lambda b: (b, 0, 0, 0)),
                   pl.BlockSpec((1, LANES, n), lambda b: (b, 0, 0))],
        out_shape=[jax.ShapeDtypeStruct((B, 8, S, LANES), BF16),
                   jax.ShapeDtypeStruct((B, 2, NSA_G, VT_ROWS, S), BF16),
                   jax.ShapeDtypeStruct((B, 4, n, LANES), BF16),
                   jax.ShapeDtypeStruct((B, LANES, n), BF16)],
        compiler_params=_cparams(("arbitrary",)),
        name="nsa_prep",
    )(zc, pe, w1, w2, zm, zm, _key_features(S), _cmp_features(S))


def _nsa_kernel(q_ref, kaug_ref, vt_ref, kcaug_ref, vct_ref, zs_ref, ovt_ref, out_ref,
                qat_ref, ocmp_ref, m_ref, acc_ref):
    S = kaug_ref.shape[2]
    tq = q_ref.shape[0]
    ncmp = kcaug_ref.shape[2]
    nsel = S // SEL_BLOCK
    n_pairs = NSA_H // 2
    qi = pl.program_id(1)
    t0 = pl.multiple_of(qi * tq, tq)
    t_row = t0 + lax.broadcasted_iota(jnp.int32, (1, tq), 1)

    for k in range(NSA_H // 2):
        q_t = (q_ref[:, k * LANES:(k + 1) * LANES].astype(F32) * (NSA_DH ** -0.5)).T.astype(BF16)
        for e in range(2):
            qat_ref[(k // 2) * 2 + e, e * NSA_DH:(e + 1) * NSA_DH, (k % 2) * tq:(k % 2 + 1) * tq] = (
                q_t[e * NSA_DH:(e + 1) * NSA_DH])
    feat_row = lax.broadcasted_iota(jnp.int32, (NSA_DH, 2 * tq), 0)
    second_head = lax.broadcasted_iota(jnp.int32, (NSA_DH, 2 * tq), 1) >= tq
    for pair in range(n_pairs):
        g, e = pair // 2, pair % 2
        slopes = [2.0 ** (-8.0 * (g * NSA_R + 2 * s + e + 1.0) / NSA_H) for s in range(2)]
        slope = jnp.where(second_head, slopes[1], slopes[0])
        feat = jnp.where(feat_row == FEAT_HI, slope * 64.0, jnp.where(feat_row == FEAT_LO, slope, 0.0))
        qat_ref[pair, NSA_DH * (1 - e):NSA_DH * (2 - e), :] = feat.astype(BF16)

    ki = lax.broadcasted_iota(jnp.int32, (TK, 2 * tq), 0)
    qu = lax.broadcasted_iota(jnp.int32, (TK, 2 * tq), 1)
    qu = jnp.where(qu >= tq, qu - tq, qu)
    tile_valid = {MASK_CAUSAL: ki <= qu,
                  MASK_WINDOW_EDGE: ki > qu}

    c_col = lax.broadcasted_iota(jnp.int32, (ncmp, 1), 0)
    t_row2 = jnp.concatenate([t_row, t_row], axis=1)
    valid_c = c_col * CMP_STRIDE + (CMP_BLOCK - 1) <= t_row2
    j_col = lax.broadcasted_iota(jnp.int32, (nsel, 1), 0)
    j_f = j_col.astype(F32)
    jt = t_row >> 6
    forced = jnp.logical_or(j_col == 0, jnp.logical_or(j_col == jt, j_col == jt - 1))
    future = j_col > jt

    for g in range(NSA_G):
        psum = jnp.zeros((ncmp, tq), F32)
        for e in range(2):
            pair = g * 2 + e
            sc = jnp.where(valid_c, _dot(kcaug_ref[0, pair], qat_ref[pair]), NEG)
            ex = jnp.where(valid_c, jnp.exp(sc - jnp.max(sc, axis=0, keepdims=True)), 0.0)
            den = jnp.sum(ex, axis=0, keepdims=True)
            p = ex * (1.0 / jnp.where(den > 0.0, den, 1.0))
            psum = psum + p[:, :tq] + p[:, tq:]
            ocmp_ref[pair] = _dot(vct_ref[0, g * NSA_DH:(g + 1) * NSA_DH, :], p.astype(BF16))

        p_hi = psum.astype(BF16)
        p_lo = (psum - p_hi.astype(F32)).astype(BF16)
        imp = _dot(ovt_ref[...], p_hi) + _dot(ovt_ref[...], p_lo)
        val = jnp.where(future, NEG, jnp.where(forced, BIG, imp))
        penalty = jnp.full((nsel, tq), NEG, F32)
        for _ in range(min(TOP_N, nsel)):
            best = jnp.max(val, axis=0, keepdims=True)
            first = jnp.min(jnp.where(val == best, j_f, float(nsel)), axis=0, keepdims=True)
            hit = j_f == first
            penalty = jnp.where(hit, 0.0, penalty)
            val = jnp.where(hit, TAKEN, val)
        penalty = penalty.astype(BF16)
        for e in range(2):
            first = NSA_DH * (1 - e) + FEAT_SEL
            qat_ref[g * 2 + e, first:first + nsel, :] = jnp.concatenate([penalty, penalty], axis=1)

    m_ref[...] = jnp.full(m_ref.shape, NEG, F32)
    acc_ref[...] = jnp.zeros(acc_ref.shape, F32)

    def attend(br, k0, mask):
        def scores(pair):
            return _dot(kaug_ref[0, br * n_pairs + pair, pl.ds(k0, TK), :], qat_ref[pair])

        s_queue = [scores(i) for i in range(SCORE_LOOKAHEAD)]
        for pair in range(n_pairs):
            s_t = s_queue.pop(0)
            if pair + SCORE_LOOKAHEAD < n_pairs:
                s_queue.append(scores(pair + SCORE_LOOKAHEAD))
            g = pair // 2
            slot = br * n_pairs + pair
            if mask is not None:
                s_t = jnp.where(tile_valid[mask], s_t, NEG)
            m_old = m_ref[slot]
            m_new = jnp.maximum(m_old, jnp.max(s_t, axis=0, keepdims=True))
            alpha = jnp.exp(m_old - m_new)
            p = jnp.exp(s_t - m_new).astype(BF16)
            acc_ref[slot] = alpha * acc_ref[slot] + _dot(vt_ref[0, br, g, :, pl.ds(k0, TK)], p)
            m_ref[slot] = m_new

    attend(BR_WIN, t0, MASK_CAUSAL)
    n_back = WINDOW // TK
    for d in range(1, n_back + 1):
        @pl.when(qi >= d)
        def _(d=d):
            attend(BR_WIN, pl.multiple_of(t0 - d * TK, TK), MASK_WINDOW_EDGE if d == n_back else None)

    def sel_body(kb, carry):
        attend(BR_SEL, pl.multiple_of(kb * TK, TK), None)
        return carry

    lax.fori_loop(0, qi, sel_body, 0)
    attend(BR_SEL, t0, MASK_CAUSAL)

    gates_t = jax.nn.sigmoid(zs_ref[...]).T
    for hp in range(NSA_H // 2):
        comb = []
        for e in range(2):
            h = 2 * hp + e
            c0 = ZS_GATE + NSA_NB * h
            pair = (h // NSA_R) * 2 + e
            cols = slice(((h % NSA_R) // 2) * tq, ((h % NSA_R) // 2 + 1) * tq)
            sel, win = BR_SEL * n_pairs + pair, BR_WIN * n_pairs + pair
            o_sel = acc_ref[sel, 0:NSA_DH, cols] * (1.0 / acc_ref[sel, NSA_DH:NSA_DH + 1, cols])
            o_win = acc_ref[win, 0:NSA_DH, cols] * (1.0 / acc_ref[win, NSA_DH:NSA_DH + 1, cols])
            comb.append(gates_t[c0:c0 + 1] * ocmp_ref[pair, :, cols] + gates_t[c0 + 1:c0 + 2] * o_sel
                        + gates_t[c0 + 2:c0 + 3] * o_win)
        out_ref[:, hp * LANES:(hp + 1) * LANES] = jnp.concatenate(comb, axis=0).T.astype(BF16)


def _overlap_t(S):
    c = np.arange(S // CMP_STRIDE)[None, :]
    j = np.arange(S // SEL_BLOCK)[:, None]
    ov = (c * CMP_STRIDE <= j * SEL_BLOCK + SEL_BLOCK - 1) & (c * CMP_STRIDE + CMP_BLOCK - 1 >= j * SEL_BLOCK)
    ov &= c < (S - CMP_BLOCK) // CMP_STRIDE + 1
    return jnp.asarray(ov.astype(np.float32), dtype=BF16)


def _nsa(zm, zs, kaug, vt, kcaug, vct, S):
    M = zm.shape[0]
    B = M // S
    tq = min(TQ, S)
    assert tq == TK and WINDOW % TK == 0 and S % TK == 0
    nq = S // tq
    n = S // CMP_STRIDE
    nsel = S // SEL_BLOCK
    return pl.pallas_call(
        _nsa_kernel,
        grid=(B, nq),
        in_specs=[pl.BlockSpec((tq, NSA_W), lambda b, i: (b * nq + i, ZM_NQ // NSA_W)),
                  pl.BlockSpec((1, 8, S, LANES), lambda b, i: (b, 0, 0, 0)),
                  pl.BlockSpec((1, 2, NSA_G, VT_ROWS, S), lambda b, i: (b, 0, 0, 0, 0)),
                  pl.BlockSpec((1, 4, n, LANES), lambda b, i: (b, 0, 0, 0)),
                  pl.BlockSpec((1, LANES, n), lambda b, i: (b, 0, 0)),
                  pl.BlockSpec((tq, ZS_W), lambda b, i: (b * nq + i, 0)),
                  pl.BlockSpec((nsel, n), lambda b, i: (0, 0))],
        out_specs=pl.BlockSpec((tq, NSA_W), lambda b, i: (b * nq + i, 0)),
        out_shape=jax.ShapeDtypeStruct((M, NSA_W), BF16),
        scratch_shapes=[pltpu.VMEM((NSA_H // 2, LANES, 2 * tq), BF16),
                        pltpu.VMEM((NSA_H // 2, NSA_DH, 2 * tq), F32),
                        pltpu.VMEM((NSA_H, 1, 2 * tq), F32),
                        pltpu.VMEM((NSA_H, VT_ROWS, 2 * tq), F32)],
        compiler_params=_cparams(("arbitrary", "arbitrary")),
        name="nsa_attention",
    )(zm, kaug, vt, kcaug, vct, zs, _overlap_t(S))


def _merge_kernel(gu0_ref, gv0_ref, gu_ref, gv_ref, lng_ref, lnb_ref, ws_ref, bst_ref, b_ref, c_ref,
                  ga_ref, gb_ref, gc_ref, x_ref, gt_ref, wa_ref, wb_ref, wc_ref, wo_ref, o_ref, a_ref):
    i = pl.program_id(0)

    @pl.when(i == 0)
    def _():
        _gmlp_rows(gu0_ref, gv0_ref, lng_ref, lnb_ref, ws_ref, bst_ref, a_ref.at[0])

    a = a_ref[i % 2]
    merged = (jax.nn.sigmoid(ga_ref[...].astype(F32)) * _dot(a, wa_ref[...])
              + jax.nn.sigmoid(gb_ref[...].astype(F32)) * _dot(b_ref[...], wb_ref[...])
              + jax.nn.sigmoid(gc_ref[...].astype(F32)) * _dot(c_ref[...], wc_ref[...]))
    o_ref[...] = x_ref[...] + gt_ref[0] * _dot(merged.astype(BF16), wo_ref[...])
    _gmlp_rows(gu_ref, gv_ref, lng_ref, lnb_ref, ws_ref, bst_ref, a_ref.at[(i + 1) % 2])


def _merge(yb, yc, zm, x2, gt, ln_g, ln_b, ws, bs, wa, wb, wc, wo, S):
    M, D = x2.shape
    tm = min(TM_MERGE, S)
    per_b = S // tm

    def rows(w, jcol=0):
        return pl.BlockSpec((tm, w), lambda i: (i, jcol))

    def next_rows(jcol):
        return pl.BlockSpec((tm, GM_W), lambda i: (jnp.minimum(i + 1, M // tm - 1), jcol))

    def full(shape):
        return pl.BlockSpec(shape, lambda i: (0,) * len(shape))

    return pl.pallas_call(
        _merge_kernel,
        grid=(M // tm,),
        in_specs=[rows(GM_W, ZM_GU // GM_W), rows(GM_W, ZM_GV // GM_W),
                  next_rows(ZM_GU // GM_W), next_rows(ZM_GV // GM_W), full((1, GM_W)), full((1, GM_W)),
                  full((GM_GROUPS, GM_CHUNK, GM_CHUNK)), full((GM_CHUNK, GM_GROUPS)),
                  rows(ML_W), rows(NSA_W),
                  rows(D, ZM_GA // D), rows(D, ZM_GBR // D), rows(D, ZM_GC // D),
                  rows(D), pl.BlockSpec((1, 1, D), lambda i: (i // per_b, 0, 0)),
                  full((GM_W, D)), full((ML_W, D)), full((NSA_W, D)), full((D, D))],
        out_specs=rows(D),
        out_shape=jax.ShapeDtypeStruct((M, D), F32),
        scratch_shapes=[pltpu.VMEM((2, tm, GM_W), BF16)],
        compiler_params=_cparams(("arbitrary",)),
        name="gmlp_merge_out",
    )(zm, zm, zm, zm, ln_g.reshape(1, GM_W), ln_b.reshape(1, GM_W), ws, bs.T, yb, yc, zm, zm, zm, x2, gt, wa, wb, wc, wo)


def _mlp_kernel(x_ref, g_ref, sc_ref, sh_ref, gt_ref, w1_ref, w2_ref, gf_ref, o_ref, acc_ref, *, final_norm):
    x = x_ref[...]
    h = _modulated_norm(x, g_ref[...], sc_ref[0], sh_ref[0]).astype(BF16)
    for c in range(D_FF // FF_CHUNK):
        mid = jnp.square(jnp.maximum(_dot(h, w1_ref[:, c * FF_CHUNK:(c + 1) * FF_CHUNK]), 0.0)).astype(BF16)
        upd = _dot(mid, w2_ref[c * FF_CHUNK:(c + 1) * FF_CHUNK, :])
        if c == 0:
            acc_ref[...] = upd
        else:
            acc_ref[...] += upd
    y = x + gt_ref[0] * acc_ref[...]
    if final_norm:
        y = (y * lax.rsqrt(jnp.mean(y * y, axis=-1, keepdims=True) + EPS)) * gf_ref[...]
    o_ref[...] = y


def _mlp(x2, g, sc, sh, gt, w1, w2, g_final, S, final_norm):
    M, D = x2.shape
    tm = min(TM_MLP, S)
    per_b = S // tm
    mod = pl.BlockSpec((1, 1, D), lambda i: (i // per_b, 0, 0))
    return pl.pallas_call(
        functools.partial(_mlp_kernel, final_norm=final_norm),
        grid=(M // tm,),
        in_specs=[pl.BlockSpec((tm, D), lambda i: (i, 0)),
                  pl.BlockSpec((1, D), lambda i: (0, 0)),
                  mod, mod, mod,
                  pl.BlockSpec((D, D_FF), lambda i: (0, 0)),
                  pl.BlockSpec((D_FF, D), lambda i: (0, 0)),
                  pl.BlockSpec((1, D), lambda i: (0, 0))],
        out_specs=pl.BlockSpec((tm, D), lambda i: (i, 0)),
        out_shape=jax.ShapeDtypeStruct((M, D), F32),
        scratch_shapes=[pltpu.VMEM((tm, D), F32)],
        compiler_params=_cparams(("arbitrary",)),
        name="relu2_mlp",
    )(x2, g, sc, sh, gt, w1, w2, g_final)


def _w_cols(w, name):
    return w[:, _OFFS[name]:_OFFS[name] + _SIZE[name]]


def _dup_heads(w):
    parts = []
    for g in range(NSA_G):
        blk = w[:, g * NSA_DH:(g + 1) * NSA_DH]
        parts += [blk, blk]
    return jnp.concatenate(parts, axis=1)


def _proj_weights(w):
    D = w.shape[0]
    wm = jnp.concatenate([_w_cols(w, n) for n in ('ga', 'gbr', 'gc', 'gu', 'gv', 'mq', 'mk', 'mv', 'mo', 'nq')]
                         + [_dup_heads(_w_cols(w, 'nks')), _dup_heads(_w_cols(w, 'nkw')),
                            _w_cols(w, 'nvs'), _w_cols(w, 'nvw')], axis=1)
    wc = jnp.concatenate([_w_cols(w, 'nkc'), _w_cols(w, 'nvc')], axis=1)
    ws = jnp.concatenate([_w_cols(w, 'ngate'), _w_cols(w, 'mi'), _w_cols(w, 'mf'),
                          jnp.zeros((D, ZS_W - NSA_H * NSA_NB - 2 * ML_H), w.dtype)], axis=1)
    return wm.astype(BF16), wc.astype(BF16), ws.astype(BF16)


def kernel(x, c, g_norm1, g_norm2, w_ada, b_ada, w_in, gm_ln_g, gm_ln_b, gm_ws, gm_bs, ml_conv_w, ml_conv_b,
           ml_gate_b, ml_norm_g, nsa_pe_k, nsa_pe_v, nsa_phi_k1, nsa_phi_k2, nsa_phi_v1, nsa_phi_v2,
           w_up_a, w_up_b, w_up_c, w_out, w_mlp1, w_mlp2, g_final):
    B, S, D = x.shape
    depth = w_in.shape[0]
    M = B * S
    mod = _ada(c, w_ada, b_ada)
    x2 = x.reshape(M, D)
    for l in range(depth):
        sh1, sc1, gt1, sh2, sc2, gt2 = [mod[l, :, i * D:(i + 1) * D].reshape(B, 1, D) for i in range(6)]
        wm, wc, ws = _proj_weights(w_in[l])
        zm, zc, zs = _inproj(x2, g_norm1[l].reshape(1, D), sc1, sh1, wm, wc, ws, S)
        yb = _mlstm(zm, zs, ml_conv_w[l], ml_conv_b[l], ml_gate_b[l], ml_norm_g[l], S)
        kaug, vt, kcaug, vct = _nsa_prep(zc, zm, nsa_pe_k[l], nsa_pe_v[l], nsa_phi_k1[l], nsa_phi_k2[l],
                                         nsa_phi_v1[l], nsa_phi_v2[l], S)
        yc = _nsa(zm, zs, kaug, vt, kcaug, vct, S)
        x2 = _merge(yb, yc, zm, x2, gt1, gm_ln_g[l], gm_ln_b[l], gm_ws[l], gm_bs[l], w_up_a[l].astype(BF16),
                    w_up_b[l].astype(BF16), w_up_c[l].astype(BF16), w_out[l].astype(BF16), S)
        x2 = _mlp(x2, g_norm2[l].reshape(1, D), sc2, sh2, gt2, w_mlp1[l].astype(BF16), w_mlp2[l].astype(BF16),
                  g_final.reshape(1, D), S, final_norm=(l == depth - 1))
    return x2.reshape(B, S, D)
```

```python
import functools

import numpy as np
import jax
import jax.numpy as jnp
from jax import lax
from jax.experimental import pallas as pl
from jax.experimental.pallas import tpu as pltpu

F32 = jnp.float32
BF16 = jnp.bfloat16

D_MODEL = 1024
GM_W = 512
GM_GROUPS = 4
GM_CHUNK = 128
ML_H = 4
ML_DH = 128
ML_W = ML_H * ML_DH
CONV_K = 4
NSA_H = 8
NSA_G = 2
NSA_R = NSA_H // NSA_G
NSA_DH = 64
NSA_W = NSA_H * NSA_DH
NSA_KV = NSA_G * NSA_DH
NSA_NB = 3
CMP_BLOCK = 32
CMP_STRIDE = 16
SEL_BLOCK = 64
TOP_N = 8
WINDOW = 512
D_FF = 4 * D_MODEL
EPS = 1e-6
NEG = -1e30
BIG = 1e4
TAKEN = -3e38
SPLIT_SIZES = (GM_W, GM_W, ML_W, ML_W, ML_W, ML_W, ML_H, ML_H, NSA_W, NSA_KV, NSA_KV, NSA_KV, NSA_KV,
               NSA_KV, NSA_KV, NSA_H * NSA_NB, D_MODEL, D_MODEL, D_MODEL)
SPLIT_NAMES = ('gu', 'gv', 'mq', 'mk', 'mv', 'mo', 'mi', 'mf', 'nq', 'nkc', 'nvc', 'nks', 'nvs', 'nkw', 'nvw',
               'ngate', 'ga', 'gbr', 'gc')
_OFFS = dict(zip(SPLIT_NAMES, np.concatenate([[0], np.cumsum(SPLIT_SIZES)[:-1]]).tolist()))
_SIZE = dict(zip(SPLIT_NAMES, SPLIT_SIZES))

LANES = 128
VMEM_LIMIT = 56 * 1024 * 1024

ZM_GA, ZM_GBR, ZM_GC = 0, 1024, 2048
ZM_GU, ZM_GV = 3072, 3584
ZM_MQ, ZM_MK, ZM_MV, ZM_MO = 4096, 4608, 5120, 5632
ZM_NQ = 6144
ZM_KK = 6656
ZM_VV = 7168
ZM_W = 7424
KK_W = 4 * LANES
VV_W = 2 * LANES
BR_SEL, BR_WIN = 0, 1
MASK_CAUSAL, MASK_WINDOW_EDGE = 0, 1
FEAT_SEL = 0
FEAT_HI = 32
FEAT_LO = 33
ZS_GATE = 0
ZS_MI = 24
ZS_MF = 28
ZS_W = 128

TM_PROJ = 1024
TN_PROJ = 3712
ML_CHUNK = 128
ML_ROWS = 2
CONV_HALO = 16
TQ = 256
TK = 256
SCORE_LOOKAHEAD = 2
VT_ROWS = NSA_DH + 16
TM_MERGE = 512
TM_MLP = 512
FF_CHUNK = 1024


def _dot(a, b):
    return jnp.dot(a, b, preferred_element_type=F32)


def _split3(x):
    x1 = x.astype(BF16)
    r1 = x - x1.astype(F32)
    x2 = r1.astype(BF16)
    x3 = (r1 - x2.astype(F32)).astype(BF16)
    return x1, x2, x3


def _cparams(sem):
    return pltpu.CompilerParams(dimension_semantics=sem, vmem_limit_bytes=VMEM_LIMIT)


def _ada_kernel(c_ref, w_ref, b_ref, o_ref):
    c = c_ref[...]
    cond = c * jax.nn.sigmoid(c)
    c1, c2, c3 = _split3(cond)
    w1, w2, w3 = _split3(w_ref[0])
    acc = _dot(c1, w1) + (_dot(c1, w2) + _dot(c2, w1)) + (_dot(c1, w3) + _dot(c2, w2) + _dot(c3, w1))
    o_ref[0] = acc + b_ref[0]


def _ada(c, w_ada, b_ada):
    L, D, N = w_ada.shape
    B = c.shape[0]
    tn = 1536
    return pl.pallas_call(
        _ada_kernel,
        grid=(L, N // tn),
        in_specs=[pl.BlockSpec((B, D), lambda l, j: (0, 0)),
                  pl.BlockSpec((1, D, tn), lambda l, j: (l, 0, j)),
                  pl.BlockSpec((1, 1, tn), lambda l, j: (l, 0, j))],
        out_specs=pl.BlockSpec((1, B, tn), lambda l, j: (l, 0, j)),
        out_shape=jax.ShapeDtypeStruct((L, B, N), F32),
        compiler_params=_cparams(("arbitrary", "arbitrary")),
        name="ada_mod",
    )(c, w_ada, b_ada.reshape(L, 1, N))


def _modulated_norm(x, g, sc, sh):
    y = x * lax.rsqrt(jnp.mean(x * x, axis=-1, keepdims=True) + EPS)
    return (y * g) * (1.0 + sc) + sh


def _inproj_kernel(x_ref, g_ref, sc_ref, sh_ref, wm_ref, wc_ref, ws_ref, zm_ref, zc_ref, zs_ref, h_ref):
    @pl.when(pl.program_id(1) == 0)
    def _():
        h = _modulated_norm(x_ref[...], g_ref[...], sc_ref[0], sh_ref[0]).astype(BF16)
        h_ref[...] = h
        zc = _dot(h, wc_ref[...])
        for part in range(zc_ref.shape[0]):
            zc_ref[part] = zc[:, part * LANES:(part + 1) * LANES]
        zs_ref[...] = _dot(h, ws_ref[...])

    zm_ref[...] = _dot(h_ref[...], wm_ref[...]).astype(BF16)


def _inproj(x2, g, sc, sh, wm, wc, ws, S):
    M, D = x2.shape
    tm, tn = min(TM_PROJ, S), TN_PROJ
    per_b = S // tm
    return pl.pallas_call(
        _inproj_kernel,
        grid=(M // tm, ZM_W // tn),
        in_specs=[pl.BlockSpec((tm, D), lambda i, j: (i, 0)),
                  pl.BlockSpec((1, D), lambda i, j: (0, 0)),
                  pl.BlockSpec((1, 1, D), lambda i, j: (i // per_b, 0, 0)),
                  pl.BlockSpec((1, 1, D), lambda i, j: (i // per_b, 0, 0)),
                  pl.BlockSpec((D, tn), lambda i, j: (0, j)),
                  pl.BlockSpec((D, 2 * NSA_KV), lambda i, j: (0, 0)),
                  pl.BlockSpec((D, ZS_W), lambda i, j: (0, 0))],
        out_specs=[pl.BlockSpec((tm, tn), lambda i, j: (i, j)),
                   pl.BlockSpec((2 * NSA_KV // LANES, tm, LANES), lambda i, j: (0, i, 0)),
                   pl.BlockSpec((tm, ZS_W), lambda i, j: (i, 0))],
        out_shape=[jax.ShapeDtypeStruct((M, ZM_W), BF16),
                   jax.ShapeDtypeStruct((2 * NSA_KV // LANES, M, LANES), F32),
                   jax.ShapeDtypeStruct((M, ZS_W), F32)],
        scratch_shapes=[pltpu.VMEM((tm, D), BF16)],
        compiler_params=_cparams(("arbitrary", "arbitrary")),
        name="in_proj",
    )(x2, g, sc, sh, wm, wc, ws)


def _gmlp_rows(u_ref, v_ref, lng_ref, lnb_ref, ws_ref, bst_ref, o_ref):
    ts = u_ref.shape[0]
    dg = GM_W // GM_GROUPS
    row = lax.broadcasted_iota(jnp.int32, (GM_CHUNK, GM_CHUNK), 0)
    col = lax.broadcasted_iota(jnp.int32, (GM_CHUNK, GM_CHUNK), 1)
    ws = [jnp.where(row >= col, ws_ref[g], 0.0).astype(BF16) for g in range(GM_GROUPS)]
    lng = lng_ref[...]
    lnb = lnb_ref[...]
    for c in range(ts // GM_CHUNK):
        r0 = c * GM_CHUNK
        u = jax.nn.gelu(u_ref[r0:r0 + GM_CHUNK, :].astype(F32))
        v = jax.nn.gelu(v_ref[r0:r0 + GM_CHUNK, :].astype(F32))
        mu = jnp.mean(v, axis=-1, keepdims=True)
        var = jnp.mean(jnp.square(v - mu), axis=-1, keepdims=True)
        vb = ((v - mu) * lax.rsqrt(var + EPS) * lng + lnb).astype(BF16)
        for g in range(GM_GROUPS):
            mixed = _dot(ws[g], vb[:, g * dg:(g + 1) * dg]) + bst_ref[:, g:g + 1]
            o_ref[r0:r0 + GM_CHUNK, g * dg:(g + 1) * dg] = (u[:, g * dg:(g + 1) * dg] * mixed).astype(BF16)


def _log_sigmoid(x):
    return jnp.minimum(x, 0.0) - jnp.log1p(jnp.exp(-jnp.abs(x)))


def _conv_silu(x_ext, w, b):
    n = x_ext.shape[0] - CONV_HALO
    y = b
    for j in range(CONV_K):
        sh = CONV_K - 1 - j
        xs = x_ext if sh == 0 else pltpu.roll(x_ext, sh, axis=0)
        y = y + xs[CONV_HALO:CONV_HALO + n] * w[j:j + 1]
    return y * jax.nn.sigmoid(y)


def _mlstm_kernel(q_ref, k_ref, v_ref, o_ref, zs_ref, cw_ref, cb_ref, gb_ref, ng_ref, out_ref, c_ref, *, S):
    n_rows = q_ref.shape[0] // S
    L = min(ML_CHUNK, S)
    c_ref[...] = jnp.zeros_like(c_ref)
    row = lax.broadcasted_iota(jnp.int32, (L, L), 0)
    col = lax.broadcasted_iota(jnp.int32, (L, L), 1)
    tril = row >= col
    tril_b = jnp.where(tril, 1.0, 0.0).astype(BF16)
    triu_b = jnp.where(row <= col, 1.0, 0.0).astype(BF16)
    ones_blk = jnp.ones((L, ML_DH), BF16)
    time_row = lax.broadcasted_iota(jnp.int32, (L, ZS_W), 0)
    gb = gb_ref[...]
    hs = [slice(h * ML_DH, (h + 1) * ML_DH) for h in range(ML_H)]
    chains = [(r, h) for r in range(n_rows) for h in range(ML_H)]

    def conv_rows(src, base, t0, c, c0):
        cur = src[pl.ds(pl.multiple_of(base + t0, L), L), :].astype(F32)
        halo0 = pl.multiple_of(base + jnp.maximum(t0 - CONV_HALO, 0), CONV_HALO)
        halo = jnp.where(c > 0, src[pl.ds(halo0, CONV_HALO), :].astype(F32), 0.0)
        return _conv_silu(jnp.concatenate([halo, cur], axis=0), cw_ref[:, c0:c0 + ML_W], cb_ref[:, c0:c0 + ML_W])

    def chunk_body(c, m_rows):
        t0 = pl.multiple_of(c * L, L)
        q, k_t, r_rows, u_cols, m_cols, u_last, w_prev_row = {}, {}, [], [], [], [], []
        for r in range(n_rows):
            base = r * S
            rows = pl.ds(pl.multiple_of(base + t0, L), L)
            gi = zs_ref[rows, :] + gb
            lf_cols = _log_sigmoid(gi)
            gi_t = gi.T
            lf_rows = _log_sigmoid(gi_t[ZS_MI:ZS_MI + 2 * ML_H])
            c1, c2, c3 = _split3(lf_cols)
            b_cols = _dot(tril_b, c1) + _dot(tril_b, c2) + _dot(tril_b, c3)
            r1, r2, r3 = _split3(lf_rows)
            b_rows = _dot(r1, triu_b) + _dot(r2, triu_b) + _dot(r3, triu_b)
            r_rows.append(gi_t[ZS_MI:ZS_MI + ML_H] - b_rows[ML_H:2 * ML_H])
            b_at_i = pltpu.roll(b_cols, ZS_W - (ZS_MF - ZS_MI), axis=1)
            u = gi - b_at_i
            shift = 1
            while shift < L:
                u = jnp.maximum(u, jnp.where(time_row >= shift, pltpu.roll(u, shift, axis=0), NEG))
                shift *= 2
            u = jnp.maximum(u, m_rows[r])
            u_cols.append(u)
            m_cols.append(b_at_i + u)
            u_last.append(u[L - 1:L, :])
            w_prev_row.append(jnp.exp(m_rows[r] - u_last[r]))
            q_all = conv_rows(q_ref, base, t0, c, 0).astype(BF16)
            k_all = conv_rows(k_ref, base, t0, c, ML_W) * (ML_DH ** -0.5)
            for h in range(ML_H):
                q[r, h] = q_all[:, hs[h]]
                k_t[r, h] = k_all[:, hs[h]].T
        cmat = {ch: c_ref[ch[0] * ML_H + ch[1]] for ch in chains}
        qk = {ch: _dot(q[ch], k_t[ch].astype(BF16)) for ch in chains}
        qc = {ch: _dot(q[ch], cmat[ch].astype(BF16)) for ch in chains}
        v_aug = {(r, h): jnp.concatenate([v_ref[pl.ds(pl.multiple_of(r * S + t0, L), L), hs[h]], ones_blk], axis=1)
                 for r, h in chains}
        u_rep, gate = {}, {}
        for ch in chains:
            r, h = ch
            lane_h = ZS_MI + h
            u_rep[ch] = jnp.broadcast_to(u_cols[r][:, lane_h:lane_h + 1], (L, ML_DH))
            gate[ch] = jnp.exp(jnp.where(tril, r_rows[r][h:h + 1, :] - u_rep[ch], NEG))
            w_s = jnp.exp(r_rows[r][h:h + 1, :] - u_last[r][:, lane_h:lane_h + 1])
            k_w = (k_t[ch] * w_s).astype(BF16)
            c_ref[r * ML_H + h] = w_prev_row[r][:, lane_h:lane_h + 1] * cmat[ch] + _dot(k_w, v_aug[ch])
        for ch in chains:
            r, h = ch
            lane_h = ZS_MI + h
            rows = pl.ds(pl.multiple_of(r * S + t0, L), L)
            w_inter = jnp.exp(m_rows[r][:, lane_h:lane_h + 1] - u_rep[ch])
            num_aug = (_dot((qk[ch] * gate[ch]).astype(BF16), v_aug[ch])
                       + jnp.concatenate([w_inter, w_inter], axis=1) * qc[ch])
            m_rep = jnp.broadcast_to(m_cols[r][:, lane_h:lane_h + 1], (L, ML_DH))
            hval = num_aug[:, :ML_DH] / jnp.maximum(jnp.abs(num_aug[:, ML_DH:]), jnp.exp(-m_rep))
            mu = jnp.mean(hval, axis=-1, keepdims=True)
            var = jnp.mean(jnp.square(hval - mu), axis=-1, keepdims=True)
            hn = (hval - mu) * lax.rsqrt(var + EPS) * ng_ref[:, hs[h]]
            og = jax.nn.sigmoid(o_ref[rows, hs[h]].astype(F32))
            out_ref[rows, hs[h]] = (og * hn).astype(BF16)
        return tuple(m[L - 1:L, :] for m in m_cols)

    lax.fori_loop(0, S // L, chunk_body, tuple(jnp.zeros((1, ZS_W), F32) for _ in range(n_rows)))


def _mlstm(zm, zs, conv_w, conv_b, gate_b, norm_g, S):
    M = zm.shape[0]
    n_rows = ML_ROWS if (M // S) % ML_ROWS == 0 else 1
    gb_row = jnp.zeros((1, ZS_W), F32).at[0, ZS_MI:ZS_MI + 2 * ML_H].set(gate_b)

    def col(off):
        return pl.BlockSpec((n_rows * S, ML_W), lambda b: (b, off // ML_W))

    return pl.pallas_call(
        functools.partial(_mlstm_kernel, S=S),
        grid=(M // (n_rows * S),),
        in_specs=[col(ZM_MQ), col(ZM_MK), col(ZM_MV), col(ZM_MO),
                  pl.BlockSpec((n_rows * S, ZS_W), lambda b: (b, 0)),
                  pl.BlockSpec((CONV_K, 2 * ML_W), lambda b: (0, 0)),
                  pl.BlockSpec((1, 2 * ML_W), lambda b: (0, 0)),
                  pl.BlockSpec((1, ZS_W), lambda b: (0, 0)),
                  pl.BlockSpec((1, ML_W), lambda b: (0, 0))],
        out_specs=pl.BlockSpec((n_rows * S, ML_W), lambda b: (b, 0)),
        out_shape=jax.ShapeDtypeStruct((M, ML_W), BF16),
        scratch_shapes=[pltpu.VMEM((n_rows * ML_H, ML_DH, 2 * ML_DH), F32)],
        compiler_params=_cparams(("arbitrary",)),
        name="mlstm_mixer",
    )(zm, zm, zm, zm, zs, conv_w, conv_b.reshape(1, 2 * ML_W), gb_row, norm_g.reshape(1, ML_W))


def _nsa_prep_kernel(x_ref, pe_ref, w1_ref, w2_ref, kk_ref, vv_ref, feat_ref, featc_ref,
                     kaug_ref, vt_ref, kcaug_ref, vct_ref):
    S = kk_ref.shape[0]
    n = S // CMP_STRIDE
    cw = 2 * NSA_KV
    half = CMP_BLOCK // 2
    acc_a = jnp.zeros((n, cw), F32)
    acc_b = jnp.zeros((n, cw), F32)
    for j in range(half):
        xj = jnp.concatenate([x_ref[part, pl.ds(j, n, stride=CMP_STRIDE), :] for part in range(cw // LANES)],
                             axis=1)
        acc_a = acc_a + _dot((xj + pe_ref[j:j + 1, :]).astype(BF16), w1_ref[j])
        acc_b = acc_b + _dot((xj + pe_ref[half + j:half + j + 1, :]).astype(BF16), w1_ref[half + j])
    pre = acc_a + pltpu.roll(acc_b, n - 1, axis=0)
    cmp = _dot(jax.nn.gelu(pre).astype(BF16), w2_ref[...])

    lane = lax.broadcasted_iota(jnp.int32, (1, LANES), 1)
    keep = [jnp.where(lane < NSA_DH, 1.0, 0.0).astype(BF16), jnp.where(lane >= NSA_DH, 1.0, 0.0).astype(BF16)]
    for g in range(NSA_G):
        kd = cmp[:, g * LANES:(g + 1) * LANES].astype(BF16)
        for e in range(2):
            kcaug_ref[0, g * 2 + e] = kd * keep[e] + featc_ref[e]
    vct_ref[0] = cmp[:, NSA_G * LANES:(NSA_G + 1) * LANES].T.astype(BF16)
    tb = min(TK, S)
    for br in range(2):
        for g in range(NSA_G):
            kd = kk_ref[:, (br * NSA_G + g) * LANES:(br * NSA_G + g + 1) * LANES]
            for e in range(2):
                kaug_ref[0, (br * NSA_G + g) * 2 + e] = kd * keep[e] + feat_ref[br * 2 + e]
        for c in range(S // tb):
            v_t = vv_ref[c * tb:(c + 1) * tb, br * LANES:(br + 1) * LANES].astype(F32).T.astype(BF16)
            for g in range(NSA_G):
                vt_ref[0, br, g, 0:NSA_DH, c * tb:(c + 1) * tb] = v_t[g * NSA_DH:(g + 1) * NSA_DH]
                vt_ref[0, br, g, NSA_DH:VT_ROWS, c * tb:(c + 1) * tb] = jnp.ones((VT_ROWS - NSA_DH, tb), BF16)


def _blockdiag(blocks):
    n = len(blocks)
    rows = []
    for i, blk in enumerate(blocks):
        rows.append(jnp.concatenate([blk if j == i else jnp.zeros((blk.shape[0], blocks[j].shape[1]), blk.dtype)
                                     for j in range(n)], axis=1))
    return jnp.concatenate(rows, axis=0)


def _key_features(S):
    pos = np.arange(S)
    f = np.zeros((4, S, LANES), np.float32)
    for e in range(2):
        base = NSA_DH * (1 - e)
        f[e, pos, base + FEAT_SEL + pos // SEL_BLOCK] = 1.0
        for br in range(2):
            f[br * 2 + e, :, base + FEAT_HI] = pos // 64
            f[br * 2 + e, :, base + FEAT_LO] = pos % 64
    return jnp.asarray(f, dtype=BF16)


def _cmp_features(S):
    c = np.arange(S // CMP_STRIDE)
    hi = (c * CMP_STRIDE) // 64
    lo = c * CMP_STRIDE + (CMP_BLOCK - 1) * 0.5 - 64 * hi
    f = np.zeros((2, c.size, LANES), np.float32)
    for e in range(2):
        base = NSA_DH * (1 - e)
        f[e, :, base + FEAT_HI] = hi
        f[e, :, base + FEAT_LO] = lo
    return jnp.asarray(f, dtype=BF16)


def _nsa_prep(zc, zm, pe_k, pe_v, phi_k1, phi_k2, phi_v1, phi_v2, S):
    M = zc.shape[1]
    B = M // S
    n = S // CMP_STRIDE
    cw = 2 * NSA_KV
    assert S // SEL_BLOCK <= FEAT_HI and S <= 64 * 64
    pe = jnp.concatenate([pe_k, pe_k, pe_v, pe_v], axis=1)
    k1 = phi_k1.reshape(CMP_BLOCK, NSA_DH, NSA_DH)
    v1 = phi_v1.reshape(CMP_BLOCK, NSA_DH, NSA_DH)
    blocks = jnp.stack([k1, k1, v1, v1], axis=1)
    w1 = jnp.einsum('jaxy,ab->jaxby', blocks, jnp.eye(4, dtype=F32)).reshape(CMP_BLOCK, cw, cw).astype(BF16)
    k2 = jnp.concatenate([phi_k2, phi_k2], axis=1)
    w2 = _blockdiag([k2, k2, phi_v2, phi_v2]).astype(BF16)

    def full(shape):
        return pl.BlockSpec(shape, lambda b: (0,) * len(shape))

    return pl.pallas_call(
        _nsa_prep_kernel,
        grid=(B,),
        in_specs=[pl.BlockSpec((cw // LANES, S, LANES), lambda b: (0, b, 0)),
                  full((CMP_BLOCK, cw)), full((CMP_BLOCK, cw, cw)), full((cw, 3 * LANES)),
                  pl.BlockSpec((S, KK_W), lambda b: (b, ZM_KK // KK_W)),
                  pl.BlockSpec((S, VV_W), lambda b: (b, ZM_VV // VV_W)),
                  full((4, S, LANES)), full((2, n, LANES))],
        out_specs=[pl.BlockSpec((1, 8, S, LANES), lambda b: (b, 0, 0, 0)),
                   pl.BlockSpec((1, 2, NSA_G, VT_ROWS, S), lambda b: (b, 0, 0, 0, 0)),
                   pl.BlockSpec((1, 4, n, LANES), lambda b: (b, 0, 0, 0)),
                   pl.BlockSpec((1, LANES, n), lambda b: (b, 0, 0))],
        out_shape=[jax.ShapeDtypeStruct((B, 8, S, LANES), BF16),
                   jax.ShapeDtypeStruct((B, 2, NSA_G, VT_ROWS, S), BF16),
                   jax.ShapeDtypeStruct((B, 4, n, LANES), BF16),
                   jax.ShapeDtypeStruct((B, LANES, n), BF16)],
        compiler_params=_cparams(("arbitrary",)),
        name="nsa_prep",
    )(zc, pe, w1, w2, zm, zm, _key_features(S), _cmp_features(S))


def _nsa_kernel(q_ref, kaug_ref, vt_ref, kcaug_ref, vct_ref, zs_ref, ovt_ref, out_ref,
                qat_ref, ocmp_ref, m_ref, acc_ref):
    S = kaug_ref.shape[2]
    tq = q_ref.shape[0]
    ncmp = kcaug_ref.shape[2]
    nsel = S // SEL_BLOCK
    n_pairs = NSA_H // 2
    qi = pl.program_id(1)
    t0 = pl.multiple_of(qi * tq, tq)
    t_row = t0 + lax.broadcasted_iota(jnp.int32, (1, tq), 1)

    for k in range(NSA_H // 2):
        q_t = (q_ref[:, k * LANES:(k + 1) * LANES].astype(F32) * (NSA_DH ** -0.5)).T.astype(BF16)
        for e in range(2):
            qat_ref[(k // 2) * 2 + e, e * NSA_DH:(e + 1) * NSA_DH, (k % 2) * tq:(k % 2 + 1) * tq] = (
                q_t[e * NSA_DH:(e + 1) * NSA_DH])
    feat_row = lax.broadcasted_iota(jnp.int32, (NSA_DH, 2 * tq), 0)
    second_head = lax.broadcasted_iota(jnp.int32, (NSA_DH, 2 * tq), 1) >= tq
    for pair in range(n_pairs):
        g, e = pair // 2, pair % 2
        slopes = [2.0 ** (-8.0 * (g * NSA_R + 2 * s + e + 1.0) / NSA_H) for s in range(2)]
        slope = jnp.where(second_head, slopes[1], slopes[0])
        feat = jnp.where(feat_row == FEAT_HI, slope * 64.0, jnp.where(feat_row == FEAT_LO, slope, 0.0))
        qat_ref[pair, NSA_DH * (1 - e):NSA_DH * (2 - e), :] = feat.astype(BF16)

    ki = lax.broadcasted_iota(jnp.int32, (TK, 2 * tq), 0)
    qu = lax.broadcasted_iota(jnp.int32, (TK, 2 * tq), 1)
    qu = jnp.where(qu >= tq, qu - tq, qu)
    tile_valid = {MASK_CAUSAL: ki <= qu,
                  MASK_WINDOW_EDGE: ki > qu}

    c_col = lax.broadcasted_iota(jnp.int32, (ncmp, 1), 0)
    t_row2 = jnp.concatenate([t_row, t_row], axis=1)
    valid_c = c_col * CMP_STRIDE + (CMP_BLOCK - 1) <= t_row2
    j_col = lax.broadcasted_iota(jnp.int32, (nsel, 1), 0)
    j_f = j_col.astype(F32)
    jt = t_row >> 6
    forced = jnp.logical_or(j_col == 0, jnp.logical_or(j_col == jt, j_col == jt - 1))
    future = j_col > jt

    for g in range(NSA_G):
        psum = jnp.zeros((ncmp, tq), F32)
        for e in range(2):
            pair = g * 2 + e
            sc = jnp.where(valid_c, _dot(kcaug_ref[0, pair], qat_ref[pair]), NEG)
            ex = jnp.where(valid_c, jnp.exp(sc - jnp.max(sc, axis=0, keepdims=True)), 0.0)
            den = jnp.sum(ex, axis=0, keepdims=True)
            p = ex * (1.0 / jnp.where(den > 0.0, den, 1.0))
            psum = psum + p[:, :tq] + p[:, tq:]
            ocmp_ref[pair] = _dot(vct_ref[0, g * NSA_DH:(g + 1) * NSA_DH, :], p.astype(BF16))

        p_hi = psum.astype(BF16)
        p_lo = (psum - p_hi.astype(F32)).astype(BF16)
        imp = _dot(ovt_ref[...], p_hi) + _dot(ovt_ref[...], p_lo)
        val = jnp.where(future, NEG, jnp.where(forced, BIG, imp))
        penalty = jnp.full((nsel, tq), NEG, F32)
        for _ in range(min(TOP_N, nsel)):
            best = jnp.max(val, axis=0, keepdims=True)
            first = jnp.min(jnp.where(val == best, j_f, float(nsel)), axis=0, keepdims=True)
            hit = j_f == first
            penalty = jnp.where(hit, 0.0, penalty)
            val = jnp.where(hit, TAKEN, val)
        penalty = penalty.astype(BF16)
        for e in range(2):
            first = NSA_DH * (1 - e) + FEAT_SEL
            qat_ref[g * 2 + e, first:first + nsel, :] = jnp.concatenate([penalty, penalty], axis=1)

    m_ref[...] = jnp.full(m_ref.shape, NEG, F32)
    acc_ref[...] = jnp.zeros(acc_ref.shape, F32)

    def attend(br, tiles):
        stages = [(pl.multiple_of(kb * TK, TK), mask, pair) for kb, mask in tiles for pair in range(n_pairs)]

        def scores(stage):
            k0, _, pair = stage
            return _dot(kaug_ref[0, br * n_pairs + pair, pl.ds(k0, TK), :], qat_ref[pair])

        s_queue = [scores(st) for st in stages[:SCORE_LOOKAHEAD]]
        for i, (k0, mask, pair) in enumerate(stages):
            s_t = s_queue.pop(0)
            if i + SCORE_LOOKAHEAD < len(stages):
                s_queue.append(scores(stages[i + SCORE_LOOKAHEAD]))
            slot = br * n_pairs + pair
            if mask is not None:
                s_t = jnp.where(tile_valid[mask], s_t, NEG)
            m_old = m_ref[slot]
            m_new = jnp.maximum(m_old, jnp.max(s_t, axis=0, keepdims=True))
            alpha = jnp.exp(m_old - m_new)
            p = jnp.exp(s_t - m_new).astype(BF16)
            acc_ref[slot] = alpha * acc_ref[slot] + _dot(vt_ref[0, br, pair // 2, :, pl.ds(k0, TK)], p)
            m_ref[slot] = m_new

    n_back = WINDOW // TK
    for n_behind in range(n_back + 1):
        tiles = [(qi, MASK_CAUSAL)] + [(qi - d, MASK_WINDOW_EDGE if d == n_back else None)
                                        for d in range(1, n_behind + 1)]

        @pl.when(qi >= n_back if n_behind == n_back else qi == n_behind)
        def _(tiles=tiles):
            attend(BR_WIN, tiles)

    def sel_body(i, carry):
        attend(BR_SEL, [(2 * i, None), (2 * i + 1, None)])
        return carry

    lax.fori_loop(0, qi // 2, sel_body, 0)

    @pl.when(qi % 2 == 1)
    def _():
        attend(BR_SEL, [(qi - 1, None), (qi, MASK_CAUSAL)])

    @pl.when(qi % 2 == 0)
    def _():
        attend(BR_SEL, [(qi, MASK_CAUSAL)])

    gates_t = jax.nn.sigmoid(zs_ref[...]).T
    for hp in range(NSA_H // 2):
        comb = []
        for e in range(2):
            h = 2 * hp + e
            c0 = ZS_GATE + NSA_NB * h
            pair = (h // NSA_R) * 2 + e
            cols = slice(((h % NSA_R) // 2) * tq, ((h % NSA_R) // 2 + 1) * tq)
            sel, win = BR_SEL * n_pairs + pair, BR_WIN * n_pairs + pair
            o_sel = acc_ref[sel, 0:NSA_DH, cols] * (1.0 / acc_ref[sel, NSA_DH:NSA_DH + 1, cols])
            o_win = acc_ref[win, 0:NSA_DH, cols] * (1.0 / acc_ref[win, NSA_DH:NSA_DH + 1, cols])
            comb.append(gates_t[c0:c0 + 1] * ocmp_ref[pair, :, cols] + gates_t[c0 + 1:c0 + 2] * o_sel
                        + gates_t[c0 + 2:c0 + 3] * o_win)
        out_ref[:, hp * LANES:(hp + 1) * LANES] = jnp.concatenate(comb, axis=0).T.astype(BF16)


def _overlap_t(S):
    c = np.arange(S // CMP_STRIDE)[None, :]
    j = np.arange(S // SEL_BLOCK)[:, None]
    ov = (c * CMP_STRIDE <= j * SEL_BLOCK + SEL_BLOCK - 1) & (c * CMP_STRIDE + CMP_BLOCK - 1 >= j * SEL_BLOCK)
    ov &= c < (S - CMP_BLOCK) // CMP_STRIDE + 1
    return jnp.asarray(ov.astype(np.float32), dtype=BF16)


def _nsa(zm, zs, kaug, vt, kcaug, vct, S):
    M = zm.shape[0]
    B = M // S
    tq = min(TQ, S)
    assert tq == TK and WINDOW % TK == 0 and S % TK == 0
    nq = S // tq
    n = S // CMP_STRIDE
    nsel = S // SEL_BLOCK
    return pl.pallas_call(
        _nsa_kernel,
        grid=(B, nq),
        in_specs=[pl.BlockSpec((tq, NSA_W), lambda b, i: (b * nq + i, ZM_NQ // NSA_W)),
                  pl.BlockSpec((1, 8, S, LANES), lambda b, i: (b, 0, 0, 0)),
                  pl.BlockSpec((1, 2, NSA_G, VT_ROWS, S), lambda b, i: (b, 0, 0, 0, 0)),
                  pl.BlockSpec((1, 4, n, LANES), lambda b, i: (b, 0, 0, 0)),
                  pl.BlockSpec((1, LANES, n), lambda b, i: (b, 0, 0)),
                  pl.BlockSpec((tq, ZS_W), lambda b, i: (b * nq + i, 0)),
                  pl.BlockSpec((nsel, n), lambda b, i: (0, 0))],
        out_specs=pl.BlockSpec((tq, NSA_W), lambda b, i: (b * nq + i, 0)),
        out_shape=jax.ShapeDtypeStruct((M, NSA_W), BF16),
        scratch_shapes=[pltpu.VMEM((NSA_H // 2, LANES, 2 * tq), BF16),
                        pltpu.VMEM((NSA_H // 2, NSA_DH, 2 * tq), F32),
                        pltpu.VMEM((NSA_H, 1, 2 * tq), F32),
                        pltpu.VMEM((NSA_H, VT_ROWS, 2 * tq), F32)],
        compiler_params=_cparams(("arbitrary", "arbitrary")),
        name="nsa_attention",
    )(zm, kaug, vt, kcaug, vct, zs, _overlap_t(S))


def _merge_kernel(gu0_ref, gv0_ref, gu_ref, gv_ref, lng_ref, lnb_ref, ws_ref, bst_ref, b_ref, c_ref,
                  ga_ref, gb_ref, gc_ref, x_ref, gt_ref, wa_ref, wb_ref, wc_ref, wo_ref, o_ref, a_ref):
    i = pl.program_id(0)

    @pl.when(i == 0)
    def _():
        _gmlp_rows(gu0_ref, gv0_ref, lng_ref, lnb_ref, ws_ref, bst_ref, a_ref.at[0])

    a = a_ref[i % 2]
    merged = (jax.nn.sigmoid(ga_ref[...].astype(F32)) * _dot(a, wa_ref[...])
              + jax.nn.sigmoid(gb_ref[...].astype(F32)) * _dot(b_ref[...], wb_ref[...])
              + jax.nn.sigmoid(gc_ref[...].astype(F32)) * _dot(c_ref[...], wc_ref[...]))
    o_ref[...] = x_ref[...] + gt_ref[0] * _dot(merged.astype(BF16), wo_ref[...])
    _gmlp_rows(gu_ref, gv_ref, lng_ref, lnb_ref, ws_ref, bst_ref, a_ref.at[(i + 1) % 2])


def _merge(yb, yc, zm, x2, gt, ln_g, ln_b, ws, bs, wa, wb, wc, wo, S):
    M, D = x2.shape
    tm = min(TM_MERGE, S)
    per_b = S // tm

    def rows(w, jcol=0):
        return pl.BlockSpec((tm, w), lambda i: (i, jcol))

    def next_rows(jcol):
        return pl.BlockSpec((tm, GM_W), lambda i: (jnp.minimum(i + 1, M // tm - 1), jcol))

    def full(shape):
        return pl.BlockSpec(shape, lambda i: (0,) * len(shape))

    return pl.pallas_call(
        _merge_kernel,
        grid=(M // tm,),
        in_specs=[rows(GM_W, ZM_GU // GM_W), rows(GM_W, ZM_GV // GM_W),
                  next_rows(ZM_GU // GM_W), next_rows(ZM_GV // GM_W), full((1, GM_W)), full((1, GM_W)),
                  full((GM_GROUPS, GM_CHUNK, GM_CHUNK)), full((GM_CHUNK, GM_GROUPS)),
                  rows(ML_W), rows(NSA_W),
                  rows(D, ZM_GA // D), rows(D, ZM_GBR // D), rows(D, ZM_GC // D),
                  rows(D), pl.BlockSpec((1, 1, D), lambda i: (i // per_b, 0, 0)),
                  full((GM_W, D)), full((ML_W, D)), full((NSA_W, D)), full((D, D))],
        out_specs=rows(D),
        out_shape=jax.ShapeDtypeStruct((M, D), F32),
        scratch_shapes=[pltpu.VMEM((2, tm, GM_W), BF16)],
        compiler_params=_cparams(("arbitrary",)),
        name="gmlp_merge_out",
    )(zm, zm, zm, zm, ln_g.reshape(1, GM_W), ln_b.reshape(1, GM_W), ws, bs.T, yb, yc, zm, zm, zm, x2, gt, wa, wb, wc, wo)


def _mlp_kernel(x_ref, g_ref, sc_ref, sh_ref, gt_ref, w1_ref, w2_ref, gf_ref, o_ref, acc_ref, *, final_norm):
    x = x_ref[...]
    h = _modulated_norm(x, g_ref[...], sc_ref[0], sh_ref[0]).astype(BF16)
    for c in range(D_FF // FF_CHUNK):
        mid = jnp.square(jnp.maximum(_dot(h, w1_ref[:, c * FF_CHUNK:(c + 1) * FF_CHUNK]), 0.0)).astype(BF16)
        upd = _dot(mid, w2_ref[c * FF_CHUNK:(c + 1) * FF_CHUNK, :])
        if c == 0:
            acc_ref[...] = upd
        else:
            acc_ref[...] += upd
    y = x + gt_ref[0] * acc_ref[...]
    if final_norm:
        y = (y * lax.rsqrt(jnp.mean(y * y, axis=-1, keepdims=True) + EPS)) * gf_ref[...]
    o_ref[...] = y


def _mlp(x2, g, sc, sh, gt, w1, w2, g_final, S, final_norm):
    M, D = x2.shape
    tm = min(TM_MLP, S)
    per_b = S // tm
    mod = pl.BlockSpec((1, 1, D), lambda i: (i // per_b, 0, 0))
    return pl.pallas_call(
        functools.partial(_mlp_kernel, final_norm=final_norm),
        grid=(M // tm,),
        in_specs=[pl.BlockSpec((tm, D), lambda i: (i, 0)),
                  pl.BlockSpec((1, D), lambda i: (0, 0)),
                  mod, mod, mod,
                  pl.BlockSpec((D, D_FF), lambda i: (0, 0)),
                  pl.BlockSpec((D_FF, D), lambda i: (0, 0)),
                  pl.BlockSpec((1, D), lambda i: (0, 0))],
        out_specs=pl.BlockSpec((tm, D), lambda i: (i, 0)),
        out_shape=jax.ShapeDtypeStruct((M, D), F32),
        scratch_shapes=[pltpu.VMEM((tm, D), F32)],
        compiler_params=_cparams(("arbitrary",)),
        name="relu2_mlp",
    )(x2, g, sc, sh, gt, w1, w2, g_final)


def _w_cols(w, name):
    return w[:, _OFFS[name]:_OFFS[name] + _SIZE[name]]


def _dup_heads(w):
    parts = []
    for g in range(NSA_G):
        blk = w[:, g * NSA_DH:(g + 1) * NSA_DH]
        parts += [blk, blk]
    return jnp.concatenate(parts, axis=1)


def _proj_weights(w):
    D = w.shape[0]
    wm = jnp.concatenate([_w_cols(w, n) for n in ('ga', 'gbr', 'gc', 'gu', 'gv', 'mq', 'mk', 'mv', 'mo', 'nq')]
                         + [_dup_heads(_w_cols(w, 'nks')), _dup_heads(_w_cols(w, 'nkw')),
                            _w_cols(w, 'nvs'), _w_cols(w, 'nvw')], axis=1)
    wc = jnp.concatenate([_w_cols(w, 'nkc'), _w_cols(w, 'nvc')], axis=1)
    ws = jnp.concatenate([_w_cols(w, 'ngate'), _w_cols(w, 'mi'), _w_cols(w, 'mf'),
                          jnp.zeros((D, ZS_W - NSA_H * NSA_NB - 2 * ML_H), w.dtype)], axis=1)
    return wm.astype(BF16), wc.astype(BF16), ws.astype(BF16)


def kernel(x, c, g_norm1, g_norm2, w_ada, b_ada, w_in, gm_ln_g, gm_ln_b, gm_ws, gm_bs, ml_conv_w, ml_conv_b,
           ml_gate_b, ml_norm_g, nsa_pe_k, nsa_pe_v, nsa_phi_k1, nsa_phi_k2, nsa_phi_v1, nsa_phi_v2,
           w_up_a, w_up_b, w_up_c, w_out, w_mlp1, w_mlp2, g_final):
    B, S, D = x.shape
    depth = w_in.shape[0]
    M = B * S
    mod = _ada(c, w_ada, b_ada)
    x2 = x.reshape(M, D)
    for l in range(depth):
        sh1, sc1, gt1, sh2, sc2, gt2 = [mod[l, :, i * D:(i + 1) * D].reshape(B, 1, D) for i in range(6)]
        wm, wc, ws = _proj_weights(w_in[l])
        zm, zc, zs = _inproj(x2, g_norm1[l].reshape(1, D), sc1, sh1, wm, wc, ws, S)
        yb = _mlstm(zm, zs, ml_conv_w[l], ml_conv_b[l], ml_gate_b[l], ml_norm_g[l], S)
        kaug, vt, kcaug, vct = _nsa_prep(zc, zm, nsa_pe_k[l], nsa_pe_v[l], nsa_phi_k1[l], nsa_phi_k2[l],
                                         nsa_phi_v1[l], nsa_phi_v2[l], S)
        yc = _nsa(zm, zs, kaug, vt, kcaug, vct, S)
        x2 = _merge(yb, yc, zm, x2, gt1, gm_ln_g[l], gm_ln_b[l], gm_ws[l], gm_bs[l], w_up_a[l].astype(BF16),
                    w_up_b[l].astype(BF16), w_up_c[l].astype(BF16), w_out[l].astype(BF16), S)
        x2 = _mlp(x2, g_norm2[l].reshape(1, D), sc2, sh2, gt2, w_mlp1[l].astype(BF16), w_mlp2[l].astype(BF16),
                  g_final.reshape(1, D), S, final_norm=(l == depth - 1))
    return x2.reshape(B, S, D)
```

```python
import functools

import numpy as np
import jax
import jax.numpy as jnp
from jax import lax
from jax.experimental import pallas as pl
from jax.experimental.pallas import tpu as pltpu

F32 = jnp.float32
BF16 = jnp.bfloat16

D_MODEL = 1024
GM_W = 512
GM_GROUPS = 4
GM_CHUNK = 128
ML_H = 4
ML_DH = 128
ML_W = ML_H * ML_DH
CONV_K = 4
NSA_H = 8
NSA_G = 2
NSA_R = NSA_H // NSA_G
NSA_DH = 64
NSA_W = NSA_H * NSA_DH
NSA_KV = NSA_G * NSA_DH
NSA_NB = 3
CMP_BLOCK = 32
CMP_STRIDE = 16
SEL_BLOCK = 64
TOP_N = 8
WINDOW = 512
D_FF = 4 * D_MODEL
EPS = 1e-6
NEG = -1e30
BIG = 1e4
TAKEN = -3e38
SPLIT_SIZES = (GM_W, GM_W, ML_W, ML_W, ML_W, ML_W, ML_H, ML_H, NSA_W, NSA_KV, NSA_KV, NSA_KV, NSA_KV,
               NSA_KV, NSA_KV, NSA_H * NSA_NB, D_MODEL, D_MODEL, D_MODEL)
SPLIT_NAMES = ('gu', 'gv', 'mq', 'mk', 'mv', 'mo', 'mi', 'mf', 'nq', 'nkc', 'nvc', 'nks', 'nvs', 'nkw', 'nvw',
               'ngate', 'ga', 'gbr', 'gc')
_OFFS = dict(zip(SPLIT_NAMES, np.concatenate([[0], np.cumsum(SPLIT_SIZES)[:-1]]).tolist()))
_SIZE = dict(zip(SPLIT_NAMES, SPLIT_SIZES))

LANES = 128
VMEM_LIMIT = 56 * 1024 * 1024

ZM_GA, ZM_GBR, ZM_GC = 0, 1024, 2048
ZM_GU, ZM_GV = 3072, 3584
ZM_MQ, ZM_MK, ZM_MV, ZM_MO = 4096, 4608, 5120, 5632
ZM_NQ = 6144
ZM_KK = 6656
ZM_VV = 7168
ZM_W = 7424
KK_W = 4 * LANES
VV_W = 2 * LANES
BR_SEL, BR_WIN = 0, 1
MASK_CAUSAL, MASK_WINDOW_EDGE = 0, 1
FEAT_SEL = 0
FEAT_HI = 32
FEAT_LO = 33
ZS_GATE = 0
ZS_MI = 24
ZS_MF = 28
ZS_W = 128

TM_PROJ = 1024
TN_PROJ = 3712
ML_CHUNK = 128
ML_ROWS = 2
CONV_HALO = 16
TQ = 256
TK = 256
SCORE_LOOKAHEAD = 3
VT_ROWS = NSA_DH + 16
TM_MERGE = 512
TM_MLP = 512
FF_CHUNK = 1024


def _dot(a, b):
    return jnp.dot(a, b, preferred_element_type=F32)


def _split3(x):
    x1 = x.astype(BF16)
    r1 = x - x1.astype(F32)
    x2 = r1.astype(BF16)
    x3 = (r1 - x2.astype(F32)).astype(BF16)
    return x1, x2, x3


def _cparams(sem):
    return pltpu.CompilerParams(dimension_semantics=sem, vmem_limit_bytes=VMEM_LIMIT)


def _ada_kernel(c_ref, w_ref, b_ref, o_ref):
    c = c_ref[...]
    cond = c * jax.nn.sigmoid(c)
    c1, c2, c3 = _split3(cond)
    w1, w2, w3 = _split3(w_ref[0])
    acc = _dot(c1, w1) + (_dot(c1, w2) + _dot(c2, w1)) + (_dot(c1, w3) + _dot(c2, w2) + _dot(c3, w1))
    o_ref[0] = acc + b_ref[0]


def _ada(c, w_ada, b_ada):
    L, D, N = w_ada.shape
    B = c.shape[0]
    tn = 1536
    return pl.pallas_call(
        _ada_kernel,
        grid=(L, N // tn),
        in_specs=[pl.BlockSpec((B, D), lambda l, j: (0, 0)),
                  pl.BlockSpec((1, D, tn), lambda l, j: (l, 0, j)),
                  pl.BlockSpec((1, 1, tn), lambda l, j: (l, 0, j))],
        out_specs=pl.BlockSpec((1, B, tn), lambda l, j: (l, 0, j)),
        out_shape=jax.ShapeDtypeStruct((L, B, N), F32),
        compiler_params=_cparams(("arbitrary", "arbitrary")),
        name="ada_mod",
    )(c, w_ada, b_ada.reshape(L, 1, N))


def _modulated_norm(x, g, sc, sh):
    y = x * lax.rsqrt(jnp.mean(x * x, axis=-1, keepdims=True) + EPS)
    return (y * g) * (1.0 + sc) + sh


def _inproj_kernel(x_ref, g_ref, sc_ref, sh_ref, wm_ref, wc_ref, ws_ref, zm_ref, zc_ref, zs_ref, h_ref):
    @pl.when(pl.program_id(1) == 0)
    def _():
        h = _modulated_norm(x_ref[...], g_ref[...], sc_ref[0], sh_ref[0]).astype(BF16)
        h_ref[...] = h
        zc = _dot(h, wc_ref[...])
        for part in range(zc_ref.shape[0]):
            zc_ref[part] = zc[:, part * LANES:(part + 1) * LANES]
        zs_ref[...] = _dot(h, ws_ref[...])

    zm_ref[...] = _dot(h_ref[...], wm_ref[...]).astype(BF16)


def _inproj(x2, g, sc, sh, wm, wc, ws, S):
    M, D = x2.shape
    tm, tn = min(TM_PROJ, S), TN_PROJ
    per_b = S // tm
    return pl.pallas_call(
        _inproj_kernel,
        grid=(M // tm, ZM_W // tn),
        in_specs=[pl.BlockSpec((tm, D), lambda i, j: (i, 0)),
                  pl.BlockSpec((1, D), lambda i, j: (0, 0)),
                  pl.BlockSpec((1, 1, D), lambda i, j: (i // per_b, 0, 0)),
                  pl.BlockSpec((1, 1, D), lambda i, j: (i // per_b, 0, 0)),
                  pl.BlockSpec((D, tn), lambda i, j: (0, j)),
                  pl.BlockSpec((D, 2 * NSA_KV), lambda i, j: (0, 0)),
                  pl.BlockSpec((D, ZS_W), lambda i, j: (0, 0))],
        out_specs=[pl.BlockSpec((tm, tn), lambda i, j: (i, j)),
                   pl.BlockSpec((2 * NSA_KV // LANES, tm, LANES), lambda i, j: (0, i, 0)),
                   pl.BlockSpec((tm, ZS_W), lambda i, j: (i, 0))],
        out_shape=[jax.ShapeDtypeStruct((M, ZM_W), BF16),
                   jax.ShapeDtypeStruct((2 * NSA_KV // LANES, M, LANES), F32),
                   jax.ShapeDtypeStruct((M, ZS_W), F32)],
        scratch_shapes=[pltpu.VMEM((tm, D), BF16)],
        compiler_params=_cparams(("arbitrary", "arbitrary")),
        name="in_proj",
    )(x2, g, sc, sh, wm, wc, ws)


def _gmlp_rows(u_ref, v_ref, lng_ref, lnb_ref, ws_ref, bst_ref, o_ref):
    ts = u_ref.shape[0]
    dg = GM_W // GM_GROUPS
    row = lax.broadcasted_iota(jnp.int32, (GM_CHUNK, GM_CHUNK), 0)
    col = lax.broadcasted_iota(jnp.int32, (GM_CHUNK, GM_CHUNK), 1)
    ws = [jnp.where(row >= col, ws_ref[g], 0.0).astype(BF16) for g in range(GM_GROUPS)]
    lng = lng_ref[...]
    lnb = lnb_ref[...]
    for c in range(ts // GM_CHUNK):
        r0 = c * GM_CHUNK
        u = jax.nn.gelu(u_ref[r0:r0 + GM_CHUNK, :].astype(F32))
        v = jax.nn.gelu(v_ref[r0:r0 + GM_CHUNK, :].astype(F32))
        mu = jnp.mean(v, axis=-1, keepdims=True)
        var = jnp.mean(jnp.square(v - mu), axis=-1, keepdims=True)
        vb = ((v - mu) * lax.rsqrt(var + EPS) * lng + lnb).astype(BF16)
        for g in range(GM_GROUPS):
            mixed = _dot(ws[g], vb[:, g * dg:(g + 1) * dg]) + bst_ref[:, g:g + 1]
            o_ref[r0:r0 + GM_CHUNK, g * dg:(g + 1) * dg] = (u[:, g * dg:(g + 1) * dg] * mixed).astype(BF16)


def _log_sigmoid(x):
    return jnp.minimum(x, 0.0) - jnp.log1p(jnp.exp(-jnp.abs(x)))


def _conv_silu(x_ext, w, b):
    n = x_ext.shape[0] - CONV_HALO
    y = b
    for j in range(CONV_K):
        sh = CONV_K - 1 - j
        xs = x_ext if sh == 0 else pltpu.roll(x_ext, sh, axis=0)
        y = y + xs[CONV_HALO:CONV_HALO + n] * w[j:j + 1]
    return y * jax.nn.sigmoid(y)


def _mlstm_kernel(q_ref, k_ref, v_ref, o_ref, zs_ref, cw_ref, cb_ref, gb_ref, ng_ref, out_ref, c_ref, *, S):
    n_rows = q_ref.shape[0] // S
    L = min(ML_CHUNK, S)
    c_ref[...] = jnp.zeros_like(c_ref)
    row = lax.broadcasted_iota(jnp.int32, (L, L), 0)
    col = lax.broadcasted_iota(jnp.int32, (L, L), 1)
    tril = row >= col
    tril_b = jnp.where(tril, 1.0, 0.0).astype(BF16)
    triu_b = jnp.where(row <= col, 1.0, 0.0).astype(BF16)
    ones_blk = jnp.ones((L, ML_DH), BF16)
    time_row = lax.broadcasted_iota(jnp.int32, (L, ZS_W), 0)
    gb = gb_ref[...]
    hs = [slice(h * ML_DH, (h + 1) * ML_DH) for h in range(ML_H)]
    chains = [(r, h) for r in range(n_rows) for h in range(ML_H)]

    def conv_rows(src, base, t0, c, c0):
        cur = src[pl.ds(pl.multiple_of(base + t0, L), L), :].astype(F32)
        halo0 = pl.multiple_of(base + jnp.maximum(t0 - CONV_HALO, 0), CONV_HALO)
        halo = jnp.where(c > 0, src[pl.ds(halo0, CONV_HALO), :].astype(F32), 0.0)
        return _conv_silu(jnp.concatenate([halo, cur], axis=0), cw_ref[:, c0:c0 + ML_W], cb_ref[:, c0:c0 + ML_W])

    def chunk_body(c, m_rows):
        t0 = pl.multiple_of(c * L, L)
        q, k_t, r_rows, u_cols, m_cols, u_last, w_prev_row = {}, {}, [], [], [], [], []
        for r in range(n_rows):
            base = r * S
            rows = pl.ds(pl.multiple_of(base + t0, L), L)
            gi = zs_ref[rows, :] + gb
            lf_cols = _log_sigmoid(gi)
            gi_t = gi.T
            lf_rows = _log_sigmoid(gi_t[ZS_MI:ZS_MI + 2 * ML_H])
            c1, c2, c3 = _split3(lf_cols)
            b_cols = _dot(tril_b, c1) + _dot(tril_b, c2) + _dot(tril_b, c3)
            r1, r2, r3 = _split3(lf_rows)
            b_rows = _dot(r1, triu_b) + _dot(r2, triu_b) + _dot(r3, triu_b)
            r_rows.append(gi_t[ZS_MI:ZS_MI + ML_H] - b_rows[ML_H:2 * ML_H])
            b_at_i = pltpu.roll(b_cols, ZS_W - (ZS_MF - ZS_MI), axis=1)
            u = gi - b_at_i
            shift = 1
            while shift < L:
                u = jnp.maximum(u, jnp.where(time_row >= shift, pltpu.roll(u, shift, axis=0), NEG))
                shift *= 2
            u = jnp.maximum(u, m_rows[r])
            u_cols.append(u)
            m_cols.append(b_at_i + u)
            u_last.append(u[L - 1:L, :])
            w_prev_row.append(jnp.exp(m_rows[r] - u_last[r]))
            q_all = conv_rows(q_ref, base, t0, c, 0).astype(BF16)
            k_all = conv_rows(k_ref, base, t0, c, ML_W) * (ML_DH ** -0.5)
            for h in range(ML_H):
                q[r, h] = q_all[:, hs[h]]
                k_t[r, h] = k_all[:, hs[h]].T
        cmat = {ch: c_ref[ch[0] * ML_H + ch[1]] for ch in chains}
        qk = {ch: _dot(q[ch], k_t[ch].astype(BF16)) for ch in chains}
        qc = {ch: _dot(q[ch], cmat[ch].astype(BF16)) for ch in chains}
        v_aug = {(r, h): jnp.concatenate([v_ref[pl.ds(pl.multiple_of(r * S + t0, L), L), hs[h]], ones_blk], axis=1)
                 for r, h in chains}
        u_rep, gate = {}, {}
        for ch in chains:
            r, h = ch
            lane_h = ZS_MI + h
            u_rep[ch] = jnp.broadcast_to(u_cols[r][:, lane_h:lane_h + 1], (L, ML_DH))
            gate[ch] = jnp.exp(jnp.where(tril, r_rows[r][h:h + 1, :] - u_rep[ch], NEG))
            w_s = jnp.exp(r_rows[r][h:h + 1, :] - u_last[r][:, lane_h:lane_h + 1])
            k_w = (k_t[ch] * w_s).astype(BF16)
            c_ref[r * ML_H + h] = w_prev_row[r][:, lane_h:lane_h + 1] * cmat[ch] + _dot(k_w, v_aug[ch])
        for ch in chains:
            r, h = ch
            lane_h = ZS_MI + h
            rows = pl.ds(pl.multiple_of(r * S + t0, L), L)
            w_inter = jnp.exp(m_rows[r][:, lane_h:lane_h + 1] - u_rep[ch])
            num_aug = (_dot((qk[ch] * gate[ch]).astype(BF16), v_aug[ch])
                       + jnp.concatenate([w_inter, w_inter], axis=1) * qc[ch])
            m_rep = jnp.broadcast_to(m_cols[r][:, lane_h:lane_h + 1], (L, ML_DH))
            hval = num_aug[:, :ML_DH] / jnp.maximum(jnp.abs(num_aug[:, ML_DH:]), jnp.exp(-m_rep))
            mu = jnp.mean(hval, axis=-1, keepdims=True)
            var = jnp.mean(jnp.square(hval - mu), axis=-1, keepdims=True)
            hn = (hval - mu) * lax.rsqrt(var + EPS) * ng_ref[:, hs[h]]
            og = jax.nn.sigmoid(o_ref[rows, hs[h]].astype(F32))
            out_ref[rows, hs[h]] = (og * hn).astype(BF16)
        return tuple(m[L - 1:L, :] for m in m_cols)

    lax.fori_loop(0, S // L, chunk_body, tuple(jnp.zeros((1, ZS_W), F32) for _ in range(n_rows)))


def _mlstm(zm, zs, conv_w, conv_b, gate_b, norm_g, S):
    M = zm.shape[0]
    n_rows = ML_ROWS if (M // S) % ML_ROWS == 0 else 1
    gb_row = jnp.zeros((1, ZS_W), F32).at[0, ZS_MI:ZS_MI + 2 * ML_H].set(gate_b)

    def col(off):
        return pl.BlockSpec((n_rows * S, ML_W), lambda b: (b, off // ML_W))

    return pl.pallas_call(
        functools.partial(_mlstm_kernel, S=S),
        grid=(M // (n_rows * S),),
        in_specs=[col(ZM_MQ), col(ZM_MK), col(ZM_MV), col(ZM_MO),
                  pl.BlockSpec((n_rows * S, ZS_W), lambda b: (b, 0)),
                  pl.BlockSpec((CONV_K, 2 * ML_W), lambda b: (0, 0)),
                  pl.BlockSpec((1, 2 * ML_W), lambda b: (0, 0)),
                  pl.BlockSpec((1, ZS_W), lambda b: (0, 0)),
                  pl.BlockSpec((1, ML_W), lambda b: (0, 0))],
        out_specs=pl.BlockSpec((n_rows * S, ML_W), lambda b: (b, 0)),
        out_shape=jax.ShapeDtypeStruct((M, ML_W), BF16),
        scratch_shapes=[pltpu.VMEM((n_rows * ML_H, ML_DH, 2 * ML_DH), F32)],
        compiler_params=_cparams(("arbitrary",)),
        name="mlstm_mixer",
    )(zm, zm, zm, zm, zs, conv_w, conv_b.reshape(1, 2 * ML_W), gb_row, norm_g.reshape(1, ML_W))


def _nsa_prep_kernel(x_ref, pe_ref, w1_ref, w2_ref, kk_ref, vv_ref, feat_ref, featc_ref,
                     kaug_ref, vt_ref, kcaug_ref, vct_ref):
    S = kk_ref.shape[0]
    n = S // CMP_STRIDE
    cw = 2 * NSA_KV
    half = CMP_BLOCK // 2
    acc_a = jnp.zeros((n, cw), F32)
    acc_b = jnp.zeros((n, cw), F32)
    for j in range(half):
        xj = jnp.concatenate([x_ref[part, pl.ds(j, n, stride=CMP_STRIDE), :] for part in range(cw // LANES)],
                             axis=1)
        acc_a = acc_a + _dot((xj + pe_ref[j:j + 1, :]).astype(BF16), w1_ref[j])
        acc_b = acc_b + _dot((xj + pe_ref[half + j:half + j + 1, :]).astype(BF16), w1_ref[half + j])
    pre = acc_a + pltpu.roll(acc_b, n - 1, axis=0)
    cmp = _dot(jax.nn.gelu(pre).astype(BF16), w2_ref[...])

    lane = lax.broadcasted_iota(jnp.int32, (1, LANES), 1)
    keep = [jnp.where(lane < NSA_DH, 1.0, 0.0).astype(BF16), jnp.where(lane >= NSA_DH, 1.0, 0.0).astype(BF16)]
    for g in range(NSA_G):
        kd = cmp[:, g * LANES:(g + 1) * LANES].astype(BF16)
        for e in range(2):
            kcaug_ref[0, g * 2 + e] = kd * keep[e] + featc_ref[e]
    vct_ref[0] = cmp[:, NSA_G * LANES:(NSA_G + 1) * LANES].T.astype(BF16)
    tb = min(TK, S)
    for br in range(2):
        for g in range(NSA_G):
            kd = kk_ref[:, (br * NSA_G + g) * LANES:(br * NSA_G + g + 1) * LANES]
            for e in range(2):
                kaug_ref[0, (br * NSA_G + g) * 2 + e] = kd * keep[e] + feat_ref[br * 2 + e]
        for c in range(S // tb):
            v_t = vv_ref[c * tb:(c + 1) * tb, br * LANES:(br + 1) * LANES].astype(F32).T.astype(BF16)
            for g in range(NSA_G):
                vt_ref[0, br, g, 0:NSA_DH, c * tb:(c + 1) * tb] = v_t[g * NSA_DH:(g + 1) * NSA_DH]
                vt_ref[0, br, g, NSA_DH:VT_ROWS, c * tb:(c + 1) * tb] = jnp.ones((VT_ROWS - NSA_DH, tb), BF16)


def _blockdiag(blocks):
    n = len(blocks)
    rows = []
    for i, blk in enumerate(blocks):
        rows.append(jnp.concatenate([blk if j == i else jnp.zeros((blk.shape[0], blocks[j].shape[1]), blk.dtype)
                                     for j in range(n)], axis=1))
    return jnp.concatenate(rows, axis=0)


def _key_features(S):
    pos = np.arange(S)
    f = np.zeros((4, S, LANES), np.float32)
    for e in range(2):
        base = NSA_DH * (1 - e)
        f[e, pos, base + FEAT_SEL + pos // SEL_BLOCK] = 1.0
        for br in range(2):
            f[br * 2 + e, :, base + FEAT_HI] = pos // 64
            f[br * 2 + e, :, base + FEAT_LO] = pos % 64
    return jnp.asarray(f, dtype=BF16)


def _cmp_features(S):
    c = np.arange(S // CMP_STRIDE)
    hi = (c * CMP_STRIDE) // 64
    lo = c * CMP_STRIDE + (CMP_BLOCK - 1) * 0.5 - 64 * hi
    f = np.zeros((2, c.size, LANES), np.float32)
    for e in range(2):
        base = NSA_DH * (1 - e)
        f[e, :, base + FEAT_HI] = hi
        f[e, :, base + FEAT_LO] = lo
    return jnp.asarray(f, dtype=BF16)


def _nsa_prep(zc, zm, pe_k, pe_v, phi_k1, phi_k2, phi_v1, phi_v2, S):
    M = zc.shape[1]
    B = M // S
    n = S // CMP_STRIDE
    cw = 2 * NSA_KV
    assert S // SEL_BLOCK <= FEAT_HI and S <= 64 * 64
    pe = jnp.concatenate([pe_k, pe_k, pe_v, pe_v], axis=1)
    k1 = phi_k1.reshape(CMP_BLOCK, NSA_DH, NSA_DH)
    v1 = phi_v1.reshape(CMP_BLOCK, NSA_DH, NSA_DH)
    blocks = jnp.stack([k1, k1, v1, v1], axis=1)
    w1 = jnp.einsum('jaxy,ab->jaxby', blocks, jnp.eye(4, dtype=F32)).reshape(CMP_BLOCK, cw, cw).astype(BF16)
    k2 = jnp.concatenate([phi_k2, phi_k2], axis=1)
    w2 = _blockdiag([k2, k2, phi_v2, phi_v2]).astype(BF16)

    def full(shape):
        return pl.BlockSpec(shape, lambda b: (0,) * len(shape))

    return pl.pallas_call(
        _nsa_prep_kernel,
        grid=(B,),
        in_specs=[pl.BlockSpec((cw // LANES, S, LANES), lambda b: (0, b, 0)),
                  full((CMP_BLOCK, cw)), full((CMP_BLOCK, cw, cw)), full((cw, 3 * LANES)),
                  pl.BlockSpec((S, KK_W), lambda b: (b, ZM_KK // KK_W)),
                  pl.BlockSpec((S, VV_W), lambda b: (b, ZM_VV // VV_W)),
                  full((4, S, LANES)), full((2, n, LANES))],
        out_specs=[pl.BlockSpec((1, 8, S, LANES), lambda b: (b, 0, 0, 0)),
                   pl.BlockSpec((1, 2, NSA_G, VT_ROWS, S), lambda b: (b, 0, 0, 0, 0)),
                   pl.BlockSpec((1, 4, n, LANES), lambda b: (b, 0, 0, 0)),
                   pl.BlockSpec((1, LANES, n), lambda b: (b, 0, 0))],
        out_shape=[jax.ShapeDtypeStruct((B, 8, S, LANES), BF16),
                   jax.ShapeDtypeStruct((B, 2, NSA_G, VT_ROWS, S), BF16),
                   jax.ShapeDtypeStruct((B, 4, n, LANES), BF16),
                   jax.ShapeDtypeStruct((B, LANES, n), BF16)],
        compiler_params=_cparams(("arbitrary",)),
        name="nsa_prep",
    )(zc, pe, w1, w2, zm, zm, _key_features(S), _cmp_features(S))


def _nsa_kernel(q_ref, kaug_ref, vt_ref, kcaug_ref, vct_ref, zs_ref, ovt_ref, out_ref,
                qat_ref, qaw_ref, ocmp_ref, m_ref, acc_ref):
    S = kaug_ref.shape[2]
    tq = q_ref.shape[0]
    ncmp = kcaug_ref.shape[2]
    nsel = S // SEL_BLOCK
    n_pairs = NSA_H // 2
    qi = pl.program_id(1)
    t0 = pl.multiple_of(qi * tq, tq)
    t_row = t0 + lax.broadcasted_iota(jnp.int32, (1, tq), 1)

    for k in range(NSA_H // 2):
        q_t = (q_ref[:, k * LANES:(k + 1) * LANES].astype(F32) * (NSA_DH ** -0.5)).T.astype(BF16)
        for e in range(2):
            for dst in (qat_ref, qaw_ref):
                dst[(k // 2) * 2 + e, e * NSA_DH:(e + 1) * NSA_DH, (k % 2) * tq:(k % 2 + 1) * tq] = (
                    q_t[e * NSA_DH:(e + 1) * NSA_DH])
    feat_row = lax.broadcasted_iota(jnp.int32, (NSA_DH, 2 * tq), 0)
    second_head = lax.broadcasted_iota(jnp.int32, (NSA_DH, 2 * tq), 1) >= tq
    for pair in range(n_pairs):
        g, e = pair // 2, pair % 2
        slopes = [2.0 ** (-8.0 * (g * NSA_R + 2 * s + e + 1.0) / NSA_H) for s in range(2)]
        slope = jnp.where(second_head, slopes[1], slopes[0])
        feat = jnp.where(feat_row == FEAT_HI, slope * 64.0, jnp.where(feat_row == FEAT_LO, slope, 0.0))
        qat_ref[pair, NSA_DH * (1 - e):NSA_DH * (2 - e), :] = feat.astype(BF16)
        qaw_ref[pair, NSA_DH * (1 - e):NSA_DH * (2 - e), :] = feat.astype(BF16)

    ki = lax.broadcasted_iota(jnp.int32, (TK, 2 * tq), 0)
    qu = lax.broadcasted_iota(jnp.int32, (TK, 2 * tq), 1)
    qu = jnp.where(qu >= tq, qu - tq, qu)
    tile_valid = {MASK_CAUSAL: ki <= qu,
                  MASK_WINDOW_EDGE: ki > qu}

    c_col = lax.broadcasted_iota(jnp.int32, (ncmp, 1), 0)
    t_row2 = jnp.concatenate([t_row, t_row], axis=1)
    valid_c = c_col * CMP_STRIDE + (CMP_BLOCK - 1) <= t_row2
    j_col = lax.broadcasted_iota(jnp.int32, (nsel, 1), 0)
    j_f = j_col.astype(F32)
    jt = t_row >> 6
    forced = jnp.logical_or(j_col == 0, jnp.logical_or(j_col == jt, j_col == jt - 1))
    future = j_col > jt

    m_ref[...] = jnp.full(m_ref.shape, NEG, F32)
    acc_ref[...] = jnp.zeros(acc_ref.shape, F32)

    def compressed_scores():
        return [_dot(kcaug_ref[0, pair], qaw_ref[pair]) for pair in range(n_pairs)]

    def compressed_branch_and_selection(scores_c):
        for g in range(NSA_G):
            psum = jnp.zeros((ncmp, tq), F32)
            for e in range(2):
                pair = g * 2 + e
                sc = jnp.where(valid_c, scores_c[pair], NEG)
                ex = jnp.where(valid_c, jnp.exp(sc - jnp.max(sc, axis=0, keepdims=True)), 0.0)
                den = jnp.sum(ex, axis=0, keepdims=True)
                p = ex * (1.0 / jnp.where(den > 0.0, den, 1.0))
                psum = psum + p[:, :tq] + p[:, tq:]
                ocmp_ref[pair] = _dot(vct_ref[0, g * NSA_DH:(g + 1) * NSA_DH, :], p.astype(BF16))

            p_hi = psum.astype(BF16)
            p_lo = (psum - p_hi.astype(F32)).astype(BF16)
            imp = _dot(ovt_ref[...], p_hi) + _dot(ovt_ref[...], p_lo)
            val = jnp.where(future, NEG, jnp.where(forced, BIG, imp))
            penalty = jnp.full((nsel, tq), NEG, F32)
            for _ in range(min(TOP_N, nsel)):
                best = jnp.max(val, axis=0, keepdims=True)
                first = jnp.min(jnp.where(val == best, j_f, float(nsel)), axis=0, keepdims=True)
                hit = j_f == first
                penalty = jnp.where(hit, 0.0, penalty)
                val = jnp.where(hit, TAKEN, val)
            penalty = penalty.astype(BF16)
            for e in range(2):
                first = NSA_DH * (1 - e) + FEAT_SEL
                qat_ref[g * 2 + e, first:first + nsel, :] = jnp.concatenate([penalty, penalty], axis=1)

    def attend(br, tiles, after_first_scores=None):
        stages = [(pl.multiple_of(kb * TK, TK), mask, pair) for kb, mask in tiles for pair in range(n_pairs)]

        def scores(stage):
            k0, _, pair = stage
            queries = qat_ref if br == BR_SEL else qaw_ref
            return _dot(kaug_ref[0, br * n_pairs + pair, pl.ds(k0, TK), :], queries[pair])

        s_queue = [scores(st) for st in stages[:SCORE_LOOKAHEAD]]
        if after_first_scores is not None:
            after_first_scores()
        for i, (k0, mask, pair) in enumerate(stages):
            s_t = s_queue.pop(0)
            if i + SCORE_LOOKAHEAD < len(stages):
                s_queue.append(scores(stages[i + SCORE_LOOKAHEAD]))
            slot = br * n_pairs + pair
            if mask is not None:
                s_t = jnp.where(tile_valid[mask], s_t, NEG)
            m_old = m_ref[slot]
            m_new = jnp.maximum(m_old, jnp.max(s_t, axis=0, keepdims=True))
            alpha = jnp.exp(m_old - m_new)
            p = jnp.exp(s_t - m_new).astype(BF16)
            acc_ref[slot] = alpha * acc_ref[slot] + _dot(vt_ref[0, br, pair // 2, :, pl.ds(k0, TK)], p)
            m_ref[slot] = m_new

    n_back = WINDOW // TK
    for n_behind in range(n_back + 1):
        tiles = [(qi, MASK_CAUSAL)] + [(qi - d, MASK_WINDOW_EDGE if d == n_back else None)
                                        for d in range(1, n_behind + 1)]

        @pl.when(qi >= n_back if n_behind == n_back else qi == n_behind)
        def _(tiles=tiles):
            scores_c = compressed_scores()
            attend(BR_WIN, tiles, functools.partial(compressed_branch_and_selection, scores_c))

    def sel_body(i, carry):
        attend(BR_SEL, [(2 * i, None), (2 * i + 1, None)])
        return carry

    lax.fori_loop(0, qi // 2, sel_body, 0)

    @pl.when(qi % 2 == 1)
    def _():
        attend(BR_SEL, [(qi - 1, None), (qi, MASK_CAUSAL)])

    @pl.when(qi % 2 == 0)
    def _():
        attend(BR_SEL, [(qi, MASK_CAUSAL)])

    gates_t = jax.nn.sigmoid(zs_ref[...]).T
    for hp in range(NSA_H // 2):
        comb = []
        for e in range(2):
            h = 2 * hp + e
            c0 = ZS_GATE + NSA_NB * h
            pair = (h // NSA_R) * 2 + e
            cols = slice(((h % NSA_R) // 2) * tq, ((h % NSA_R) // 2 + 1) * tq)
            sel, win = BR_SEL * n_pairs + pair, BR_WIN * n_pairs + pair
            o_sel = acc_ref[sel, 0:NSA_DH, cols] * (1.0 / acc_ref[sel, NSA_DH:NSA_DH + 1, cols])
            o_win = acc_ref[win, 0:NSA_DH, cols] * (1.0 / acc_ref[win, NSA_DH:NSA_DH + 1, cols])
            comb.append(gates_t[c0:c0 + 1] * ocmp_ref[pair, :, cols] + gates_t[c0 + 1:c0 + 2] * o_sel
                        + gates_t[c0 + 2:c0 + 3] * o_win)
        out_ref[:, hp * LANES:(hp + 1) * LANES] = jnp.concatenate(comb, axis=0).T.astype(BF16)


def _overlap_t(S):
    c = np.arange(S // CMP_STRIDE)[None, :]
    j = np.arange(S // SEL_BLOCK)[:, None]
    ov = (c * CMP_STRIDE <= j * SEL_BLOCK + SEL_BLOCK - 1) & (c * CMP_STRIDE + CMP_BLOCK - 1 >= j * SEL_BLOCK)
    ov &= c < (S - CMP_BLOCK) // CMP_STRIDE + 1
    return jnp.asarray(ov.astype(np.float32), dtype=BF16)


def _nsa(zm, zs, kaug, vt, kcaug, vct, S):
    M = zm.shape[0]
    B = M // S
    tq = min(TQ, S)
    assert tq == TK and WINDOW % TK == 0 and S % TK == 0
    nq = S // tq
    n = S // CMP_STRIDE
    nsel = S // SEL_BLOCK
    return pl.pallas_call(
        _nsa_kernel,
        grid=(B, nq),
        in_specs=[pl.BlockSpec((tq, NSA_W), lambda b, i: (b * nq + i, ZM_NQ // NSA_W)),
                  pl.BlockSpec((1, 8, S, LANES), lambda b, i: (b, 0, 0, 0)),
                  pl.BlockSpec((1, 2, NSA_G, VT_ROWS, S), lambda b, i: (b, 0, 0, 0, 0)),
                  pl.BlockSpec((1, 4, n, LANES), lambda b, i: (b, 0, 0, 0)),
                  pl.BlockSpec((1, LANES, n), lambda b, i: (b, 0, 0)),
                  pl.BlockSpec((tq, ZS_W), lambda b, i: (b * nq + i, 0)),
                  pl.BlockSpec((nsel, n), lambda b, i: (0, 0))],
        out_specs=pl.BlockSpec((tq, NSA_W), lambda b, i: (b * nq + i, 0)),
        out_shape=jax.ShapeDtypeStruct((M, NSA_W), BF16),
        scratch_shapes=[pltpu.VMEM((NSA_H // 2, LANES, 2 * tq), BF16),
                        pltpu.VMEM((NSA_H // 2, LANES, 2 * tq), BF16),
                        pltpu.VMEM((NSA_H // 2, NSA_DH, 2 * tq), F32),
                        pltpu.VMEM((NSA_H, 1, 2 * tq), F32),
                        pltpu.VMEM((NSA_H, VT_ROWS, 2 * tq), F32)],
        compiler_params=_cparams(("arbitrary", "arbitrary")),
        name="nsa_attention",
    )(zm, kaug, vt, kcaug, vct, zs, _overlap_t(S))


def _merge_kernel(gu0_ref, gv0_ref, gu_ref, gv_ref, lng_ref, lnb_ref, ws_ref, bst_ref, b_ref, c_ref,
                  ga_ref, gb_ref, gc_ref, x_ref, gt_ref, wa_ref, wb_ref, wc_ref, wo_ref, o_ref, a_ref):
    i = pl.program_id(0)

    @pl.when(i == 0)
    def _():
        _gmlp_rows(gu0_ref, gv0_ref, lng_ref, lnb_ref, ws_ref, bst_ref, a_ref.at[0])

    a = a_ref[i % 2]
    merged = (jax.nn.sigmoid(ga_ref[...].astype(F32)) * _dot(a, wa_ref[...])
              + jax.nn.sigmoid(gb_ref[...].astype(F32)) * _dot(b_ref[...], wb_ref[...])
              + jax.nn.sigmoid(gc_ref[...].astype(F32)) * _dot(c_ref[...], wc_ref[...]))
    o_ref[...] = x_ref[...] + gt_ref[0] * _dot(merged.astype(BF16), wo_ref[...])
    _gmlp_rows(gu_ref, gv_ref, lng_ref, lnb_ref, ws_ref, bst_ref, a_ref.at[(i + 1) % 2])


def _merge(yb, yc, zm, x2, gt, ln_g, ln_b, ws, bs, wa, wb, wc, wo, S):
    M, D = x2.shape
    tm = min(TM_MERGE, S)
    per_b = S // tm

    def rows(w, jcol=0):
        return pl.BlockSpec((tm, w), lambda i: (i, jcol))

    def next_rows(jcol):
        return pl.BlockSpec((tm, GM_W), lambda i: (jnp.minimum(i + 1, M // tm - 1), jcol))

    def full(shape):
        return pl.BlockSpec(shape, lambda i: (0,) * len(shape))

    return pl.pallas_call(
        _merge_kernel,
        grid=(M // tm,),
        in_specs=[rows(GM_W, ZM_GU // GM_W), rows(GM_W, ZM_GV // GM_W),
                  next_rows(ZM_GU // GM_W), next_rows(ZM_GV // GM_W), full((1, GM_W)), full((1, GM_W)),
                  full((GM_GROUPS, GM_CHUNK, GM_CHUNK)), full((GM_CHUNK, GM_GROUPS)),
                  rows(ML_W), rows(NSA_W),
                  rows(D, ZM_GA // D), rows(D, ZM_GBR // D), rows(D, ZM_GC // D),
                  rows(D), pl.BlockSpec((1, 1, D), lambda i: (i // per_b, 0, 0)),
                  full((GM_W, D)), full((ML_W, D)), full((NSA_W, D)), full((D, D))],
        out_specs=rows(D),
        out_shape=jax.ShapeDtypeStruct((M, D), F32),
        scratch_shapes=[pltpu.VMEM((2, tm, GM_W), BF16)],
        compiler_params=_cparams(("arbitrary",)),
        name="gmlp_merge_out",
    )(zm, zm, zm, zm, ln_g.reshape(1, GM_W), ln_b.reshape(1, GM_W), ws, bs.T, yb, yc, zm, zm, zm, x2, gt, wa, wb, wc, wo)


def _mlp_kernel(x_ref, g_ref, sc_ref, sh_ref, gt_ref, w1_ref, w2_ref, gf_ref, o_ref, acc_ref, *, final_norm):
    x = x_ref[...]
    h = _modulated_norm(x, g_ref[...], sc_ref[0], sh_ref[0]).astype(BF16)
    for c in range(D_FF // FF_CHUNK):
        mid = jnp.square(jnp.maximum(_dot(h, w1_ref[:, c * FF_CHUNK:(c + 1) * FF_CHUNK]), 0.0)).astype(BF16)
        upd = _dot(mid, w2_ref[c * FF_CHUNK:(c + 1) * FF_CHUNK, :])
        if c == 0:
            acc_ref[...] = upd
        else:
            acc_ref[...] += upd
    y = x + gt_ref[0] * acc_ref[...]
    if final_norm:
        y = (y * lax.rsqrt(jnp.mean(y * y, axis=-1, keepdims=True) + EPS)) * gf_ref[...]
    o_ref[...] = y


def _mlp(x2, g, sc, sh, gt, w1, w2, g_final, S, final_norm):
    M, D = x2.shape
    tm = min(TM_MLP, S)
    per_b = S // tm
    mod = pl.BlockSpec((1, 1, D), lambda i: (i // per_b, 0, 0))
    return pl.pallas_call(
        functools.partial(_mlp_kernel, final_norm=final_norm),
        grid=(M // tm,),
        in_specs=[pl.BlockSpec((tm, D), lambda i: (i, 0)),
                  pl.BlockSpec((1, D), lambda i: (0, 0)),
                  mod, mod, mod,
                  pl.BlockSpec((D, D_FF), lambda i: (0, 0)),
                  pl.BlockSpec((D_FF, D), lambda i: (0, 0)),
                  pl.BlockSpec((1, D), lambda i: (0, 0))],
        out_specs=pl.BlockSpec((tm, D), lambda i: (i, 0)),
        out_shape=jax.ShapeDtypeStruct((M, D), F32),
        scratch_shapes=[pltpu.VMEM((tm, D), F32)],
        compiler_params=_cparams(("arbitrary",)),
        name="relu2_mlp",
    )(x2, g, sc, sh, gt, w1, w2, g_final)


def _w_cols(w, name):
    return w[:, _OFFS[name]:_OFFS[name] + _SIZE[name]]


def _dup_heads(w):
    parts = []
    for g in range(NSA_G):
        blk = w[:, g * NSA_DH:(g + 1) * NSA_DH]
        parts += [blk, blk]
    return jnp.concatenate(parts, axis=1)


def _proj_weights(w):
    D = w.shape[0]
    wm = jnp.concatenate([_w_cols(w, n) for n in ('ga', 'gbr', 'gc', 'gu', 'gv', 'mq', 'mk', 'mv', 'mo', 'nq')]
                         + [_dup_heads(_w_cols(w, 'nks')), _dup_heads(_w_cols(w, 'nkw')),
                            _w_cols(w, 'nvs'), _w_cols(w, 'nvw')], axis=1)
    wc = jnp.concatenate([_w_cols(w, 'nkc'), _w_cols(w, 'nvc')], axis=1)
    ws = jnp.concatenate([_w_cols(w, 'ngate'), _w_cols(w, 'mi'), _w_cols(w, 'mf'),
                          jnp.zeros((D, ZS_W - NSA_H * NSA_NB - 2 * ML_H), w.dtype)], axis=1)
    return wm.astype(BF16), wc.astype(BF16), ws.astype(BF16)


def kernel(x, c, g_norm1, g_norm2, w_ada, b_ada, w_in, gm_ln_g, gm_ln_b, gm_ws, gm_bs, ml_conv_w, ml_conv_b,
           ml_gate_b, ml_norm_g, nsa_pe_k, nsa_pe_v, nsa_phi_k1, nsa_phi_k2, nsa_phi_v1, nsa_phi_v2,
           w_up_a, w_up_b, w_up_c, w_out, w_mlp1, w_mlp2, g_final):
    B, S, D = x.shape
    depth = w_in.shape[0]
    M = B * S
    mod = _ada(c, w_ada, b_ada)
    x2 = x.reshape(M, D)
    for l in range(depth):
        sh1, sc1, gt1, sh2, sc2, gt2 = [mod[l, :, i * D:(i + 1) * D].reshape(B, 1, D) for i in range(6)]
        wm, wc, ws = _proj_weights(w_in[l])
        zm, zc, zs = _inproj(x2, g_norm1[l].reshape(1, D), sc1, sh1, wm, wc, ws, S)
        yb = _mlstm(zm, zs, ml_conv_w[l], ml_conv_b[l], ml_gate_b[l], ml_norm_g[l], S)
        kaug, vt, kcaug, vct = _nsa_prep(zc, zm, nsa_pe_k[l], nsa_pe_v[l], nsa_phi_k1[l], nsa_phi_k2[l],
                                         nsa_phi_v1[l], nsa_phi_v2[l], S)
        yc = _nsa(zm, zs, kaug, vt, kcaug, vct, S)
        x2 = _merge(yb, yc, zm, x2, gt1, gm_ln_g[l], gm_ln_b[l], gm_ws[l], gm_bs[l], w_up_a[l].astype(BF16),
                    w_up_b[l].astype(BF16), w_up_c[l].astype(BF16), w_out[l].astype(BF16), S)
        x2 = _mlp(x2, g_norm2[l].reshape(1, D), sc2, sh2, gt2, w_mlp1[l].astype(BF16), w_mlp2[l].astype(BF16),
                  g_final.reshape(1, D), S, final_norm=(l == depth - 1))
    return x2.reshape(B, S, D)
```

```python
import functools

import numpy as np
import jax
import jax.numpy as jnp
from jax import lax
from jax.experimental import pallas as pl
from jax.experimental.pallas import tpu as pltpu

F32 = jnp.float32
BF16 = jnp.bfloat16

D_MODEL = 1024
GM_W = 512
GM_GROUPS = 4
GM_CHUNK = 128
ML_H = 4
ML_DH = 128
ML_W = ML_H * ML_DH
CONV_K = 4
NSA_H = 8
NSA_G = 2
NSA_R = NSA_H // NSA_G
NSA_DH = 64
NSA_W = NSA_H * NSA_DH
NSA_KV = NSA_G * NSA_DH
NSA_NB = 3
CMP_BLOCK = 32
CMP_STRIDE = 16
SEL_BLOCK = 64
TOP_N = 8
WINDOW = 512
D_FF = 4 * D_MODEL
EPS = 1e-6
NEG = -1e30
BIG = 1e4
TAKEN = -3e38
SPLIT_SIZES = (GM_W, GM_W, ML_W, ML_W, ML_W, ML_W, ML_H, ML_H, NSA_W, NSA_KV, NSA_KV, NSA_KV, NSA_KV,
               NSA_KV, NSA_KV, NSA_H * NSA_NB, D_MODEL, D_MODEL, D_MODEL)
SPLIT_NAMES = ('gu', 'gv', 'mq', 'mk', 'mv', 'mo', 'mi', 'mf', 'nq', 'nkc', 'nvc', 'nks', 'nvs', 'nkw', 'nvw',
               'ngate', 'ga', 'gbr', 'gc')
_OFFS = dict(zip(SPLIT_NAMES, np.concatenate([[0], np.cumsum(SPLIT_SIZES)[:-1]]).tolist()))
_SIZE = dict(zip(SPLIT_NAMES, SPLIT_SIZES))

LANES = 128
VMEM_LIMIT = 56 * 1024 * 1024

ZM_GA, ZM_GBR, ZM_GC = 0, 1024, 2048
ZM_GU, ZM_GV = 3072, 3584
ZM_MQ, ZM_MK, ZM_MV, ZM_MO = 4096, 4608, 5120, 5632
ZM_NQ = 6144
ZM_KK = 6656
ZM_VV = 7168
ZM_W = 7424
KK_W = 4 * LANES
VV_W = 2 * LANES
BR_SEL, BR_WIN = 0, 1
MASK_CAUSAL, MASK_WINDOW_EDGE = 0, 1
FEAT_SEL = 0
FEAT_HI = 32
FEAT_LO = 33
ZS_GATE = 0
ZS_MI = 24
ZS_MF = 28
ZS_W = 128

TM_PROJ = 1024
TN_PROJ = 3712
ML_CHUNK = 128
CONV_HALO = 16
TQ = 256
TK = 256
SCORE_LOOKAHEAD = 3
VT_ROWS = NSA_DH + 16
TM_MERGE = 512
TM_MLP = 512
FF_CHUNK = 1024


def _dot(a, b):
    return jnp.dot(a, b, preferred_element_type=F32)


def _split3(x):
    x1 = x.astype(BF16)
    r1 = x - x1.astype(F32)
    x2 = r1.astype(BF16)
    x3 = (r1 - x2.astype(F32)).astype(BF16)
    return x1, x2, x3


def _cparams(sem):
    return pltpu.CompilerParams(dimension_semantics=sem, vmem_limit_bytes=VMEM_LIMIT)


def _ada_kernel(c_ref, w_ref, b_ref, o_ref):
    c = c_ref[...]
    cond = c * jax.nn.sigmoid(c)
    c1, c2, c3 = _split3(cond)
    w1, w2, w3 = _split3(w_ref[0])
    acc = _dot(c1, w1) + (_dot(c1, w2) + _dot(c2, w1)) + (_dot(c1, w3) + _dot(c2, w2) + _dot(c3, w1))
    o_ref[0] = acc + b_ref[0]


def _ada(c, w_ada, b_ada):
    L, D, N = w_ada.shape
    B = c.shape[0]
    tn = 1536
    return pl.pallas_call(
        _ada_kernel,
        grid=(L, N // tn),
        in_specs=[pl.BlockSpec((B, D), lambda l, j: (0, 0)),
                  pl.BlockSpec((1, D, tn), lambda l, j: (l, 0, j)),
                  pl.BlockSpec((1, 1, tn), lambda l, j: (l, 0, j))],
        out_specs=pl.BlockSpec((1, B, tn), lambda l, j: (l, 0, j)),
        out_shape=jax.ShapeDtypeStruct((L, B, N), F32),
        compiler_params=_cparams(("arbitrary", "arbitrary")),
        name="ada_mod",
    )(c, w_ada, b_ada.reshape(L, 1, N))


def _modulated_norm(x, g, sc, sh):
    y = x * lax.rsqrt(jnp.mean(x * x, axis=-1, keepdims=True) + EPS)
    return (y * g) * (1.0 + sc) + sh


def _inproj_kernel(x_ref, g_ref, sc_ref, sh_ref, wm_ref, wc_ref, ws_ref, zm_ref, zc_ref, zs_ref, h_ref):
    @pl.when(pl.program_id(1) == 0)
    def _():
        h = _modulated_norm(x_ref[...], g_ref[...], sc_ref[0], sh_ref[0]).astype(BF16)
        h_ref[...] = h
        zc = _dot(h, wc_ref[...])
        for part in range(zc_ref.shape[0]):
            zc_ref[part] = zc[:, part * LANES:(part + 1) * LANES]
        zs_ref[...] = _dot(h, ws_ref[...])

    zm_ref[...] = _dot(h_ref[...], wm_ref[...]).astype(BF16)


def _inproj(x2, g, sc, sh, wm, wc, ws, S):
    M, D = x2.shape
    tm, tn = min(TM_PROJ, S), TN_PROJ
    per_b = S // tm
    return pl.pallas_call(
        _inproj_kernel,
        grid=(M // tm, ZM_W // tn),
        in_specs=[pl.BlockSpec((tm, D), lambda i, j: (i, 0)),
                  pl.BlockSpec((1, D), lambda i, j: (0, 0)),
                  pl.BlockSpec((1, 1, D), lambda i, j: (i // per_b, 0, 0)),
                  pl.BlockSpec((1, 1, D), lambda i, j: (i // per_b, 0, 0)),
                  pl.BlockSpec((D, tn), lambda i, j: (0, j)),
                  pl.BlockSpec((D, 2 * NSA_KV), lambda i, j: (0, 0)),
                  pl.BlockSpec((D, ZS_W), lambda i, j: (0, 0))],
        out_specs=[pl.BlockSpec((tm, tn), lambda i, j: (i, j)),
                   pl.BlockSpec((2 * NSA_KV // LANES, tm, LANES), lambda i, j: (0, i, 0)),
                   pl.BlockSpec((tm, ZS_W), lambda i, j: (i, 0))],
        out_shape=[jax.ShapeDtypeStruct((M, ZM_W), BF16),
                   jax.ShapeDtypeStruct((2 * NSA_KV // LANES, M, LANES), F32),
                   jax.ShapeDtypeStruct((M, ZS_W), F32)],
        scratch_shapes=[pltpu.VMEM((tm, D), BF16)],
        compiler_params=_cparams(("arbitrary", "arbitrary")),
        name="in_proj",
    )(x2, g, sc, sh, wm, wc, ws)


def _gmlp_rows(u_ref, v_ref, lng_ref, lnb_ref, ws_ref, bst_ref, o_ref):
    ts = u_ref.shape[0]
    dg = GM_W // GM_GROUPS
    row = lax.broadcasted_iota(jnp.int32, (GM_CHUNK, GM_CHUNK), 0)
    col = lax.broadcasted_iota(jnp.int32, (GM_CHUNK, GM_CHUNK), 1)
    ws = [jnp.where(row >= col, ws_ref[g], 0.0).astype(BF16) for g in range(GM_GROUPS)]
    lng = lng_ref[...]
    lnb = lnb_ref[...]
    for c in range(ts // GM_CHUNK):
        r0 = c * GM_CHUNK
        u = jax.nn.gelu(u_ref[r0:r0 + GM_CHUNK, :].astype(F32))
        v = jax.nn.gelu(v_ref[r0:r0 + GM_CHUNK, :].astype(F32))
        mu = jnp.mean(v, axis=-1, keepdims=True)
        var = jnp.mean(jnp.square(v - mu), axis=-1, keepdims=True)
        vb = ((v - mu) * lax.rsqrt(var + EPS) * lng + lnb).astype(BF16)
        for g in range(GM_GROUPS):
            mixed = _dot(ws[g], vb[:, g * dg:(g + 1) * dg]) + bst_ref[:, g:g + 1]
            o_ref[r0:r0 + GM_CHUNK, g * dg:(g + 1) * dg] = (u[:, g * dg:(g + 1) * dg] * mixed).astype(BF16)


def _log_sigmoid(x):
    return jnp.minimum(x, 0.0) - jnp.log1p(jnp.exp(-jnp.abs(x)))


def _conv_silu(x_ext, w, b):
    n = x_ext.shape[0] - CONV_HALO
    y = b
    for j in range(CONV_K):
        sh = CONV_K - 1 - j
        xs = x_ext if sh == 0 else pltpu.roll(x_ext, sh, axis=0)
        y = y + xs[CONV_HALO:CONV_HALO + n] * w[j:j + 1]
    return y * jax.nn.sigmoid(y)


def _mlstm_chunk_stages(q_ref, k_ref, v_ref, o_ref, zs_ref, cw_ref, cb_ref, gb_ref, ng_ref, out_ref, c_ref, m_ref,
                        row0, t_abs, has_history):
    L = ML_CHUNK
    rows = slice(row0, row0 + L)
    hs = [slice(h * ML_DH, (h + 1) * ML_DH) for h in range(ML_H)]
    heads = range(ML_H)
    st = {}

    def conv_rows(src, c0):
        cur = src[pl.ds(pl.multiple_of(t_abs, L), L), :].astype(F32)
        halo0 = pl.multiple_of(jnp.maximum(t_abs - CONV_HALO, 0), CONV_HALO)
        halo = jnp.where(has_history, src[pl.ds(halo0, CONV_HALO), :].astype(F32), 0.0)
        return _conv_silu(jnp.concatenate([halo, cur], axis=0), cw_ref[:, c0:c0 + ML_W], cb_ref[:, c0:c0 + ML_W])

    def projections():
        row = lax.broadcasted_iota(jnp.int32, (L, L), 0)
        col = lax.broadcasted_iota(jnp.int32, (L, L), 1)
        tril_b = jnp.where(row >= col, 1.0, 0.0).astype(BF16)
        triu_b = jnp.where(row <= col, 1.0, 0.0).astype(BF16)
        gi = zs_ref[rows, :] + gb_ref[...]
        gi_t = gi.T
        c1, c2, c3 = _split3(_log_sigmoid(gi))
        b_cols = _dot(tril_b, c1) + _dot(tril_b, c2) + _dot(tril_b, c3)
        r1, r2, r3 = _split3(_log_sigmoid(gi_t[ZS_MI:ZS_MI + 2 * ML_H]))
        b_rows = _dot(r1, triu_b) + _dot(r2, triu_b) + _dot(r3, triu_b)
        st['gi'] = gi
        st['r_rows'] = gi_t[ZS_MI:ZS_MI + ML_H] - b_rows[ML_H:2 * ML_H]
        st['b_at_i'] = pltpu.roll(b_cols, ZS_W - (ZS_MF - ZS_MI), axis=1)
        q_all = conv_rows(q_ref, 0).astype(BF16)
        k_all = conv_rows(k_ref, ML_W) * (ML_DH ** -0.5)
        st['q'] = [q_all[:, hs[h]] for h in heads]
        st['k_t'] = [k_all[:, hs[h]].T for h in heads]
        st['cmat'] = [c_ref[h] for h in heads]
        st['qk'] = [_dot(st['q'][h], st['k_t'][h].astype(BF16)) for h in heads]
        st['qc'] = [_dot(st['q'][h], st['cmat'][h].astype(BF16)) for h in heads]
        ones_blk = jnp.ones((L, ML_DH), BF16)
        st['v_aug'] = [jnp.concatenate([v_ref[rows, hs[h]], ones_blk], axis=1) for h in heads]

    def memory_update():
        tril = lax.broadcasted_iota(jnp.int32, (L, L), 0) >= lax.broadcasted_iota(jnp.int32, (L, L), 1)
        time_row = lax.broadcasted_iota(jnp.int32, (L, ZS_W), 0)
        m_row = m_ref[...]
        u = st['gi'] - st['b_at_i']
        shift = 1
        while shift < L:
            u = jnp.maximum(u, jnp.where(time_row >= shift, pltpu.roll(u, shift, axis=0), NEG))
            shift *= 2
        u = jnp.maximum(u, m_row)
        m_cols = st['b_at_i'] + u
        u_last = u[L - 1:L, :]
        w_prev_row = jnp.exp(m_row - u_last)
        st['m_cols'], st['m_prev'], st['u_rep'], st['gate'] = m_cols, m_row, [], []
        for h in heads:
            lane_h = ZS_MI + h
            r_row = st['r_rows'][h:h + 1, :]
            u_rep = jnp.broadcast_to(u[:, lane_h:lane_h + 1], (L, ML_DH))
            st['u_rep'].append(u_rep)
            st['gate'].append(jnp.exp(jnp.where(tril, r_row - u_rep, NEG)))
            w_s = jnp.exp(r_row - u_last[:, lane_h:lane_h + 1])
            k_w = (st['k_t'][h] * w_s).astype(BF16)
            c_ref[h] = w_prev_row[:, lane_h:lane_h + 1] * st['cmat'][h] + _dot(k_w, st['v_aug'][h])
        m_ref[...] = m_cols[L - 1:L, :]

    def read_out():
        st['num_aug'] = []
        for h in heads:
            lane_h = ZS_MI + h
            w_inter = jnp.exp(st['m_prev'][:, lane_h:lane_h + 1] - st['u_rep'][h])
            st['num_aug'].append(_dot((st['qk'][h] * st['gate'][h]).astype(BF16), st['v_aug'][h])
                                 + jnp.concatenate([w_inter, w_inter], axis=1) * st['qc'][h])

    def normalise_and_store():
        for h in heads:
            lane_h = ZS_MI + h
            num_aug = st['num_aug'][h]
            m_rep = jnp.broadcast_to(st['m_cols'][:, lane_h:lane_h + 1], (L, ML_DH))
            hval = num_aug[:, :ML_DH] / jnp.maximum(jnp.abs(num_aug[:, ML_DH:]), jnp.exp(-m_rep))
            mu = jnp.mean(hval, axis=-1, keepdims=True)
            var = jnp.mean(jnp.square(hval - mu), axis=-1, keepdims=True)
            hn = (hval - mu) * lax.rsqrt(var + EPS) * ng_ref[:, hs[h]]
            og = jax.nn.sigmoid(o_ref[rows, hs[h]].astype(F32))
            out_ref[rows, hs[h]] = (og * hn).astype(BF16)

    return [projections, memory_update, read_out, normalise_and_store]


def _nsa_prep_kernel(x_ref, pe_ref, w1_ref, w2_ref, kk_ref, vv_ref, feat_ref, featc_ref,
                     kaug_ref, vt_ref, kcaug_ref, vct_ref):
    S = kk_ref.shape[0]
    n = S // CMP_STRIDE
    cw = 2 * NSA_KV
    half = CMP_BLOCK // 2
    acc_a = jnp.zeros((n, cw), F32)
    acc_b = jnp.zeros((n, cw), F32)
    for j in range(half):
        xj = jnp.concatenate([x_ref[part, pl.ds(j, n, stride=CMP_STRIDE), :] for part in range(cw // LANES)],
                             axis=1)
        acc_a = acc_a + _dot((xj + pe_ref[j:j + 1, :]).astype(BF16), w1_ref[j])
        acc_b = acc_b + _dot((xj + pe_ref[half + j:half + j + 1, :]).astype(BF16), w1_ref[half + j])
    pre = acc_a + pltpu.roll(acc_b, n - 1, axis=0)
    cmp = _dot(jax.nn.gelu(pre).astype(BF16), w2_ref[...])

    lane = lax.broadcasted_iota(jnp.int32, (1, LANES), 1)
    keep = [jnp.where(lane < NSA_DH, 1.0, 0.0).astype(BF16), jnp.where(lane >= NSA_DH, 1.0, 0.0).astype(BF16)]
    for g in range(NSA_G):
        kd = cmp[:, g * LANES:(g + 1) * LANES].astype(BF16)
        for e in range(2):
            kcaug_ref[0, g * 2 + e] = kd * keep[e] + featc_ref[e]
    vct_ref[0] = cmp[:, NSA_G * LANES:(NSA_G + 1) * LANES].T.astype(BF16)
    tb = min(TK, S)
    for br in range(2):
        for g in range(NSA_G):
            kd = kk_ref[:, (br * NSA_G + g) * LANES:(br * NSA_G + g + 1) * LANES]
            for e in range(2):
                kaug_ref[0, (br * NSA_G + g) * 2 + e] = kd * keep[e] + feat_ref[br * 2 + e]
        for c in range(S // tb):
            v_t = vv_ref[c * tb:(c + 1) * tb, br * LANES:(br + 1) * LANES].astype(F32).T.astype(BF16)
            for g in range(NSA_G):
                vt_ref[0, br, g, 0:NSA_DH, c * tb:(c + 1) * tb] = v_t[g * NSA_DH:(g + 1) * NSA_DH]
                vt_ref[0, br, g, NSA_DH:VT_ROWS, c * tb:(c + 1) * tb] = jnp.ones((VT_ROWS - NSA_DH, tb), BF16)


def _blockdiag(blocks):
    n = len(blocks)
    rows = []
    for i, blk in enumerate(blocks):
        rows.append(jnp.concatenate([blk if j == i else jnp.zeros((blk.shape[0], blocks[j].shape[1]), blk.dtype)
                                     for j in range(n)], axis=1))
    return jnp.concatenate(rows, axis=0)


def _key_features(S):
    pos = np.arange(S)
    f = np.zeros((4, S, LANES), np.float32)
    for e in range(2):
        base = NSA_DH * (1 - e)
        f[e, pos, base + FEAT_SEL + pos // SEL_BLOCK] = 1.0
        for br in range(2):
            f[br * 2 + e, :, base + FEAT_HI] = pos // 64
            f[br * 2 + e, :, base + FEAT_LO] = pos % 64
    return jnp.asarray(f, dtype=BF16)


def _cmp_features(S):
    c = np.arange(S // CMP_STRIDE)
    hi = (c * CMP_STRIDE) // 64
    lo = c * CMP_STRIDE + (CMP_BLOCK - 1) * 0.5 - 64 * hi
    f = np.zeros((2, c.size, LANES), np.float32)
    for e in range(2):
        base = NSA_DH * (1 - e)
        f[e, :, base + FEAT_HI] = hi
        f[e, :, base + FEAT_LO] = lo
    return jnp.asarray(f, dtype=BF16)


def _nsa_prep(zc, zm, pe_k, pe_v, phi_k1, phi_k2, phi_v1, phi_v2, S):
    M = zc.shape[1]
    B = M // S
    n = S // CMP_STRIDE
    cw = 2 * NSA_KV
    assert S // SEL_BLOCK <= FEAT_HI and S <= 64 * 64
    pe = jnp.concatenate([pe_k, pe_k, pe_v, pe_v], axis=1)
    k1 = phi_k1.reshape(CMP_BLOCK, NSA_DH, NSA_DH)
    v1 = phi_v1.reshape(CMP_BLOCK, NSA_DH, NSA_DH)
    blocks = jnp.stack([k1, k1, v1, v1], axis=1)
    w1 = jnp.einsum('jaxy,ab->jaxby', blocks, jnp.eye(4, dtype=F32)).reshape(CMP_BLOCK, cw, cw).astype(BF16)
    k2 = jnp.concatenate([phi_k2, phi_k2], axis=1)
    w2 = _blockdiag([k2, k2, phi_v2, phi_v2]).astype(BF16)

    def full(shape):
        return pl.BlockSpec(shape, lambda b: (0,) * len(shape))

    return pl.pallas_call(
        _nsa_prep_kernel,
        grid=(B,),
        in_specs=[pl.BlockSpec((cw // LANES, S, LANES), lambda b: (0, b, 0)),
                  full((CMP_BLOCK, cw)), full((CMP_BLOCK, cw, cw)), full((cw, 3 * LANES)),
                  pl.BlockSpec((S, KK_W), lambda b: (b, ZM_KK // KK_W)),
                  pl.BlockSpec((S, VV_W), lambda b: (b, ZM_VV // VV_W)),
                  full((4, S, LANES)), full((2, n, LANES))],
        out_specs=[pl.BlockSpec((1, 8, S, LANES), lambda b: (b, 0, 0, 0)),
                   pl.BlockSpec((1, 2, NSA_G, VT_ROWS, S), lambda b: (b, 0, 0, 0, 0)),
                   pl.BlockSpec((1, 4, n, LANES), lambda b: (b, 0, 0, 0)),
                   pl.BlockSpec((1, LANES, n), lambda b: (b, 0, 0))],
        out_shape=[jax.ShapeDtypeStruct((B, 8, S, LANES), BF16),
                   jax.ShapeDtypeStruct((B, 2, NSA_G, VT_ROWS, S), BF16),
                   jax.ShapeDtypeStruct((B, 4, n, LANES), BF16),
                   jax.ShapeDtypeStruct((B, LANES, n), BF16)],
        compiler_params=_cparams(("arbitrary",)),
        name="nsa_prep",
    )(zc, pe, w1, w2, zm, zm, _key_features(S), _cmp_features(S))


def _nsa_mlstm_kernel(q_ref, kaug_ref, vt_ref, kcaug_ref, vct_ref, zs_ref, ovt_ref,
                      mq_ref, mk_ref, mv_ref, mo_ref, cw_ref, cb_ref, gb_ref, ng_ref, out_ref, yb_ref,
                      qat_ref, qaw_ref, ocmp_ref, m_ref, acc_ref, mc_ref, mm_ref):
    S = kaug_ref.shape[2]
    tq = q_ref.shape[0]
    ncmp = kcaug_ref.shape[2]
    nsel = S // SEL_BLOCK
    n_pairs = NSA_H // 2
    qi = pl.program_id(1)
    t0 = pl.multiple_of(qi * tq, tq)
    t_row = t0 + lax.broadcasted_iota(jnp.int32, (1, tq), 1)

    @pl.when(qi == 0)
    def _():
        mc_ref[...] = jnp.zeros_like(mc_ref)
        mm_ref[...] = jnp.zeros_like(mm_ref)

    def mlstm_chunk(c):
        return _mlstm_chunk_stages(mq_ref, mk_ref, mv_ref, mo_ref, zs_ref, cw_ref, cb_ref, gb_ref, ng_ref, yb_ref,
                                   mc_ref, mm_ref, row0=c * ML_CHUNK, t_abs=t0 + c * ML_CHUNK,
                                   has_history=(qi > 0) if c == 0 else True)

    for k in range(NSA_H // 2):
        q_t = (q_ref[:, k * LANES:(k + 1) * LANES].astype(F32) * (NSA_DH ** -0.5)).T.astype(BF16)
        for e in range(2):
            for dst in (qat_ref, qaw_ref):
                dst[(k // 2) * 2 + e, e * NSA_DH:(e + 1) * NSA_DH, (k % 2) * tq:(k % 2 + 1) * tq] = (
                    q_t[e * NSA_DH:(e + 1) * NSA_DH])
    feat_row = lax.broadcasted_iota(jnp.int32, (NSA_DH, 2 * tq), 0)
    second_head = lax.broadcasted_iota(jnp.int32, (NSA_DH, 2 * tq), 1) >= tq
    for pair in range(n_pairs):
        g, e = pair // 2, pair % 2
        slopes = [2.0 ** (-8.0 * (g * NSA_R + 2 * s + e + 1.0) / NSA_H) for s in range(2)]
        slope = jnp.where(second_head, slopes[1], slopes[0])
        feat = jnp.where(feat_row == FEAT_HI, slope * 64.0, jnp.where(feat_row == FEAT_LO, slope, 0.0))
        qat_ref[pair, NSA_DH * (1 - e):NSA_DH * (2 - e), :] = feat.astype(BF16)
        qaw_ref[pair, NSA_DH * (1 - e):NSA_DH * (2 - e), :] = feat.astype(BF16)

    ki = lax.broadcasted_iota(jnp.int32, (TK, 2 * tq), 0)
    qu = lax.broadcasted_iota(jnp.int32, (TK, 2 * tq), 1)
    qu = jnp.where(qu >= tq, qu - tq, qu)
    tile_valid = {MASK_CAUSAL: ki <= qu,
                  MASK_WINDOW_EDGE: ki > qu}

    c_col = lax.broadcasted_iota(jnp.int32, (ncmp, 1), 0)
    t_row2 = jnp.concatenate([t_row, t_row], axis=1)
    valid_c = c_col * CMP_STRIDE + (CMP_BLOCK - 1) <= t_row2
    j_col = lax.broadcasted_iota(jnp.int32, (nsel, 1), 0)
    j_f = j_col.astype(F32)
    jt = t_row >> 6
    forced = jnp.logical_or(j_col == 0, jnp.logical_or(j_col == jt, j_col == jt - 1))
    future = j_col > jt

    m_ref[...] = jnp.full(m_ref.shape, NEG, F32)
    acc_ref[...] = jnp.zeros(acc_ref.shape, F32)

    def compressed_scores():
        return [_dot(kcaug_ref[0, pair], qaw_ref[pair]) for pair in range(n_pairs)]

    def compressed_branch_and_selection(scores_c):
        for g in range(NSA_G):
            psum = jnp.zeros((ncmp, tq), F32)
            for e in range(2):
                pair = g * 2 + e
                sc = jnp.where(valid_c, scores_c[pair], NEG)
                ex = jnp.where(valid_c, jnp.exp(sc - jnp.max(sc, axis=0, keepdims=True)), 0.0)
                den = jnp.sum(ex, axis=0, keepdims=True)
                p = ex * (1.0 / jnp.where(den > 0.0, den, 1.0))
                psum = psum + p[:, :tq] + p[:, tq:]
                ocmp_ref[pair] = _dot(vct_ref[0, g * NSA_DH:(g + 1) * NSA_DH, :], p.astype(BF16))

            p_hi = psum.astype(BF16)
            p_lo = (psum - p_hi.astype(F32)).astype(BF16)
            imp = _dot(ovt_ref[...], p_hi) + _dot(ovt_ref[...], p_lo)
            val = jnp.where(future, NEG, jnp.where(forced, BIG, imp))
            penalty = jnp.full((nsel, tq), NEG, F32)
            for _ in range(min(TOP_N, nsel)):
                best = jnp.max(val, axis=0, keepdims=True)
                first = jnp.min(jnp.where(val == best, j_f, float(nsel)), axis=0, keepdims=True)
                hit = j_f == first
                penalty = jnp.where(hit, 0.0, penalty)
                val = jnp.where(hit, TAKEN, val)
            penalty = penalty.astype(BF16)
            for e in range(2):
                first = NSA_DH * (1 - e) + FEAT_SEL
                qat_ref[g * 2 + e, first:first + nsel, :] = jnp.concatenate([penalty, penalty], axis=1)

    def attend(br, tiles, interleave=()):
        stages = [(pl.multiple_of(kb * TK, TK), mask, pair) for kb, mask in tiles for pair in range(n_pairs)]

        def scores(stage):
            k0, _, pair = stage
            queries = qat_ref if br == BR_SEL else qaw_ref
            return _dot(kaug_ref[0, br * n_pairs + pair, pl.ds(k0, TK), :], queries[pair])

        s_queue = [scores(st) for st in stages[:SCORE_LOOKAHEAD]]
        pending = list(interleave)
        if pending:
            pending.pop(0)()
        for i, (k0, mask, pair) in enumerate(stages):
            s_t = s_queue.pop(0)
            if i + SCORE_LOOKAHEAD < len(stages):
                s_queue.append(scores(stages[i + SCORE_LOOKAHEAD]))
            slot = br * n_pairs + pair
            if mask is not None:
                s_t = jnp.where(tile_valid[mask], s_t, NEG)
            m_old = m_ref[slot]
            m_new = jnp.maximum(m_old, jnp.max(s_t, axis=0, keepdims=True))
            alpha = jnp.exp(m_old - m_new)
            p = jnp.exp(s_t - m_new).astype(BF16)
            acc_ref[slot] = alpha * acc_ref[slot] + _dot(vt_ref[0, br, pair // 2, :, pl.ds(k0, TK)], p)
            m_ref[slot] = m_new
            if pending and i % 2 == 1:
                pending.pop(0)()
        for emit in pending:
            emit()

    n_back = WINDOW // TK
    for n_behind in range(n_back + 1):
        tiles = [(qi, MASK_CAUSAL)] + [(qi - d, MASK_WINDOW_EDGE if d == n_back else None)
                                        for d in range(1, n_behind + 1)]

        @pl.when(qi >= n_back if n_behind == n_back else qi == n_behind)
        def _(tiles=tiles):
            scores_c = compressed_scores()
            project, *rest = mlstm_chunk(0)

            def selection_and_projections():
                compressed_branch_and_selection(scores_c)
                project()

            attend(BR_WIN, tiles, [selection_and_projections] + rest)

    def sel_body(i, carry):
        attend(BR_SEL, [(2 * i, None), (2 * i + 1, None)])
        return carry

    lax.fori_loop(0, qi // 2, sel_body, 0)

    @pl.when(qi % 2 == 1)
    def _():
        attend(BR_SEL, [(qi - 1, None), (qi, MASK_CAUSAL)], mlstm_chunk(1))

    @pl.when(qi % 2 == 0)
    def _():
        attend(BR_SEL, [(qi, MASK_CAUSAL)], mlstm_chunk(1))

    gates_t = jax.nn.sigmoid(zs_ref[...]).T
    for hp in range(NSA_H // 2):
        comb = []
        for e in range(2):
            h = 2 * hp + e
            c0 = ZS_GATE + NSA_NB * h
            pair = (h // NSA_R) * 2 + e
            cols = slice(((h % NSA_R) // 2) * tq, ((h % NSA_R) // 2 + 1) * tq)
            sel, win = BR_SEL * n_pairs + pair, BR_WIN * n_pairs + pair
            o_sel = acc_ref[sel, 0:NSA_DH, cols] * (1.0 / acc_ref[sel, NSA_DH:NSA_DH + 1, cols])
            o_win = acc_ref[win, 0:NSA_DH, cols] * (1.0 / acc_ref[win, NSA_DH:NSA_DH + 1, cols])
            comb.append(gates_t[c0:c0 + 1] * ocmp_ref[pair, :, cols] + gates_t[c0 + 1:c0 + 2] * o_sel
                        + gates_t[c0 + 2:c0 + 3] * o_win)
        out_ref[:, hp * LANES:(hp + 1) * LANES] = jnp.concatenate(comb, axis=0).T.astype(BF16)


def _overlap_t(S):
    c = np.arange(S // CMP_STRIDE)[None, :]
    j = np.arange(S // SEL_BLOCK)[:, None]
    ov = (c * CMP_STRIDE <= j * SEL_BLOCK + SEL_BLOCK - 1) & (c * CMP_STRIDE + CMP_BLOCK - 1 >= j * SEL_BLOCK)
    ov &= c < (S - CMP_BLOCK) // CMP_STRIDE + 1
    return jnp.asarray(ov.astype(np.float32), dtype=BF16)


def _nsa_mlstm(zm, zs, kaug, vt, kcaug, vct, conv_w, conv_b, gate_b, norm_g, S):
    M = zm.shape[0]
    B = M // S
    tq = min(TQ, S)
    assert tq == TK and WINDOW % TK == 0 and S % TK == 0 and tq == 2 * ML_CHUNK and ML_CHUNK == ML_DH
    nq = S // tq
    n = S // CMP_STRIDE
    nsel = S // SEL_BLOCK
    gb_row = jnp.zeros((1, ZS_W), F32).at[0, ZS_MI:ZS_MI + 2 * ML_H].set(gate_b)

    def tile(off):
        return pl.BlockSpec((tq, ML_W), lambda b, i: (b * nq + i, off // ML_W))

    def row(off):
        return pl.BlockSpec((S, ML_W), lambda b, i: (b, off // ML_W))

    def full(shape):
        return pl.BlockSpec(shape, lambda b, i: (0,) * len(shape))

    return pl.pallas_call(
        _nsa_mlstm_kernel,
        grid=(B, nq),
        in_specs=[tile(ZM_NQ),
                  pl.BlockSpec((1, 8, S, LANES), lambda b, i: (b, 0, 0, 0)),
                  pl.BlockSpec((1, 2, NSA_G, VT_ROWS, S), lambda b, i: (b, 0, 0, 0, 0)),
                  pl.BlockSpec((1, 4, n, LANES), lambda b, i: (b, 0, 0, 0)),
                  pl.BlockSpec((1, LANES, n), lambda b, i: (b, 0, 0)),
                  pl.BlockSpec((tq, ZS_W), lambda b, i: (b * nq + i, 0)),
                  full((nsel, n)),
                  row(ZM_MQ), row(ZM_MK), tile(ZM_MV), tile(ZM_MO),
                  full((CONV_K, 2 * ML_W)), full((1, 2 * ML_W)), full((1, ZS_W)), full((1, ML_W))],
        out_specs=[pl.BlockSpec((tq, NSA_W), lambda b, i: (b * nq + i, 0)),
                   pl.BlockSpec((tq, ML_W), lambda b, i: (b * nq + i, 0))],
        out_shape=[jax.ShapeDtypeStruct((M, NSA_W), BF16), jax.ShapeDtypeStruct((M, ML_W), BF16)],
        scratch_shapes=[pltpu.VMEM((NSA_H // 2, LANES, 2 * tq), BF16),
                        pltpu.VMEM((NSA_H // 2, LANES, 2 * tq), BF16),
                        pltpu.VMEM((NSA_H // 2, NSA_DH, 2 * tq), F32),
                        pltpu.VMEM((NSA_H, 1, 2 * tq), F32),
                        pltpu.VMEM((NSA_H, VT_ROWS, 2 * tq), F32),
                        pltpu.VMEM((ML_H, ML_DH, 2 * ML_DH), F32),
                        pltpu.VMEM((1, ZS_W), F32)],
        compiler_params=_cparams(("arbitrary", "arbitrary")),
        name="nsa_mlstm",
    )(zm, kaug, vt, kcaug, vct, zs, _overlap_t(S), zm, zm, zm, zm,
      conv_w, conv_b.reshape(1, 2 * ML_W), gb_row, norm_g.reshape(1, ML_W))


def _merge_kernel(gu0_ref, gv0_ref, gu_ref, gv_ref, lng_ref, lnb_ref, ws_ref, bst_ref, b_ref, c_ref,
                  ga_ref, gb_ref, gc_ref, x_ref, gt_ref, wa_ref, wb_ref, wc_ref, wo_ref, o_ref, a_ref):
    i = pl.program_id(0)

    @pl.when(i == 0)
    def _():
        _gmlp_rows(gu0_ref, gv0_ref, lng_ref, lnb_ref, ws_ref, bst_ref, a_ref.at[0])

    a = a_ref[i % 2]
    merged = (jax.nn.sigmoid(ga_ref[...].astype(F32)) * _dot(a, wa_ref[...])
              + jax.nn.sigmoid(gb_ref[...].astype(F32)) * _dot(b_ref[...], wb_ref[...])
              + jax.nn.sigmoid(gc_ref[...].astype(F32)) * _dot(c_ref[...], wc_ref[...]))
    o_ref[...] = x_ref[...] + gt_ref[0] * _dot(merged.astype(BF16), wo_ref[...])
    _gmlp_rows(gu_ref, gv_ref, lng_ref, lnb_ref, ws_ref, bst_ref, a_ref.at[(i + 1) % 2])


def _merge(yb, yc, zm, x2, gt, ln_g, ln_b, ws, bs, wa, wb, wc, wo, S):
    M, D = x2.shape
    tm = min(TM_MERGE, S)
    per_b = S // tm

    def rows(w, jcol=0):
        return pl.BlockSpec((tm, w), lambda i: (i, jcol))

    def next_rows(jcol):
        return pl.BlockSpec((tm, GM_W), lambda i: (jnp.minimum(i + 1, M // tm - 1), jcol))

    def full(shape):
        return pl.BlockSpec(shape, lambda i: (0,) * len(shape))

    return pl.pallas_call(
        _merge_kernel,
        grid=(M // tm,),
        in_specs=[rows(GM_W, ZM_GU // GM_W), rows(GM_W, ZM_GV // GM_W),
                  next_rows(ZM_GU // GM_W), next_rows(ZM_GV // GM_W), full((1, GM_W)), full((1, GM_W)),
                  full((GM_GROUPS, GM_CHUNK, GM_CHUNK)), full((GM_CHUNK, GM_GROUPS)),
                  rows(ML_W), rows(NSA_W),
                  rows(D, ZM_GA // D), rows(D, ZM_GBR // D), rows(D, ZM_GC // D),
                  rows(D), pl.BlockSpec((1, 1, D), lambda i: (i // per_b, 0, 0)),
                  full((GM_W, D)), full((ML_W, D)), full((NSA_W, D)), full((D, D))],
        out_specs=rows(D),
        out_shape=jax.ShapeDtypeStruct((M, D), F32),
        scratch_shapes=[pltpu.VMEM((2, tm, GM_W), BF16)],
        compiler_params=_cparams(("arbitrary",)),
        name="gmlp_merge_out",
    )(zm, zm, zm, zm, ln_g.reshape(1, GM_W), ln_b.reshape(1, GM_W), ws, bs.T, yb, yc, zm, zm, zm, x2, gt, wa, wb, wc, wo)


def _mlp_kernel(x_ref, g_ref, sc_ref, sh_ref, gt_ref, w1_ref, w2_ref, gf_ref, o_ref, acc_ref, *, final_norm):
    x = x_ref[...]
    h = _modulated_norm(x, g_ref[...], sc_ref[0], sh_ref[0]).astype(BF16)
    for c in range(D_FF // FF_CHUNK):
        mid = jnp.square(jnp.maximum(_dot(h, w1_ref[:, c * FF_CHUNK:(c + 1) * FF_CHUNK]), 0.0)).astype(BF16)
        upd = _dot(mid, w2_ref[c * FF_CHUNK:(c + 1) * FF_CHUNK, :])
        if c == 0:
            acc_ref[...] = upd
        else:
            acc_ref[...] += upd
    y = x + gt_ref[0] * acc_ref[...]
    if final_norm:
        y = (y * lax.rsqrt(jnp.mean(y * y, axis=-1, keepdims=True) + EPS)) * gf_ref[...]
    o_ref[...] = y


def _mlp(x2, g, sc, sh, gt, w1, w2, g_final, S, final_norm):
    M, D = x2.shape
    tm = min(TM_MLP, S)
    per_b = S // tm
    mod = pl.BlockSpec((1, 1, D), lambda i: (i // per_b, 0, 0))
    return pl.pallas_call(
        functools.partial(_mlp_kernel, final_norm=final_norm),
        grid=(M // tm,),
        in_specs=[pl.BlockSpec((tm, D), lambda i: (i, 0)),
                  pl.BlockSpec((1, D), lambda i: (0, 0)),
                  mod, mod, mod,
                  pl.BlockSpec((D, D_FF), lambda i: (0, 0)),
                  pl.BlockSpec((D_FF, D), lambda i: (0, 0)),
                  pl.BlockSpec((1, D), lambda i: (0, 0))],
        out_specs=pl.BlockSpec((tm, D), lambda i: (i, 0)),
        out_shape=jax.ShapeDtypeStruct((M, D), F32),
        scratch_shapes=[pltpu.VMEM((tm, D), F32)],
        compiler_params=_cparams(("arbitrary",)),
        name="relu2_mlp",
    )(x2, g, sc, sh, gt, w1, w2, g_final)


def _w_cols(w, name):
    return w[:, _OFFS[name]:_OFFS[name] + _SIZE[name]]


def _dup_heads(w):
    parts = []
    for g in range(NSA_G):
        blk = w[:, g * NSA_DH:(g + 1) * NSA_DH]
        parts += [blk, blk]
    return jnp.concatenate(parts, axis=1)


def _proj_weights(w):
    D = w.shape[0]
    wm = jnp.concatenate([_w_cols(w, n) for n in ('ga', 'gbr', 'gc', 'gu', 'gv', 'mq', 'mk', 'mv', 'mo', 'nq')]
                         + [_dup_heads(_w_cols(w, 'nks')), _dup_heads(_w_cols(w, 'nkw')),
                            _w_cols(w, 'nvs'), _w_cols(w, 'nvw')], axis=1)
    wc = jnp.concatenate([_w_cols(w, 'nkc'), _w_cols(w, 'nvc')], axis=1)
    ws = jnp.concatenate([_w_cols(w, 'ngate'), _w_cols(w, 'mi'), _w_cols(w, 'mf'),
                          jnp.zeros((D, ZS_W - NSA_H * NSA_NB - 2 * ML_H), w.dtype)], axis=1)
    return wm.astype(BF16), wc.astype(BF16), ws.astype(BF16)


def kernel(x, c, g_norm1, g_norm2, w_ada, b_ada, w_in, gm_ln_g, gm_ln_b, gm_ws, gm_bs, ml_conv_w, ml_conv_b,
           ml_gate_b, ml_norm_g, nsa_pe_k, nsa_pe_v, nsa_phi_k1, nsa_phi_k2, nsa_phi_v1, nsa_phi_v2,
           w_up_a, w_up_b, w_up_c, w_out, w_mlp1, w_mlp2, g_final):
    B, S, D = x.shape
    depth = w_in.shape[0]
    M = B * S
    mod = _ada(c, w_ada, b_ada)
    x2 = x.reshape(M, D)
    for l in range(depth):
        sh1, sc1, gt1, sh2, sc2, gt2 = [mod[l, :, i * D:(i + 1) * D].reshape(B, 1, D) for i in range(6)]
        wm, wc, ws = _proj_weights(w_in[l])
        zm, zc, zs = _inproj(x2, g_norm1[l].reshape(1, D), sc1, sh1, wm, wc, ws, S)
        kaug, vt, kcaug, vct = _nsa_prep(zc, zm, nsa_pe_k[l], nsa_pe_v[l], nsa_phi_k1[l], nsa_phi_k2[l],
                                         nsa_phi_v1[l], nsa_phi_v2[l], S)
        yc, yb = _nsa_mlstm(zm, zs, kaug, vt, kcaug, vct, ml_conv_w[l], ml_conv_b[l], ml_gate_b[l], ml_norm_g[l], S)
        x2 = _merge(yb, yc, zm, x2, gt1, gm_ln_g[l], gm_ln_b[l], gm_ws[l], gm_bs[l], w_up_a[l].astype(BF16),
                    w_up_b[l].astype(BF16), w_up_c[l].astype(BF16), w_out[l].astype(BF16), S)
        x2 = _mlp(x2, g_norm2[l].reshape(1, D), sc2, sh2, gt2, w_mlp1[l].astype(BF16), w_mlp2[l].astype(BF16),
                  g_final.reshape(1, D), S, final_norm=(l == depth - 1))
    return x2.reshape(B, S, D)
```

```python
import functools

import numpy as np
import jax
import jax.numpy as jnp
from jax import lax
from jax.experimental import pallas as pl
from jax.experimental.pallas import tpu as pltpu

F32 = jnp.float32
BF16 = jnp.bfloat16

D_MODEL = 1024
GM_W = 512
GM_GROUPS = 4
GM_CHUNK = 128
ML_H = 4
ML_DH = 128
ML_W = ML_H * ML_DH
CONV_K = 4
NSA_H = 8
NSA_G = 2
NSA_R = NSA_H // NSA_G
NSA_DH = 64
NSA_W = NSA_H * NSA_DH
NSA_KV = NSA_G * NSA_DH
NSA_NB = 3
CMP_BLOCK = 32
CMP_STRIDE = 16
SEL_BLOCK = 64
TOP_N = 8
WINDOW = 512
D_FF = 4 * D_MODEL
EPS = 1e-6
NEG = -1e30
BIG = 1e4
TAKEN = -3e38
SPLIT_SIZES = (GM_W, GM_W, ML_W, ML_W, ML_W, ML_W, ML_H, ML_H, NSA_W, NSA_KV, NSA_KV, NSA_KV, NSA_KV,
               NSA_KV, NSA_KV, NSA_H * NSA_NB, D_MODEL, D_MODEL, D_MODEL)
SPLIT_NAMES = ('gu', 'gv', 'mq', 'mk', 'mv', 'mo', 'mi', 'mf', 'nq', 'nkc', 'nvc', 'nks', 'nvs', 'nkw', 'nvw',
               'ngate', 'ga', 'gbr', 'gc')
_OFFS = dict(zip(SPLIT_NAMES, np.concatenate([[0], np.cumsum(SPLIT_SIZES)[:-1]]).tolist()))
_SIZE = dict(zip(SPLIT_NAMES, SPLIT_SIZES))

LANES = 128
VMEM_LIMIT = 56 * 1024 * 1024

ZM_GA, ZM_GBR, ZM_GC = 0, 1024, 2048
ZM_GU, ZM_GV = 3072, 3584
ZM_MQ, ZM_MK, ZM_MV, ZM_MO = 4096, 4608, 5120, 5632
ZM_NQ = 6144
ZM_KK = 6656
ZM_VV = 7168
ZM_W = 7424
KK_W = 4 * LANES
VV_W = 2 * LANES
BR_SEL, BR_WIN = 0, 1
MASK_CAUSAL, MASK_WINDOW_EDGE = 0, 1
FEAT_SEL = 0
FEAT_TERMS = 3
FEAT_HI = 32
FEAT_LO = 35
LOG2E = 1.4426950408889634
ZS_GATE = 0
ZS_MI = 24
ZS_MF = 28
ZS_W = 128

TM_PROJ = 1024
TN_PROJ = 3712
ML_CHUNK = 128
CONV_HALO = 16
TQ = 256
TK = 256
SCORE_LOOKAHEAD = 3
VT_ROWS = NSA_DH + 16
TM_MERGE = 512
TM_MLP = 512
FF_CHUNK = 1024


def _dot(a, b):
    return jnp.dot(a, b, preferred_element_type=F32)


def _split3(x):
    x1 = x.astype(BF16)
    r1 = x - x1.astype(F32)
    x2 = r1.astype(BF16)
    x3 = (r1 - x2.astype(F32)).astype(BF16)
    return x1, x2, x3


def _cparams(sem):
    return pltpu.CompilerParams(dimension_semantics=sem, vmem_limit_bytes=VMEM_LIMIT)


def _ada_kernel(c_ref, w_ref, b_ref, o_ref):
    c = c_ref[...]
    cond = c * jax.nn.sigmoid(c)
    c1, c2, c3 = _split3(cond)
    w1, w2, w3 = _split3(w_ref[0])
    acc = _dot(c1, w1) + (_dot(c1, w2) + _dot(c2, w1)) + (_dot(c1, w3) + _dot(c2, w2) + _dot(c3, w1))
    o_ref[0] = acc + b_ref[0]


def _ada(c, w_ada, b_ada):
    L, D, N = w_ada.shape
    B = c.shape[0]
    tn = 1536
    return pl.pallas_call(
        _ada_kernel,
        grid=(L, N // tn),
        in_specs=[pl.BlockSpec((B, D), lambda l, j: (0, 0)),
                  pl.BlockSpec((1, D, tn), lambda l, j: (l, 0, j)),
                  pl.BlockSpec((1, 1, tn), lambda l, j: (l, 0, j))],
        out_specs=pl.BlockSpec((1, B, tn), lambda l, j: (l, 0, j)),
        out_shape=jax.ShapeDtypeStruct((L, B, N), F32),
        compiler_params=_cparams(("arbitrary", "arbitrary")),
        name="ada_mod",
    )(c, w_ada, b_ada.reshape(L, 1, N))


def _modulated_norm(x, g, sc, sh):
    y = x * lax.rsqrt(jnp.mean(x * x, axis=-1, keepdims=True) + EPS)
    return (y * g) * (1.0 + sc) + sh


def _inproj_kernel(x_ref, g_ref, sc_ref, sh_ref, wm_ref, wc_ref, ws_ref, zm_ref, zc_ref, zs_ref, h_ref):
    @pl.when(pl.program_id(1) == 0)
    def _():
        h = _modulated_norm(x_ref[...], g_ref[...], sc_ref[0], sh_ref[0]).astype(BF16)
        h_ref[...] = h
        zc = _dot(h, wc_ref[...])
        for part in range(zc_ref.shape[0]):
            zc_ref[part] = zc[:, part * LANES:(part + 1) * LANES]
        zs_ref[...] = _dot(h, ws_ref[...])

    zm_ref[...] = _dot(h_ref[...], wm_ref[...]).astype(BF16)


def _inproj(x2, g, sc, sh, wm, wc, ws, S):
    M, D = x2.shape
    tm, tn = min(TM_PROJ, S), TN_PROJ
    per_b = S // tm
    return pl.pallas_call(
        _inproj_kernel,
        grid=(M // tm, ZM_W // tn),
        in_specs=[pl.BlockSpec((tm, D), lambda i, j: (i, 0)),
                  pl.BlockSpec((1, D), lambda i, j: (0, 0)),
                  pl.BlockSpec((1, 1, D), lambda i, j: (i // per_b, 0, 0)),
                  pl.BlockSpec((1, 1, D), lambda i, j: (i // per_b, 0, 0)),
                  pl.BlockSpec((D, tn), lambda i, j: (0, j)),
                  pl.BlockSpec((D, 2 * NSA_KV), lambda i, j: (0, 0)),
                  pl.BlockSpec((D, ZS_W), lambda i, j: (0, 0))],
        out_specs=[pl.BlockSpec((tm, tn), lambda i, j: (i, j)),
                   pl.BlockSpec((2 * NSA_KV // LANES, tm, LANES), lambda i, j: (0, i, 0)),
                   pl.BlockSpec((tm, ZS_W), lambda i, j: (i, 0))],
        out_shape=[jax.ShapeDtypeStruct((M, ZM_W), BF16),
                   jax.ShapeDtypeStruct((2 * NSA_KV // LANES, M, LANES), F32),
                   jax.ShapeDtypeStruct((M, ZS_W), F32)],
        scratch_shapes=[pltpu.VMEM((tm, D), BF16)],
        compiler_params=_cparams(("arbitrary", "arbitrary")),
        name="in_proj",
    )(x2, g, sc, sh, wm, wc, ws)


def _gmlp_rows(u_ref, v_ref, lng_ref, lnb_ref, ws_ref, bst_ref, o_ref):
    ts = u_ref.shape[0]
    dg = GM_W // GM_GROUPS
    row = lax.broadcasted_iota(jnp.int32, (GM_CHUNK, GM_CHUNK), 0)
    col = lax.broadcasted_iota(jnp.int32, (GM_CHUNK, GM_CHUNK), 1)
    ws = [jnp.where(row >= col, ws_ref[g], 0.0).astype(BF16) for g in range(GM_GROUPS)]
    lng = lng_ref[...]
    lnb = lnb_ref[...]
    for c in range(ts // GM_CHUNK):
        r0 = c * GM_CHUNK
        u = jax.nn.gelu(u_ref[r0:r0 + GM_CHUNK, :].astype(F32))
        v = jax.nn.gelu(v_ref[r0:r0 + GM_CHUNK, :].astype(F32))
        mu = jnp.mean(v, axis=-1, keepdims=True)
        var = jnp.mean(jnp.square(v - mu), axis=-1, keepdims=True)
        vb = ((v - mu) * lax.rsqrt(var + EPS) * lng + lnb).astype(BF16)
        for g in range(GM_GROUPS):
            mixed = _dot(ws[g], vb[:, g * dg:(g + 1) * dg]) + bst_ref[:, g:g + 1]
            o_ref[r0:r0 + GM_CHUNK, g * dg:(g + 1) * dg] = (u[:, g * dg:(g + 1) * dg] * mixed).astype(BF16)


def _log_sigmoid(x):
    return jnp.minimum(x, 0.0) - jnp.log1p(jnp.exp(-jnp.abs(x)))


def _conv_silu(x_ext, w, b):
    n = x_ext.shape[0] - CONV_HALO
    y = b
    for j in range(CONV_K):
        sh = CONV_K - 1 - j
        xs = x_ext if sh == 0 else pltpu.roll(x_ext, sh, axis=0)
        y = y + xs[CONV_HALO:CONV_HALO + n] * w[j:j + 1]
    return y * jax.nn.sigmoid(y)


def _mlstm_chunk_stages(q_ref, k_ref, v_ref, o_ref, zs_ref, cw_ref, cb_ref, gb_ref, ng_ref, out_ref, c_ref, m_ref,
                        row0, t_abs, has_history):
    L = ML_CHUNK
    rows = slice(row0, row0 + L)
    hs = [slice(h * ML_DH, (h + 1) * ML_DH) for h in range(ML_H)]
    heads = range(ML_H)
    st = {}

    def conv_rows(src, c0):
        cur = src[pl.ds(pl.multiple_of(t_abs, L), L), :].astype(F32)
        halo0 = pl.multiple_of(jnp.maximum(t_abs - CONV_HALO, 0), CONV_HALO)
        halo = jnp.where(has_history, src[pl.ds(halo0, CONV_HALO), :].astype(F32), 0.0)
        return _conv_silu(jnp.concatenate([halo, cur], axis=0), cw_ref[:, c0:c0 + ML_W], cb_ref[:, c0:c0 + ML_W])

    def projections():
        row = lax.broadcasted_iota(jnp.int32, (L, L), 0)
        col = lax.broadcasted_iota(jnp.int32, (L, L), 1)
        tril_b = jnp.where(row >= col, 1.0, 0.0).astype(BF16)
        triu_b = jnp.where(row <= col, 1.0, 0.0).astype(BF16)
        gi = zs_ref[rows, :] + gb_ref[...]
        gi_t = gi.T
        c1, c2, c3 = _split3(_log_sigmoid(gi) * LOG2E)
        b_cols = _dot(tril_b, c1) + _dot(tril_b, c2) + _dot(tril_b, c3)
        r1, r2, r3 = _split3(_log_sigmoid(gi_t[ZS_MI:ZS_MI + 2 * ML_H]) * LOG2E)
        b_rows = _dot(r1, triu_b) + _dot(r2, triu_b) + _dot(r3, triu_b)
        st['gi'] = gi * LOG2E
        st['r_rows'] = gi_t[ZS_MI:ZS_MI + ML_H] * LOG2E - b_rows[ML_H:2 * ML_H]
        st['b_at_i'] = pltpu.roll(b_cols, ZS_W - (ZS_MF - ZS_MI), axis=1)
        q_all = conv_rows(q_ref, 0).astype(BF16)
        k_all = conv_rows(k_ref, ML_W) * (ML_DH ** -0.5)
        st['q'] = [q_all[:, hs[h]] for h in heads]
        st['k_t'] = [k_all[:, hs[h]].T for h in heads]
        st['cmat'] = [c_ref[h] for h in heads]
        st['qk'] = [_dot(st['q'][h], st['k_t'][h].astype(BF16)) for h in heads]
        st['qc'] = [_dot(st['q'][h], st['cmat'][h].astype(BF16)) for h in heads]
        ones_blk = jnp.ones((L, ML_DH), BF16)
        st['v_aug'] = [jnp.concatenate([v_ref[rows, hs[h]], ones_blk], axis=1) for h in heads]

    def memory_update():
        tril = lax.broadcasted_iota(jnp.int32, (L, L), 0) >= lax.broadcasted_iota(jnp.int32, (L, L), 1)
        time_row = lax.broadcasted_iota(jnp.int32, (L, ZS_W), 0)
        m_row = m_ref[...]
        u = st['gi'] - st['b_at_i']
        shift = 1
        while shift < L:
            u = jnp.maximum(u, jnp.where(time_row >= shift, pltpu.roll(u, shift, axis=0), NEG))
            shift *= 2
        u = jnp.maximum(u, m_row)
        m_cols = st['b_at_i'] + u
        u_last = u[L - 1:L, :]
        w_prev_row = jnp.exp2(m_row - u_last)
        st['m_cols'], st['m_prev'], st['u_rep'], st['gate'] = m_cols, m_row, [], []
        for h in heads:
            lane_h = ZS_MI + h
            r_row = st['r_rows'][h:h + 1, :]
            u_rep = jnp.broadcast_to(u[:, lane_h:lane_h + 1], (L, ML_DH))
            st['u_rep'].append(u_rep)
            st['gate'].append(jnp.exp2(jnp.where(tril, r_row - u_rep, NEG)))
            w_s = jnp.exp2(r_row - u_last[:, lane_h:lane_h + 1])
            k_w = (st['k_t'][h] * w_s).astype(BF16)
            c_ref[h] = w_prev_row[:, lane_h:lane_h + 1] * st['cmat'][h] + _dot(k_w, st['v_aug'][h])
        m_ref[...] = m_cols[L - 1:L, :]

    def read_out():
        st['num_aug'] = []
        for h in heads:
            lane_h = ZS_MI + h
            w_inter = jnp.exp2(st['m_prev'][:, lane_h:lane_h + 1] - st['u_rep'][h])
            st['num_aug'].append(_dot((st['qk'][h] * st['gate'][h]).astype(BF16), st['v_aug'][h])
                                 + jnp.concatenate([w_inter, w_inter], axis=1) * st['qc'][h])

    def normalise_and_store():
        for h in heads:
            lane_h = ZS_MI + h
            num_aug = st['num_aug'][h]
            m_rep = jnp.broadcast_to(st['m_cols'][:, lane_h:lane_h + 1], (L, ML_DH))
            hval = num_aug[:, :ML_DH] / jnp.maximum(jnp.abs(num_aug[:, ML_DH:]), jnp.exp2(-m_rep))
            mu = jnp.mean(hval, axis=-1, keepdims=True)
            var = jnp.mean(jnp.square(hval - mu), axis=-1, keepdims=True)
            hn = (hval - mu) * lax.rsqrt(var + EPS) * ng_ref[:, hs[h]]
            og = jax.nn.sigmoid(o_ref[rows, hs[h]].astype(F32))
            out_ref[rows, hs[h]] = (og * hn).astype(BF16)

    return [projections, memory_update, read_out, normalise_and_store]


def _nsa_prep_kernel(x_ref, pe_ref, w1_ref, w2_ref, kk_ref, vv_ref, feat_ref, featc_ref,
                     kaug_ref, vt_ref, kcaug_ref, vct_ref):
    S = kk_ref.shape[0]
    n = S // CMP_STRIDE
    cw = 2 * NSA_KV
    half = CMP_BLOCK // 2
    acc_a = jnp.zeros((n, cw), F32)
    acc_b = jnp.zeros((n, cw), F32)
    for j in range(half):
        xj = jnp.concatenate([x_ref[part, pl.ds(j, n, stride=CMP_STRIDE), :] for part in range(cw // LANES)],
                             axis=1)
        acc_a = acc_a + _dot((xj + pe_ref[j:j + 1, :]).astype(BF16), w1_ref[j])
        acc_b = acc_b + _dot((xj + pe_ref[half + j:half + j + 1, :]).astype(BF16), w1_ref[half + j])
    pre = acc_a + pltpu.roll(acc_b, n - 1, axis=0)
    cmp = _dot(jax.nn.gelu(pre).astype(BF16), w2_ref[...])

    lane = lax.broadcasted_iota(jnp.int32, (1, LANES), 1)
    keep = [jnp.where(lane < NSA_DH, 1.0, 0.0).astype(BF16), jnp.where(lane >= NSA_DH, 1.0, 0.0).astype(BF16)]
    for g in range(NSA_G):
        kd = cmp[:, g * LANES:(g + 1) * LANES].astype(BF16)
        for e in range(2):
            kcaug_ref[0, g * 2 + e] = kd * keep[e] + featc_ref[e]
    vct_ref[0] = cmp[:, NSA_G * LANES:(NSA_G + 1) * LANES].T.astype(BF16)
    tb = min(TK, S)
    for br in range(2):
        for g in range(NSA_G):
            kd = kk_ref[:, (br * NSA_G + g) * LANES:(br * NSA_G + g + 1) * LANES]
            for e in range(2):
                kaug_ref[0, (br * NSA_G + g) * 2 + e] = kd * keep[e] + feat_ref[br * 2 + e]
        for c in range(S // tb):
            v_t = vv_ref[c * tb:(c + 1) * tb, br * LANES:(br + 1) * LANES].astype(F32).T.astype(BF16)
            for g in range(NSA_G):
                vt_ref[0, br, g, 0:NSA_DH, c * tb:(c + 1) * tb] = v_t[g * NSA_DH:(g + 1) * NSA_DH]
                vt_ref[0, br, g, NSA_DH:VT_ROWS, c * tb:(c + 1) * tb] = jnp.ones((VT_ROWS - NSA_DH, tb), BF16)


def _blockdiag(blocks):
    n = len(blocks)
    rows = []
    for i, blk in enumerate(blocks):
        rows.append(jnp.concatenate([blk if j == i else jnp.zeros((blk.shape[0], blocks[j].shape[1]), blk.dtype)
                                     for j in range(n)], axis=1))
    return jnp.concatenate(rows, axis=0)


def _key_features(S):
    pos = np.arange(S)
    f = np.zeros((4, S, LANES), np.float32)
    for e in range(2):
        base = NSA_DH * (1 - e)
        f[e, pos, base + FEAT_SEL + pos // SEL_BLOCK] = 1.0
        for br in range(2):
            f[br * 2 + e, :, base + FEAT_HI:base + FEAT_HI + FEAT_TERMS] = (pos // 64)[:, None]
            f[br * 2 + e, :, base + FEAT_LO:base + FEAT_LO + FEAT_TERMS] = (pos % 64)[:, None]
    return jnp.asarray(f, dtype=BF16)


def _cmp_features(S):
    c = np.arange(S // CMP_STRIDE)
    hi = (c * CMP_STRIDE) // 64
    lo = c * CMP_STRIDE + (CMP_BLOCK - 1) * 0.5 - 64 * hi
    f = np.zeros((2, c.size, LANES), np.float32)
    for e in range(2):
        base = NSA_DH * (1 - e)
        f[e, :, base + FEAT_HI:base + FEAT_HI + FEAT_TERMS] = hi[:, None]
        f[e, :, base + FEAT_LO:base + FEAT_LO + FEAT_TERMS] = lo[:, None]
    return jnp.asarray(f, dtype=BF16)


def _bf16_terms(x):
    terms = []
    for _ in range(FEAT_TERMS):
        terms.append(float(np.asarray(x - sum(terms), dtype=BF16)))
    return terms


def _nsa_prep(zc, zm, pe_k, pe_v, phi_k1, phi_k2, phi_v1, phi_v2, S):
    M = zc.shape[1]
    B = M // S
    n = S // CMP_STRIDE
    cw = 2 * NSA_KV
    assert S // SEL_BLOCK <= FEAT_HI and S <= 64 * 64
    pe = jnp.concatenate([pe_k, pe_k, pe_v, pe_v], axis=1)
    k1 = phi_k1.reshape(CMP_BLOCK, NSA_DH, NSA_DH)
    v1 = phi_v1.reshape(CMP_BLOCK, NSA_DH, NSA_DH)
    blocks = jnp.stack([k1, k1, v1, v1], axis=1)
    w1 = jnp.einsum('jaxy,ab->jaxby', blocks, jnp.eye(4, dtype=F32)).reshape(CMP_BLOCK, cw, cw).astype(BF16)
    k2 = jnp.concatenate([phi_k2, phi_k2], axis=1)
    w2 = _blockdiag([k2, k2, phi_v2, phi_v2]).astype(BF16)

    def full(shape):
        return pl.BlockSpec(shape, lambda b: (0,) * len(shape))

    return pl.pallas_call(
        _nsa_prep_kernel,
        grid=(B,),
        in_specs=[pl.BlockSpec((cw // LANES, S, LANES), lambda b: (0, b, 0)),
                  full((CMP_BLOCK, cw)), full((CMP_BLOCK, cw, cw)), full((cw, 3 * LANES)),
                  pl.BlockSpec((S, KK_W), lambda b: (b, ZM_KK // KK_W)),
                  pl.BlockSpec((S, VV_W), lambda b: (b, ZM_VV // VV_W)),
                  full((4, S, LANES)), full((2, n, LANES))],
        out_specs=[pl.BlockSpec((1, 8, S, LANES), lambda b: (b, 0, 0, 0)),
                   pl.BlockSpec((1, 2, NSA_G, VT_ROWS, S), lambda b: (b, 0, 0, 0, 0)),
                   pl.BlockSpec((1, 4, n, LANES), lambda b: (b, 0, 0, 0)),
                   pl.BlockSpec((1, LANES, n), lambda b: (b, 0, 0))],
        out_shape=[jax.ShapeDtypeStruct((B, 8, S, LANES), BF16),
                   jax.ShapeDtypeStruct((B, 2, NSA_G, VT_ROWS, S), BF16),
                   jax.ShapeDtypeStruct((B, 4, n, LANES), BF16),
                   jax.ShapeDtypeStruct((B, LANES, n), BF16)],
        compiler_params=_cparams(("arbitrary",)),
        name="nsa_prep",
    )(zc, pe, w1, w2, zm, zm, _key_features(S), _cmp_features(S))


def _nsa_mlstm_kernel(q_ref, kaug_ref, vt_ref, kcaug_ref, vct_ref, zs_ref, ovt_ref,
                      mq_ref, mk_ref, mv_ref, mo_ref, cw_ref, cb_ref, gb_ref, ng_ref, out_ref, yb_ref,
                      qat_ref, qaw_ref, ocmp_ref, m_ref, acc_ref, mc_ref, mm_ref):
    S = kaug_ref.shape[2]
    tq = q_ref.shape[0]
    ncmp = kcaug_ref.shape[2]
    nsel = S // SEL_BLOCK
    n_pairs = NSA_H // 2
    qi = pl.program_id(1)
    t0 = pl.multiple_of(qi * tq, tq)
    t_row = t0 + lax.broadcasted_iota(jnp.int32, (1, tq), 1)

    @pl.when(qi == 0)
    def _():
        mc_ref[...] = jnp.zeros_like(mc_ref)
        mm_ref[...] = jnp.zeros_like(mm_ref)

    def mlstm_chunk(c):
        return _mlstm_chunk_stages(mq_ref, mk_ref, mv_ref, mo_ref, zs_ref, cw_ref, cb_ref, gb_ref, ng_ref, yb_ref,
                                   mc_ref, mm_ref, row0=c * ML_CHUNK, t_abs=t0 + c * ML_CHUNK,
                                   has_history=(qi > 0) if c == 0 else True)

    for k in range(NSA_H // 2):
        q_t = (q_ref[:, k * LANES:(k + 1) * LANES].astype(F32) * (NSA_DH ** -0.5 * LOG2E)).T.astype(BF16)
        for e in range(2):
            for dst in (qat_ref, qaw_ref):
                dst[(k // 2) * 2 + e, e * NSA_DH:(e + 1) * NSA_DH, (k % 2) * tq:(k % 2 + 1) * tq] = (
                    q_t[e * NSA_DH:(e + 1) * NSA_DH])
    feat_row = lax.broadcasted_iota(jnp.int32, (NSA_DH, 2 * tq), 0)
    second_head = lax.broadcasted_iota(jnp.int32, (NSA_DH, 2 * tq), 1) >= tq
    for pair in range(n_pairs):
        g, e = pair // 2, pair % 2
        terms = [_bf16_terms(LOG2E * 2.0 ** (-8.0 * (g * NSA_R + 2 * s + e + 1.0) / NSA_H)) for s in range(2)]
        feat = jnp.zeros((NSA_DH, 2 * tq), F32)
        for i in range(FEAT_TERMS):
            term = jnp.where(second_head, terms[1][i], terms[0][i])
            feat = jnp.where(feat_row == FEAT_HI + i, term * 64.0, jnp.where(feat_row == FEAT_LO + i, term, feat))
        qat_ref[pair, NSA_DH * (1 - e):NSA_DH * (2 - e), :] = feat.astype(BF16)
        qaw_ref[pair, NSA_DH * (1 - e):NSA_DH * (2 - e), :] = feat.astype(BF16)

    ki = lax.broadcasted_iota(jnp.int32, (TK, 2 * tq), 0)
    qu = lax.broadcasted_iota(jnp.int32, (TK, 2 * tq), 1)
    qu = jnp.where(qu >= tq, qu - tq, qu)
    tile_valid = {MASK_CAUSAL: ki <= qu,
                  MASK_WINDOW_EDGE: ki > qu}

    c_col = lax.broadcasted_iota(jnp.int32, (ncmp, 1), 0)
    t_row2 = jnp.concatenate([t_row, t_row], axis=1)
    valid_c = c_col * CMP_STRIDE + (CMP_BLOCK - 1) <= t_row2
    j_col = lax.broadcasted_iota(jnp.int32, (nsel, 1), 0)
    j_f = j_col.astype(F32)
    jt = t_row >> 6
    forced = jnp.logical_or(j_col == 0, jnp.logical_or(j_col == jt, j_col == jt - 1))
    future = j_col > jt

    m_ref[...] = jnp.full(m_ref.shape, NEG, F32)
    acc_ref[...] = jnp.zeros(acc_ref.shape, F32)

    def compressed_scores():
        return [_dot(kcaug_ref[0, pair], qaw_ref[pair]) for pair in range(n_pairs)]

    def compressed_branch_and_selection(scores_c):
        for g in range(NSA_G):
            psum = jnp.zeros((ncmp, tq), F32)
            for e in range(2):
                pair = g * 2 + e
                sc = jnp.where(valid_c, scores_c[pair], NEG)
                ex = jnp.where(valid_c, jnp.exp2(sc - jnp.max(sc, axis=0, keepdims=True)), 0.0)
                den = jnp.sum(ex, axis=0, keepdims=True)
                p = ex * (1.0 / jnp.where(den > 0.0, den, 1.0))
                psum = psum + p[:, :tq] + p[:, tq:]
                ocmp_ref[pair] = _dot(vct_ref[0, g * NSA_DH:(g + 1) * NSA_DH, :], p.astype(BF16))

            p_hi = psum.astype(BF16)
            p_lo = (psum - p_hi.astype(F32)).astype(BF16)
            imp = _dot(ovt_ref[...], p_hi) + _dot(ovt_ref[...], p_lo)
            val = jnp.where(future, NEG, jnp.where(forced, BIG, imp))
            penalty = jnp.full((nsel, tq), NEG, F32)
            for _ in range(min(TOP_N, nsel)):
                best = jnp.max(val, axis=0, keepdims=True)
                first = jnp.min(jnp.where(val == best, j_f, float(nsel)), axis=0, keepdims=True)
                hit = j_f == first
                penalty = jnp.where(hit, 0.0, penalty)
                val = jnp.where(hit, TAKEN, val)
            penalty = penalty.astype(BF16)
            for e in range(2):
                first = NSA_DH * (1 - e) + FEAT_SEL
                qat_ref[g * 2 + e, first:first + nsel, :] = jnp.concatenate([penalty, penalty], axis=1)

    def attend(br, tiles, interleave=()):
        stages = [(pl.multiple_of(kb * TK, TK), mask, pair) for kb, mask in tiles for pair in range(n_pairs)]

        def scores(stage):
            k0, _, pair = stage
            queries = qat_ref if br == BR_SEL else qaw_ref
            return _dot(kaug_ref[0, br * n_pairs + pair, pl.ds(k0, TK), :], queries[pair])

        s_queue = [scores(st) for st in stages[:SCORE_LOOKAHEAD]]
        pending = list(interleave)
        if pending:
            pending.pop(0)()
        for i, (k0, mask, pair) in enumerate(stages):
            s_t = s_queue.pop(0)
            if i + SCORE_LOOKAHEAD < len(stages):
                s_queue.append(scores(stages[i + SCORE_LOOKAHEAD]))
            slot = br * n_pairs + pair
            if mask is not None:
                s_t = jnp.where(tile_valid[mask], s_t, NEG)
            m_old = m_ref[slot]
            m_new = jnp.maximum(m_old, jnp.max(s_t, axis=0, keepdims=True))
            alpha = jnp.exp2(m_old - m_new)
            p = jnp.exp2(s_t - m_new).astype(BF16)
            acc_ref[slot] = alpha * acc_ref[slot] + _dot(vt_ref[0, br, pair // 2, :, pl.ds(k0, TK)], p)
            m_ref[slot] = m_new
            if pending and i % 2 == 1:
                pending.pop(0)()
        for emit in pending:
            emit()

    n_back = WINDOW // TK
    for n_behind in range(n_back + 1):
        tiles = [(qi, MASK_CAUSAL)] + [(qi - d, MASK_WINDOW_EDGE if d == n_back else None)
                                        for d in range(1, n_behind + 1)]

        @pl.when(qi >= n_back if n_behind == n_back else qi == n_behind)
        def _(tiles=tiles):
            scores_c = compressed_scores()
            project, *rest = mlstm_chunk(0)

            def selection_and_projections():
                compressed_branch_and_selection(scores_c)
                project()

            attend(BR_WIN, tiles, [selection_and_projections] + rest)

    def sel_body(i, carry):
        attend(BR_SEL, [(2 * i, None), (2 * i + 1, None)])
        return carry

    lax.fori_loop(0, qi // 2, sel_body, 0)

    @pl.when(qi % 2 == 1)
    def _():
        attend(BR_SEL, [(qi - 1, None), (qi, MASK_CAUSAL)], mlstm_chunk(1))

    @pl.when(qi % 2 == 0)
    def _():
        attend(BR_SEL, [(qi, MASK_CAUSAL)], mlstm_chunk(1))

    gates_t = jax.nn.sigmoid(zs_ref[...]).T
    for hp in range(NSA_H // 2):
        comb = []
        for e in range(2):
            h = 2 * hp + e
            c0 = ZS_GATE + NSA_NB * h
            pair = (h // NSA_R) * 2 + e
            cols = slice(((h % NSA_R) // 2) * tq, ((h % NSA_R) // 2 + 1) * tq)
            sel, win = BR_SEL * n_pairs + pair, BR_WIN * n_pairs + pair
            o_sel = acc_ref[sel, 0:NSA_DH, cols] * (1.0 / acc_ref[sel, NSA_DH:NSA_DH + 1, cols])
            o_win = acc_ref[win, 0:NSA_DH, cols] * (1.0 / acc_ref[win, NSA_DH:NSA_DH + 1, cols])
            comb.append(gates_t[c0:c0 + 1] * ocmp_ref[pair, :, cols] + gates_t[c0 + 1:c0 + 2] * o_sel
                        + gates_t[c0 + 2:c0 + 3] * o_win)
        out_ref[:, hp * LANES:(hp + 1) * LANES] = jnp.concatenate(comb, axis=0).T.astype(BF16)


def _overlap_t(S):
    c = np.arange(S // CMP_STRIDE)[None, :]
    j = np.arange(S // SEL_BLOCK)[:, None]
    ov = (c * CMP_STRIDE <= j * SEL_BLOCK + SEL_BLOCK - 1) & (c * CMP_STRIDE + CMP_BLOCK - 1 >= j * SEL_BLOCK)
    ov &= c < (S - CMP_BLOCK) // CMP_STRIDE + 1
    return jnp.asarray(ov.astype(np.float32), dtype=BF16)


def _nsa_mlstm(zm, zs, kaug, vt, kcaug, vct, conv_w, conv_b, gate_b, norm_g, S):
    M = zm.shape[0]
    B = M // S
    tq = min(TQ, S)
    assert tq == TK and WINDOW % TK == 0 and S % TK == 0 and tq == 2 * ML_CHUNK and ML_CHUNK == ML_DH
    nq = S // tq
    n = S // CMP_STRIDE
    nsel = S // SEL_BLOCK
    gb_row = jnp.zeros((1, ZS_W), F32).at[0, ZS_MI:ZS_MI + 2 * ML_H].set(gate_b)

    def tile(off):
        return pl.BlockSpec((tq, ML_W), lambda b, i: (b * nq + i, off // ML_W))

    def row(off):
        return pl.BlockSpec((S, ML_W), lambda b, i: (b, off // ML_W))

    def full(shape):
        return pl.BlockSpec(shape, lambda b, i: (0,) * len(shape))

    return pl.pallas_call(
        _nsa_mlstm_kernel,
        grid=(B, nq),
        in_specs=[tile(ZM_NQ),
                  pl.BlockSpec((1, 8, S, LANES), lambda b, i: (b, 0, 0, 0)),
                  pl.BlockSpec((1, 2, NSA_G, VT_ROWS, S), lambda b, i: (b, 0, 0, 0, 0)),
                  pl.BlockSpec((1, 4, n, LANES), lambda b, i: (b, 0, 0, 0)),
                  pl.BlockSpec((1, LANES, n), lambda b, i: (b, 0, 0)),
                  pl.BlockSpec((tq, ZS_W), lambda b, i: (b * nq + i, 0)),
                  full((nsel, n)),
                  row(ZM_MQ), row(ZM_MK), tile(ZM_MV), tile(ZM_MO),
                  full((CONV_K, 2 * ML_W)), full((1, 2 * ML_W)), full((1, ZS_W)), full((1, ML_W))],
        out_specs=[pl.BlockSpec((tq, NSA_W), lambda b, i: (b * nq + i, 0)),
                   pl.BlockSpec((tq, ML_W), lambda b, i: (b * nq + i, 0))],
        out_shape=[jax.ShapeDtypeStruct((M, NSA_W), BF16), jax.ShapeDtypeStruct((M, ML_W), BF16)],
        scratch_shapes=[pltpu.VMEM((NSA_H // 2, LANES, 2 * tq), BF16),
                        pltpu.VMEM((NSA_H // 2, LANES, 2 * tq), BF16),
                        pltpu.VMEM((NSA_H // 2, NSA_DH, 2 * tq), F32),
                        pltpu.VMEM((NSA_H, 1, 2 * tq), F32),
                        pltpu.VMEM((NSA_H, VT_ROWS, 2 * tq), F32),
                        pltpu.VMEM((ML_H, ML_DH, 2 * ML_DH), F32),
                        pltpu.VMEM((1, ZS_W), F32)],
        compiler_params=_cparams(("arbitrary", "arbitrary")),
        name="nsa_mlstm",
    )(zm, kaug, vt, kcaug, vct, zs, _overlap_t(S), zm, zm, zm, zm,
      conv_w, conv_b.reshape(1, 2 * ML_W), gb_row, norm_g.reshape(1, ML_W))


def _merge_kernel(gu0_ref, gv0_ref, gu_ref, gv_ref, lng_ref, lnb_ref, ws_ref, bst_ref, b_ref, c_ref,
                  ga_ref, gb_ref, gc_ref, x_ref, gt_ref, wa_ref, wb_ref, wc_ref, wo_ref, o_ref, a_ref):
    i = pl.program_id(0)

    @pl.when(i == 0)
    def _():
        _gmlp_rows(gu0_ref, gv0_ref, lng_ref, lnb_ref, ws_ref, bst_ref, a_ref.at[0])

    a = a_ref[i % 2]
    merged = (jax.nn.sigmoid(ga_ref[...].astype(F32)) * _dot(a, wa_ref[...])
              + jax.nn.sigmoid(gb_ref[...].astype(F32)) * _dot(b_ref[...], wb_ref[...])
              + jax.nn.sigmoid(gc_ref[...].astype(F32)) * _dot(c_ref[...], wc_ref[...]))
    o_ref[...] = x_ref[...] + gt_ref[0] * _dot(merged.astype(BF16), wo_ref[...])
    _gmlp_rows(gu_ref, gv_ref, lng_ref, lnb_ref, ws_ref, bst_ref, a_ref.at[(i + 1) % 2])


def _merge(yb, yc, zm, x2, gt, ln_g, ln_b, ws, bs, wa, wb, wc, wo, S):
    M, D = x2.shape
    tm = min(TM_MERGE, S)
    per_b = S // tm

    def rows(w, jcol=0):
        return pl.BlockSpec((tm, w), lambda i: (i, jcol))

    def next_rows(jcol):
        return pl.BlockSpec((tm, GM_W), lambda i: (jnp.minimum(i + 1, M // tm - 1), jcol))

    def full(shape):
        return pl.BlockSpec(shape, lambda i: (0,) * len(shape))

    return pl.pallas_call(
        _merge_kernel,
        grid=(M // tm,),
        in_specs=[rows(GM_W, ZM_GU // GM_W), rows(GM_W, ZM_GV // GM_W),
                  next_rows(ZM_GU // GM_W), next_rows(ZM_GV // GM_W), full((1, GM_W)), full((1, GM_W)),
                  full((GM_GROUPS, GM_CHUNK, GM_CHUNK)), full((GM_CHUNK, GM_GROUPS)),
                  rows(ML_W), rows(NSA_W),
                  rows(D, ZM_GA // D), rows(D, ZM_GBR // D), rows(D, ZM_GC // D),
                  rows(D), pl.BlockSpec((1, 1, D), lambda i: (i // per_b, 0, 0)),
                  full((GM_W, D)), full((ML_W, D)), full((NSA_W, D)), full((D, D))],
        out_specs=rows(D),
        out_shape=jax.ShapeDtypeStruct((M, D), F32),
        scratch_shapes=[pltpu.VMEM((2, tm, GM_W), BF16)],
        compiler_params=_cparams(("arbitrary",)),
        name="gmlp_merge_out",
    )(zm, zm, zm, zm, ln_g.reshape(1, GM_W), ln_b.reshape(1, GM_W), ws, bs.T, yb, yc, zm, zm, zm, x2, gt, wa, wb, wc, wo)


def _mlp_kernel(x_ref, g_ref, sc_ref, sh_ref, gt_ref, w1_ref, w2_ref, gf_ref, o_ref, acc_ref, *, final_norm):
    x = x_ref[...]
    h = _modulated_norm(x, g_ref[...], sc_ref[0], sh_ref[0]).astype(BF16)
    for c in range(D_FF // FF_CHUNK):
        mid = jnp.square(jnp.maximum(_dot(h, w1_ref[:, c * FF_CHUNK:(c + 1) * FF_CHUNK]), 0.0)).astype(BF16)
        upd = _dot(mid, w2_ref[c * FF_CHUNK:(c + 1) * FF_CHUNK, :])
        if c == 0:
            acc_ref[...] = upd
        else:
            acc_ref[...] += upd
    y = x + gt_ref[0] * acc_ref[...]
    if final_norm:
        y = (y * lax.rsqrt(jnp.mean(y * y, axis=-1, keepdims=True) + EPS)) * gf_ref[...]
    o_ref[...] = y


def _mlp(x2, g, sc, sh, gt, w1, w2, g_final, S, final_norm):
    M, D = x2.shape
    tm = min(TM_MLP, S)
    per_b = S // tm
    mod = pl.BlockSpec((1, 1, D), lambda i: (i // per_b, 0, 0))
    return pl.pallas_call(
        functools.partial(_mlp_kernel, final_norm=final_norm),
        grid=(M // tm,),
        in_specs=[pl.BlockSpec((tm, D), lambda i: (i, 0)),
                  pl.BlockSpec((1, D), lambda i: (0, 0)),
                  mod, mod, mod,
                  pl.BlockSpec((D, D_FF), lambda i: (0, 0)),
                  pl.BlockSpec((D_FF, D), lambda i: (0, 0)),
                  pl.BlockSpec((1, D), lambda i: (0, 0))],
        out_specs=pl.BlockSpec((tm, D), lambda i: (i, 0)),
        out_shape=jax.ShapeDtypeStruct((M, D), F32),
        scratch_shapes=[pltpu.VMEM((tm, D), F32)],
        compiler_params=_cparams(("arbitrary",)),
        name="relu2_mlp",
    )(x2, g, sc, sh, gt, w1, w2, g_final)


def _w_cols(w, name):
    return w[:, _OFFS[name]:_OFFS[name] + _SIZE[name]]


def _dup_heads(w):
    parts = []
    for g in range(NSA_G):
        blk = w[:, g * NSA_DH:(g + 1) * NSA_DH]
        parts += [blk, blk]
    return jnp.concatenate(parts, axis=1)


def _proj_weights(w):
    D = w.shape[0]
    wm = jnp.concatenate([_w_cols(w, n) for n in ('ga', 'gbr', 'gc', 'gu', 'gv', 'mq', 'mk', 'mv', 'mo', 'nq')]
                         + [_dup_heads(_w_cols(w, 'nks')), _dup_heads(_w_cols(w, 'nkw')),
                            _w_cols(w, 'nvs'), _w_cols(w, 'nvw')], axis=1)
    wc = jnp.concatenate([_w_cols(w, 'nkc'), _w_cols(w, 'nvc')], axis=1)
    ws = jnp.concatenate([_w_cols(w, 'ngate'), _w_cols(w, 'mi'), _w_cols(w, 'mf'),
                          jnp.zeros((D, ZS_W - NSA_H * NSA_NB - 2 * ML_H), w.dtype)], axis=1)
    return wm.astype(BF16), wc.astype(BF16), ws.astype(BF16)


def kernel(x, c, g_norm1, g_norm2, w_ada, b_ada, w_in, gm_ln_g, gm_ln_b, gm_ws, gm_bs, ml_conv_w, ml_conv_b,
           ml_gate_b, ml_norm_g, nsa_pe_k, nsa_pe_v, nsa_phi_k1, nsa_phi_k2, nsa_phi_v1, nsa_phi_v2,
           w_up_a, w_up_b, w_up_c, w_out, w_mlp1, w_mlp2, g_final):
    B, S, D = x.shape
    depth = w_in.shape[0]
    M = B * S
    mod = _ada(c, w_ada, b_ada)
    x2 = x.reshape(M, D)
    for l in range(depth):
        sh1, sc1, gt1, sh2, sc2, gt2 = [mod[l, :, i * D:(i + 1) * D].reshape(B, 1, D) for i in range(6)]
        wm, wc, ws = _proj_weights(w_in[l])
        zm, zc, zs = _inproj(x2, g_norm1[l].reshape(1, D), sc1, sh1, wm, wc, ws, S)
        kaug, vt, kcaug, vct = _nsa_prep(zc, zm, nsa_pe_k[l], nsa_pe_v[l], nsa_phi_k1[l], nsa_phi_k2[l],
                                         nsa_phi_v1[l], nsa_phi_v2[l], S)
        yc, yb = _nsa_mlstm(zm, zs, kaug, vt, kcaug, vct, ml_conv_w[l], ml_conv_b[l], ml_gate_b[l], ml_norm_g[l], S)
        x2 = _merge(yb, yc, zm, x2, gt1, gm_ln_g[l], gm_ln_b[l], gm_ws[l], gm_bs[l], w_up_a[l].astype(BF16),
                    w_up_b[l].astype(BF16), w_up_c[l].astype(BF16), w_out[l].astype(BF16), S)
        x2 = _mlp(x2, g_norm2[l].reshape(1, D), sc2, sh2, gt2, w_mlp1[l].astype(BF16), w_mlp2[l].astype(BF16),
                  g_final.reshape(1, D), S, final_norm=(l == depth - 1))
    return x2.reshape(B, S, D)
```

```python
import functools

import numpy as np
import jax
import jax.numpy as jnp
from jax import lax
from jax.experimental import pallas as pl
from jax.experimental.pallas import tpu as pltpu

F32 = jnp.float32
BF16 = jnp.bfloat16

D_MODEL = 1024
GM_W = 512
GM_GROUPS = 4
GM_CHUNK = 128
ML_H = 4
ML_DH = 128
ML_W = ML_H * ML_DH
CONV_K = 4
NSA_H = 8
NSA_G = 2
NSA_R = NSA_H // NSA_G
NSA_DH = 64
NSA_W = NSA_H * NSA_DH
NSA_KV = NSA_G * NSA_DH
NSA_NB = 3
CMP_BLOCK = 32
CMP_STRIDE = 16
SEL_BLOCK = 64
TOP_N = 8
WINDOW = 512
D_FF = 4 * D_MODEL
EPS = 1e-6
NEG = -1e30
BIG = 1e4
TAKEN = -3e38
SPLIT_SIZES = (GM_W, GM_W, ML_W, ML_W, ML_W, ML_W, ML_H, ML_H, NSA_W, NSA_KV, NSA_KV, NSA_KV, NSA_KV,
               NSA_KV, NSA_KV, NSA_H * NSA_NB, D_MODEL, D_MODEL, D_MODEL)
SPLIT_NAMES = ('gu', 'gv', 'mq', 'mk', 'mv', 'mo', 'mi', 'mf', 'nq', 'nkc', 'nvc', 'nks', 'nvs', 'nkw', 'nvw',
               'ngate', 'ga', 'gbr', 'gc')
_OFFS = dict(zip(SPLIT_NAMES, np.concatenate([[0], np.cumsum(SPLIT_SIZES)[:-1]]).tolist()))
_SIZE = dict(zip(SPLIT_NAMES, SPLIT_SIZES))

LANES = 128
VMEM_LIMIT = 56 * 1024 * 1024

ZM_GA, ZM_GBR, ZM_GC = 0, 1024, 2048
ZM_GU, ZM_GV = 3072, 3584
ZM_MQ, ZM_MK, ZM_MV, ZM_MO = 4096, 4608, 5120, 5632
ZM_NQ = 6144
ZM_KK = 6656
ZM_VV = 7168
ZM_W = 7424
KK_W = 4 * LANES
VV_W = 2 * LANES
BR_SEL, BR_WIN = 0, 1
MASK_CAUSAL, MASK_WINDOW_EDGE = 0, 1
FEAT_SEL = 0
FEAT_TERMS = 3
FEAT_HI = 32
FEAT_LO = 35
LOG2E = 1.4426950408889634
ZS_GATE = 0
ZS_MI = 24
ZS_MF = 28
ZS_W = 128

TM_PROJ = 1024
TN_PROJ = 3712
ML_CHUNK = 128
CONV_HALO = 16
TQ = 256
TK = 256
SCORE_LOOKAHEAD = 3
SEL_TILES = 4
VT_ROWS = NSA_DH + 16
TM_MERGE = 512
TM_MLP = 512
FF_CHUNK = 1024


def _dot(a, b):
    return jnp.dot(a, b, preferred_element_type=F32)


def _split3(x):
    x1 = x.astype(BF16)
    r1 = x - x1.astype(F32)
    x2 = r1.astype(BF16)
    x3 = (r1 - x2.astype(F32)).astype(BF16)
    return x1, x2, x3


def _cparams(sem):
    return pltpu.CompilerParams(dimension_semantics=sem, vmem_limit_bytes=VMEM_LIMIT)


def _ada_kernel(c_ref, w_ref, b_ref, o_ref):
    c = c_ref[...]
    cond = c * jax.nn.sigmoid(c)
    c1, c2, c3 = _split3(cond)
    w1, w2, w3 = _split3(w_ref[0])
    acc = _dot(c1, w1) + (_dot(c1, w2) + _dot(c2, w1)) + (_dot(c1, w3) + _dot(c2, w2) + _dot(c3, w1))
    o_ref[0] = acc + b_ref[0]


def _ada(c, w_ada, b_ada):
    L, D, N = w_ada.shape
    B = c.shape[0]
    tn = 1536
    return pl.pallas_call(
        _ada_kernel,
        grid=(L, N // tn),
        in_specs=[pl.BlockSpec((B, D), lambda l, j: (0, 0)),
                  pl.BlockSpec((1, D, tn), lambda l, j: (l, 0, j)),
                  pl.BlockSpec((1, 1, tn), lambda l, j: (l, 0, j))],
        out_specs=pl.BlockSpec((1, B, tn), lambda l, j: (l, 0, j)),
        out_shape=jax.ShapeDtypeStruct((L, B, N), F32),
        compiler_params=_cparams(("arbitrary", "arbitrary")),
        name="ada_mod",
    )(c, w_ada, b_ada.reshape(L, 1, N))


def _modulated_norm(x, g, sc, sh):
    y = x * lax.rsqrt(jnp.mean(x * x, axis=-1, keepdims=True) + EPS)
    return (y * g) * (1.0 + sc) + sh


def _inproj_kernel(x_ref, g_ref, sc_ref, sh_ref, wm_ref, wc_ref, ws_ref, zm_ref, zc_ref, zs_ref, h_ref):
    @pl.when(pl.program_id(1) == 0)
    def _():
        h = _modulated_norm(x_ref[...], g_ref[...], sc_ref[0], sh_ref[0]).astype(BF16)
        h_ref[...] = h
        zc = _dot(h, wc_ref[...])
        for part in range(zc_ref.shape[0]):
            zc_ref[part] = zc[:, part * LANES:(part + 1) * LANES]
        zs_ref[...] = _dot(h, ws_ref[...])

    zm_ref[...] = _dot(h_ref[...], wm_ref[...]).astype(BF16)


def _inproj(x2, g, sc, sh, wm, wc, ws, S):
    M, D = x2.shape
    tm, tn = min(TM_PROJ, S), TN_PROJ
    per_b = S // tm
    return pl.pallas_call(
        _inproj_kernel,
        grid=(M // tm, ZM_W // tn),
        in_specs=[pl.BlockSpec((tm, D), lambda i, j: (i, 0)),
                  pl.BlockSpec((1, D), lambda i, j: (0, 0)),
                  pl.BlockSpec((1, 1, D), lambda i, j: (i // per_b, 0, 0)),
                  pl.BlockSpec((1, 1, D), lambda i, j: (i // per_b, 0, 0)),
                  pl.BlockSpec((D, tn), lambda i, j: (0, j)),
                  pl.BlockSpec((D, 2 * NSA_KV), lambda i, j: (0, 0)),
                  pl.BlockSpec((D, ZS_W), lambda i, j: (0, 0))],
        out_specs=[pl.BlockSpec((tm, tn), lambda i, j: (i, j)),
                   pl.BlockSpec((2 * NSA_KV // LANES, tm, LANES), lambda i, j: (0, i, 0)),
                   pl.BlockSpec((tm, ZS_W), lambda i, j: (i, 0))],
        out_shape=[jax.ShapeDtypeStruct((M, ZM_W), BF16),
                   jax.ShapeDtypeStruct((2 * NSA_KV // LANES, M, LANES), F32),
                   jax.ShapeDtypeStruct((M, ZS_W), F32)],
        scratch_shapes=[pltpu.VMEM((tm, D), BF16)],
        compiler_params=_cparams(("arbitrary", "arbitrary")),
        name="in_proj",
    )(x2, g, sc, sh, wm, wc, ws)


def _gmlp_rows(u_ref, v_ref, lng_ref, lnb_ref, ws_ref, bst_ref, o_ref):
    ts = u_ref.shape[0]
    dg = GM_W // GM_GROUPS
    row = lax.broadcasted_iota(jnp.int32, (GM_CHUNK, GM_CHUNK), 0)
    col = lax.broadcasted_iota(jnp.int32, (GM_CHUNK, GM_CHUNK), 1)
    ws = [jnp.where(row >= col, ws_ref[g], 0.0).astype(BF16) for g in range(GM_GROUPS)]
    lng = lng_ref[...]
    lnb = lnb_ref[...]
    for c in range(ts // GM_CHUNK):
        r0 = c * GM_CHUNK
        u = jax.nn.gelu(u_ref[r0:r0 + GM_CHUNK, :].astype(F32))
        v = jax.nn.gelu(v_ref[r0:r0 + GM_CHUNK, :].astype(F32))
        mu = jnp.mean(v, axis=-1, keepdims=True)
        var = jnp.mean(jnp.square(v - mu), axis=-1, keepdims=True)
        vb = ((v - mu) * lax.rsqrt(var + EPS) * lng + lnb).astype(BF16)
        for g in range(GM_GROUPS):
            mixed = _dot(ws[g], vb[:, g * dg:(g + 1) * dg]) + bst_ref[:, g:g + 1]
            o_ref[r0:r0 + GM_CHUNK, g * dg:(g + 1) * dg] = (u[:, g * dg:(g + 1) * dg] * mixed).astype(BF16)


def _log_sigmoid(x):
    return jnp.minimum(x, 0.0) - jnp.log1p(jnp.exp(-jnp.abs(x)))


def _conv_silu(x_ext, w, b):
    n = x_ext.shape[0] - CONV_HALO
    y = b
    for j in range(CONV_K):
        sh = CONV_K - 1 - j
        xs = x_ext if sh == 0 else pltpu.roll(x_ext, sh, axis=0)
        y = y + xs[CONV_HALO:CONV_HALO + n] * w[j:j + 1]
    return y * jax.nn.sigmoid(y)


def _mlstm_chunk_stages(q_ref, k_ref, v_ref, o_ref, zs_ref, cw_ref, cb_ref, gb_ref, ng_ref, out_ref, c_ref, m_ref,
                        row0, t_abs, has_history):
    L = ML_CHUNK
    rows = slice(row0, row0 + L)
    hs = [slice(h * ML_DH, (h + 1) * ML_DH) for h in range(ML_H)]
    heads = range(ML_H)
    st = {}

    def conv_rows(src, c0):
        cur = src[pl.ds(pl.multiple_of(t_abs, L), L), :].astype(F32)
        halo0 = pl.multiple_of(jnp.maximum(t_abs - CONV_HALO, 0), CONV_HALO)
        halo = jnp.where(has_history, src[pl.ds(halo0, CONV_HALO), :].astype(F32), 0.0)
        return _conv_silu(jnp.concatenate([halo, cur], axis=0), cw_ref[:, c0:c0 + ML_W], cb_ref[:, c0:c0 + ML_W])

    def projections():
        row = lax.broadcasted_iota(jnp.int32, (L, L), 0)
        col = lax.broadcasted_iota(jnp.int32, (L, L), 1)
        tril_b = jnp.where(row >= col, 1.0, 0.0).astype(BF16)
        triu_b = jnp.where(row <= col, 1.0, 0.0).astype(BF16)
        gi = zs_ref[rows, :] + gb_ref[...]
        gi_t = gi.T
        c1, c2, c3 = _split3(_log_sigmoid(gi) * LOG2E)
        b_cols = _dot(tril_b, c1) + _dot(tril_b, c2) + _dot(tril_b, c3)
        r1, r2, r3 = _split3(_log_sigmoid(gi_t[ZS_MI:ZS_MI + 2 * ML_H]) * LOG2E)
        b_rows = _dot(r1, triu_b) + _dot(r2, triu_b) + _dot(r3, triu_b)
        st['gi'] = gi * LOG2E
        st['r_rows'] = gi_t[ZS_MI:ZS_MI + ML_H] * LOG2E - b_rows[ML_H:2 * ML_H]
        st['b_at_i'] = pltpu.roll(b_cols, ZS_W - (ZS_MF - ZS_MI), axis=1)
        q_all = conv_rows(q_ref, 0).astype(BF16)
        k_all = conv_rows(k_ref, ML_W) * (ML_DH ** -0.5)
        st['q'] = [q_all[:, hs[h]] for h in heads]
        st['k_t'] = [k_all[:, hs[h]].T for h in heads]
        st['cmat'] = [c_ref[h] for h in heads]
        st['qk'] = [_dot(st['q'][h], st['k_t'][h].astype(BF16)) for h in heads]
        st['qc'] = [_dot(st['q'][h], st['cmat'][h].astype(BF16)) for h in heads]
        ones_blk = jnp.ones((L, ML_DH), BF16)
        st['v_aug'] = [jnp.concatenate([v_ref[rows, hs[h]], ones_blk], axis=1) for h in heads]

    def memory_update():
        tril = lax.broadcasted_iota(jnp.int32, (L, L), 0) >= lax.broadcasted_iota(jnp.int32, (L, L), 1)
        time_row = lax.broadcasted_iota(jnp.int32, (L, ZS_W), 0)
        m_row = m_ref[...]
        u = st['gi'] - st['b_at_i']
        shift = 1
        while shift < L:
            u = jnp.maximum(u, jnp.where(time_row >= shift, pltpu.roll(u, shift, axis=0), NEG))
            shift *= 2
        u = jnp.maximum(u, m_row)
        m_cols = st['b_at_i'] + u
        u_last = u[L - 1:L, :]
        w_prev_row = jnp.exp2(m_row - u_last)
        st['m_cols'], st['m_prev'], st['u_rep'], st['gate'] = m_cols, m_row, [], []
        for h in heads:
            lane_h = ZS_MI + h
            r_row = st['r_rows'][h:h + 1, :]
            u_rep = jnp.broadcast_to(u[:, lane_h:lane_h + 1], (L, ML_DH))
            st['u_rep'].append(u_rep)
            st['gate'].append(jnp.exp2(jnp.where(tril, r_row - u_rep, NEG)))
            w_s = jnp.exp2(r_row - u_last[:, lane_h:lane_h + 1])
            k_w = (st['k_t'][h] * w_s).astype(BF16)
            c_ref[h] = w_prev_row[:, lane_h:lane_h + 1] * st['cmat'][h] + _dot(k_w, st['v_aug'][h])
        m_ref[...] = m_cols[L - 1:L, :]

    def read_out():
        st['num_aug'] = []
        for h in heads:
            lane_h = ZS_MI + h
            w_inter = jnp.exp2(st['m_prev'][:, lane_h:lane_h + 1] - st['u_rep'][h])
            st['num_aug'].append(_dot((st['qk'][h] * st['gate'][h]).astype(BF16), st['v_aug'][h])
                                 + jnp.concatenate([w_inter, w_inter], axis=1) * st['qc'][h])

    def normalise_and_store():
        for h in heads:
            lane_h = ZS_MI + h
            num_aug = st['num_aug'][h]
            m_rep = jnp.broadcast_to(st['m_cols'][:, lane_h:lane_h + 1], (L, ML_DH))
            hval = num_aug[:, :ML_DH] / jnp.maximum(jnp.abs(num_aug[:, ML_DH:]), jnp.exp2(-m_rep))
            mu = jnp.mean(hval, axis=-1, keepdims=True)
            var = jnp.mean(jnp.square(hval - mu), axis=-1, keepdims=True)
            hn = (hval - mu) * lax.rsqrt(var + EPS) * ng_ref[:, hs[h]]
            og = jax.nn.sigmoid(o_ref[rows, hs[h]].astype(F32))
            out_ref[rows, hs[h]] = (og * hn).astype(BF16)

    return [projections, memory_update, read_out, normalise_and_store]


def _nsa_prep_kernel(x_ref, pe_ref, w1_ref, w2_ref, kk_ref, vv_ref, feat_ref, featc_ref,
                     kaug_ref, vt_ref, kcaug_ref, vct_ref):
    S = kk_ref.shape[0]
    n = S // CMP_STRIDE
    cw = 2 * NSA_KV
    half = CMP_BLOCK // 2
    acc_a = jnp.zeros((n, cw), F32)
    acc_b = jnp.zeros((n, cw), F32)
    for j in range(half):
        xj = jnp.concatenate([x_ref[part, pl.ds(j, n, stride=CMP_STRIDE), :] for part in range(cw // LANES)],
                             axis=1)
        acc_a = acc_a + _dot((xj + pe_ref[j:j + 1, :]).astype(BF16), w1_ref[j])
        acc_b = acc_b + _dot((xj + pe_ref[half + j:half + j + 1, :]).astype(BF16), w1_ref[half + j])
    pre = acc_a + pltpu.roll(acc_b, n - 1, axis=0)
    cmp = _dot(jax.nn.gelu(pre).astype(BF16), w2_ref[...])

    lane = lax.broadcasted_iota(jnp.int32, (1, LANES), 1)
    keep = [jnp.where(lane < NSA_DH, 1.0, 0.0).astype(BF16), jnp.where(lane >= NSA_DH, 1.0, 0.0).astype(BF16)]
    for g in range(NSA_G):
        kd = cmp[:, g * LANES:(g + 1) * LANES].astype(BF16)
        for e in range(2):
            kcaug_ref[0, g * 2 + e] = kd * keep[e] + featc_ref[e]
    vct_ref[0] = cmp[:, NSA_G * LANES:(NSA_G + 1) * LANES].T.astype(BF16)
    tb = min(TK, S)
    for br in range(2):
        for g in range(NSA_G):
            kd = kk_ref[:, (br * NSA_G + g) * LANES:(br * NSA_G + g + 1) * LANES]
            for e in range(2):
                kaug_ref[0, (br * NSA_G + g) * 2 + e] = kd * keep[e] + feat_ref[br * 2 + e]
        for c in range(S // tb):
            v_t = vv_ref[c * tb:(c + 1) * tb, br * LANES:(br + 1) * LANES].astype(F32).T.astype(BF16)
            for g in range(NSA_G):
                vt_ref[0, br, g, 0:NSA_DH, c * tb:(c + 1) * tb] = v_t[g * NSA_DH:(g + 1) * NSA_DH]
                vt_ref[0, br, g, NSA_DH:VT_ROWS, c * tb:(c + 1) * tb] = jnp.ones((VT_ROWS - NSA_DH, tb), BF16)


def _blockdiag(blocks):
    n = len(blocks)
    rows = []
    for i, blk in enumerate(blocks):
        rows.append(jnp.concatenate([blk if j == i else jnp.zeros((blk.shape[0], blocks[j].shape[1]), blk.dtype)
                                     for j in range(n)], axis=1))
    return jnp.concatenate(rows, axis=0)


def _key_features(S):
    pos = np.arange(S)
    f = np.zeros((4, S, LANES), np.float32)
    for e in range(2):
        base = NSA_DH * (1 - e)
        f[e, pos, base + FEAT_SEL + pos // SEL_BLOCK] = 1.0
        for br in range(2):
            f[br * 2 + e, :, base + FEAT_HI:base + FEAT_HI + FEAT_TERMS] = (pos // 64)[:, None]
            f[br * 2 + e, :, base + FEAT_LO:base + FEAT_LO + FEAT_TERMS] = (pos % 64)[:, None]
    return jnp.asarray(f, dtype=BF16)


def _cmp_features(S):
    c = np.arange(S // CMP_STRIDE)
    hi = (c * CMP_STRIDE) // 64
    lo = c * CMP_STRIDE + (CMP_BLOCK - 1) * 0.5 - 64 * hi
    f = np.zeros((2, c.size, LANES), np.float32)
    for e in range(2):
        base = NSA_DH * (1 - e)
        f[e, :, base + FEAT_HI:base + FEAT_HI + FEAT_TERMS] = hi[:, None]
        f[e, :, base + FEAT_LO:base + FEAT_LO + FEAT_TERMS] = lo[:, None]
    return jnp.asarray(f, dtype=BF16)


def _bf16_terms(x):
    terms = []
    for _ in range(FEAT_TERMS):
        terms.append(float(np.asarray(x - sum(terms), dtype=BF16)))
    return terms


def _nsa_prep(zc, zm, pe_k, pe_v, phi_k1, phi_k2, phi_v1, phi_v2, S):
    M = zc.shape[1]
    B = M // S
    n = S // CMP_STRIDE
    cw = 2 * NSA_KV
    assert S // SEL_BLOCK <= FEAT_HI and S <= 64 * 64
    pe = jnp.concatenate([pe_k, pe_k, pe_v, pe_v], axis=1)
    k1 = phi_k1.reshape(CMP_BLOCK, NSA_DH, NSA_DH)
    v1 = phi_v1.reshape(CMP_BLOCK, NSA_DH, NSA_DH)
    blocks = jnp.stack([k1, k1, v1, v1], axis=1)
    w1 = jnp.einsum('jaxy,ab->jaxby', blocks, jnp.eye(4, dtype=F32)).reshape(CMP_BLOCK, cw, cw).astype(BF16)
    k2 = jnp.concatenate([phi_k2, phi_k2], axis=1)
    w2 = _blockdiag([k2, k2, phi_v2, phi_v2]).astype(BF16)

    def full(shape):
        return pl.BlockSpec(shape, lambda b: (0,) * len(shape))

    return pl.pallas_call(
        _nsa_prep_kernel,
        grid=(B,),
        in_specs=[pl.BlockSpec((cw // LANES, S, LANES), lambda b: (0, b, 0)),
                  full((CMP_BLOCK, cw)), full((CMP_BLOCK, cw, cw)), full((cw, 3 * LANES)),
                  pl.BlockSpec((S, KK_W), lambda b: (b, ZM_KK // KK_W)),
                  pl.BlockSpec((S, VV_W), lambda b: (b, ZM_VV // VV_W)),
                  full((4, S, LANES)), full((2, n, LANES))],
        out_specs=[pl.BlockSpec((1, 8, S, LANES), lambda b: (b, 0, 0, 0)),
                   pl.BlockSpec((1, 2, NSA_G, VT_ROWS, S), lambda b: (b, 0, 0, 0, 0)),
                   pl.BlockSpec((1, 4, n, LANES), lambda b: (b, 0, 0, 0)),
                   pl.BlockSpec((1, LANES, n), lambda b: (b, 0, 0))],
        out_shape=[jax.ShapeDtypeStruct((B, 8, S, LANES), BF16),
                   jax.ShapeDtypeStruct((B, 2, NSA_G, VT_ROWS, S), BF16),
                   jax.ShapeDtypeStruct((B, 4, n, LANES), BF16),
                   jax.ShapeDtypeStruct((B, LANES, n), BF16)],
        compiler_params=_cparams(("arbitrary",)),
        name="nsa_prep",
    )(zc, pe, w1, w2, zm, zm, _key_features(S), _cmp_features(S))


def _nsa_mlstm_kernel(q_ref, kaug_ref, vt_ref, kcaug_ref, vct_ref, zs_ref, ovt_ref,
                      mq_ref, mk_ref, mv_ref, mo_ref, cw_ref, cb_ref, gb_ref, ng_ref, out_ref, yb_ref,
                      qat_ref, qaw_ref, ocmp_ref, m_ref, acc_ref, mc_ref, mm_ref):
    S = kaug_ref.shape[2]
    tq = q_ref.shape[0]
    ncmp = kcaug_ref.shape[2]
    nsel = S // SEL_BLOCK
    n_pairs = NSA_H // 2
    qi = pl.program_id(1)
    t0 = pl.multiple_of(qi * tq, tq)
    t_row = t0 + lax.broadcasted_iota(jnp.int32, (1, tq), 1)

    @pl.when(qi == 0)
    def _():
        mc_ref[...] = jnp.zeros_like(mc_ref)
        mm_ref[...] = jnp.zeros_like(mm_ref)

    def mlstm_chunk(c):
        return _mlstm_chunk_stages(mq_ref, mk_ref, mv_ref, mo_ref, zs_ref, cw_ref, cb_ref, gb_ref, ng_ref, yb_ref,
                                   mc_ref, mm_ref, row0=c * ML_CHUNK, t_abs=t0 + c * ML_CHUNK,
                                   has_history=(qi > 0) if c == 0 else True)

    for k in range(NSA_H // 2):
        q_t = (q_ref[:, k * LANES:(k + 1) * LANES].astype(F32) * (NSA_DH ** -0.5 * LOG2E)).T.astype(BF16)
        for e in range(2):
            for dst in (qat_ref, qaw_ref):
                dst[(k // 2) * 2 + e, e * NSA_DH:(e + 1) * NSA_DH, (k % 2) * tq:(k % 2 + 1) * tq] = (
                    q_t[e * NSA_DH:(e + 1) * NSA_DH])
    feat_row = lax.broadcasted_iota(jnp.int32, (NSA_DH, 2 * tq), 0)
    second_head = lax.broadcasted_iota(jnp.int32, (NSA_DH, 2 * tq), 1) >= tq
    for pair in range(n_pairs):
        g, e = pair // 2, pair % 2
        terms = [_bf16_terms(LOG2E * 2.0 ** (-8.0 * (g * NSA_R + 2 * s + e + 1.0) / NSA_H)) for s in range(2)]
        feat = jnp.zeros((NSA_DH, 2 * tq), F32)
        for i in range(FEAT_TERMS):
            term = jnp.where(second_head, terms[1][i], terms[0][i])
            feat = jnp.where(feat_row == FEAT_HI + i, term * 64.0, jnp.where(feat_row == FEAT_LO + i, term, feat))
        qat_ref[pair, NSA_DH * (1 - e):NSA_DH * (2 - e), :] = feat.astype(BF16)
        qaw_ref[pair, NSA_DH * (1 - e):NSA_DH * (2 - e), :] = feat.astype(BF16)

    ki = lax.broadcasted_iota(jnp.int32, (TK, 2 * tq), 0)
    qu = lax.broadcasted_iota(jnp.int32, (TK, 2 * tq), 1)
    qu = jnp.where(qu >= tq, qu - tq, qu)
    tile_valid = {MASK_CAUSAL: ki <= qu,
                  MASK_WINDOW_EDGE: ki > qu}

    c_col = lax.broadcasted_iota(jnp.int32, (ncmp, 1), 0)
    t_row2 = jnp.concatenate([t_row, t_row], axis=1)
    valid_c = c_col * CMP_STRIDE + (CMP_BLOCK - 1) <= t_row2
    j_col = lax.broadcasted_iota(jnp.int32, (nsel, 1), 0)
    j_f = j_col.astype(F32)
    jt = t_row >> 6
    forced = jnp.logical_or(j_col == 0, jnp.logical_or(j_col == jt, j_col == jt - 1))
    future = j_col > jt

    m_ref[...] = jnp.full(m_ref.shape, NEG, F32)
    acc_ref[...] = jnp.zeros(acc_ref.shape, F32)

    def compressed_scores():
        return [_dot(kcaug_ref[0, pair], qaw_ref[pair]) for pair in range(n_pairs)]

    def compressed_branch_and_selection(scores_c):
        for g in range(NSA_G):
            psum = jnp.zeros((ncmp, tq), F32)
            for e in range(2):
                pair = g * 2 + e
                sc = jnp.where(valid_c, scores_c[pair], NEG)
                ex = jnp.where(valid_c, jnp.exp2(sc - jnp.max(sc, axis=0, keepdims=True)), 0.0)
                den = jnp.sum(ex, axis=0, keepdims=True)
                p = ex * (1.0 / jnp.where(den > 0.0, den, 1.0))
                psum = psum + p[:, :tq] + p[:, tq:]
                ocmp_ref[pair] = _dot(vct_ref[0, g * NSA_DH:(g + 1) * NSA_DH, :], p.astype(BF16))

            p_hi = psum.astype(BF16)
            p_lo = (psum - p_hi.astype(F32)).astype(BF16)
            imp = _dot(ovt_ref[...], p_hi) + _dot(ovt_ref[...], p_lo)
            val = jnp.where(future, NEG, jnp.where(forced, BIG, imp))
            penalty = jnp.full((nsel, tq), NEG, F32)
            for _ in range(min(TOP_N, nsel)):
                best = jnp.max(val, axis=0, keepdims=True)
                first = jnp.min(jnp.where(val == best, j_f, float(nsel)), axis=0, keepdims=True)
                hit = j_f == first
                penalty = jnp.where(hit, 0.0, penalty)
                val = jnp.where(hit, TAKEN, val)
            penalty = penalty.astype(BF16)
            for e in range(2):
                first = NSA_DH * (1 - e) + FEAT_SEL
                qat_ref[g * 2 + e, first:first + nsel, :] = jnp.concatenate([penalty, penalty], axis=1)

    def attend(br, tiles, interleave=()):
        stages = [(pl.multiple_of(kb * TK, TK), mask, pair) for kb, mask in tiles for pair in range(n_pairs)]

        def scores(stage):
            k0, _, pair = stage
            queries = qat_ref if br == BR_SEL else qaw_ref
            return _dot(kaug_ref[0, br * n_pairs + pair, pl.ds(k0, TK), :], queries[pair])

        s_queue = [scores(st) for st in stages[:SCORE_LOOKAHEAD]]
        pending = list(interleave)
        if pending:
            pending.pop(0)()
        for i, (k0, mask, pair) in enumerate(stages):
            s_t = s_queue.pop(0)
            if i + SCORE_LOOKAHEAD < len(stages):
                s_queue.append(scores(stages[i + SCORE_LOOKAHEAD]))
            slot = br * n_pairs + pair
            if mask is not None:
                s_t = jnp.where(tile_valid[mask], s_t, NEG)
            m_old = m_ref[slot]
            m_new = jnp.maximum(m_old, jnp.max(s_t, axis=0, keepdims=True))
            alpha = jnp.exp2(m_old - m_new)
            p = jnp.exp2(s_t - m_new).astype(BF16)
            acc_ref[slot] = alpha * acc_ref[slot] + _dot(vt_ref[0, br, pair // 2, :, pl.ds(k0, TK)], p)
            m_ref[slot] = m_new
            if pending and i % 2 == 1:
                pending.pop(0)()
        for emit in pending:
            emit()

    n_back = WINDOW // TK
    for n_behind in range(n_back + 1):
        tiles = [(qi, MASK_CAUSAL)] + [(qi - d, MASK_WINDOW_EDGE if d == n_back else None)
                                        for d in range(1, n_behind + 1)]

        @pl.when(qi >= n_back if n_behind == n_back else qi == n_behind)
        def _(tiles=tiles):
            scores_c = compressed_scores()
            project, *rest = mlstm_chunk(0)

            def selection_and_projections():
                compressed_branch_and_selection(scores_c)
                project()

            attend(BR_WIN, tiles, [selection_and_projections] + rest)

    def sel_body(i, carry):
        attend(BR_SEL, [(SEL_TILES * i + j, None) for j in range(SEL_TILES)])
        return carry

    lax.fori_loop(0, qi // SEL_TILES, sel_body, 0)
    for n_full in range(SEL_TILES):
        @pl.when(qi % SEL_TILES == n_full)
        def _(n_full=n_full):
            attend(BR_SEL, [(qi - n_full + j, None) for j in range(n_full)] + [(qi, MASK_CAUSAL)], mlstm_chunk(1))

    gates_t = jax.nn.sigmoid(zs_ref[...]).T
    for hp in range(NSA_H // 2):
        comb = []
        for e in range(2):
            h = 2 * hp + e
            c0 = ZS_GATE + NSA_NB * h
            pair = (h // NSA_R) * 2 + e
            cols = slice(((h % NSA_R) // 2) * tq, ((h % NSA_R) // 2 + 1) * tq)
            sel, win = BR_SEL * n_pairs + pair, BR_WIN * n_pairs + pair
            o_sel = acc_ref[sel, 0:NSA_DH, cols] * (1.0 / acc_ref[sel, NSA_DH:NSA_DH + 1, cols])
            o_win = acc_ref[win, 0:NSA_DH, cols] * (1.0 / acc_ref[win, NSA_DH:NSA_DH + 1, cols])
            comb.append(gates_t[c0:c0 + 1] * ocmp_ref[pair, :, cols] + gates_t[c0 + 1:c0 + 2] * o_sel
                        + gates_t[c0 + 2:c0 + 3] * o_win)
        out_ref[:, hp * LANES:(hp + 1) * LANES] = jnp.concatenate(comb, axis=0).T.astype(BF16)


def _overlap_t(S):
    c = np.arange(S // CMP_STRIDE)[None, :]
    j = np.arange(S // SEL_BLOCK)[:, None]
    ov = (c * CMP_STRIDE <= j * SEL_BLOCK + SEL_BLOCK - 1) & (c * CMP_STRIDE + CMP_BLOCK - 1 >= j * SEL_BLOCK)
    ov &= c < (S - CMP_BLOCK) // CMP_STRIDE + 1
    return jnp.asarray(ov.astype(np.float32), dtype=BF16)


def _nsa_mlstm(zm, zs, kaug, vt, kcaug, vct, conv_w, conv_b, gate_b, norm_g, S):
    M = zm.shape[0]
    B = M // S
    tq = min(TQ, S)
    assert tq == TK and WINDOW % TK == 0 and S % TK == 0 and tq == 2 * ML_CHUNK and ML_CHUNK == ML_DH
    nq = S // tq
    n = S // CMP_STRIDE
    nsel = S // SEL_BLOCK
    gb_row = jnp.zeros((1, ZS_W), F32).at[0, ZS_MI:ZS_MI + 2 * ML_H].set(gate_b)

    def tile(off):
        return pl.BlockSpec((tq, ML_W), lambda b, i: (b * nq + i, off // ML_W))

    def row(off):
        return pl.BlockSpec((S, ML_W), lambda b, i: (b, off // ML_W))

    def full(shape):
        return pl.BlockSpec(shape, lambda b, i: (0,) * len(shape))

    return pl.pallas_call(
        _nsa_mlstm_kernel,
        grid=(B, nq),
        in_specs=[tile(ZM_NQ),
                  pl.BlockSpec((1, 8, S, LANES), lambda b, i: (b, 0, 0, 0)),
                  pl.BlockSpec((1, 2, NSA_G, VT_ROWS, S), lambda b, i: (b, 0, 0, 0, 0)),
                  pl.BlockSpec((1, 4, n, LANES), lambda b, i: (b, 0, 0, 0)),
                  pl.BlockSpec((1, LANES, n), lambda b, i: (b, 0, 0)),
                  pl.BlockSpec((tq, ZS_W), lambda b, i: (b * nq + i, 0)),
                  full((nsel, n)),
                  row(ZM_MQ), row(ZM_MK), tile(ZM_MV), tile(ZM_MO),
                  full((CONV_K, 2 * ML_W)), full((1, 2 * ML_W)), full((1, ZS_W)), full((1, ML_W))],
        out_specs=[pl.BlockSpec((tq, NSA_W), lambda b, i: (b * nq + i, 0)),
                   pl.BlockSpec((tq, ML_W), lambda b, i: (b * nq + i, 0))],
        out_shape=[jax.ShapeDtypeStruct((M, NSA_W), BF16), jax.ShapeDtypeStruct((M, ML_W), BF16)],
        scratch_shapes=[pltpu.VMEM((NSA_H // 2, LANES, 2 * tq), BF16),
                        pltpu.VMEM((NSA_H // 2, LANES, 2 * tq), BF16),
                        pltpu.VMEM((NSA_H // 2, NSA_DH, 2 * tq), F32),
                        pltpu.VMEM((NSA_H, 1, 2 * tq), F32),
                        pltpu.VMEM((NSA_H, VT_ROWS, 2 * tq), F32),
                        pltpu.VMEM((ML_H, ML_DH, 2 * ML_DH), F32),
                        pltpu.VMEM((1, ZS_W), F32)],
        compiler_params=_cparams(("arbitrary", "arbitrary")),
        name="nsa_mlstm",
    )(zm, kaug, vt, kcaug, vct, zs, _overlap_t(S), zm, zm, zm, zm,
      conv_w, conv_b.reshape(1, 2 * ML_W), gb_row, norm_g.reshape(1, ML_W))


def _merge_kernel(gu0_ref, gv0_ref, gu_ref, gv_ref, lng_ref, lnb_ref, ws_ref, bst_ref, b_ref, c_ref,
                  ga_ref, gb_ref, gc_ref, x_ref, gt_ref, wa_ref, wb_ref, wc_ref, wo_ref, o_ref, a_ref):
    i = pl.program_id(0)

    @pl.when(i == 0)
    def _():
        _gmlp_rows(gu0_ref, gv0_ref, lng_ref, lnb_ref, ws_ref, bst_ref, a_ref.at[0])

    a = a_ref[i % 2]
    merged = (jax.nn.sigmoid(ga_ref[...].astype(F32)) * _dot(a, wa_ref[...])
              + jax.nn.sigmoid(gb_ref[...].astype(F32)) * _dot(b_ref[...], wb_ref[...])
              + jax.nn.sigmoid(gc_ref[...].astype(F32)) * _dot(c_ref[...], wc_ref[...]))
    o_ref[...] = x_ref[...] + gt_ref[0] * _dot(merged.astype(BF16), wo_ref[...])
    _gmlp_rows(gu_ref, gv_ref, lng_ref, lnb_ref, ws_ref, bst_ref, a_ref.at[(i + 1) % 2])


def _merge(yb, yc, zm, x2, gt, ln_g, ln_b, ws, bs, wa, wb, wc, wo, S):
    M, D = x2.shape
    tm = min(TM_MERGE, S)
    per_b = S // tm

    def rows(w, jcol=0):
        return pl.BlockSpec((tm, w), lambda i: (i, jcol))

    def next_rows(jcol):
        return pl.BlockSpec((tm, GM_W), lambda i: (jnp.minimum(i + 1, M // tm - 1), jcol))

    def full(shape):
        return pl.BlockSpec(shape, lambda i: (0,) * len(shape))

    return pl.pallas_call(
        _merge_kernel,
        grid=(M // tm,),
        in_specs=[rows(GM_W, ZM_GU // GM_W), rows(GM_W, ZM_GV // GM_W),
                  next_rows(ZM_GU // GM_W), next_rows(ZM_GV // GM_W), full((1, GM_W)), full((1, GM_W)),
                  full((GM_GROUPS, GM_CHUNK, GM_CHUNK)), full((GM_CHUNK, GM_GROUPS)),
                  rows(ML_W), rows(NSA_W),
                  rows(D, ZM_GA // D), rows(D, ZM_GBR // D), rows(D, ZM_GC // D),
                  rows(D), pl.BlockSpec((1, 1, D), lambda i: (i // per_b, 0, 0)),
                  full((GM_W, D)), full((ML_W, D)), full((NSA_W, D)), full((D, D))],
        out_specs=rows(D),
        out_shape=jax.ShapeDtypeStruct((M, D), F32),
        scratch_shapes=[pltpu.VMEM((2, tm, GM_W), BF16)],
        compiler_params=_cparams(("arbitrary",)),
        name="gmlp_merge_out",
    )(zm, zm, zm, zm, ln_g.reshape(1, GM_W), ln_b.reshape(1, GM_W), ws, bs.T, yb, yc, zm, zm, zm, x2, gt, wa, wb, wc, wo)


def _mlp_kernel(x_ref, g_ref, sc_ref, sh_ref, gt_ref, w1_ref, w2_ref, gf_ref, o_ref, acc_ref, *, final_norm):
    x = x_ref[...]
    h = _modulated_norm(x, g_ref[...], sc_ref[0], sh_ref[0]).astype(BF16)
    for c in range(D_FF // FF_CHUNK):
        mid = jnp.square(jnp.maximum(_dot(h, w1_ref[:, c * FF_CHUNK:(c + 1) * FF_CHUNK]), 0.0)).astype(BF16)
        upd = _dot(mid, w2_ref[c * FF_CHUNK:(c + 1) * FF_CHUNK, :])
        if c == 0:
            acc_ref[...] = upd
        else:
            acc_ref[...] += upd
    y = x + gt_ref[0] * acc_ref[...]
    if final_norm:
        y = (y * lax.rsqrt(jnp.mean(y * y, axis=-1, keepdims=True) + EPS)) * gf_ref[...]
    o_ref[...] = y


def _mlp(x2, g, sc, sh, gt, w1, w2, g_final, S, final_norm):
    M, D = x2.shape
    tm = min(TM_MLP, S)
    per_b = S // tm
    mod = pl.BlockSpec((1, 1, D), lambda i: (i // per_b, 0, 0))
    return pl.pallas_call(
        functools.partial(_mlp_kernel, final_norm=final_norm),
        grid=(M // tm,),
        in_specs=[pl.BlockSpec((tm, D), lambda i: (i, 0)),
                  pl.BlockSpec((1, D), lambda i: (0, 0)),
                  mod, mod, mod,
                  pl.BlockSpec((D, D_FF), lambda i: (0, 0)),
                  pl.BlockSpec((D_FF, D), lambda i: (0, 0)),
                  pl.BlockSpec((1, D), lambda i: (0, 0))],
        out_specs=pl.BlockSpec((tm, D), lambda i: (i, 0)),
        out_shape=jax.ShapeDtypeStruct((M, D), F32),
        scratch_shapes=[pltpu.VMEM((tm, D), F32)],
        compiler_params=_cparams(("arbitrary",)),
        name="relu2_mlp",
    )(x2, g, sc, sh, gt, w1, w2, g_final)


def _w_cols(w, name):
    return w[:, _OFFS[name]:_OFFS[name] + _SIZE[name]]


def _dup_heads(w):
    parts = []
    for g in range(NSA_G):
        blk = w[:, g * NSA_DH:(g + 1) * NSA_DH]
        parts += [blk, blk]
    return jnp.concatenate(parts, axis=1)


def _proj_weights(w):
    D = w.shape[0]
    wm = jnp.concatenate([_w_cols(w, n) for n in ('ga', 'gbr', 'gc', 'gu', 'gv', 'mq', 'mk', 'mv', 'mo', 'nq')]
                         + [_dup_heads(_w_cols(w, 'nks')), _dup_heads(_w_cols(w, 'nkw')),
                            _w_cols(w, 'nvs'), _w_cols(w, 'nvw')], axis=1)
    wc = jnp.concatenate([_w_cols(w, 'nkc'), _w_cols(w, 'nvc')], axis=1)
    ws = jnp.concatenate([_w_cols(w, 'ngate'), _w_cols(w, 'mi'), _w_cols(w, 'mf'),
                          jnp.zeros((D, ZS_W - NSA_H * NSA_NB - 2 * ML_H), w.dtype)], axis=1)
    return wm.astype(BF16), wc.astype(BF16), ws.astype(BF16)


def kernel(x, c, g_norm1, g_norm2, w_ada, b_ada, w_in, gm_ln_g, gm_ln_b, gm_ws, gm_bs, ml_conv_w, ml_conv_b,
           ml_gate_b, ml_norm_g, nsa_pe_k, nsa_pe_v, nsa_phi_k1, nsa_phi_k2, nsa_phi_v1, nsa_phi_v2,
           w_up_a, w_up_b, w_up_c, w_out, w_mlp1, w_mlp2, g_final):
    B, S, D = x.shape
    depth = w_in.shape[0]
    M = B * S
    mod = _ada(c, w_ada, b_ada)
    x2 = x.reshape(M, D)
    for l in range(depth):
        sh1, sc1, gt1, sh2, sc2, gt2 = [mod[l, :, i * D:(i + 1) * D].reshape(B, 1, D) for i in range(6)]
        wm, wc, ws = _proj_weights(w_in[l])
        zm, zc, zs = _inproj(x2, g_norm1[l].reshape(1, D), sc1, sh1, wm, wc, ws, S)
        kaug, vt, kcaug, vct = _nsa_prep(zc, zm, nsa_pe_k[l], nsa_pe_v[l], nsa_phi_k1[l], nsa_phi_k2[l],
                                         nsa_phi_v1[l], nsa_phi_v2[l], S)
        yc, yb = _nsa_mlstm(zm, zs, kaug, vt, kcaug, vct, ml_conv_w[l], ml_conv_b[l], ml_gate_b[l], ml_norm_g[l], S)
        x2 = _merge(yb, yc, zm, x2, gt1, gm_ln_g[l], gm_ln_b[l], gm_ws[l], gm_bs[l], w_up_a[l].astype(BF16),
                    w_up_b[l].astype(BF16), w_up_c[l].astype(BF16), w_out[l].astype(BF16), S)
        x2 = _mlp(x2, g_norm2[l].reshape(1, D), sc2, sh2, gt2, w_mlp1[l].astype(BF16), w_mlp2[l].astype(BF16),
                  g_final.reshape(1, D), S, final_norm=(l == depth - 1))
    return x2.reshape(B, S, D)
```

```python
import functools

import numpy as np
import jax
import jax.numpy as jnp
from jax import lax
from jax.experimental import pallas as pl
from jax.experimental.pallas import tpu as pltpu

F32 = jnp.float32
BF16 = jnp.bfloat16

D_MODEL = 1024
GM_W = 512
GM_GROUPS = 4
GM_CHUNK = 128
ML_H = 4
ML_DH = 128
ML_W = ML_H * ML_DH
CONV_K = 4
NSA_H = 8
NSA_G = 2
NSA_R = NSA_H // NSA_G
NSA_DH = 64
NSA_W = NSA_H * NSA_DH
NSA_KV = NSA_G * NSA_DH
NSA_NB = 3
CMP_BLOCK = 32
CMP_STRIDE = 16
SEL_BLOCK = 64
TOP_N = 8
WINDOW = 512
D_FF = 4 * D_MODEL
EPS = 1e-6
NEG = -1e30
TAKEN = -3e38
N_FORCED = 3
SPLIT_SIZES = (GM_W, GM_W, ML_W, ML_W, ML_W, ML_W, ML_H, ML_H, NSA_W, NSA_KV, NSA_KV, NSA_KV, NSA_KV,
               NSA_KV, NSA_KV, NSA_H * NSA_NB, D_MODEL, D_MODEL, D_MODEL)
SPLIT_NAMES = ('gu', 'gv', 'mq', 'mk', 'mv', 'mo', 'mi', 'mf', 'nq', 'nkc', 'nvc', 'nks', 'nvs', 'nkw', 'nvw',
               'ngate', 'ga', 'gbr', 'gc')
_OFFS = dict(zip(SPLIT_NAMES, np.concatenate([[0], np.cumsum(SPLIT_SIZES)[:-1]]).tolist()))
_SIZE = dict(zip(SPLIT_NAMES, SPLIT_SIZES))

LANES = 128
VMEM_LIMIT = 56 * 1024 * 1024

ZM_GA, ZM_GBR, ZM_GC = 0, 1024, 2048
ZM_GU, ZM_GV = 3072, 3584
ZM_MQ, ZM_MK, ZM_MV, ZM_MO = 4096, 4608, 5120, 5632
ZM_NQ = 6144
ZM_KK = 6656
ZM_VV = 7168
ZM_W = 7424
KK_W = 4 * LANES
VV_W = 2 * LANES
BR_SEL, BR_WIN = 0, 1
MASK_CAUSAL, MASK_WINDOW_EDGE = 0, 1
FEAT_SEL = 0
FEAT_TERMS = 3
FEAT_HI = 32
FEAT_LO = 35
LOG2E = 1.4426950408889634
ZS_GATE = 0
ZS_MI = 24
ZS_MF = 28
ZS_W = 128

TM_PROJ = 1024
TN_PROJ = 3712
ML_CHUNK = 128
CONV_HALO = 16
TQ = 256
TK = 256
SCORE_LOOKAHEAD = 3
SEL_TILES = 4
VT_ROWS = NSA_DH + 16
TM_MERGE = 512
TM_MLP = 512
FF_CHUNK = 1024


def _dot(a, b):
    return jnp.dot(a, b, preferred_element_type=F32)


def _split3(x):
    x1 = x.astype(BF16)
    r1 = x - x1.astype(F32)
    x2 = r1.astype(BF16)
    x3 = (r1 - x2.astype(F32)).astype(BF16)
    return x1, x2, x3


def _cparams(sem):
    return pltpu.CompilerParams(dimension_semantics=sem, vmem_limit_bytes=VMEM_LIMIT)


def _ada_kernel(c_ref, w_ref, b_ref, o_ref):
    c = c_ref[...]
    cond = c * jax.nn.sigmoid(c)
    c1, c2, c3 = _split3(cond)
    w1, w2, w3 = _split3(w_ref[0])
    acc = _dot(c1, w1) + (_dot(c1, w2) + _dot(c2, w1)) + (_dot(c1, w3) + _dot(c2, w2) + _dot(c3, w1))
    o_ref[0] = acc + b_ref[0]


def _ada(c, w_ada, b_ada):
    L, D, N = w_ada.shape
    B = c.shape[0]
    tn = 1536
    return pl.pallas_call(
        _ada_kernel,
        grid=(L, N // tn),
        in_specs=[pl.BlockSpec((B, D), lambda l, j: (0, 0)),
                  pl.BlockSpec((1, D, tn), lambda l, j: (l, 0, j)),
                  pl.BlockSpec((1, 1, tn), lambda l, j: (l, 0, j))],
        out_specs=pl.BlockSpec((1, B, tn), lambda l, j: (l, 0, j)),
        out_shape=jax.ShapeDtypeStruct((L, B, N), F32),
        compiler_params=_cparams(("arbitrary", "arbitrary")),
        name="ada_mod",
    )(c, w_ada, b_ada.reshape(L, 1, N))


def _modulated_norm(x, g, sc, sh):
    y = x * lax.rsqrt(jnp.mean(x * x, axis=-1, keepdims=True) + EPS)
    return (y * g) * (1.0 + sc) + sh


def _inproj_kernel(x_ref, g_ref, sc_ref, sh_ref, wm_ref, wc_ref, ws_ref, zm_ref, zc_ref, zs_ref, h_ref):
    @pl.when(pl.program_id(1) == 0)
    def _():
        h = _modulated_norm(x_ref[...], g_ref[...], sc_ref[0], sh_ref[0]).astype(BF16)
        h_ref[...] = h
        zc = _dot(h, wc_ref[...])
        for part in range(zc_ref.shape[0]):
            zc_ref[part] = zc[:, part * LANES:(part + 1) * LANES]
        zs_ref[...] = _dot(h, ws_ref[...])

    zm_ref[...] = _dot(h_ref[...], wm_ref[...]).astype(BF16)


def _inproj(x2, g, sc, sh, wm, wc, ws, S):
    M, D = x2.shape
    tm, tn = min(TM_PROJ, S), TN_PROJ
    per_b = S // tm
    return pl.pallas_call(
        _inproj_kernel,
        grid=(M // tm, ZM_W // tn),
        in_specs=[pl.BlockSpec((tm, D), lambda i, j: (i, 0)),
                  pl.BlockSpec((1, D), lambda i, j: (0, 0)),
                  pl.BlockSpec((1, 1, D), lambda i, j: (i // per_b, 0, 0)),
                  pl.BlockSpec((1, 1, D), lambda i, j: (i // per_b, 0, 0)),
                  pl.BlockSpec((D, tn), lambda i, j: (0, j)),
                  pl.BlockSpec((D, 2 * NSA_KV), lambda i, j: (0, 0)),
                  pl.BlockSpec((D, ZS_W), lambda i, j: (0, 0))],
        out_specs=[pl.BlockSpec((tm, tn), lambda i, j: (i, j)),
                   pl.BlockSpec((2 * NSA_KV // LANES, tm, LANES), lambda i, j: (0, i, 0)),
                   pl.BlockSpec((tm, ZS_W), lambda i, j: (i, 0))],
        out_shape=[jax.ShapeDtypeStruct((M, ZM_W), BF16),
                   jax.ShapeDtypeStruct((2 * NSA_KV // LANES, M, LANES), F32),
                   jax.ShapeDtypeStruct((M, ZS_W), F32)],
        scratch_shapes=[pltpu.VMEM((tm, D), BF16)],
        compiler_params=_cparams(("arbitrary", "arbitrary")),
        name="in_proj",
    )(x2, g, sc, sh, wm, wc, ws)


def _gmlp_rows(u_ref, v_ref, lng_ref, lnb_ref, ws_ref, bst_ref, o_ref):
    ts = u_ref.shape[0]
    dg = GM_W // GM_GROUPS
    row = lax.broadcasted_iota(jnp.int32, (GM_CHUNK, GM_CHUNK), 0)
    col = lax.broadcasted_iota(jnp.int32, (GM_CHUNK, GM_CHUNK), 1)
    ws = [jnp.where(row >= col, ws_ref[g], 0.0).astype(BF16) for g in range(GM_GROUPS)]
    lng = lng_ref[...]
    lnb = lnb_ref[...]
    for c in range(ts // GM_CHUNK):
        r0 = c * GM_CHUNK
        u = jax.nn.gelu(u_ref[r0:r0 + GM_CHUNK, :].astype(F32))
        v = jax.nn.gelu(v_ref[r0:r0 + GM_CHUNK, :].astype(F32))
        mu = jnp.mean(v, axis=-1, keepdims=True)
        var = jnp.mean(jnp.square(v - mu), axis=-1, keepdims=True)
        vb = ((v - mu) * lax.rsqrt(var + EPS) * lng + lnb).astype(BF16)
        for g in range(GM_GROUPS):
            mixed = _dot(ws[g], vb[:, g * dg:(g + 1) * dg]) + bst_ref[:, g:g + 1]
            o_ref[r0:r0 + GM_CHUNK, g * dg:(g + 1) * dg] = (u[:, g * dg:(g + 1) * dg] * mixed).astype(BF16)


def _log_sigmoid(x):
    return jnp.minimum(x, 0.0) - jnp.log1p(jnp.exp(-jnp.abs(x)))


def _conv_silu(x_ext, w, b):
    n = x_ext.shape[0] - CONV_HALO
    y = b
    for j in range(CONV_K):
        sh = CONV_K - 1 - j
        xs = x_ext if sh == 0 else pltpu.roll(x_ext, sh, axis=0)
        y = y + xs[CONV_HALO:CONV_HALO + n] * w[j:j + 1]
    return y * jax.nn.sigmoid(y)


def _mlstm_chunk_stages(q_ref, k_ref, v_ref, o_ref, zs_ref, cw_ref, cb_ref, gb_ref, ng_ref, out_ref, c_ref, m_ref,
                        row0, t_abs, has_history):
    L = ML_CHUNK
    rows = slice(row0, row0 + L)
    hs = [slice(h * ML_DH, (h + 1) * ML_DH) for h in range(ML_H)]
    heads = range(ML_H)
    st = {}

    def conv_rows(src, c0):
        cur = src[pl.ds(pl.multiple_of(t_abs, L), L), :].astype(F32)
        halo0 = pl.multiple_of(jnp.maximum(t_abs - CONV_HALO, 0), CONV_HALO)
        halo = jnp.where(has_history, src[pl.ds(halo0, CONV_HALO), :].astype(F32), 0.0)
        return _conv_silu(jnp.concatenate([halo, cur], axis=0), cw_ref[:, c0:c0 + ML_W], cb_ref[:, c0:c0 + ML_W])

    def projections():
        row = lax.broadcasted_iota(jnp.int32, (L, L), 0)
        col = lax.broadcasted_iota(jnp.int32, (L, L), 1)
        tril_b = jnp.where(row >= col, 1.0, 0.0).astype(BF16)
        triu_b = jnp.where(row <= col, 1.0, 0.0).astype(BF16)
        gi = zs_ref[rows, :] + gb_ref[...]
        gi_t = gi.T
        c1, c2, c3 = _split3(_log_sigmoid(gi) * LOG2E)
        b_cols = _dot(tril_b, c1) + _dot(tril_b, c2) + _dot(tril_b, c3)
        r1, r2, r3 = _split3(_log_sigmoid(gi_t[ZS_MI:ZS_MI + 2 * ML_H]) * LOG2E)
        b_rows = _dot(r1, triu_b) + _dot(r2, triu_b) + _dot(r3, triu_b)
        st['gi'] = gi * LOG2E
        st['r_rows'] = gi_t[ZS_MI:ZS_MI + ML_H] * LOG2E - b_rows[ML_H:2 * ML_H]
        st['b_at_i'] = pltpu.roll(b_cols, ZS_W - (ZS_MF - ZS_MI), axis=1)
        q_all = conv_rows(q_ref, 0).astype(BF16)
        k_all = conv_rows(k_ref, ML_W) * (ML_DH ** -0.5)
        st['q'] = [q_all[:, hs[h]] for h in heads]
        st['k_t'] = [k_all[:, hs[h]].T for h in heads]
        st['cmat'] = [c_ref[h] for h in heads]
        st['qk'] = [_dot(st['q'][h], st['k_t'][h].astype(BF16)) for h in heads]
        st['qc'] = [_dot(st['q'][h], st['cmat'][h].astype(BF16)) for h in heads]
        ones_blk = jnp.ones((L, ML_DH), BF16)
        st['v_aug'] = [jnp.concatenate([v_ref[rows, hs[h]], ones_blk], axis=1) for h in heads]

    def memory_update():
        tril = lax.broadcasted_iota(jnp.int32, (L, L), 0) >= lax.broadcasted_iota(jnp.int32, (L, L), 1)
        time_row = lax.broadcasted_iota(jnp.int32, (L, ZS_W), 0)
        m_row = m_ref[...]
        u = st['gi'] - st['b_at_i']
        shift = 1
        while shift < L:
            u = jnp.maximum(u, jnp.where(time_row >= shift, pltpu.roll(u, shift, axis=0), NEG))
            shift *= 2
        u = jnp.maximum(u, m_row)
        m_cols = st['b_at_i'] + u
        u_last = u[L - 1:L, :]
        w_prev_row = jnp.exp2(m_row - u_last)
        st['m_cols'], st['m_prev'], st['u_rep'], st['gate'] = m_cols, m_row, [], []
        for h in heads:
            lane_h = ZS_MI + h
            r_row = st['r_rows'][h:h + 1, :]
            u_rep = jnp.broadcast_to(u[:, lane_h:lane_h + 1], (L, ML_DH))
            st['u_rep'].append(u_rep)
            st['gate'].append(jnp.exp2(jnp.where(tril, r_row - u_rep, NEG)))
            w_s = jnp.exp2(r_row - u_last[:, lane_h:lane_h + 1])
            k_w = (st['k_t'][h] * w_s).astype(BF16)
            c_ref[h] = w_prev_row[:, lane_h:lane_h + 1] * st['cmat'][h] + _dot(k_w, st['v_aug'][h])
        m_ref[...] = m_cols[L - 1:L, :]

    def read_out():
        st['num_aug'] = []
        for h in heads:
            lane_h = ZS_MI + h
            w_inter = jnp.exp2(st['m_prev'][:, lane_h:lane_h + 1] - st['u_rep'][h])
            st['num_aug'].append(_dot((st['qk'][h] * st['gate'][h]).astype(BF16), st['v_aug'][h])
                                 + jnp.concatenate([w_inter, w_inter], axis=1) * st['qc'][h])

    def normalise_and_store():
        for h in heads:
            lane_h = ZS_MI + h
            num_aug = st['num_aug'][h]
            m_rep = jnp.broadcast_to(st['m_cols'][:, lane_h:lane_h + 1], (L, ML_DH))
            hval = num_aug[:, :ML_DH] / jnp.maximum(jnp.abs(num_aug[:, ML_DH:]), jnp.exp2(-m_rep))
            mu = jnp.mean(hval, axis=-1, keepdims=True)
            var = jnp.mean(jnp.square(hval - mu), axis=-1, keepdims=True)
            hn = (hval - mu) * lax.rsqrt(var + EPS) * ng_ref[:, hs[h]]
            og = jax.nn.sigmoid(o_ref[rows, hs[h]].astype(F32))
            out_ref[rows, hs[h]] = (og * hn).astype(BF16)

    return [projections, memory_update, read_out, normalise_and_store]


def _nsa_prep_kernel(x_ref, pe_ref, w1_ref, w2_ref, kk_ref, vv_ref, feat_ref, featc_ref,
                     kaug_ref, vt_ref, kcaug_ref, vct_ref):
    S = kk_ref.shape[0]
    n = S // CMP_STRIDE
    cw = 2 * NSA_KV
    half = CMP_BLOCK // 2
    acc_a = jnp.zeros((n, cw), F32)
    acc_b = jnp.zeros((n, cw), F32)
    for j in range(half):
        xj = jnp.concatenate([x_ref[part, pl.ds(j, n, stride=CMP_STRIDE), :] for part in range(cw // LANES)],
                             axis=1)
        acc_a = acc_a + _dot((xj + pe_ref[j:j + 1, :]).astype(BF16), w1_ref[j])
        acc_b = acc_b + _dot((xj + pe_ref[half + j:half + j + 1, :]).astype(BF16), w1_ref[half + j])
    pre = acc_a + pltpu.roll(acc_b, n - 1, axis=0)
    cmp = _dot(jax.nn.gelu(pre).astype(BF16), w2_ref[...])

    lane = lax.broadcasted_iota(jnp.int32, (1, LANES), 1)
    keep = [jnp.where(lane < NSA_DH, 1.0, 0.0).astype(BF16), jnp.where(lane >= NSA_DH, 1.0, 0.0).astype(BF16)]
    for g in range(NSA_G):
        kd = cmp[:, g * LANES:(g + 1) * LANES].astype(BF16)
        for e in range(2):
            kcaug_ref[0, g * 2 + e] = kd * keep[e] + featc_ref[e]
    vct_ref[0] = cmp[:, NSA_G * LANES:(NSA_G + 1) * LANES].T.astype(BF16)
    tb = min(TK, S)
    for br in range(2):
        for g in range(NSA_G):
            kd = kk_ref[:, (br * NSA_G + g) * LANES:(br * NSA_G + g + 1) * LANES]
            for e in range(2):
                kaug_ref[0, (br * NSA_G + g) * 2 + e] = kd * keep[e] + feat_ref[br * 2 + e]
        for c in range(S // tb):
            v_t = vv_ref[c * tb:(c + 1) * tb, br * LANES:(br + 1) * LANES].astype(F32).T.astype(BF16)
            for g in range(NSA_G):
                vt_ref[0, br, g, 0:NSA_DH, c * tb:(c + 1) * tb] = v_t[g * NSA_DH:(g + 1) * NSA_DH]
                vt_ref[0, br, g, NSA_DH:VT_ROWS, c * tb:(c + 1) * tb] = jnp.ones((VT_ROWS - NSA_DH, tb), BF16)


def _blockdiag(blocks):
    n = len(blocks)
    rows = []
    for i, blk in enumerate(blocks):
        rows.append(jnp.concatenate([blk if j == i else jnp.zeros((blk.shape[0], blocks[j].shape[1]), blk.dtype)
                                     for j in range(n)], axis=1))
    return jnp.concatenate(rows, axis=0)


def _key_features(S):
    pos = np.arange(S)
    f = np.zeros((4, S, LANES), np.float32)
    for e in range(2):
        base = NSA_DH * (1 - e)
        f[e, pos, base + FEAT_SEL + pos // SEL_BLOCK] = 1.0
        for br in range(2):
            f[br * 2 + e, :, base + FEAT_HI:base + FEAT_HI + FEAT_TERMS] = (pos // 64)[:, None]
            f[br * 2 + e, :, base + FEAT_LO:base + FEAT_LO + FEAT_TERMS] = (pos % 64)[:, None]
    return jnp.asarray(f, dtype=BF16)


def _cmp_features(S):
    c = np.arange(S // CMP_STRIDE)
    hi = (c * CMP_STRIDE) // 64
    lo = c * CMP_STRIDE + (CMP_BLOCK - 1) * 0.5 - 64 * hi
    f = np.zeros((2, c.size, LANES), np.float32)
    for e in range(2):
        base = NSA_DH * (1 - e)
        f[e, :, base + FEAT_HI:base + FEAT_HI + FEAT_TERMS] = hi[:, None]
        f[e, :, base + FEAT_LO:base + FEAT_LO + FEAT_TERMS] = lo[:, None]
    return jnp.asarray(f, dtype=BF16)


def _bf16_terms(x):
    terms = []
    for _ in range(FEAT_TERMS):
        terms.append(float(np.asarray(x - sum(terms), dtype=BF16)))
    return terms


def _nsa_prep(zc, zm, pe_k, pe_v, phi_k1, phi_k2, phi_v1, phi_v2, S):
    M = zc.shape[1]
    B = M // S
    n = S // CMP_STRIDE
    cw = 2 * NSA_KV
    assert S // SEL_BLOCK <= FEAT_HI and S <= 64 * 64
    pe = jnp.concatenate([pe_k, pe_k, pe_v, pe_v], axis=1)
    k1 = phi_k1.reshape(CMP_BLOCK, NSA_DH, NSA_DH)
    v1 = phi_v1.reshape(CMP_BLOCK, NSA_DH, NSA_DH)
    blocks = jnp.stack([k1, k1, v1, v1], axis=1)
    w1 = jnp.einsum('jaxy,ab->jaxby', blocks, jnp.eye(4, dtype=F32)).reshape(CMP_BLOCK, cw, cw).astype(BF16)
    k2 = jnp.concatenate([phi_k2, phi_k2], axis=1)
    w2 = _blockdiag([k2, k2, phi_v2, phi_v2]).astype(BF16)

    def full(shape):
        return pl.BlockSpec(shape, lambda b: (0,) * len(shape))

    return pl.pallas_call(
        _nsa_prep_kernel,
        grid=(B,),
        in_specs=[pl.BlockSpec((cw // LANES, S, LANES), lambda b: (0, b, 0)),
                  full((CMP_BLOCK, cw)), full((CMP_BLOCK, cw, cw)), full((cw, 3 * LANES)),
                  pl.BlockSpec((S, KK_W), lambda b: (b, ZM_KK // KK_W)),
                  pl.BlockSpec((S, VV_W), lambda b: (b, ZM_VV // VV_W)),
                  full((4, S, LANES)), full((2, n, LANES))],
        out_specs=[pl.BlockSpec((1, 8, S, LANES), lambda b: (b, 0, 0, 0)),
                   pl.BlockSpec((1, 2, NSA_G, VT_ROWS, S), lambda b: (b, 0, 0, 0, 0)),
                   pl.BlockSpec((1, 4, n, LANES), lambda b: (b, 0, 0, 0)),
                   pl.BlockSpec((1, LANES, n), lambda b: (b, 0, 0))],
        out_shape=[jax.ShapeDtypeStruct((B, 8, S, LANES), BF16),
                   jax.ShapeDtypeStruct((B, 2, NSA_G, VT_ROWS, S), BF16),
                   jax.ShapeDtypeStruct((B, 4, n, LANES), BF16),
                   jax.ShapeDtypeStruct((B, LANES, n), BF16)],
        compiler_params=_cparams(("arbitrary",)),
        name="nsa_prep",
    )(zc, pe, w1, w2, zm, zm, _key_features(S), _cmp_features(S))


def _nsa_mlstm_kernel(q_ref, kaug_ref, vt_ref, kcaug_ref, vct_ref, zs_ref, ovt_ref,
                      mq_ref, mk_ref, mv_ref, mo_ref, cw_ref, cb_ref, gb_ref, ng_ref, out_ref, yb_ref,
                      qat_ref, qaw_ref, ocmp_ref, m_ref, acc_ref, mc_ref, mm_ref):
    S = kaug_ref.shape[2]
    tq = q_ref.shape[0]
    ncmp = kcaug_ref.shape[2]
    nsel = S // SEL_BLOCK
    n_pairs = NSA_H // 2
    qi = pl.program_id(1)
    t0 = pl.multiple_of(qi * tq, tq)
    t_row = t0 + lax.broadcasted_iota(jnp.int32, (1, tq), 1)

    @pl.when(qi == 0)
    def _():
        mc_ref[...] = jnp.zeros_like(mc_ref)
        mm_ref[...] = jnp.zeros_like(mm_ref)

    def mlstm_chunk(c):
        return _mlstm_chunk_stages(mq_ref, mk_ref, mv_ref, mo_ref, zs_ref, cw_ref, cb_ref, gb_ref, ng_ref, yb_ref,
                                   mc_ref, mm_ref, row0=c * ML_CHUNK, t_abs=t0 + c * ML_CHUNK,
                                   has_history=(qi > 0) if c == 0 else True)

    for k in range(NSA_H // 2):
        q_t = (q_ref[:, k * LANES:(k + 1) * LANES].astype(F32) * (NSA_DH ** -0.5 * LOG2E)).T.astype(BF16)
        for e in range(2):
            for dst in (qat_ref, qaw_ref):
                dst[(k // 2) * 2 + e, e * NSA_DH:(e + 1) * NSA_DH, (k % 2) * tq:(k % 2 + 1) * tq] = (
                    q_t[e * NSA_DH:(e + 1) * NSA_DH])
    feat_row = lax.broadcasted_iota(jnp.int32, (NSA_DH, 2 * tq), 0)
    second_head = lax.broadcasted_iota(jnp.int32, (NSA_DH, 2 * tq), 1) >= tq
    for pair in range(n_pairs):
        g, e = pair // 2, pair % 2
        terms = [_bf16_terms(LOG2E * 2.0 ** (-8.0 * (g * NSA_R + 2 * s + e + 1.0) / NSA_H)) for s in range(2)]
        feat = jnp.zeros((NSA_DH, 2 * tq), F32)
        for i in range(FEAT_TERMS):
            term = jnp.where(second_head, terms[1][i], terms[0][i])
            feat = jnp.where(feat_row == FEAT_HI + i, term * 64.0, jnp.where(feat_row == FEAT_LO + i, term, feat))
        qat_ref[pair, NSA_DH * (1 - e):NSA_DH * (2 - e), :] = feat.astype(BF16)
        qaw_ref[pair, NSA_DH * (1 - e):NSA_DH * (2 - e), :] = feat.astype(BF16)

    ki = lax.broadcasted_iota(jnp.int32, (TK, 2 * tq), 0)
    qu = lax.broadcasted_iota(jnp.int32, (TK, 2 * tq), 1)
    qu = jnp.where(qu >= tq, qu - tq, qu)
    tile_valid = {MASK_CAUSAL: ki <= qu,
                  MASK_WINDOW_EDGE: ki > qu}

    c_col = lax.broadcasted_iota(jnp.int32, (ncmp, 1), 0)
    t_row2 = jnp.concatenate([t_row, t_row], axis=1)
    valid_c = c_col * CMP_STRIDE + (CMP_BLOCK - 1) <= t_row2
    j_col = lax.broadcasted_iota(jnp.int32, (nsel, 1), 0)
    j_f = j_col.astype(F32)
    jt = t_row >> 6
    forced = jnp.logical_or(j_col == 0, jnp.logical_or(j_col == jt, j_col == jt - 1))
    future = j_col > jt

    m_ref[...] = jnp.full(m_ref.shape, NEG, F32)
    acc_ref[...] = jnp.zeros(acc_ref.shape, F32)

    def compressed_scores():
        return [_dot(kcaug_ref[0, pair], qaw_ref[pair]) for pair in range(n_pairs)]

    def compressed_branch_and_selection(scores_c):
        for g in range(NSA_G):
            psum = jnp.zeros((ncmp, tq), F32)
            for e in range(2):
                pair = g * 2 + e
                sc = jnp.where(valid_c, scores_c[pair], NEG)
                ex = jnp.where(valid_c, jnp.exp2(sc - jnp.max(sc, axis=0, keepdims=True)), 0.0)
                den = jnp.sum(ex, axis=0, keepdims=True)
                p = ex * (1.0 / jnp.where(den > 0.0, den, 1.0))
                psum = psum + p[:, :tq] + p[:, tq:]
                ocmp_ref[pair] = _dot(vct_ref[0, g * NSA_DH:(g + 1) * NSA_DH, :], p.astype(BF16))

            p_hi = psum.astype(BF16)
            p_lo = (psum - p_hi.astype(F32)).astype(BF16)
            imp = _dot(ovt_ref[...], p_hi) + _dot(ovt_ref[...], p_lo)
            penalty = jnp.where(forced, 0.0, NEG)
            val = jnp.where(future, NEG, jnp.where(forced, TAKEN, imp))
            for _ in range(min(TOP_N, nsel) - N_FORCED):
                best = jnp.max(val, axis=0, keepdims=True)
                first = jnp.min(jnp.where(val == best, j_f, float(nsel)), axis=0, keepdims=True)
                hit = j_f == first
                penalty = jnp.where(hit, 0.0, penalty)
                val = jnp.where(hit, TAKEN, val)
            penalty = penalty.astype(BF16)
            for e in range(2):
                first = NSA_DH * (1 - e) + FEAT_SEL
                qat_ref[g * 2 + e, first:first + nsel, :] = jnp.concatenate([penalty, penalty], axis=1)

    def attend(br, tiles, interleave=()):
        stages = [(pl.multiple_of(kb * TK, TK), mask, pair) for kb, mask in tiles for pair in range(n_pairs)]

        def scores(stage):
            k0, _, pair = stage
            queries = qat_ref if br == BR_SEL else qaw_ref
            return _dot(kaug_ref[0, br * n_pairs + pair, pl.ds(k0, TK), :], queries[pair])

        s_queue = [scores(st) for st in stages[:SCORE_LOOKAHEAD]]
        pending = list(interleave)
        if pending:
            pending.pop(0)()
        for i, (k0, mask, pair) in enumerate(stages):
            s_t = s_queue.pop(0)
            if i + SCORE_LOOKAHEAD < len(stages):
                s_queue.append(scores(stages[i + SCORE_LOOKAHEAD]))
            slot = br * n_pairs + pair
            if mask is not None:
                s_t = jnp.where(tile_valid[mask], s_t, NEG)
            m_old = m_ref[slot]
            m_new = jnp.maximum(m_old, jnp.max(s_t, axis=0, keepdims=True))
            alpha = jnp.exp2(m_old - m_new)
            p = jnp.exp2(s_t - m_new).astype(BF16)
            acc_ref[slot] = alpha * acc_ref[slot] + _dot(vt_ref[0, br, pair // 2, :, pl.ds(k0, TK)], p)
            m_ref[slot] = m_new
            if pending and i % 2 == 1:
                pending.pop(0)()
        for emit in pending:
            emit()

    n_back = WINDOW // TK
    for n_behind in range(n_back + 1):
        tiles = [(qi, MASK_CAUSAL)] + [(qi - d, MASK_WINDOW_EDGE if d == n_back else None)
                                        for d in range(1, n_behind + 1)]

        @pl.when(qi >= n_back if n_behind == n_back else qi == n_behind)
        def _(tiles=tiles):
            scores_c = compressed_scores()
            project, *rest = mlstm_chunk(0)

            def selection_and_projections():
                compressed_branch_and_selection(scores_c)
                project()

            attend(BR_WIN, tiles, [selection_and_projections] + rest)

    def sel_body(i, carry):
        attend(BR_SEL, [(SEL_TILES * i + j, None) for j in range(SEL_TILES)])
        return carry

    lax.fori_loop(0, qi // SEL_TILES, sel_body, 0)
    for n_full in range(SEL_TILES):
        @pl.when(qi % SEL_TILES == n_full)
        def _(n_full=n_full):
            attend(BR_SEL, [(qi - n_full + j, None) for j in range(n_full)] + [(qi, MASK_CAUSAL)], mlstm_chunk(1))

    gates_t = jax.nn.sigmoid(zs_ref[...]).T
    for hp in range(NSA_H // 2):
        comb = []
        for e in range(2):
            h = 2 * hp + e
            c0 = ZS_GATE + NSA_NB * h
            pair = (h // NSA_R) * 2 + e
            cols = slice(((h % NSA_R) // 2) * tq, ((h % NSA_R) // 2 + 1) * tq)
            sel, win = BR_SEL * n_pairs + pair, BR_WIN * n_pairs + pair
            o_sel = acc_ref[sel, 0:NSA_DH, cols] * (1.0 / acc_ref[sel, NSA_DH:NSA_DH + 1, cols])
            o_win = acc_ref[win, 0:NSA_DH, cols] * (1.0 / acc_ref[win, NSA_DH:NSA_DH + 1, cols])
            comb.append(gates_t[c0:c0 + 1] * ocmp_ref[pair, :, cols] + gates_t[c0 + 1:c0 + 2] * o_sel
                        + gates_t[c0 + 2:c0 + 3] * o_win)
        out_ref[:, hp * LANES:(hp + 1) * LANES] = jnp.concatenate(comb, axis=0).T.astype(BF16)


def _overlap_t(S):
    c = np.arange(S // CMP_STRIDE)[None, :]
    j = np.arange(S // SEL_BLOCK)[:, None]
    ov = (c * CMP_STRIDE <= j * SEL_BLOCK + SEL_BLOCK - 1) & (c * CMP_STRIDE + CMP_BLOCK - 1 >= j * SEL_BLOCK)
    ov &= c < (S - CMP_BLOCK) // CMP_STRIDE + 1
    return jnp.asarray(ov.astype(np.float32), dtype=BF16)


def _nsa_mlstm(zm, zs, kaug, vt, kcaug, vct, conv_w, conv_b, gate_b, norm_g, S):
    M = zm.shape[0]
    B = M // S
    tq = min(TQ, S)
    assert tq == TK and WINDOW % TK == 0 and S % TK == 0 and tq == 2 * ML_CHUNK and ML_CHUNK == ML_DH
    nq = S // tq
    n = S // CMP_STRIDE
    nsel = S // SEL_BLOCK
    gb_row = jnp.zeros((1, ZS_W), F32).at[0, ZS_MI:ZS_MI + 2 * ML_H].set(gate_b)

    def tile(off):
        return pl.BlockSpec((tq, ML_W), lambda b, i: (b * nq + i, off // ML_W))

    def row(off):
        return pl.BlockSpec((S, ML_W), lambda b, i: (b, off // ML_W))

    def full(shape):
        return pl.BlockSpec(shape, lambda b, i: (0,) * len(shape))

    return pl.pallas_call(
        _nsa_mlstm_kernel,
        grid=(B, nq),
        in_specs=[tile(ZM_NQ),
                  pl.BlockSpec((1, 8, S, LANES), lambda b, i: (b, 0, 0, 0)),
                  pl.BlockSpec((1, 2, NSA_G, VT_ROWS, S), lambda b, i: (b, 0, 0, 0, 0)),
                  pl.BlockSpec((1, 4, n, LANES), lambda b, i: (b, 0, 0, 0)),
                  pl.BlockSpec((1, LANES, n), lambda b, i: (b, 0, 0)),
                  pl.BlockSpec((tq, ZS_W), lambda b, i: (b * nq + i, 0)),
                  full((nsel, n)),
                  row(ZM_MQ), row(ZM_MK), tile(ZM_MV), tile(ZM_MO),
                  full((CONV_K, 2 * ML_W)), full((1, 2 * ML_W)), full((1, ZS_W)), full((1, ML_W))],
        out_specs=[pl.BlockSpec((tq, NSA_W), lambda b, i: (b * nq + i, 0)),
                   pl.BlockSpec((tq, ML_W), lambda b, i: (b * nq + i, 0))],
        out_shape=[jax.ShapeDtypeStruct((M, NSA_W), BF16), jax.ShapeDtypeStruct((M, ML_W), BF16)],
        scratch_shapes=[pltpu.VMEM((NSA_H // 2, LANES, 2 * tq), BF16),
                        pltpu.VMEM((NSA_H // 2, LANES, 2 * tq), BF16),
                        pltpu.VMEM((NSA_H // 2, NSA_DH, 2 * tq), F32),
                        pltpu.VMEM((NSA_H, 1, 2 * tq), F32),
                        pltpu.VMEM((NSA_H, VT_ROWS, 2 * tq), F32),
                        pltpu.VMEM((ML_H, ML_DH, 2 * ML_DH), F32),
                        pltpu.VMEM((1, ZS_W), F32)],
        compiler_params=_cparams(("arbitrary", "arbitrary")),
        name="nsa_mlstm",
    )(zm, kaug, vt, kcaug, vct, zs, _overlap_t(S), zm, zm, zm, zm,
      conv_w, conv_b.reshape(1, 2 * ML_W), gb_row, norm_g.reshape(1, ML_W))


def _merge_kernel(gu0_ref, gv0_ref, gu_ref, gv_ref, lng_ref, lnb_ref, ws_ref, bst_ref, b_ref, c_ref,
                  ga_ref, gb_ref, gc_ref, x_ref, gt_ref, wa_ref, wb_ref, wc_ref, wo_ref, o_ref, a_ref):
    i = pl.program_id(0)

    @pl.when(i == 0)
    def _():
        _gmlp_rows(gu0_ref, gv0_ref, lng_ref, lnb_ref, ws_ref, bst_ref, a_ref.at[0])

    a = a_ref[i % 2]
    merged = (jax.nn.sigmoid(ga_ref[...].astype(F32)) * _dot(a, wa_ref[...])
              + jax.nn.sigmoid(gb_ref[...].astype(F32)) * _dot(b_ref[...], wb_ref[...])
              + jax.nn.sigmoid(gc_ref[...].astype(F32)) * _dot(c_ref[...], wc_ref[...]))
    o_ref[...] = x_ref[...] + gt_ref[0] * _dot(merged.astype(BF16), wo_ref[...])
    _gmlp_rows(gu_ref, gv_ref, lng_ref, lnb_ref, ws_ref, bst_ref, a_ref.at[(i + 1) % 2])


def _merge(yb, yc, zm, x2, gt, ln_g, ln_b, ws, bs, wa, wb, wc, wo, S):
    M, D = x2.shape
    tm = min(TM_MERGE, S)
    per_b = S // tm

    def rows(w, jcol=0):
        return pl.BlockSpec((tm, w), lambda i: (i, jcol))

    def next_rows(jcol):
        return pl.BlockSpec((tm, GM_W), lambda i: (jnp.minimum(i + 1, M // tm - 1), jcol))

    def full(shape):
        return pl.BlockSpec(shape, lambda i: (0,) * len(shape))

    return pl.pallas_call(
        _merge_kernel,
        grid=(M // tm,),
        in_specs=[rows(GM_W, ZM_GU // GM_W), rows(GM_W, ZM_GV // GM_W),
                  next_rows(ZM_GU // GM_W), next_rows(ZM_GV // GM_W), full((1, GM_W)), full((1, GM_W)),
                  full((GM_GROUPS, GM_CHUNK, GM_CHUNK)), full((GM_CHUNK, GM_GROUPS)),
                  rows(ML_W), rows(NSA_W),
                  rows(D, ZM_GA // D), rows(D, ZM_GBR // D), rows(D, ZM_GC // D),
                  rows(D), pl.BlockSpec((1, 1, D), lambda i: (i // per_b, 0, 0)),
                  full((GM_W, D)), full((ML_W, D)), full((NSA_W, D)), full((D, D))],
        out_specs=rows(D),
        out_shape=jax.ShapeDtypeStruct((M, D), F32),
        scratch_shapes=[pltpu.VMEM((2, tm, GM_W), BF16)],
        compiler_params=_cparams(("arbitrary",)),
        name="gmlp_merge_out",
    )(zm, zm, zm, zm, ln_g.reshape(1, GM_W), ln_b.reshape(1, GM_W), ws, bs.T, yb, yc, zm, zm, zm, x2, gt, wa, wb, wc, wo)


def _mlp_kernel(x_ref, g_ref, sc_ref, sh_ref, gt_ref, w1_ref, w2_ref, gf_ref, o_ref, acc_ref, *, final_norm):
    x = x_ref[...]
    h = _modulated_norm(x, g_ref[...], sc_ref[0], sh_ref[0]).astype(BF16)
    for c in range(D_FF // FF_CHUNK):
        mid = jnp.square(jnp.maximum(_dot(h, w1_ref[:, c * FF_CHUNK:(c + 1) * FF_CHUNK]), 0.0)).astype(BF16)
        upd = _dot(mid, w2_ref[c * FF_CHUNK:(c + 1) * FF_CHUNK, :])
        if c == 0:
            acc_ref[...] = upd
        else:
            acc_ref[...] += upd
    y = x + gt_ref[0] * acc_ref[...]
    if final_norm:
        y = (y * lax.rsqrt(jnp.mean(y * y, axis=-1, keepdims=True) + EPS)) * gf_ref[...]
    o_ref[...] = y


def _mlp(x2, g, sc, sh, gt, w1, w2, g_final, S, final_norm):
    M, D = x2.shape
    tm = min(TM_MLP, S)
    per_b = S // tm
    mod = pl.BlockSpec((1, 1, D), lambda i: (i // per_b, 0, 0))
    return pl.pallas_call(
        functools.partial(_mlp_kernel, final_norm=final_norm),
        grid=(M // tm,),
        in_specs=[pl.BlockSpec((tm, D), lambda i: (i, 0)),
                  pl.BlockSpec((1, D), lambda i: (0, 0)),
                  mod, mod, mod,
                  pl.BlockSpec((D, D_FF), lambda i: (0, 0)),
                  pl.BlockSpec((D_FF, D), lambda i: (0, 0)),
                  pl.BlockSpec((1, D), lambda i: (0, 0))],
        out_specs=pl.BlockSpec((tm, D), lambda i: (i, 0)),
        out_shape=jax.ShapeDtypeStruct((M, D), F32),
        scratch_shapes=[pltpu.VMEM((tm, D), F32)],
        compiler_params=_cparams(("arbitrary",)),
        name="relu2_mlp",
    )(x2, g, sc, sh, gt, w1, w2, g_final)


def _w_cols(w, name):
    return w[:, _OFFS[name]:_OFFS[name] + _SIZE[name]]


def _dup_heads(w):
    parts = []
    for g in range(NSA_G):
        blk = w[:, g * NSA_DH:(g + 1) * NSA_DH]
        parts += [blk, blk]
    return jnp.concatenate(parts, axis=1)


def _proj_weights(w):
    D = w.shape[0]
    wm = jnp.concatenate([_w_cols(w, n) for n in ('ga', 'gbr', 'gc', 'gu', 'gv', 'mq', 'mk', 'mv', 'mo', 'nq')]
                         + [_dup_heads(_w_cols(w, 'nks')), _dup_heads(_w_cols(w, 'nkw')),
                            _w_cols(w, 'nvs'), _w_cols(w, 'nvw')], axis=1)
    wc = jnp.concatenate([_w_cols(w, 'nkc'), _w_cols(w, 'nvc')], axis=1)
    ws = jnp.concatenate([_w_cols(w, 'ngate'), _w_cols(w, 'mi'), _w_cols(w, 'mf'),
                          jnp.zeros((D, ZS_W - NSA_H * NSA_NB - 2 * ML_H), w.dtype)], axis=1)
    return wm.astype(BF16), wc.astype(BF16), ws.astype(BF16)


def kernel(x, c, g_norm1, g_norm2, w_ada, b_ada, w_in, gm_ln_g, gm_ln_b, gm_ws, gm_bs, ml_conv_w, ml_conv_b,
           ml_gate_b, ml_norm_g, nsa_pe_k, nsa_pe_v, nsa_phi_k1, nsa_phi_k2, nsa_phi_v1, nsa_phi_v2,
           w_up_a, w_up_b, w_up_c, w_out, w_mlp1, w_mlp2, g_final):
    B, S, D = x.shape
    depth = w_in.shape[0]
    M = B * S
    mod = _ada(c, w_ada, b_ada)
    x2 = x.reshape(M, D)
    w_up_a, w_up_b, w_up_c, w_out, w_mlp1, w_mlp2 = [w.astype(BF16) for w in (w_up_a, w_up_b, w_up_c, w_out,
                                                                                w_mlp1, w_mlp2)]
    for l in range(depth):
        sh1, sc1, gt1, sh2, sc2, gt2 = [mod[l, :, i * D:(i + 1) * D].reshape(B, 1, D) for i in range(6)]
        wm, wc, ws = _proj_weights(w_in[l])
        zm, zc, zs = _inproj(x2, g_norm1[l].reshape(1, D), sc1, sh1, wm, wc, ws, S)
        kaug, vt, kcaug, vct = _nsa_prep(zc, zm, nsa_pe_k[l], nsa_pe_v[l], nsa_phi_k1[l], nsa_phi_k2[l],
                                         nsa_phi_v1[l], nsa_phi_v2[l], S)
        yc, yb = _nsa_mlstm(zm, zs, kaug, vt, kcaug, vct, ml_conv_w[l], ml_conv_b[l], ml_gate_b[l], ml_norm_g[l], S)
        x2 = _merge(yb, yc, zm, x2, gt1, gm_ln_g[l], gm_ln_b[l], gm_ws[l], gm_bs[l], w_up_a[l], w_up_b[l],
                    w_up_c[l], w_out[l], S)
        x2 = _mlp(x2, g_norm2[l].reshape(1, D), sc2, sh2, gt2, w_mlp1[l], w_mlp2[l], g_final.reshape(1, D), S,
                  final_norm=(l == depth - 1))
    return x2.reshape(B, S, D)
```

```python
import functools

import numpy as np
import jax
import jax.numpy as jnp
from jax import lax
from jax.experimental import pallas as pl
from jax.experimental.pallas import tpu as pltpu

F32 = jnp.float32
BF16 = jnp.bfloat16

D_MODEL = 1024
GM_W = 512
GM_GROUPS = 4
GM_CHUNK = 128
ML_H = 4
ML_DH = 128
ML_W = ML_H * ML_DH
CONV_K = 4
NSA_H = 8
NSA_G = 2
NSA_R = NSA_H // NSA_G
NSA_DH = 64
NSA_W = NSA_H * NSA_DH
NSA_KV = NSA_G * NSA_DH
NSA_NB = 3
CMP_BLOCK = 32
CMP_STRIDE = 16
SEL_BLOCK = 64
TOP_N = 8
WINDOW = 512
D_FF = 4 * D_MODEL
EPS = 1e-6
NEG = -1e30
TAKEN = -3e38
N_FORCED = 3
SPLIT_SIZES = (GM_W, GM_W, ML_W, ML_W, ML_W, ML_W, ML_H, ML_H, NSA_W, NSA_KV, NSA_KV, NSA_KV, NSA_KV,
               NSA_KV, NSA_KV, NSA_H * NSA_NB, D_MODEL, D_MODEL, D_MODEL)
SPLIT_NAMES = ('gu', 'gv', 'mq', 'mk', 'mv', 'mo', 'mi', 'mf', 'nq', 'nkc', 'nvc', 'nks', 'nvs', 'nkw', 'nvw',
               'ngate', 'ga', 'gbr', 'gc')
_OFFS = dict(zip(SPLIT_NAMES, np.concatenate([[0], np.cumsum(SPLIT_SIZES)[:-1]]).tolist()))
_SIZE = dict(zip(SPLIT_NAMES, SPLIT_SIZES))

LANES = 128
VMEM_LIMIT = 56 * 1024 * 1024

ZM_GA, ZM_GBR, ZM_GC = 0, 1024, 2048
ZM_GU, ZM_GV = 3072, 3584
ZM_MQ, ZM_MK, ZM_MV, ZM_MO = 4096, 4608, 5120, 5632
ZM_NQ = 6144
ZM_KK = 6656
ZM_VV = 7168
ZM_W = 7424
KK_W = 4 * LANES
VV_W = 2 * LANES
BR_SEL, BR_WIN = 0, 1
MASK_CAUSAL, MASK_WINDOW_EDGE = 0, 1
FEAT_SEL = 0
FEAT_TERMS = 3
FEAT_HI = 32
FEAT_LO = 35
LOG2E = 1.4426950408889634
ZS_GATE = 0
ZS_MI = 24
ZS_MF = 28
ZS_W = 128

TN_ADA = 1536
TM_PROJ = 1024
TN_PROJ = 3712
ML_CHUNK = 128
CONV_HALO = 16
TQ = 256
TK = 256
SCORE_LOOKAHEAD = 3
SEL_TILES = 4
VT_ROWS = NSA_DH + 16
TM_MERGE = 512
TM_MLP = 512
FF_CHUNK = 1024


def _dot(a, b):
    return jnp.dot(a, b, preferred_element_type=F32)


def _split3(x):
    x1 = x.astype(BF16)
    r1 = x - x1.astype(F32)
    x2 = r1.astype(BF16)
    x3 = (r1 - x2.astype(F32)).astype(BF16)
    return x1, x2, x3


def _cparams(sem):
    return pltpu.CompilerParams(dimension_semantics=sem, vmem_limit_bytes=VMEM_LIMIT)


def _ada_kernel(c_ref, w_ref, b_ref, o_ref):
    c = c_ref[...]
    cond = c * jax.nn.sigmoid(c)
    c1, c2, c3 = _split3(cond)
    w1, w2, w3 = _split3(w_ref[0])
    acc = _dot(c1, w1) + (_dot(c1, w2) + _dot(c2, w1)) + (_dot(c1, w3) + _dot(c2, w2) + _dot(c3, w1))
    o_ref[0] = acc + b_ref[0]


def _ada(c, w_ada, b_ada):
    L, D, N = w_ada.shape
    B = c.shape[0]
    tn = TN_ADA
    return pl.pallas_call(
        _ada_kernel,
        grid=(L, N // tn),
        in_specs=[pl.BlockSpec((B, D), lambda l, j: (0, 0)),
                  pl.BlockSpec((1, D, tn), lambda l, j: (l, 0, j)),
                  pl.BlockSpec((1, 1, tn), lambda l, j: (l, 0, j))],
        out_specs=pl.BlockSpec((1, B, tn), lambda l, j: (l, 0, j)),
        out_shape=jax.ShapeDtypeStruct((L, B, N), F32),
        compiler_params=_cparams(("arbitrary", "arbitrary")),
        name="ada_mod",
    )(c, w_ada, b_ada.reshape(L, 1, N))


def _modulated_norm(x, g, sc, sh):
    y = x * lax.rsqrt(jnp.mean(x * x, axis=-1, keepdims=True) + EPS)
    return (y * g) * (1.0 + sc) + sh


def _inproj_kernel(x_ref, g_ref, sc_ref, sh_ref, wm_ref, wc_ref, ws_ref, zm_ref, zc_ref, zs_ref, h_ref):
    @pl.when(pl.program_id(1) == 0)
    def _():
        h = _modulated_norm(x_ref[...], g_ref[...], sc_ref[0], sh_ref[0]).astype(BF16)
        h_ref[...] = h
        zc = _dot(h, wc_ref[...])
        for part in range(zc_ref.shape[0]):
            zc_ref[part] = zc[:, part * LANES:(part + 1) * LANES]
        zs_ref[...] = _dot(h, ws_ref[...])

    zm_ref[...] = _dot(h_ref[...], wm_ref[...]).astype(BF16)


def _inproj(x2, g, sc, sh, wm, wc, ws, S):
    M, D = x2.shape
    tm, tn = min(TM_PROJ, S), TN_PROJ
    per_b = S // tm
    return pl.pallas_call(
        _inproj_kernel,
        grid=(M // tm, ZM_W // tn),
        in_specs=[pl.BlockSpec((tm, D), lambda i, j: (i, 0)),
                  pl.BlockSpec((1, D), lambda i, j: (0, 0)),
                  pl.BlockSpec((1, 1, D), lambda i, j: (i // per_b, 0, 0)),
                  pl.BlockSpec((1, 1, D), lambda i, j: (i // per_b, 0, 0)),
                  pl.BlockSpec((D, tn), lambda i, j: (0, j)),
                  pl.BlockSpec((D, 2 * NSA_KV), lambda i, j: (0, 0)),
                  pl.BlockSpec((D, ZS_W), lambda i, j: (0, 0))],
        out_specs=[pl.BlockSpec((tm, tn), lambda i, j: (i, j)),
                   pl.BlockSpec((2 * NSA_KV // LANES, tm, LANES), lambda i, j: (0, i, 0)),
                   pl.BlockSpec((tm, ZS_W), lambda i, j: (i, 0))],
        out_shape=[jax.ShapeDtypeStruct((M, ZM_W), BF16),
                   jax.ShapeDtypeStruct((2 * NSA_KV // LANES, M, LANES), F32),
                   jax.ShapeDtypeStruct((M, ZS_W), F32)],
        scratch_shapes=[pltpu.VMEM((tm, D), BF16)],
        compiler_params=_cparams(("arbitrary", "arbitrary")),
        name="in_proj",
    )(x2, g, sc, sh, wm, wc, ws)


def _gmlp_rows(u_ref, v_ref, lng_ref, lnb_ref, ws_ref, bst_ref, o_ref):
    ts = u_ref.shape[0]
    dg = GM_W // GM_GROUPS
    row = lax.broadcasted_iota(jnp.int32, (GM_CHUNK, GM_CHUNK), 0)
    col = lax.broadcasted_iota(jnp.int32, (GM_CHUNK, GM_CHUNK), 1)
    ws = [jnp.where(row >= col, ws_ref[g], 0.0).astype(BF16) for g in range(GM_GROUPS)]
    lng = lng_ref[...]
    lnb = lnb_ref[...]
    for c in range(ts // GM_CHUNK):
        r0 = c * GM_CHUNK
        u = jax.nn.gelu(u_ref[r0:r0 + GM_CHUNK, :].astype(F32))
        v = jax.nn.gelu(v_ref[r0:r0 + GM_CHUNK, :].astype(F32))
        mu = jnp.mean(v, axis=-1, keepdims=True)
        var = jnp.mean(jnp.square(v - mu), axis=-1, keepdims=True)
        vb = ((v - mu) * lax.rsqrt(var + EPS) * lng + lnb).astype(BF16)
        for g in range(GM_GROUPS):
            mixed = _dot(ws[g], vb[:, g * dg:(g + 1) * dg]) + bst_ref[:, g:g + 1]
            o_ref[r0:r0 + GM_CHUNK, g * dg:(g + 1) * dg] = (u[:, g * dg:(g + 1) * dg] * mixed).astype(BF16)


def _log_sigmoid(x):
    return jnp.minimum(x, 0.0) - jnp.log1p(jnp.exp(-jnp.abs(x)))


def _conv_silu(x_ext, w, b):
    n = x_ext.shape[0] - CONV_HALO
    y = b
    for j in range(CONV_K):
        sh = CONV_K - 1 - j
        xs = x_ext if sh == 0 else pltpu.roll(x_ext, sh, axis=0)
        y = y + xs[CONV_HALO:CONV_HALO + n] * w[j:j + 1]
    return y * jax.nn.sigmoid(y)


def _mlstm_chunk_stages(q_ref, k_ref, v_ref, o_ref, zs_ref, cw_ref, cb_ref, gb_ref, ng_ref, out_ref, c_ref, m_ref,
                        row0, t_abs, has_history):
    L = ML_CHUNK
    rows = slice(row0, row0 + L)
    hs = [slice(h * ML_DH, (h + 1) * ML_DH) for h in range(ML_H)]
    heads = range(ML_H)
    st = {}

    def conv_rows(src, c0):
        cur = src[pl.ds(pl.multiple_of(t_abs, L), L), :].astype(F32)
        halo0 = pl.multiple_of(jnp.maximum(t_abs - CONV_HALO, 0), CONV_HALO)
        halo = jnp.where(has_history, src[pl.ds(halo0, CONV_HALO), :].astype(F32), 0.0)
        return _conv_silu(jnp.concatenate([halo, cur], axis=0), cw_ref[:, c0:c0 + ML_W], cb_ref[:, c0:c0 + ML_W])

    def projections():
        row = lax.broadcasted_iota(jnp.int32, (L, L), 0)
        col = lax.broadcasted_iota(jnp.int32, (L, L), 1)
        tril_b = jnp.where(row >= col, 1.0, 0.0).astype(BF16)
        triu_b = jnp.where(row <= col, 1.0, 0.0).astype(BF16)
        gi = zs_ref[rows, :] + gb_ref[...]
        gi_t = gi.T
        c1, c2, c3 = _split3(_log_sigmoid(gi) * LOG2E)
        b_cols = _dot(tril_b, c1) + _dot(tril_b, c2) + _dot(tril_b, c3)
        r1, r2, r3 = _split3(_log_sigmoid(gi_t[ZS_MI:ZS_MI + 2 * ML_H]) * LOG2E)
        b_rows = _dot(r1, triu_b) + _dot(r2, triu_b) + _dot(r3, triu_b)
        st['gi'] = gi * LOG2E
        st['r_rows'] = gi_t[ZS_MI:ZS_MI + ML_H] * LOG2E - b_rows[ML_H:2 * ML_H]
        st['b_at_i'] = pltpu.roll(b_cols, ZS_W - (ZS_MF - ZS_MI), axis=1)
        q_all = conv_rows(q_ref, 0).astype(BF16)
        k_all = conv_rows(k_ref, ML_W) * (ML_DH ** -0.5)
        st['q'] = [q_all[:, hs[h]] for h in heads]
        st['k_t'] = [k_all[:, hs[h]].T for h in heads]
        st['cmat'] = [c_ref[h] for h in heads]
        st['qk'] = [_dot(st['q'][h], st['k_t'][h].astype(BF16)) for h in heads]
        st['qc'] = [_dot(st['q'][h], st['cmat'][h].astype(BF16)) for h in heads]
        ones_blk = jnp.ones((L, ML_DH), BF16)
        st['v_aug'] = [jnp.concatenate([v_ref[rows, hs[h]], ones_blk], axis=1) for h in heads]

    def memory_update():
        tril = lax.broadcasted_iota(jnp.int32, (L, L), 0) >= lax.broadcasted_iota(jnp.int32, (L, L), 1)
        time_row = lax.broadcasted_iota(jnp.int32, (L, ZS_W), 0)
        m_row = m_ref[...]
        u = st['gi'] - st['b_at_i']
        shift = 1
        while shift < L:
            u = jnp.maximum(u, jnp.where(time_row >= shift, pltpu.roll(u, shift, axis=0), NEG))
            shift *= 2
        u = jnp.maximum(u, m_row)
        m_cols = st['b_at_i'] + u
        u_last = u[L - 1:L, :]
        w_prev_row = jnp.exp2(m_row - u_last)
        st['m_cols'], st['m_prev'], st['u_rep'], st['gate'] = m_cols, m_row, [], []
        for h in heads:
            lane_h = ZS_MI + h
            r_row = st['r_rows'][h:h + 1, :]
            u_rep = jnp.broadcast_to(u[:, lane_h:lane_h + 1], (L, ML_DH))
            st['u_rep'].append(u_rep)
            st['gate'].append(jnp.exp2(jnp.where(tril, r_row - u_rep, NEG)))
            w_s = jnp.exp2(r_row - u_last[:, lane_h:lane_h + 1])
            k_w = (st['k_t'][h] * w_s).astype(BF16)
            c_ref[h] = w_prev_row[:, lane_h:lane_h + 1] * st['cmat'][h] + _dot(k_w, st['v_aug'][h])
        m_ref[...] = m_cols[L - 1:L, :]

    def read_out():
        st['num_aug'] = []
        for h in heads:
            lane_h = ZS_MI + h
            w_inter = jnp.exp2(st['m_prev'][:, lane_h:lane_h + 1] - st['u_rep'][h])
            st['num_aug'].append(_dot((st['qk'][h] * st['gate'][h]).astype(BF16), st['v_aug'][h])
                                 + jnp.concatenate([w_inter, w_inter], axis=1) * st['qc'][h])

    def normalise_and_store():
        for h in heads:
            lane_h = ZS_MI + h
            num_aug = st['num_aug'][h]
            m_rep = jnp.broadcast_to(st['m_cols'][:, lane_h:lane_h + 1], (L, ML_DH))
            hval = num_aug[:, :ML_DH] / jnp.maximum(jnp.abs(num_aug[:, ML_DH:]), jnp.exp2(-m_rep))
            mu = jnp.mean(hval, axis=-1, keepdims=True)
            var = jnp.mean(jnp.square(hval - mu), axis=-1, keepdims=True)
            hn = (hval - mu) * lax.rsqrt(var + EPS) * ng_ref[:, hs[h]]
            og = jax.nn.sigmoid(o_ref[rows, hs[h]].astype(F32))
            out_ref[rows, hs[h]] = (og * hn).astype(BF16)

    return [projections, memory_update, read_out, normalise_and_store]


def _nsa_prep_kernel(x_ref, pe_ref, w1_ref, w2_ref, kk_ref, vv_ref, feat_ref, featc_ref,
                     kaug_ref, vt_ref, kcaug_ref, vct_ref):
    S = kk_ref.shape[0]
    n = S // CMP_STRIDE
    cw = 2 * NSA_KV
    half = CMP_BLOCK // 2
    acc_a = jnp.zeros((n, cw), F32)
    acc_b = jnp.zeros((n, cw), F32)
    for j in range(half):
        xj = jnp.concatenate([x_ref[part, pl.ds(j, n, stride=CMP_STRIDE), :] for part in range(cw // LANES)],
                             axis=1)
        acc_a = acc_a + _dot((xj + pe_ref[j:j + 1, :]).astype(BF16), w1_ref[j])
        acc_b = acc_b + _dot((xj + pe_ref[half + j:half + j + 1, :]).astype(BF16), w1_ref[half + j])
    pre = acc_a + pltpu.roll(acc_b, n - 1, axis=0)
    cmp = _dot(jax.nn.gelu(pre).astype(BF16), w2_ref[...])

    lane = lax.broadcasted_iota(jnp.int32, (1, LANES), 1)
    keep = [jnp.where(lane < NSA_DH, 1.0, 0.0).astype(BF16), jnp.where(lane >= NSA_DH, 1.0, 0.0).astype(BF16)]
    for g in range(NSA_G):
        kd = cmp[:, g * LANES:(g + 1) * LANES].astype(BF16)
        for e in range(2):
            kcaug_ref[0, g * 2 + e] = kd * keep[e] + featc_ref[e]
    vct_ref[0] = cmp[:, NSA_G * LANES:(NSA_G + 1) * LANES].T.astype(BF16)
    tb = min(TK, S)
    for br in range(2):
        for g in range(NSA_G):
            kd = kk_ref[:, (br * NSA_G + g) * LANES:(br * NSA_G + g + 1) * LANES]
            for e in range(2):
                kaug_ref[0, (br * NSA_G + g) * 2 + e] = kd * keep[e] + feat_ref[br * 2 + e]
        for c in range(S // tb):
            v_t = vv_ref[c * tb:(c + 1) * tb, br * LANES:(br + 1) * LANES].astype(F32).T.astype(BF16)
            for g in range(NSA_G):
                vt_ref[0, br, g, 0:NSA_DH, c * tb:(c + 1) * tb] = v_t[g * NSA_DH:(g + 1) * NSA_DH]
                vt_ref[0, br, g, NSA_DH:VT_ROWS, c * tb:(c + 1) * tb] = jnp.ones((VT_ROWS - NSA_DH, tb), BF16)


def _blockdiag(blocks):
    n = len(blocks)
    rows = []
    for i, blk in enumerate(blocks):
        rows.append(jnp.concatenate([blk if j == i else jnp.zeros((blk.shape[0], blocks[j].shape[1]), blk.dtype)
                                     for j in range(n)], axis=1))
    return jnp.concatenate(rows, axis=0)


def _key_features(S):
    pos = np.arange(S)
    f = np.zeros((4, S, LANES), np.float32)
    for e in range(2):
        base = NSA_DH * (1 - e)
        f[e, pos, base + FEAT_SEL + pos // SEL_BLOCK] = 1.0
        for br in range(2):
            f[br * 2 + e, :, base + FEAT_HI:base + FEAT_HI + FEAT_TERMS] = (pos // 64)[:, None]
            f[br * 2 + e, :, base + FEAT_LO:base + FEAT_LO + FEAT_TERMS] = (pos % 64)[:, None]
    return jnp.asarray(f, dtype=BF16)


def _cmp_features(S):
    c = np.arange(S // CMP_STRIDE)
    hi = (c * CMP_STRIDE) // 64
    lo = c * CMP_STRIDE + (CMP_BLOCK - 1) * 0.5 - 64 * hi
    f = np.zeros((2, c.size, LANES), np.float32)
    for e in range(2):
        base = NSA_DH * (1 - e)
        f[e, :, base + FEAT_HI:base + FEAT_HI + FEAT_TERMS] = hi[:, None]
        f[e, :, base + FEAT_LO:base + FEAT_LO + FEAT_TERMS] = lo[:, None]
    return jnp.asarray(f, dtype=BF16)


def _bf16_terms(x):
    terms = []
    for _ in range(FEAT_TERMS):
        terms.append(float(np.asarray(x - sum(terms), dtype=BF16)))
    return terms


def _nsa_prep(zc, zm, pe_k, pe_v, phi_k1, phi_k2, phi_v1, phi_v2, S):
    M = zc.shape[1]
    B = M // S
    n = S // CMP_STRIDE
    cw = 2 * NSA_KV
    assert S // SEL_BLOCK <= FEAT_HI and S <= 64 * 64
    pe = jnp.concatenate([pe_k, pe_k, pe_v, pe_v], axis=1)
    k1 = phi_k1.reshape(CMP_BLOCK, NSA_DH, NSA_DH)
    v1 = phi_v1.reshape(CMP_BLOCK, NSA_DH, NSA_DH)
    blocks = jnp.stack([k1, k1, v1, v1], axis=1)
    w1 = jnp.einsum('jaxy,ab->jaxby', blocks, jnp.eye(4, dtype=F32)).reshape(CMP_BLOCK, cw, cw).astype(BF16)
    k2 = jnp.concatenate([phi_k2, phi_k2], axis=1)
    w2 = _blockdiag([k2, k2, phi_v2, phi_v2]).astype(BF16)

    def full(shape):
        return pl.BlockSpec(shape, lambda b: (0,) * len(shape))

    return pl.pallas_call(
        _nsa_prep_kernel,
        grid=(B,),
        in_specs=[pl.BlockSpec((cw // LANES, S, LANES), lambda b: (0, b, 0)),
                  full((CMP_BLOCK, cw)), full((CMP_BLOCK, cw, cw)), full((cw, 3 * LANES)),
                  pl.BlockSpec((S, KK_W), lambda b: (b, ZM_KK // KK_W)),
                  pl.BlockSpec((S, VV_W), lambda b: (b, ZM_VV // VV_W)),
                  full((4, S, LANES)), full((2, n, LANES))],
        out_specs=[pl.BlockSpec((1, 8, S, LANES), lambda b: (b, 0, 0, 0)),
                   pl.BlockSpec((1, 2, NSA_G, VT_ROWS, S), lambda b: (b, 0, 0, 0, 0)),
                   pl.BlockSpec((1, 4, n, LANES), lambda b: (b, 0, 0, 0)),
                   pl.BlockSpec((1, LANES, n), lambda b: (b, 0, 0))],
        out_shape=[jax.ShapeDtypeStruct((B, 8, S, LANES), BF16),
                   jax.ShapeDtypeStruct((B, 2, NSA_G, VT_ROWS, S), BF16),
                   jax.ShapeDtypeStruct((B, 4, n, LANES), BF16),
                   jax.ShapeDtypeStruct((B, LANES, n), BF16)],
        compiler_params=_cparams(("arbitrary",)),
        name="nsa_prep",
    )(zc, pe, w1, w2, zm, zm, _key_features(S), _cmp_features(S))


def _nsa_mlstm_kernel(q_ref, kaug_ref, vt_ref, kcaug_ref, vct_ref, zs_ref, ovt_ref,
                      mq_ref, mk_ref, mv_ref, mo_ref, cw_ref, cb_ref, gb_ref, ng_ref, out_ref, yb_ref,
                      qat_ref, qaw_ref, ocmp_ref, m_ref, acc_ref, mc_ref, mm_ref):
    S = kaug_ref.shape[2]
    tq = q_ref.shape[0]
    ncmp = kcaug_ref.shape[2]
    nsel = S // SEL_BLOCK
    n_pairs = NSA_H // 2
    qi = pl.program_id(1)
    t0 = pl.multiple_of(qi * tq, tq)
    t_row = t0 + lax.broadcasted_iota(jnp.int32, (1, tq), 1)

    @pl.when(qi == 0)
    def _():
        mc_ref[...] = jnp.zeros_like(mc_ref)
        mm_ref[...] = jnp.zeros_like(mm_ref)

    def mlstm_chunk(c):
        return _mlstm_chunk_stages(mq_ref, mk_ref, mv_ref, mo_ref, zs_ref, cw_ref, cb_ref, gb_ref, ng_ref, yb_ref,
                                   mc_ref, mm_ref, row0=c * ML_CHUNK, t_abs=t0 + c * ML_CHUNK,
                                   has_history=(qi > 0) if c == 0 else True)

    for k in range(NSA_H // 2):
        q_t = (q_ref[:, k * LANES:(k + 1) * LANES].astype(F32) * (NSA_DH ** -0.5 * LOG2E)).T.astype(BF16)
        for e in range(2):
            for dst in (qat_ref, qaw_ref):
                dst[(k // 2) * 2 + e, e * NSA_DH:(e + 1) * NSA_DH, (k % 2) * tq:(k % 2 + 1) * tq] = (
                    q_t[e * NSA_DH:(e + 1) * NSA_DH])
    feat_row = lax.broadcasted_iota(jnp.int32, (NSA_DH, 2 * tq), 0)
    second_head = lax.broadcasted_iota(jnp.int32, (NSA_DH, 2 * tq), 1) >= tq
    for pair in range(n_pairs):
        g, e = pair // 2, pair % 2
        terms = [_bf16_terms(LOG2E * 2.0 ** (-8.0 * (g * NSA_R + 2 * s + e + 1.0) / NSA_H)) for s in range(2)]
        feat = jnp.zeros((NSA_DH, 2 * tq), F32)
        for i in range(FEAT_TERMS):
            term = jnp.where(second_head, terms[1][i], terms[0][i])
            feat = jnp.where(feat_row == FEAT_HI + i, term * 64.0, jnp.where(feat_row == FEAT_LO + i, term, feat))
        qat_ref[pair, NSA_DH * (1 - e):NSA_DH * (2 - e), :] = feat.astype(BF16)
        qaw_ref[pair, NSA_DH * (1 - e):NSA_DH * (2 - e), :] = feat.astype(BF16)

    ki = lax.broadcasted_iota(jnp.int32, (TK, 2 * tq), 0)
    qu = lax.broadcasted_iota(jnp.int32, (TK, 2 * tq), 1)
    qu = jnp.where(qu >= tq, qu - tq, qu)
    tile_valid = {MASK_CAUSAL: ki <= qu,
                  MASK_WINDOW_EDGE: ki > qu}

    c_col = lax.broadcasted_iota(jnp.int32, (ncmp, 1), 0)
    t_row2 = jnp.concatenate([t_row, t_row], axis=1)
    valid_c = c_col * CMP_STRIDE + (CMP_BLOCK - 1) <= t_row2
    j_col = lax.broadcasted_iota(jnp.int32, (nsel, 1), 0)
    j_f = j_col.astype(F32)
    jt = t_row >> 6
    forced = jnp.logical_or(j_col == 0, jnp.logical_or(j_col == jt, j_col == jt - 1))
    future = j_col > jt

    m_ref[...] = jnp.full(m_ref.shape, NEG, F32)
    acc_ref[...] = jnp.zeros(acc_ref.shape, F32)

    def compressed_scores():
        return [_dot(kcaug_ref[0, pair], qaw_ref[pair]) for pair in range(n_pairs)]

    def compressed_branch_and_selection(scores_c):
        for g in range(NSA_G):
            psum = jnp.zeros((ncmp, tq), F32)
            for e in range(2):
                pair = g * 2 + e
                sc = jnp.where(valid_c, scores_c[pair], NEG)
                ex = jnp.where(valid_c, jnp.exp2(sc - jnp.max(sc, axis=0, keepdims=True)), 0.0)
                den = jnp.sum(ex, axis=0, keepdims=True)
                p = ex * (1.0 / jnp.where(den > 0.0, den, 1.0))
                psum = psum + p[:, :tq] + p[:, tq:]
                ocmp_ref[pair] = _dot(vct_ref[0, g * NSA_DH:(g + 1) * NSA_DH, :], p.astype(BF16))

            p_hi = psum.astype(BF16)
            p_lo = (psum - p_hi.astype(F32)).astype(BF16)
            imp = _dot(ovt_ref[...], p_hi) + _dot(ovt_ref[...], p_lo)
            penalty = jnp.where(forced, 0.0, NEG)
            val = jnp.where(future, NEG, jnp.where(forced, TAKEN, imp))
            for _ in range(min(TOP_N, nsel) - N_FORCED):
                best = jnp.max(val, axis=0, keepdims=True)
                first = jnp.min(jnp.where(val == best, j_f, float(nsel)), axis=0, keepdims=True)
                hit = j_f == first
                penalty = jnp.where(hit, 0.0, penalty)
                val = jnp.where(hit, TAKEN, val)
            penalty = penalty.astype(BF16)
            for e in range(2):
                first = NSA_DH * (1 - e) + FEAT_SEL
                qat_ref[g * 2 + e, first:first + nsel, :] = jnp.concatenate([penalty, penalty], axis=1)

    def attend(br, tiles, interleave=()):
        stages = [(pl.multiple_of(kb * TK, TK), mask, pair) for kb, mask in tiles for pair in range(n_pairs)]

        def scores(stage):
            k0, _, pair = stage
            queries = qat_ref if br == BR_SEL else qaw_ref
            return _dot(kaug_ref[0, br * n_pairs + pair, pl.ds(k0, TK), :], queries[pair])

        s_queue = [scores(st) for st in stages[:SCORE_LOOKAHEAD]]
        pending = list(interleave)
        if pending:
            pending.pop(0)()
        for i, (k0, mask, pair) in enumerate(stages):
            s_t = s_queue.pop(0)
            if i + SCORE_LOOKAHEAD < len(stages):
                s_queue.append(scores(stages[i + SCORE_LOOKAHEAD]))
            slot = br * n_pairs + pair
            if mask is not None:
                s_t = jnp.where(tile_valid[mask], s_t, NEG)
            m_old = m_ref[slot]
            m_new = jnp.maximum(m_old, jnp.max(s_t, axis=0, keepdims=True))
            alpha = jnp.exp2(m_old - m_new)
            p = jnp.exp2(s_t - m_new).astype(BF16)
            acc_ref[slot] = alpha * acc_ref[slot] + _dot(vt_ref[0, br, pair // 2, :, pl.ds(k0, TK)], p)
            m_ref[slot] = m_new
            if pending and i % 2 == 1:
                pending.pop(0)()
        for emit in pending:
            emit()

    n_back = WINDOW // TK
    for n_behind in range(n_back + 1):
        tiles = [(qi, MASK_CAUSAL)] + [(qi - d, MASK_WINDOW_EDGE if d == n_back else None)
                                        for d in range(1, n_behind + 1)]

        @pl.when(qi >= n_back if n_behind == n_back else qi == n_behind)
        def _(tiles=tiles):
            scores_c = compressed_scores()
            project, *rest = mlstm_chunk(0)

            def selection_and_projections():
                compressed_branch_and_selection(scores_c)
                project()

            attend(BR_WIN, tiles, [selection_and_projections] + rest)

    def sel_body(i, carry):
        attend(BR_SEL, [(SEL_TILES * i + j, None) for j in range(SEL_TILES)])
        return carry

    lax.fori_loop(0, qi // SEL_TILES, sel_body, 0)
    for n_full in range(SEL_TILES):
        @pl.when(qi % SEL_TILES == n_full)
        def _(n_full=n_full):
            attend(BR_SEL, [(qi - n_full + j, None) for j in range(n_full)] + [(qi, MASK_CAUSAL)], mlstm_chunk(1))

    gates_t = jax.nn.sigmoid(zs_ref[...]).T
    for hp in range(NSA_H // 2):
        comb = []
        for e in range(2):
            h = 2 * hp + e
            c0 = ZS_GATE + NSA_NB * h
            pair = (h // NSA_R) * 2 + e
            cols = slice(((h % NSA_R) // 2) * tq, ((h % NSA_R) // 2 + 1) * tq)
            sel, win = BR_SEL * n_pairs + pair, BR_WIN * n_pairs + pair
            o_sel = acc_ref[sel, 0:NSA_DH, cols] * (1.0 / acc_ref[sel, NSA_DH:NSA_DH + 1, cols])
            o_win = acc_ref[win, 0:NSA_DH, cols] * (1.0 / acc_ref[win, NSA_DH:NSA_DH + 1, cols])
            comb.append(gates_t[c0:c0 + 1] * ocmp_ref[pair, :, cols] + gates_t[c0 + 1:c0 + 2] * o_sel
                        + gates_t[c0 + 2:c0 + 3] * o_win)
        out_ref[:, hp * LANES:(hp + 1) * LANES] = jnp.concatenate(comb, axis=0).T.astype(BF16)


def _overlap_t(S):
    c = np.arange(S // CMP_STRIDE)[None, :]
    j = np.arange(S // SEL_BLOCK)[:, None]
    ov = (c * CMP_STRIDE <= j * SEL_BLOCK + SEL_BLOCK - 1) & (c * CMP_STRIDE + CMP_BLOCK - 1 >= j * SEL_BLOCK)
    ov &= c < (S - CMP_BLOCK) // CMP_STRIDE + 1
    return jnp.asarray(ov.astype(np.float32), dtype=BF16)


def _nsa_mlstm(zm, zs, kaug, vt, kcaug, vct, conv_w, conv_b, gate_b, norm_g, S):
    M = zm.shape[0]
    B = M // S
    tq = min(TQ, S)
    assert tq == TK and WINDOW % TK == 0 and S % TK == 0 and tq == 2 * ML_CHUNK and ML_CHUNK == ML_DH
    nq = S // tq
    n = S // CMP_STRIDE
    nsel = S // SEL_BLOCK
    gb_row = jnp.zeros((1, ZS_W), F32).at[0, ZS_MI:ZS_MI + 2 * ML_H].set(gate_b)

    def tile(off):
        return pl.BlockSpec((tq, ML_W), lambda b, i: (b * nq + i, off // ML_W))

    def row(off):
        return pl.BlockSpec((S, ML_W), lambda b, i: (b, off // ML_W))

    def full(shape):
        return pl.BlockSpec(shape, lambda b, i: (0,) * len(shape))

    return pl.pallas_call(
        _nsa_mlstm_kernel,
        grid=(B, nq),
        in_specs=[tile(ZM_NQ),
                  pl.BlockSpec((1, 8, S, LANES), lambda b, i: (b, 0, 0, 0)),
                  pl.BlockSpec((1, 2, NSA_G, VT_ROWS, S), lambda b, i: (b, 0, 0, 0, 0)),
                  pl.BlockSpec((1, 4, n, LANES), lambda b, i: (b, 0, 0, 0)),
                  pl.BlockSpec((1, LANES, n), lambda b, i: (b, 0, 0)),
                  pl.BlockSpec((tq, ZS_W), lambda b, i: (b * nq + i, 0)),
                  full((nsel, n)),
                  row(ZM_MQ), row(ZM_MK), tile(ZM_MV), tile(ZM_MO),
                  full((CONV_K, 2 * ML_W)), full((1, 2 * ML_W)), full((1, ZS_W)), full((1, ML_W))],
        out_specs=[pl.BlockSpec((tq, NSA_W), lambda b, i: (b * nq + i, 0)),
                   pl.BlockSpec((tq, ML_W), lambda b, i: (b * nq + i, 0))],
        out_shape=[jax.ShapeDtypeStruct((M, NSA_W), BF16), jax.ShapeDtypeStruct((M, ML_W), BF16)],
        scratch_shapes=[pltpu.VMEM((NSA_H // 2, LANES, 2 * tq), BF16),
                        pltpu.VMEM((NSA_H // 2, LANES, 2 * tq), BF16),
                        pltpu.VMEM((NSA_H // 2, NSA_DH, 2 * tq), F32),
                        pltpu.VMEM((NSA_H, 1, 2 * tq), F32),
                        pltpu.VMEM((NSA_H, VT_ROWS, 2 * tq), F32),
                        pltpu.VMEM((ML_H, ML_DH, 2 * ML_DH), F32),
                        pltpu.VMEM((1, ZS_W), F32)],
        compiler_params=_cparams(("arbitrary", "arbitrary")),
        name="nsa_mlstm",
    )(zm, kaug, vt, kcaug, vct, zs, _overlap_t(S), zm, zm, zm, zm,
      conv_w, conv_b.reshape(1, 2 * ML_W), gb_row, norm_g.reshape(1, ML_W))


def _merge_kernel(gu0_ref, gv0_ref, gu_ref, gv_ref, lng_ref, lnb_ref, ws_ref, bst_ref, b_ref, c_ref,
                  ga_ref, gb_ref, gc_ref, x_ref, gt_ref, wa_ref, wb_ref, wc_ref, wo_ref, o_ref, a_ref):
    i = pl.program_id(0)

    @pl.when(i == 0)
    def _():
        _gmlp_rows(gu0_ref, gv0_ref, lng_ref, lnb_ref, ws_ref, bst_ref, a_ref.at[0])

    a = a_ref[i % 2]
    merged = (jax.nn.sigmoid(ga_ref[...].astype(F32)) * _dot(a, wa_ref[...])
              + jax.nn.sigmoid(gb_ref[...].astype(F32)) * _dot(b_ref[...], wb_ref[...])
              + jax.nn.sigmoid(gc_ref[...].astype(F32)) * _dot(c_ref[...], wc_ref[...]))
    o_ref[...] = x_ref[...] + gt_ref[0] * _dot(merged.astype(BF16), wo_ref[...])
    _gmlp_rows(gu_ref, gv_ref, lng_ref, lnb_ref, ws_ref, bst_ref, a_ref.at[(i + 1) % 2])


def _merge(yb, yc, zm, x2, gt, ln_g, ln_b, ws, bs, wa, wb, wc, wo, S):
    M, D = x2.shape
    tm = min(TM_MERGE, S)
    per_b = S // tm

    def rows(w, jcol=0):
        return pl.BlockSpec((tm, w), lambda i: (i, jcol))

    def next_rows(jcol):
        return pl.BlockSpec((tm, GM_W), lambda i: (jnp.minimum(i + 1, M // tm - 1), jcol))

    def full(shape):
        return pl.BlockSpec(shape, lambda i: (0,) * len(shape))

    return pl.pallas_call(
        _merge_kernel,
        grid=(M // tm,),
        in_specs=[rows(GM_W, ZM_GU // GM_W), rows(GM_W, ZM_GV // GM_W),
                  next_rows(ZM_GU // GM_W), next_rows(ZM_GV // GM_W), full((1, GM_W)), full((1, GM_W)),
                  full((GM_GROUPS, GM_CHUNK, GM_CHUNK)), full((GM_CHUNK, GM_GROUPS)),
                  rows(ML_W), rows(NSA_W),
                  rows(D, ZM_GA // D), rows(D, ZM_GBR // D), rows(D, ZM_GC // D),
                  rows(D), pl.BlockSpec((1, 1, D), lambda i: (i // per_b, 0, 0)),
                  full((GM_W, D)), full((ML_W, D)), full((NSA_W, D)), full((D, D))],
        out_specs=rows(D),
        out_shape=jax.ShapeDtypeStruct((M, D), F32),
        scratch_shapes=[pltpu.VMEM((2, tm, GM_W), BF16)],
        compiler_params=_cparams(("arbitrary",)),
        name="gmlp_merge_out",
    )(zm, zm, zm, zm, ln_g.reshape(1, GM_W), ln_b.reshape(1, GM_W), ws, bs.T, yb, yc, zm, zm, zm, x2, gt, wa, wb, wc, wo)


def _mlp_kernel(x_ref, g_ref, sc_ref, sh_ref, gt_ref, w1_ref, w2_ref, gf_ref, o_ref, acc_ref, *, final_norm):
    x = x_ref[...]
    h = _modulated_norm(x, g_ref[...], sc_ref[0], sh_ref[0]).astype(BF16)
    for c in range(D_FF // FF_CHUNK):
        mid = jnp.square(jnp.maximum(_dot(h, w1_ref[:, c * FF_CHUNK:(c + 1) * FF_CHUNK]), 0.0)).astype(BF16)
        upd = _dot(mid, w2_ref[c * FF_CHUNK:(c + 1) * FF_CHUNK, :])
        if c == 0:
            acc_ref[...] = upd
        else:
            acc_ref[...] += upd
    y = x + gt_ref[0] * acc_ref[...]
    if final_norm:
        y = (y * lax.rsqrt(jnp.mean(y * y, axis=-1, keepdims=True) + EPS)) * gf_ref[...]
    o_ref[...] = y


def _mlp(x2, g, sc, sh, gt, w1, w2, g_final, S, final_norm):
    M, D = x2.shape
    tm = min(TM_MLP, S)
    per_b = S // tm
    mod = pl.BlockSpec((1, 1, D), lambda i: (i // per_b, 0, 0))
    return pl.pallas_call(
        functools.partial(_mlp_kernel, final_norm=final_norm),
        grid=(M // tm,),
        in_specs=[pl.BlockSpec((tm, D), lambda i: (i, 0)),
                  pl.BlockSpec((1, D), lambda i: (0, 0)),
                  mod, mod, mod,
                  pl.BlockSpec((D, D_FF), lambda i: (0, 0)),
                  pl.BlockSpec((D_FF, D), lambda i: (0, 0)),
                  pl.BlockSpec((1, D), lambda i: (0, 0))],
        out_specs=pl.BlockSpec((tm, D), lambda i: (i, 0)),
        out_shape=jax.ShapeDtypeStruct((M, D), F32),
        scratch_shapes=[pltpu.VMEM((tm, D), F32)],
        compiler_params=_cparams(("arbitrary",)),
        name="relu2_mlp",
    )(x2, g, sc, sh, gt, w1, w2, g_final)


def _w_cols(w, name):
    return w[:, _OFFS[name]:_OFFS[name] + _SIZE[name]]


def _dup_heads(w):
    parts = []
    for g in range(NSA_G):
        blk = w[:, g * NSA_DH:(g + 1) * NSA_DH]
        parts += [blk, blk]
    return jnp.concatenate(parts, axis=1)


def _proj_weights(w):
    D = w.shape[0]
    wm = jnp.concatenate([_w_cols(w, n) for n in ('ga', 'gbr', 'gc', 'gu', 'gv', 'mq', 'mk', 'mv', 'mo', 'nq')]
                         + [_dup_heads(_w_cols(w, 'nks')), _dup_heads(_w_cols(w, 'nkw')),
                            _w_cols(w, 'nvs'), _w_cols(w, 'nvw')], axis=1)
    wc = jnp.concatenate([_w_cols(w, 'nkc'), _w_cols(w, 'nvc')], axis=1)
    ws = jnp.concatenate([_w_cols(w, 'ngate'), _w_cols(w, 'mi'), _w_cols(w, 'mf'),
                          jnp.zeros((D, ZS_W - NSA_H * NSA_NB - 2 * ML_H), w.dtype)], axis=1)
    return wm, wc, ws


def kernel(x, c, g_norm1, g_norm2, w_ada, b_ada, w_in, gm_ln_g, gm_ln_b, gm_ws, gm_bs, ml_conv_w, ml_conv_b,
           ml_gate_b, ml_norm_g, nsa_pe_k, nsa_pe_v, nsa_phi_k1, nsa_phi_k2, nsa_phi_v1, nsa_phi_v2,
           w_up_a, w_up_b, w_up_c, w_out, w_mlp1, w_mlp2, g_final):
    B, S, D = x.shape
    depth = w_in.shape[0]
    M = B * S
    mod = _ada(c, w_ada, b_ada)
    x2 = x.reshape(M, D)
    w_in, w_up_a, w_up_b, w_up_c, w_out, w_mlp1, w_mlp2 = [
        w.astype(BF16) for w in (w_in, w_up_a, w_up_b, w_up_c, w_out, w_mlp1, w_mlp2)]
    for l in range(depth):
        sh1, sc1, gt1, sh2, sc2, gt2 = [mod[l, :, i * D:(i + 1) * D].reshape(B, 1, D) for i in range(6)]
        wm, wc, ws = _proj_weights(w_in[l])
        zm, zc, zs = _inproj(x2, g_norm1[l].reshape(1, D), sc1, sh1, wm, wc, ws, S)
        kaug, vt, kcaug, vct = _nsa_prep(zc, zm, nsa_pe_k[l], nsa_pe_v[l], nsa_phi_k1[l], nsa_phi_k2[l],
                                         nsa_phi_v1[l], nsa_phi_v2[l], S)
        yc, yb = _nsa_mlstm(zm, zs, kaug, vt, kcaug, vct, ml_conv_w[l], ml_conv_b[l], ml_gate_b[l], ml_norm_g[l], S)
        x2 = _merge(yb, yc, zm, x2, gt1, gm_ln_g[l], gm_ln_b[l], gm_ws[l], gm_bs[l], w_up_a[l], w_up_b[l],
                    w_up_c[l], w_out[l], S)
        x2 = _mlp(x2, g_norm2[l].reshape(1, D), sc2, sh2, gt2, w_mlp1[l], w_mlp2[l], g_final.reshape(1, D), S,
                  final_norm=(l == depth - 1))
    return x2.reshape(B, S, D)
```

```python
import functools

import numpy as np
import jax
import jax.numpy as jnp
from jax import lax
from jax.experimental import pallas as pl
from jax.experimental.pallas import tpu as pltpu

F32 = jnp.float32
BF16 = jnp.bfloat16

D_MODEL = 1024
GM_W = 512
GM_GROUPS = 4
GM_CHUNK = 128
ML_H = 4
ML_DH = 128
ML_W = ML_H * ML_DH
CONV_K = 4
NSA_H = 8
NSA_G = 2
NSA_R = NSA_H // NSA_G
NSA_DH = 64
NSA_W = NSA_H * NSA_DH
NSA_KV = NSA_G * NSA_DH
NSA_NB = 3
CMP_BLOCK = 32
CMP_STRIDE = 16
SEL_BLOCK = 64
TOP_N = 8
WINDOW = 512
D_FF = 4 * D_MODEL
EPS = 1e-6
NEG = -1e30
TAKEN = -3e38
N_FORCED = 3
SPLIT_SIZES = (GM_W, GM_W, ML_W, ML_W, ML_W, ML_W, ML_H, ML_H, NSA_W, NSA_KV, NSA_KV, NSA_KV, NSA_KV,
               NSA_KV, NSA_KV, NSA_H * NSA_NB, D_MODEL, D_MODEL, D_MODEL)
SPLIT_NAMES = ('gu', 'gv', 'mq', 'mk', 'mv', 'mo', 'mi', 'mf', 'nq', 'nkc', 'nvc', 'nks', 'nvs', 'nkw', 'nvw',
               'ngate', 'ga', 'gbr', 'gc')
_OFFS = dict(zip(SPLIT_NAMES, np.concatenate([[0], np.cumsum(SPLIT_SIZES)[:-1]]).tolist()))
_SIZE = dict(zip(SPLIT_NAMES, SPLIT_SIZES))

LANES = 128
VMEM_LIMIT = 56 * 1024 * 1024

ZM_GA, ZM_GBR, ZM_GC = 0, 1024, 2048
ZM_GU, ZM_GV = 3072, 3584
ZM_MQ, ZM_MK, ZM_MV, ZM_MO = 4096, 4608, 5120, 5632
ZM_NQ = 6144
ZM_KK = 6656
ZM_VV = 7168
ZM_W = 7424
KK_W = 4 * LANES
VV_W = 2 * LANES
BR_SEL, BR_WIN = 0, 1
MASK_CAUSAL, MASK_WINDOW_EDGE = 0, 1
FEAT_SEL = 0
FEAT_TERMS = 3
FEAT_HI = 32
FEAT_LO = 35
LOG2E = 1.4426950408889634
ZS_GATE = 0
ZS_MI = 24
ZS_MF = 28
ZS_W = 128

TN_ADA = 1536
TM_PROJ = 1024
TN_PROJ = 3712
ML_CHUNK = 128
CONV_HALO = 16
TQ = 256
TK = 256
SCORE_LOOKAHEAD = 3
SEL_TILES = 4
VT_ROWS = NSA_DH + 16
TM_MERGE = 512
TM_MLP = 512
FF_CHUNK = 1024


def _dot(a, b):
    return jnp.dot(a, b, preferred_element_type=F32)


def _split3(x):
    x1 = x.astype(BF16)
    r1 = x - x1.astype(F32)
    x2 = r1.astype(BF16)
    x3 = (r1 - x2.astype(F32)).astype(BF16)
    return x1, x2, x3


def _cparams(sem):
    return pltpu.CompilerParams(dimension_semantics=sem, vmem_limit_bytes=VMEM_LIMIT)


def _ada_kernel(c_ref, w_ref, b_ref, o_ref):
    c = c_ref[...]
    cond = c * jax.nn.sigmoid(c)
    c1, c2, c3 = _split3(cond)
    w1, w2, w3 = _split3(w_ref[0])
    acc = _dot(c1, w1) + (_dot(c1, w2) + _dot(c2, w1)) + (_dot(c1, w3) + _dot(c2, w2) + _dot(c3, w1))
    o_ref[0] = acc + b_ref[0]


def _ada(c, w_ada, b_ada):
    L, D, N = w_ada.shape
    B = c.shape[0]
    tn = TN_ADA
    return pl.pallas_call(
        _ada_kernel,
        grid=(L, N // tn),
        in_specs=[pl.BlockSpec((B, D), lambda l, j: (0, 0)),
                  pl.BlockSpec((1, D, tn), lambda l, j: (l, 0, j)),
                  pl.BlockSpec((1, 1, tn), lambda l, j: (l, 0, j))],
        out_specs=pl.BlockSpec((1, B, tn), lambda l, j: (l, 0, j)),
        out_shape=jax.ShapeDtypeStruct((L, B, N), F32),
        compiler_params=_cparams(("arbitrary", "arbitrary")),
        name="ada_mod",
    )(c, w_ada, b_ada.reshape(L, 1, N))


def _modulated_norm(x, g, sc, sh):
    y = x * lax.rsqrt(jnp.mean(x * x, axis=-1, keepdims=True) + EPS)
    return (y * g) * (1.0 + sc) + sh


def _inproj_kernel(x_ref, g_ref, sc_ref, sh_ref, wm_ref, wc_ref, ws_ref, zm_ref, zc_ref, zs_ref, h_ref):
    @pl.when(pl.program_id(1) == 0)
    def _():
        h = _modulated_norm(x_ref[...], g_ref[...], sc_ref[0], sh_ref[0]).astype(BF16)
        h_ref[...] = h
        zc = _dot(h, wc_ref[...])
        for part in range(zc_ref.shape[0]):
            zc_ref[part] = zc[:, part * LANES:(part + 1) * LANES]
        zs_ref[...] = _dot(h, ws_ref[...])

    zm_ref[...] = _dot(h_ref[...], wm_ref[...]).astype(BF16)


def _inproj(x2, g, sc, sh, wm, wc, ws, S):
    M, D = x2.shape
    tm, tn = min(TM_PROJ, S), TN_PROJ
    per_b = S // tm
    return pl.pallas_call(
        _inproj_kernel,
        grid=(M // tm, ZM_W // tn),
        in_specs=[pl.BlockSpec((tm, D), lambda i, j: (i, 0)),
                  pl.BlockSpec((1, D), lambda i, j: (0, 0)),
                  pl.BlockSpec((1, 1, D), lambda i, j: (i // per_b, 0, 0)),
                  pl.BlockSpec((1, 1, D), lambda i, j: (i // per_b, 0, 0)),
                  pl.BlockSpec((D, tn), lambda i, j: (0, j)),
                  pl.BlockSpec((D, 2 * NSA_KV), lambda i, j: (0, 0)),
                  pl.BlockSpec((D, ZS_W), lambda i, j: (0, 0))],
        out_specs=[pl.BlockSpec((tm, tn), lambda i, j: (i, j)),
                   pl.BlockSpec((2 * NSA_KV // LANES, tm, LANES), lambda i, j: (0, i, 0)),
                   pl.BlockSpec((tm, ZS_W), lambda i, j: (i, 0))],
        out_shape=[jax.ShapeDtypeStruct((M, ZM_W), BF16),
                   jax.ShapeDtypeStruct((2 * NSA_KV // LANES, M, LANES), F32),
                   jax.ShapeDtypeStruct((M, ZS_W), F32)],
        scratch_shapes=[pltpu.VMEM((tm, D), BF16)],
        compiler_params=_cparams(("arbitrary", "arbitrary")),
        name="in_proj",
    )(x2, g, sc, sh, wm, wc, ws)


def _gmlp_rows(u_ref, v_ref, lng_ref, lnb_ref, ws_ref, bst_ref, o_ref):
    ts = u_ref.shape[0]
    dg = GM_W // GM_GROUPS
    row = lax.broadcasted_iota(jnp.int32, (GM_CHUNK, GM_CHUNK), 0)
    col = lax.broadcasted_iota(jnp.int32, (GM_CHUNK, GM_CHUNK), 1)
    ws = [jnp.where(row >= col, ws_ref[g], 0.0).astype(BF16) for g in range(GM_GROUPS)]
    lng = lng_ref[...]
    lnb = lnb_ref[...]
    for c in range(ts // GM_CHUNK):
        r0 = c * GM_CHUNK
        u = jax.nn.gelu(u_ref[r0:r0 + GM_CHUNK, :].astype(F32))
        v = jax.nn.gelu(v_ref[r0:r0 + GM_CHUNK, :].astype(F32))
        mu = jnp.mean(v, axis=-1, keepdims=True)
        var = jnp.mean(jnp.square(v - mu), axis=-1, keepdims=True)
        vb = ((v - mu) * lax.rsqrt(var + EPS) * lng + lnb).astype(BF16)
        for g in range(GM_GROUPS):
            mixed = _dot(ws[g], vb[:, g * dg:(g + 1) * dg]) + bst_ref[:, g:g + 1]
            o_ref[r0:r0 + GM_CHUNK, g * dg:(g + 1) * dg] = (u[:, g * dg:(g + 1) * dg] * mixed).astype(BF16)


def _log_sigmoid(x):
    return jnp.minimum(x, 0.0) - jnp.log1p(jnp.exp(-jnp.abs(x)))


def _conv_silu(x_ext, w, b):
    n = x_ext.shape[0] - CONV_HALO
    y = b
    for j in range(CONV_K):
        sh = CONV_K - 1 - j
        xs = x_ext if sh == 0 else pltpu.roll(x_ext, sh, axis=0)
        y = y + xs[CONV_HALO:CONV_HALO + n] * w[j:j + 1]
    return y * jax.nn.sigmoid(y)


def _mlstm_chunk_stages(q_ref, k_ref, v_ref, o_ref, zs_ref, cw_ref, cb_ref, gb_ref, ng_ref, out_ref, c_ref, m_ref,
                        row0, t_abs, has_history):
    L = ML_CHUNK
    rows = slice(row0, row0 + L)
    hs = [slice(h * ML_DH, (h + 1) * ML_DH) for h in range(ML_H)]
    heads = range(ML_H)
    st = {}

    def conv_rows(src, c0):
        cur = src[pl.ds(pl.multiple_of(t_abs, L), L), :].astype(F32)
        halo0 = pl.multiple_of(jnp.maximum(t_abs - CONV_HALO, 0), CONV_HALO)
        halo = jnp.where(has_history, src[pl.ds(halo0, CONV_HALO), :].astype(F32), 0.0)
        return _conv_silu(jnp.concatenate([halo, cur], axis=0), cw_ref[:, c0:c0 + ML_W], cb_ref[:, c0:c0 + ML_W])

    def projections():
        row = lax.broadcasted_iota(jnp.int32, (L, L), 0)
        col = lax.broadcasted_iota(jnp.int32, (L, L), 1)
        tril_b = jnp.where(row >= col, 1.0, 0.0).astype(BF16)
        triu_b = jnp.where(row <= col, 1.0, 0.0).astype(BF16)
        gi = zs_ref[rows, :] + gb_ref[...]
        gi_t = gi.T
        c1, c2, c3 = _split3(_log_sigmoid(gi) * LOG2E)
        b_cols = _dot(tril_b, c1) + _dot(tril_b, c2) + _dot(tril_b, c3)
        r1, r2, r3 = _split3(_log_sigmoid(gi_t[ZS_MI:ZS_MI + 2 * ML_H]) * LOG2E)
        b_rows = _dot(r1, triu_b) + _dot(r2, triu_b) + _dot(r3, triu_b)
        st['gi'] = gi * LOG2E
        st['r_rows'] = gi_t[ZS_MI:ZS_MI + ML_H] * LOG2E - b_rows[ML_H:2 * ML_H]
        st['b_at_i'] = pltpu.roll(b_cols, ZS_W - (ZS_MF - ZS_MI), axis=1)
        q_all = conv_rows(q_ref, 0).astype(BF16)
        k_all = conv_rows(k_ref, ML_W) * (ML_DH ** -0.5)
        st['q'] = [q_all[:, hs[h]] for h in heads]
        st['k_t'] = [k_all[:, hs[h]].T for h in heads]
        st['cmat'] = [c_ref[h] for h in heads]
        st['qk'] = [_dot(st['q'][h], st['k_t'][h].astype(BF16)) for h in heads]
        st['qc'] = [_dot(st['q'][h], st['cmat'][h].astype(BF16)) for h in heads]
        ones_blk = jnp.ones((L, ML_DH), BF16)
        st['v_aug'] = [jnp.concatenate([v_ref[rows, hs[h]], ones_blk], axis=1) for h in heads]

    def memory_update():
        tril = lax.broadcasted_iota(jnp.int32, (L, L), 0) >= lax.broadcasted_iota(jnp.int32, (L, L), 1)
        time_row = lax.broadcasted_iota(jnp.int32, (L, ZS_W), 0)
        m_row = m_ref[...]
        u = st['gi'] - st['b_at_i']
        shift = 1
        while shift < L:
            u = jnp.maximum(u, jnp.where(time_row >= shift, pltpu.roll(u, shift, axis=0), NEG))
            shift *= 2
        u = jnp.maximum(u, m_row)
        m_cols = st['b_at_i'] + u
        u_last = u[L - 1:L, :]
        w_prev_row = jnp.exp2(m_row - u_last)
        st['m_cols'], st['m_prev'], st['u_rep'], st['gate'] = m_cols, m_row, [], []
        for h in heads:
            lane_h = ZS_MI + h
            r_row = st['r_rows'][h:h + 1, :]
            u_rep = jnp.broadcast_to(u[:, lane_h:lane_h + 1], (L, ML_DH))
            st['u_rep'].append(u_rep)
            st['gate'].append(jnp.exp2(jnp.where(tril, r_row - u_rep, NEG)))
            w_s = jnp.exp2(r_row - u_last[:, lane_h:lane_h + 1])
            k_w = (st['k_t'][h] * w_s).astype(BF16)
            c_ref[h] = w_prev_row[:, lane_h:lane_h + 1] * st['cmat'][h] + _dot(k_w, st['v_aug'][h])
        m_ref[...] = m_cols[L - 1:L, :]

    def read_out():
        st['num_aug'] = []
        for h in heads:
            lane_h = ZS_MI + h
            w_inter = jnp.exp2(st['m_prev'][:, lane_h:lane_h + 1] - st['u_rep'][h])
            st['num_aug'].append(_dot((st['qk'][h] * st['gate'][h]).astype(BF16), st['v_aug'][h])
                                 + jnp.concatenate([w_inter, w_inter], axis=1) * st['qc'][h])

    def normalise_and_store():
        for h in heads:
            lane_h = ZS_MI + h
            num_aug = st['num_aug'][h]
            m_rep = jnp.broadcast_to(st['m_cols'][:, lane_h:lane_h + 1], (L, ML_DH))
            hval = num_aug[:, :ML_DH] / jnp.maximum(jnp.abs(num_aug[:, ML_DH:]), jnp.exp2(-m_rep))
            mu = jnp.mean(hval, axis=-1, keepdims=True)
            var = jnp.mean(jnp.square(hval - mu), axis=-1, keepdims=True)
            hn = (hval - mu) * lax.rsqrt(var + EPS) * ng_ref[:, hs[h]]
            og = jax.nn.sigmoid(o_ref[rows, hs[h]].astype(F32))
            out_ref[rows, hs[h]] = (og * hn).astype(BF16)

    return [projections, memory_update, read_out, normalise_and_store]


def _nsa_prep_kernel(x_ref, pe_ref, w1_ref, w2_ref, kk_ref, vv_ref, feat_ref, featc_ref,
                     kaug_ref, vt_ref, kcaug_ref, vct_ref):
    S = kk_ref.shape[0]
    n = S // CMP_STRIDE
    cw = 2 * NSA_KV
    half = CMP_BLOCK // 2
    acc_a = jnp.zeros((n, cw), F32)
    acc_b = jnp.zeros((n, cw), F32)
    for j in range(half):
        xj = jnp.concatenate([x_ref[part, pl.ds(j, n, stride=CMP_STRIDE), :] for part in range(cw // LANES)],
                             axis=1)
        acc_a = acc_a + _dot((xj + pe_ref[j:j + 1, :]).astype(BF16), w1_ref[j])
        acc_b = acc_b + _dot((xj + pe_ref[half + j:half + j + 1, :]).astype(BF16), w1_ref[half + j])
    pre = acc_a + pltpu.roll(acc_b, n - 1, axis=0)
    cmp = _dot(jax.nn.gelu(pre).astype(BF16), w2_ref[...])

    lane = lax.broadcasted_iota(jnp.int32, (1, LANES), 1)
    keep = [jnp.where(lane < NSA_DH, 1.0, 0.0).astype(BF16), jnp.where(lane >= NSA_DH, 1.0, 0.0).astype(BF16)]
    for g in range(NSA_G):
        kd = cmp[:, g * LANES:(g + 1) * LANES].astype(BF16)
        for e in range(2):
            kcaug_ref[0, g * 2 + e] = kd * keep[e] + featc_ref[e]
    vct_ref[0] = cmp[:, NSA_G * LANES:(NSA_G + 1) * LANES].T.astype(BF16)
    tb = min(TK, S)
    for br in range(2):
        for g in range(NSA_G):
            kd = kk_ref[:, (br * NSA_G + g) * LANES:(br * NSA_G + g + 1) * LANES]
            for e in range(2):
                kaug_ref[0, (br * NSA_G + g) * 2 + e] = kd * keep[e] + feat_ref[br * 2 + e]
        for c in range(S // tb):
            v_t = vv_ref[c * tb:(c + 1) * tb, br * LANES:(br + 1) * LANES].astype(F32).T.astype(BF16)
            for g in range(NSA_G):
                vt_ref[0, br, g, 0:NSA_DH, c * tb:(c + 1) * tb] = v_t[g * NSA_DH:(g + 1) * NSA_DH]
                vt_ref[0, br, g, NSA_DH:VT_ROWS, c * tb:(c + 1) * tb] = jnp.ones((VT_ROWS - NSA_DH, tb), BF16)


def _blockdiag(blocks):
    n = len(blocks)
    rows = []
    for i, blk in enumerate(blocks):
        rows.append(jnp.concatenate([blk if j == i else jnp.zeros((blk.shape[0], blocks[j].shape[1]), blk.dtype)
                                     for j in range(n)], axis=1))
    return jnp.concatenate(rows, axis=0)


def _key_features(S):
    pos = np.arange(S)
    f = np.zeros((4, S, LANES), np.float32)
    for e in range(2):
        base = NSA_DH * (1 - e)
        f[e, pos, base + FEAT_SEL + pos // SEL_BLOCK] = 1.0
        for br in range(2):
            f[br * 2 + e, :, base + FEAT_HI:base + FEAT_HI + FEAT_TERMS] = (pos // 64)[:, None]
            f[br * 2 + e, :, base + FEAT_LO:base + FEAT_LO + FEAT_TERMS] = (pos % 64)[:, None]
    return jnp.asarray(f, dtype=BF16)


def _cmp_features(S):
    c = np.arange(S // CMP_STRIDE)
    hi = (c * CMP_STRIDE) // 64
    lo = c * CMP_STRIDE + (CMP_BLOCK - 1) * 0.5 - 64 * hi
    f = np.zeros((2, c.size, LANES), np.float32)
    for e in range(2):
        base = NSA_DH * (1 - e)
        f[e, :, base + FEAT_HI:base + FEAT_HI + FEAT_TERMS] = hi[:, None]
        f[e, :, base + FEAT_LO:base + FEAT_LO + FEAT_TERMS] = lo[:, None]
    return jnp.asarray(f, dtype=BF16)


def _bf16_terms(x):
    terms = []
    for _ in range(FEAT_TERMS):
        terms.append(float(np.asarray(x - sum(terms), dtype=BF16)))
    return terms


def _nsa_prep(zc, zm, pe_k, pe_v, phi_k1, phi_k2, phi_v1, phi_v2, S):
    M = zc.shape[1]
    B = M // S
    n = S // CMP_STRIDE
    cw = 2 * NSA_KV
    assert S // SEL_BLOCK <= FEAT_HI and S <= 64 * 64
    pe = jnp.concatenate([pe_k, pe_k, pe_v, pe_v], axis=1)
    k1 = phi_k1.reshape(CMP_BLOCK, NSA_DH, NSA_DH)
    v1 = phi_v1.reshape(CMP_BLOCK, NSA_DH, NSA_DH)
    blocks = jnp.stack([k1, k1, v1, v1], axis=1)
    w1 = jnp.einsum('jaxy,ab->jaxby', blocks, jnp.eye(4, dtype=F32)).reshape(CMP_BLOCK, cw, cw).astype(BF16)
    k2 = jnp.concatenate([phi_k2, phi_k2], axis=1)
    w2 = _blockdiag([k2, k2, phi_v2, phi_v2]).astype(BF16)

    def full(shape):
        return pl.BlockSpec(shape, lambda b: (0,) * len(shape))

    return pl.pallas_call(
        _nsa_prep_kernel,
        grid=(B,),
        in_specs=[pl.BlockSpec((cw // LANES, S, LANES), lambda b: (0, b, 0)),
                  full((CMP_BLOCK, cw)), full((CMP_BLOCK, cw, cw)), full((cw, 3 * LANES)),
                  pl.BlockSpec((S, KK_W), lambda b: (b, ZM_KK // KK_W)),
                  pl.BlockSpec((S, VV_W), lambda b: (b, ZM_VV // VV_W)),
                  full((4, S, LANES)), full((2, n, LANES))],
        out_specs=[pl.BlockSpec((1, 8, S, LANES), lambda b: (b, 0, 0, 0)),
                   pl.BlockSpec((1, 2, NSA_G, VT_ROWS, S), lambda b: (b, 0, 0, 0, 0)),
                   pl.BlockSpec((1, 4, n, LANES), lambda b: (b, 0, 0, 0)),
                   pl.BlockSpec((1, LANES, n), lambda b: (b, 0, 0))],
        out_shape=[jax.ShapeDtypeStruct((B, 8, S, LANES), BF16),
                   jax.ShapeDtypeStruct((B, 2, NSA_G, VT_ROWS, S), BF16),
                   jax.ShapeDtypeStruct((B, 4, n, LANES), BF16),
                   jax.ShapeDtypeStruct((B, LANES, n), BF16)],
        compiler_params=_cparams(("arbitrary",)),
        name="nsa_prep",
    )(zc, pe, w1, w2, zm, zm, _key_features(S), _cmp_features(S))


def _nsa_mlstm_kernel(q_ref, kaug_ref, vt_ref, kcaug_ref, vct_ref, zs_ref, ovt_ref,
                      mq_ref, mk_ref, mv_ref, mo_ref, cw_ref, cb_ref, gb_ref, ng_ref, out_ref, yb_ref,
                      qat_ref, qaw_ref, ocmp_ref, m_ref, acc_ref, mc_ref, mm_ref):
    S = kaug_ref.shape[2]
    tq = q_ref.shape[0]
    ncmp = kcaug_ref.shape[2]
    nsel = S // SEL_BLOCK
    n_pairs = NSA_H // 2
    qi = pl.program_id(1)
    t0 = pl.multiple_of(qi * tq, tq)
    t_row = t0 + lax.broadcasted_iota(jnp.int32, (1, tq), 1)

    @pl.when(qi == 0)
    def _():
        mc_ref[...] = jnp.zeros_like(mc_ref)
        mm_ref[...] = jnp.zeros_like(mm_ref)

    def mlstm_chunk(c):
        return _mlstm_chunk_stages(mq_ref, mk_ref, mv_ref, mo_ref, zs_ref, cw_ref, cb_ref, gb_ref, ng_ref, yb_ref,
                                   mc_ref, mm_ref, row0=c * ML_CHUNK, t_abs=t0 + c * ML_CHUNK,
                                   has_history=(qi > 0) if c == 0 else True)

    for k in range(NSA_H // 2):
        q_t = (q_ref[:, k * LANES:(k + 1) * LANES].astype(F32) * (NSA_DH ** -0.5 * LOG2E)).T.astype(BF16)
        for e in range(2):
            for dst in (qat_ref, qaw_ref):
                dst[(k // 2) * 2 + e, e * NSA_DH:(e + 1) * NSA_DH, (k % 2) * tq:(k % 2 + 1) * tq] = (
                    q_t[e * NSA_DH:(e + 1) * NSA_DH])
    feat_row = lax.broadcasted_iota(jnp.int32, (NSA_DH, 2 * tq), 0)
    second_head = lax.broadcasted_iota(jnp.int32, (NSA_DH, 2 * tq), 1) >= tq
    for pair in range(n_pairs):
        g, e = pair // 2, pair % 2
        terms = [_bf16_terms(LOG2E * 2.0 ** (-8.0 * (g * NSA_R + 2 * s + e + 1.0) / NSA_H)) for s in range(2)]
        feat = jnp.zeros((NSA_DH, 2 * tq), F32)
        for i in range(FEAT_TERMS):
            term = jnp.where(second_head, terms[1][i], terms[0][i])
            feat = jnp.where(feat_row == FEAT_HI + i, term * 64.0, jnp.where(feat_row == FEAT_LO + i, term, feat))
        qat_ref[pair, NSA_DH * (1 - e):NSA_DH * (2 - e), :] = feat.astype(BF16)
        qaw_ref[pair, NSA_DH * (1 - e):NSA_DH * (2 - e), :] = feat.astype(BF16)

    ki = lax.broadcasted_iota(jnp.int32, (TK, 2 * tq), 0)
    qu = lax.broadcasted_iota(jnp.int32, (TK, 2 * tq), 1)
    qu = jnp.where(qu >= tq, qu - tq, qu)
    tile_valid = {MASK_CAUSAL: ki <= qu,
                  MASK_WINDOW_EDGE: ki > qu}

    c_col = lax.broadcasted_iota(jnp.int32, (ncmp, 1), 0)
    t_row2 = jnp.concatenate([t_row, t_row], axis=1)
    valid_c = c_col * CMP_STRIDE + (CMP_BLOCK - 1) <= t_row2
    j_col = lax.broadcasted_iota(jnp.int32, (nsel, 1), 0)
    j_f = j_col.astype(F32)
    jt = t_row >> 6
    forced = jnp.logical_or(j_col == 0, jnp.logical_or(j_col == jt, j_col == jt - 1))
    future = j_col > jt

    sel_slots = slice(BR_SEL * n_pairs, (BR_SEL + 1) * n_pairs)
    m_ref[sel_slots] = jnp.full((n_pairs,) + m_ref.shape[1:], NEG, F32)
    acc_ref[sel_slots] = jnp.zeros((n_pairs,) + acc_ref.shape[1:], F32)

    def compressed_scores():
        return [_dot(kcaug_ref[0, pair], qaw_ref[pair]) for pair in range(n_pairs)]

    def compressed_branch_and_selection(scores_c):
        for g in range(NSA_G):
            psum = jnp.zeros((ncmp, tq), F32)
            for e in range(2):
                pair = g * 2 + e
                sc = jnp.where(valid_c, scores_c[pair], NEG)
                ex = jnp.where(valid_c, jnp.exp2(sc - jnp.max(sc, axis=0, keepdims=True)), 0.0)
                den = jnp.sum(ex, axis=0, keepdims=True)
                p = ex * (1.0 / jnp.where(den > 0.0, den, 1.0))
                psum = psum + p[:, :tq] + p[:, tq:]
                ocmp_ref[pair] = _dot(vct_ref[0, g * NSA_DH:(g + 1) * NSA_DH, :], p.astype(BF16))

            p_hi = psum.astype(BF16)
            p_lo = (psum - p_hi.astype(F32)).astype(BF16)
            imp = _dot(ovt_ref[...], p_hi) + _dot(ovt_ref[...], p_lo)
            penalty = jnp.where(forced, 0.0, NEG)
            val = jnp.where(future, NEG, jnp.where(forced, TAKEN, imp))
            for _ in range(min(TOP_N, nsel) - N_FORCED):
                best = jnp.max(val, axis=0, keepdims=True)
                first = jnp.min(jnp.where(val == best, j_f, float(nsel)), axis=0, keepdims=True)
                hit = j_f == first
                penalty = jnp.where(hit, 0.0, penalty)
                val = jnp.where(hit, TAKEN, val)
            penalty = penalty.astype(BF16)
            for e in range(2):
                first = NSA_DH * (1 - e) + FEAT_SEL
                qat_ref[g * 2 + e, first:first + nsel, :] = jnp.concatenate([penalty, penalty], axis=1)

    def attend(br, tiles, interleave=(), first_tile_starts_state=False):
        stages = [(pl.multiple_of(kb * TK, TK), mask, pair) for kb, mask in tiles for pair in range(n_pairs)]

        def scores(stage):
            k0, _, pair = stage
            queries = qat_ref if br == BR_SEL else qaw_ref
            return _dot(kaug_ref[0, br * n_pairs + pair, pl.ds(k0, TK), :], queries[pair])

        s_queue = [scores(st) for st in stages[:SCORE_LOOKAHEAD]]
        pending = list(interleave)
        if pending:
            pending.pop(0)()
        for i, (k0, mask, pair) in enumerate(stages):
            s_t = s_queue.pop(0)
            if i + SCORE_LOOKAHEAD < len(stages):
                s_queue.append(scores(stages[i + SCORE_LOOKAHEAD]))
            slot = br * n_pairs + pair
            if mask is not None:
                s_t = jnp.where(tile_valid[mask], s_t, NEG)
            starts_state = first_tile_starts_state and i < n_pairs
            m_new = jnp.max(s_t, axis=0, keepdims=True)
            if not starts_state:
                m_old = m_ref[slot]
                m_new = jnp.maximum(m_old, m_new)
            p = jnp.exp2(s_t - m_new).astype(BF16)
            update = _dot(vt_ref[0, br, pair // 2, :, pl.ds(k0, TK)], p)
            acc_ref[slot] = update if starts_state else jnp.exp2(m_old - m_new) * acc_ref[slot] + update
            m_ref[slot] = m_new
            if pending and i % 2 == 1:
                pending.pop(0)()
        for emit in pending:
            emit()

    n_back = WINDOW // TK
    for n_behind in range(n_back + 1):
        tiles = [(qi, MASK_CAUSAL)] + [(qi - d, MASK_WINDOW_EDGE if d == n_back else None)
                                        for d in range(1, n_behind + 1)]

        @pl.when(qi >= n_back if n_behind == n_back else qi == n_behind)
        def _(tiles=tiles):
            scores_c = compressed_scores()
            project, *rest = mlstm_chunk(0)

            def selection_and_projections():
                compressed_branch_and_selection(scores_c)
                project()

            attend(BR_WIN, tiles, [selection_and_projections] + rest, first_tile_starts_state=True)

    def sel_body(i, carry):
        attend(BR_SEL, [(SEL_TILES * i + j, None) for j in range(SEL_TILES)])
        return carry

    lax.fori_loop(0, qi // SEL_TILES, sel_body, 0)
    for n_full in range(SEL_TILES):
        @pl.when(qi % SEL_TILES == n_full)
        def _(n_full=n_full):
            attend(BR_SEL, [(qi - n_full + j, None) for j in range(n_full)] + [(qi, MASK_CAUSAL)], mlstm_chunk(1))

    gates_t = jax.nn.sigmoid(zs_ref[...]).T
    for hp in range(NSA_H // 2):
        comb = []
        for e in range(2):
            h = 2 * hp + e
            c0 = ZS_GATE + NSA_NB * h
            pair = (h // NSA_R) * 2 + e
            cols = slice(((h % NSA_R) // 2) * tq, ((h % NSA_R) // 2 + 1) * tq)
            sel, win = BR_SEL * n_pairs + pair, BR_WIN * n_pairs + pair
            o_sel = acc_ref[sel, 0:NSA_DH, cols] * (1.0 / acc_ref[sel, NSA_DH:NSA_DH + 1, cols])
            o_win = acc_ref[win, 0:NSA_DH, cols] * (1.0 / acc_ref[win, NSA_DH:NSA_DH + 1, cols])
            comb.append(gates_t[c0:c0 + 1] * ocmp_ref[pair, :, cols] + gates_t[c0 + 1:c0 + 2] * o_sel
                        + gates_t[c0 + 2:c0 + 3] * o_win)
        out_ref[:, hp * LANES:(hp + 1) * LANES] = jnp.concatenate(comb, axis=0).T.astype(BF16)


def _overlap_t(S):
    c = np.arange(S // CMP_STRIDE)[None, :]
    j = np.arange(S // SEL_BLOCK)[:, None]
    ov = (c * CMP_STRIDE <= j * SEL_BLOCK + SEL_BLOCK - 1) & (c * CMP_STRIDE + CMP_BLOCK - 1 >= j * SEL_BLOCK)
    ov &= c < (S - CMP_BLOCK) // CMP_STRIDE + 1
    return jnp.asarray(ov.astype(np.float32), dtype=BF16)


def _nsa_mlstm(zm, zs, kaug, vt, kcaug, vct, conv_w, conv_b, gate_b, norm_g, S):
    M = zm.shape[0]
    B = M // S
    tq = min(TQ, S)
    assert tq == TK and WINDOW % TK == 0 and S % TK == 0 and tq == 2 * ML_CHUNK and ML_CHUNK == ML_DH
    nq = S // tq
    n = S // CMP_STRIDE
    nsel = S // SEL_BLOCK
    gb_row = jnp.zeros((1, ZS_W), F32).at[0, ZS_MI:ZS_MI + 2 * ML_H].set(gate_b)

    def tile(off):
        return pl.BlockSpec((tq, ML_W), lambda b, i: (b * nq + i, off // ML_W))

    def row(off):
        return pl.BlockSpec((S, ML_W), lambda b, i: (b, off // ML_W))

    def full(shape):
        return pl.BlockSpec(shape, lambda b, i: (0,) * len(shape))

    return pl.pallas_call(
        _nsa_mlstm_kernel,
        grid=(B, nq),
        in_specs=[tile(ZM_NQ),
                  pl.BlockSpec((1, 8, S, LANES), lambda b, i: (b, 0, 0, 0)),
                  pl.BlockSpec((1, 2, NSA_G, VT_ROWS, S), lambda b, i: (b, 0, 0, 0, 0)),
                  pl.BlockSpec((1, 4, n, LANES), lambda b, i: (b, 0, 0, 0)),
                  pl.BlockSpec((1, LANES, n), lambda b, i: (b, 0, 0)),
                  pl.BlockSpec((tq, ZS_W), lambda b, i: (b * nq + i, 0)),
                  full((nsel, n)),
                  row(ZM_MQ), row(ZM_MK), tile(ZM_MV), tile(ZM_MO),
                  full((CONV_K, 2 * ML_W)), full((1, 2 * ML_W)), full((1, ZS_W)), full((1, ML_W))],
        out_specs=[pl.BlockSpec((tq, NSA_W), lambda b, i: (b * nq + i, 0)),
                   pl.BlockSpec((tq, ML_W), lambda b, i: (b * nq + i, 0))],
        out_shape=[jax.ShapeDtypeStruct((M, NSA_W), BF16), jax.ShapeDtypeStruct((M, ML_W), BF16)],
        scratch_shapes=[pltpu.VMEM((NSA_H // 2, LANES, 2 * tq), BF16),
                        pltpu.VMEM((NSA_H // 2, LANES, 2 * tq), BF16),
                        pltpu.VMEM((NSA_H // 2, NSA_DH, 2 * tq), F32),
                        pltpu.VMEM((NSA_H, 1, 2 * tq), F32),
                        pltpu.VMEM((NSA_H, VT_ROWS, 2 * tq), F32),
                        pltpu.VMEM((ML_H, ML_DH, 2 * ML_DH), F32),
                        pltpu.VMEM((1, ZS_W), F32)],
        compiler_params=_cparams(("arbitrary", "arbitrary")),
        name="nsa_mlstm",
    )(zm, kaug, vt, kcaug, vct, zs, _overlap_t(S), zm, zm, zm, zm,
      conv_w, conv_b.reshape(1, 2 * ML_W), gb_row, norm_g.reshape(1, ML_W))


def _merge_kernel(gu0_ref, gv0_ref, gu_ref, gv_ref, lng_ref, lnb_ref, ws_ref, bst_ref, b_ref, c_ref,
                  ga_ref, gb_ref, gc_ref, x_ref, gt_ref, wa_ref, wb_ref, wc_ref, wo_ref, o_ref, a_ref):
    i = pl.program_id(0)

    @pl.when(i == 0)
    def _():
        _gmlp_rows(gu0_ref, gv0_ref, lng_ref, lnb_ref, ws_ref, bst_ref, a_ref.at[0])

    a = a_ref[i % 2]
    merged = (jax.nn.sigmoid(ga_ref[...].astype(F32)) * _dot(a, wa_ref[...])
              + jax.nn.sigmoid(gb_ref[...].astype(F32)) * _dot(b_ref[...], wb_ref[...])
              + jax.nn.sigmoid(gc_ref[...].astype(F32)) * _dot(c_ref[...], wc_ref[...]))
    o_ref[...] = x_ref[...] + gt_ref[0] * _dot(merged.astype(BF16), wo_ref[...])
    _gmlp_rows(gu_ref, gv_ref, lng_ref, lnb_ref, ws_ref, bst_ref, a_ref.at[(i + 1) % 2])


def _merge(yb, yc, zm, x2, gt, ln_g, ln_b, ws, bs, wa, wb, wc, wo, S):
    M, D = x2.shape
    tm = min(TM_MERGE, S)
    per_b = S // tm

    def rows(w, jcol=0):
        return pl.BlockSpec((tm, w), lambda i: (i, jcol))

    def next_rows(jcol):
        return pl.BlockSpec((tm, GM_W), lambda i: (jnp.minimum(i + 1, M // tm - 1), jcol))

    def full(shape):
        return pl.BlockSpec(shape, lambda i: (0,) * len(shape))

    return pl.pallas_call(
        _merge_kernel,
        grid=(M // tm,),
        in_specs=[rows(GM_W, ZM_GU // GM_W), rows(GM_W, ZM_GV // GM_W),
                  next_rows(ZM_GU // GM_W), next_rows(ZM_GV // GM_W), full((1, GM_W)), full((1, GM_W)),
                  full((GM_GROUPS, GM_CHUNK, GM_CHUNK)), full((GM_CHUNK, GM_GROUPS)),
                  rows(ML_W), rows(NSA_W),
                  rows(D, ZM_GA // D), rows(D, ZM_GBR // D), rows(D, ZM_GC // D),
                  rows(D), pl.BlockSpec((1, 1, D), lambda i: (i // per_b, 0, 0)),
                  full((GM_W, D)), full((ML_W, D)), full((NSA_W, D)), full((D, D))],
        out_specs=rows(D),
        out_shape=jax.ShapeDtypeStruct((M, D), F32),
        scratch_shapes=[pltpu.VMEM((2, tm, GM_W), BF16)],
        compiler_params=_cparams(("arbitrary",)),
        name="gmlp_merge_out",
    )(zm, zm, zm, zm, ln_g.reshape(1, GM_W), ln_b.reshape(1, GM_W), ws, bs.T, yb, yc, zm, zm, zm, x2, gt, wa, wb, wc, wo)


def _mlp_kernel(x_ref, g_ref, sc_ref, sh_ref, gt_ref, w1_ref, w2_ref, gf_ref, o_ref, acc_ref, *, final_norm):
    x = x_ref[...]
    h = _modulated_norm(x, g_ref[...], sc_ref[0], sh_ref[0]).astype(BF16)
    for c in range(D_FF // FF_CHUNK):
        mid = jnp.square(jnp.maximum(_dot(h, w1_ref[:, c * FF_CHUNK:(c + 1) * FF_CHUNK]), 0.0)).astype(BF16)
        upd = _dot(mid, w2_ref[c * FF_CHUNK:(c + 1) * FF_CHUNK, :])
        if c == 0:
            acc_ref[...] = upd
        else:
            acc_ref[...] += upd
    y = x + gt_ref[0] * acc_ref[...]
    if final_norm:
        y = (y * lax.rsqrt(jnp.mean(y * y, axis=-1, keepdims=True) + EPS)) * gf_ref[...]
    o_ref[...] = y


def _mlp(x2, g, sc, sh, gt, w1, w2, g_final, S, final_norm):
    M, D = x2.shape
    tm = min(TM_MLP, S)
    per_b = S // tm
    mod = pl.BlockSpec((1, 1, D), lambda i: (i // per_b, 0, 0))
    return pl.pallas_call(
        functools.partial(_mlp_kernel, final_norm=final_norm),
        grid=(M // tm,),
        in_specs=[pl.BlockSpec((tm, D), lambda i: (i, 0)),
                  pl.BlockSpec((1, D), lambda i: (0, 0)),
                  mod, mod, mod,
                  pl.BlockSpec((D, D_FF), lambda i: (0, 0)),
                  pl.BlockSpec((D_FF, D), lambda i: (0, 0)),
                  pl.BlockSpec((1, D), lambda i: (0, 0))],
        out_specs=pl.BlockSpec((tm, D), lambda i: (i, 0)),
        out_shape=jax.ShapeDtypeStruct((M, D), F32),
        scratch_shapes=[pltpu.VMEM((tm, D), F32)],
        compiler_params=_cparams(("arbitrary",)),
        name="relu2_mlp",
    )(x2, g, sc, sh, gt, w1, w2, g_final)


def _w_cols(w, name):
    return w[:, _OFFS[name]:_OFFS[name] + _SIZE[name]]


def _dup_heads(w):
    parts = []
    for g in range(NSA_G):
        blk = w[:, g * NSA_DH:(g + 1) * NSA_DH]
        parts += [blk, blk]
    return jnp.concatenate(parts, axis=1)


def _proj_weights(w):
    D = w.shape[0]
    wm = jnp.concatenate([_w_cols(w, n) for n in ('ga', 'gbr', 'gc', 'gu', 'gv', 'mq', 'mk', 'mv', 'mo', 'nq')]
                         + [_dup_heads(_w_cols(w, 'nks')), _dup_heads(_w_cols(w, 'nkw')),
                            _w_cols(w, 'nvs'), _w_cols(w, 'nvw')], axis=1)
    wc = jnp.concatenate([_w_cols(w, 'nkc'), _w_cols(w, 'nvc')], axis=1)
    ws = jnp.concatenate([_w_cols(w, 'ngate'), _w_cols(w, 'mi'), _w_cols(w, 'mf'),
                          jnp.zeros((D, ZS_W - NSA_H * NSA_NB - 2 * ML_H), w.dtype)], axis=1)
    return wm, wc, ws


def kernel(x, c, g_norm1, g_norm2, w_ada, b_ada, w_in, gm_ln_g, gm_ln_b, gm_ws, gm_bs, ml_conv_w, ml_conv_b,
           ml_gate_b, ml_norm_g, nsa_pe_k, nsa_pe_v, nsa_phi_k1, nsa_phi_k2, nsa_phi_v1, nsa_phi_v2,
           w_up_a, w_up_b, w_up_c, w_out, w_mlp1, w_mlp2, g_final):
    B, S, D = x.shape
    depth = w_in.shape[0]
    M = B * S
    mod = _ada(c, w_ada, b_ada)
    x2 = x.reshape(M, D)
    w_in, w_up_a, w_up_b, w_up_c, w_out, w_mlp1, w_mlp2 = [
        w.astype(BF16) for w in (w_in, w_up_a, w_up_b, w_up_c, w_out, w_mlp1, w_mlp2)]
    for l in range(depth):
        sh1, sc1, gt1, sh2, sc2, gt2 = [mod[l, :, i * D:(i + 1) * D].reshape(B, 1, D) for i in range(6)]
        wm, wc, ws = _proj_weights(w_in[l])
        zm, zc, zs = _inproj(x2, g_norm1[l].reshape(1, D), sc1, sh1, wm, wc, ws, S)
        kaug, vt, kcaug, vct = _nsa_prep(zc, zm, nsa_pe_k[l], nsa_pe_v[l], nsa_phi_k1[l], nsa_phi_k2[l],
                                         nsa_phi_v1[l], nsa_phi_v2[l], S)
        yc, yb = _nsa_mlstm(zm, zs, kaug, vt, kcaug, vct, ml_conv_w[l], ml_conv_b[l], ml_gate_b[l], ml_norm_g[l], S)
        x2 = _merge(yb, yc, zm, x2, gt1, gm_ln_g[l], gm_ln_b[l], gm_ws[l], gm_bs[l], w_up_a[l], w_up_b[l],
                    w_up_c[l], w_out[l], S)
        x2 = _mlp(x2, g_norm2[l].reshape(1, D), sc2, sh2, gt2, w_mlp1[l], w_mlp2[l], g_final.reshape(1, D), S,
                  final_norm=(l == depth - 1))
    return x2.reshape(B, S, D)
```

```python
import functools

import numpy as np
import jax
import jax.numpy as jnp
from jax import lax
from jax.experimental import pallas as pl
from jax.experimental.pallas import tpu as pltpu

F32 = jnp.float32
BF16 = jnp.bfloat16

D_MODEL = 1024
GM_W = 512
GM_GROUPS = 4
GM_CHUNK = 128
ML_H = 4
ML_DH = 128
ML_W = ML_H * ML_DH
CONV_K = 4
NSA_H = 8
NSA_G = 2
NSA_R = NSA_H // NSA_G
NSA_DH = 64
NSA_W = NSA_H * NSA_DH
NSA_KV = NSA_G * NSA_DH
NSA_NB = 3
CMP_BLOCK = 32
CMP_STRIDE = 16
SEL_BLOCK = 64
TOP_N = 8
WINDOW = 512
D_FF = 4 * D_MODEL
EPS = 1e-6
NEG = -1e30
TAKEN = -3e38
N_FORCED = 3
SPLIT_SIZES = (GM_W, GM_W, ML_W, ML_W, ML_W, ML_W, ML_H, ML_H, NSA_W, NSA_KV, NSA_KV, NSA_KV, NSA_KV,
               NSA_KV, NSA_KV, NSA_H * NSA_NB, D_MODEL, D_MODEL, D_MODEL)
SPLIT_NAMES = ('gu', 'gv', 'mq', 'mk', 'mv', 'mo', 'mi', 'mf', 'nq', 'nkc', 'nvc', 'nks', 'nvs', 'nkw', 'nvw',
               'ngate', 'ga', 'gbr', 'gc')
_OFFS = dict(zip(SPLIT_NAMES, np.concatenate([[0], np.cumsum(SPLIT_SIZES)[:-1]]).tolist()))
_SIZE = dict(zip(SPLIT_NAMES, SPLIT_SIZES))

LANES = 128
VMEM_LIMIT = 56 * 1024 * 1024

ZM_GA, ZM_GBR, ZM_GC = 0, 1024, 2048
ZM_GU, ZM_GV = 3072, 3584
ZM_MQ, ZM_MK, ZM_MV, ZM_MO = 4096, 4608, 5120, 5632
ZM_NQ = 6144
ZM_KK = 6656
ZM_VV = 6912
ZM_W = 7168
KK_W = 2 * LANES
VV_W = 2 * LANES
BR_SEL, BR_WIN = 0, 1
MASK_CAUSAL, MASK_WINDOW_EDGE = 0, 1
FEAT_SEL = 0
FEAT_TERMS = 3
FEAT_HI = 32
FEAT_LO = 35
LOG2E = 1.4426950408889634
ZS_GATE = 0
ZS_MI = 24
ZS_MF = 28
ZS_W = 128

TN_ADA = 1536
TM_PROJ = 1024
TN_PROJ = 3584
ML_CHUNK = 128
CONV_HALO = 16
TQ = 256
TK = 256
SCORE_LOOKAHEAD = 3
SEL_TILES = 4
VT_ROWS = NSA_DH + 16
TM_MERGE = 512
TM_MLP = 512
FF_CHUNK = 1024


def _dot(a, b):
    return jnp.dot(a, b, preferred_element_type=F32)


def _split3(x):
    x1 = x.astype(BF16)
    r1 = x - x1.astype(F32)
    x2 = r1.astype(BF16)
    x3 = (r1 - x2.astype(F32)).astype(BF16)
    return x1, x2, x3


def _cparams(sem):
    return pltpu.CompilerParams(dimension_semantics=sem, vmem_limit_bytes=VMEM_LIMIT)


def _ada_kernel(c_ref, w_ref, b_ref, o_ref):
    c = c_ref[...]
    cond = c * jax.nn.sigmoid(c)
    c1, c2, c3 = _split3(cond)
    w1, w2, w3 = _split3(w_ref[0])
    acc = _dot(c1, w1) + (_dot(c1, w2) + _dot(c2, w1)) + (_dot(c1, w3) + _dot(c2, w2) + _dot(c3, w1))
    o_ref[0] = acc + b_ref[0]


def _ada(c, w_ada, b_ada):
    L, D, N = w_ada.shape
    B = c.shape[0]
    tn = TN_ADA
    return pl.pallas_call(
        _ada_kernel,
        grid=(L, N // tn),
        in_specs=[pl.BlockSpec((B, D), lambda l, j: (0, 0)),
                  pl.BlockSpec((1, D, tn), lambda l, j: (l, 0, j)),
                  pl.BlockSpec((1, 1, tn), lambda l, j: (l, 0, j))],
        out_specs=pl.BlockSpec((1, B, tn), lambda l, j: (l, 0, j)),
        out_shape=jax.ShapeDtypeStruct((L, B, N), F32),
        compiler_params=_cparams(("arbitrary", "arbitrary")),
        name="ada_mod",
    )(c, w_ada, b_ada.reshape(L, 1, N))


def _modulated_norm(x, g, sc, sh):
    y = x * lax.rsqrt(jnp.mean(x * x, axis=-1, keepdims=True) + EPS)
    return (y * g) * (1.0 + sc) + sh


def _inproj_kernel(x_ref, g_ref, sc_ref, sh_ref, wm_ref, wc_ref, ws_ref, zm_ref, zc_ref, zs_ref, h_ref):
    @pl.when(pl.program_id(1) == 0)
    def _():
        h = _modulated_norm(x_ref[...], g_ref[...], sc_ref[0], sh_ref[0]).astype(BF16)
        h_ref[...] = h
        zc = _dot(h, wc_ref[...])
        for part in range(zc_ref.shape[0]):
            zc_ref[part] = zc[:, part * LANES:(part + 1) * LANES]
        zs_ref[...] = _dot(h, ws_ref[...])

    zm_ref[...] = _dot(h_ref[...], wm_ref[...]).astype(BF16)


def _inproj(x2, g, sc, sh, wm, wc, ws, S):
    M, D = x2.shape
    tm, tn = min(TM_PROJ, S), TN_PROJ
    per_b = S // tm
    return pl.pallas_call(
        _inproj_kernel,
        grid=(M // tm, ZM_W // tn),
        in_specs=[pl.BlockSpec((tm, D), lambda i, j: (i, 0)),
                  pl.BlockSpec((1, D), lambda i, j: (0, 0)),
                  pl.BlockSpec((1, 1, D), lambda i, j: (i // per_b, 0, 0)),
                  pl.BlockSpec((1, 1, D), lambda i, j: (i // per_b, 0, 0)),
                  pl.BlockSpec((D, tn), lambda i, j: (0, j)),
                  pl.BlockSpec((D, 2 * NSA_KV), lambda i, j: (0, 0)),
                  pl.BlockSpec((D, ZS_W), lambda i, j: (0, 0))],
        out_specs=[pl.BlockSpec((tm, tn), lambda i, j: (i, j)),
                   pl.BlockSpec((2 * NSA_KV // LANES, tm, LANES), lambda i, j: (0, i, 0)),
                   pl.BlockSpec((tm, ZS_W), lambda i, j: (i, 0))],
        out_shape=[jax.ShapeDtypeStruct((M, ZM_W), BF16),
                   jax.ShapeDtypeStruct((2 * NSA_KV // LANES, M, LANES), F32),
                   jax.ShapeDtypeStruct((M, ZS_W), F32)],
        scratch_shapes=[pltpu.VMEM((tm, D), BF16)],
        compiler_params=_cparams(("arbitrary", "arbitrary")),
        name="in_proj",
    )(x2, g, sc, sh, wm, wc, ws)


def _gmlp_rows(u_ref, v_ref, lng_ref, lnb_ref, ws_ref, bst_ref, o_ref):
    ts = u_ref.shape[0]
    dg = GM_W // GM_GROUPS
    row = lax.broadcasted_iota(jnp.int32, (GM_CHUNK, GM_CHUNK), 0)
    col = lax.broadcasted_iota(jnp.int32, (GM_CHUNK, GM_CHUNK), 1)
    ws = [jnp.where(row >= col, ws_ref[g], 0.0).astype(BF16) for g in range(GM_GROUPS)]
    lng = lng_ref[...]
    lnb = lnb_ref[...]
    for c in range(ts // GM_CHUNK):
        r0 = c * GM_CHUNK
        u = jax.nn.gelu(u_ref[r0:r0 + GM_CHUNK, :].astype(F32))
        v = jax.nn.gelu(v_ref[r0:r0 + GM_CHUNK, :].astype(F32))
        mu = jnp.mean(v, axis=-1, keepdims=True)
        var = jnp.mean(jnp.square(v - mu), axis=-1, keepdims=True)
        vb = ((v - mu) * lax.rsqrt(var + EPS) * lng + lnb).astype(BF16)
        for g in range(GM_GROUPS):
            mixed = _dot(ws[g], vb[:, g * dg:(g + 1) * dg]) + bst_ref[:, g:g + 1]
            o_ref[r0:r0 + GM_CHUNK, g * dg:(g + 1) * dg] = (u[:, g * dg:(g + 1) * dg] * mixed).astype(BF16)


def _log_sigmoid(x):
    return jnp.minimum(x, 0.0) - jnp.log1p(jnp.exp(-jnp.abs(x)))


def _conv_silu(x_ext, w, b):
    n = x_ext.shape[0] - CONV_HALO
    y = b
    for j in range(CONV_K):
        sh = CONV_K - 1 - j
        xs = x_ext if sh == 0 else pltpu.roll(x_ext, sh, axis=0)
        y = y + xs[CONV_HALO:CONV_HALO + n] * w[j:j + 1]
    return y * jax.nn.sigmoid(y)


def _mlstm_chunk_stages(q_ref, k_ref, v_ref, o_ref, zs_ref, cw_ref, cb_ref, gb_ref, ng_ref, out_ref, c_ref, m_ref,
                        row0, t_abs, has_history):
    L = ML_CHUNK
    rows = slice(row0, row0 + L)
    hs = [slice(h * ML_DH, (h + 1) * ML_DH) for h in range(ML_H)]
    heads = range(ML_H)
    st = {}

    def conv_rows(src, c0):
        cur = src[pl.ds(pl.multiple_of(t_abs, L), L), :].astype(F32)
        halo0 = pl.multiple_of(jnp.maximum(t_abs - CONV_HALO, 0), CONV_HALO)
        halo = jnp.where(has_history, src[pl.ds(halo0, CONV_HALO), :].astype(F32), 0.0)
        return _conv_silu(jnp.concatenate([halo, cur], axis=0), cw_ref[:, c0:c0 + ML_W], cb_ref[:, c0:c0 + ML_W])

    def projections():
        row = lax.broadcasted_iota(jnp.int32, (L, L), 0)
        col = lax.broadcasted_iota(jnp.int32, (L, L), 1)
        tril_b = jnp.where(row >= col, 1.0, 0.0).astype(BF16)
        triu_b = jnp.where(row <= col, 1.0, 0.0).astype(BF16)
        gi = zs_ref[rows, :] + gb_ref[...]
        gi_t = gi.T
        c1, c2, c3 = _split3(_log_sigmoid(gi) * LOG2E)
        b_cols = _dot(tril_b, c1) + _dot(tril_b, c2) + _dot(tril_b, c3)
        r1, r2, r3 = _split3(_log_sigmoid(gi_t[ZS_MI:ZS_MI + 2 * ML_H]) * LOG2E)
        b_rows = _dot(r1, triu_b) + _dot(r2, triu_b) + _dot(r3, triu_b)
        st['gi'] = gi * LOG2E
        st['r_rows'] = gi_t[ZS_MI:ZS_MI + ML_H] * LOG2E - b_rows[ML_H:2 * ML_H]
        st['b_at_i'] = pltpu.roll(b_cols, ZS_W - (ZS_MF - ZS_MI), axis=1)
        q_all = conv_rows(q_ref, 0).astype(BF16)
        k_all = conv_rows(k_ref, ML_W) * (ML_DH ** -0.5)
        st['q'] = [q_all[:, hs[h]] for h in heads]
        st['k_t'] = [k_all[:, hs[h]].T for h in heads]
        st['cmat'] = [c_ref[h] for h in heads]
        st['qk'] = [_dot(st['q'][h], st['k_t'][h].astype(BF16)) for h in heads]
        st['qc'] = [_dot(st['q'][h], st['cmat'][h].astype(BF16)) for h in heads]
        ones_blk = jnp.ones((L, ML_DH), BF16)
        st['v_aug'] = [jnp.concatenate([v_ref[rows, hs[h]], ones_blk], axis=1) for h in heads]

    def memory_update():
        tril = lax.broadcasted_iota(jnp.int32, (L, L), 0) >= lax.broadcasted_iota(jnp.int32, (L, L), 1)
        time_row = lax.broadcasted_iota(jnp.int32, (L, ZS_W), 0)
        m_row = m_ref[...]
        u = st['gi'] - st['b_at_i']
        shift = 1
        while shift < L:
            u = jnp.maximum(u, jnp.where(time_row >= shift, pltpu.roll(u, shift, axis=0), NEG))
            shift *= 2
        u = jnp.maximum(u, m_row)
        m_cols = st['b_at_i'] + u
        u_last = u[L - 1:L, :]
        w_prev_row = jnp.exp2(m_row - u_last)
        st['m_cols'], st['m_prev'], st['u_rep'], st['gate'] = m_cols, m_row, [], []
        for h in heads:
            lane_h = ZS_MI + h
            r_row = st['r_rows'][h:h + 1, :]
            u_rep = jnp.broadcast_to(u[:, lane_h:lane_h + 1], (L, ML_DH))
            st['u_rep'].append(u_rep)
            st['gate'].append(jnp.exp2(jnp.where(tril, r_row - u_rep, NEG)))
            w_s = jnp.exp2(r_row - u_last[:, lane_h:lane_h + 1])
            k_w = (st['k_t'][h] * w_s).astype(BF16)
            c_ref[h] = w_prev_row[:, lane_h:lane_h + 1] * st['cmat'][h] + _dot(k_w, st['v_aug'][h])
        m_ref[...] = m_cols[L - 1:L, :]

    def read_out():
        st['num_aug'] = []
        for h in heads:
            lane_h = ZS_MI + h
            w_inter = jnp.exp2(st['m_prev'][:, lane_h:lane_h + 1] - st['u_rep'][h])
            st['num_aug'].append(_dot((st['qk'][h] * st['gate'][h]).astype(BF16), st['v_aug'][h])
                                 + jnp.concatenate([w_inter, w_inter], axis=1) * st['qc'][h])

    def normalise_and_store():
        for h in heads:
            lane_h = ZS_MI + h
            num_aug = st['num_aug'][h]
            m_rep = jnp.broadcast_to(st['m_cols'][:, lane_h:lane_h + 1], (L, ML_DH))
            hval = num_aug[:, :ML_DH] / jnp.maximum(jnp.abs(num_aug[:, ML_DH:]), jnp.exp2(-m_rep))
            mu = jnp.mean(hval, axis=-1, keepdims=True)
            var = jnp.mean(jnp.square(hval - mu), axis=-1, keepdims=True)
            hn = (hval - mu) * lax.rsqrt(var + EPS) * ng_ref[:, hs[h]]
            og = jax.nn.sigmoid(o_ref[rows, hs[h]].astype(F32))
            out_ref[rows, hs[h]] = (og * hn).astype(BF16)

    return [projections, memory_update, read_out, normalise_and_store]


def _nsa_prep_kernel(x_ref, pe_ref, w1_ref, w2_ref, kk_ref, vv_ref, feat_ref, featc_ref,
                     kaug_ref, vt_ref, kcaug_ref, vct_ref):
    S = kk_ref.shape[0]
    n = S // CMP_STRIDE
    cw = 2 * NSA_KV
    half = CMP_BLOCK // 2
    acc_a = jnp.zeros((n, cw), F32)
    acc_b = jnp.zeros((n, cw), F32)
    for j in range(half):
        xj = jnp.concatenate([x_ref[part, pl.ds(j, n, stride=CMP_STRIDE), :] for part in range(cw // LANES)],
                             axis=1)
        acc_a = acc_a + _dot((xj + pe_ref[j:j + 1, :]).astype(BF16), w1_ref[j])
        acc_b = acc_b + _dot((xj + pe_ref[half + j:half + j + 1, :]).astype(BF16), w1_ref[half + j])
    pre = acc_a + pltpu.roll(acc_b, n - 1, axis=0)
    cmp = _dot(jax.nn.gelu(pre).astype(BF16), w2_ref[...])

    lane = lax.broadcasted_iota(jnp.int32, (1, LANES), 1)
    keep = [jnp.where(lane < NSA_DH, 1.0, 0.0).astype(BF16), jnp.where(lane >= NSA_DH, 1.0, 0.0).astype(BF16)]
    for g in range(NSA_G):
        kd = cmp[:, g * LANES:(g + 1) * LANES].astype(BF16)
        for e in range(2):
            kcaug_ref[0, g * 2 + e] = kd * keep[e] + featc_ref[e]
    vct_ref[0] = cmp[:, NSA_G * LANES:(NSA_G + 1) * LANES].T.astype(BF16)
    tb = min(TK, S)
    for br in range(2):
        for c in range(S // tb):
            rows = slice(c * tb, (c + 1) * tb)
            k = kk_ref[rows, br * LANES:(br + 1) * LANES]
            k_swapped = pltpu.roll(k.astype(F32), NSA_DH, axis=1).astype(BF16)
            for g in range(NSA_G):
                for e in range(2):
                    kaug_ref[0, (br * NSA_G + g) * 2 + e, rows, :] = (
                        (k if g == e else k_swapped) * keep[e] + feat_ref[br * 2 + e, rows, :])
        for c in range(S // tb):
            v_t = vv_ref[c * tb:(c + 1) * tb, br * LANES:(br + 1) * LANES].astype(F32).T.astype(BF16)
            for g in range(NSA_G):
                vt_ref[0, br, g, 0:NSA_DH, c * tb:(c + 1) * tb] = v_t[g * NSA_DH:(g + 1) * NSA_DH]
                vt_ref[0, br, g, NSA_DH:VT_ROWS, c * tb:(c + 1) * tb] = jnp.ones((VT_ROWS - NSA_DH, tb), BF16)


def _blockdiag(blocks):
    n = len(blocks)
    rows = []
    for i, blk in enumerate(blocks):
        rows.append(jnp.concatenate([blk if j == i else jnp.zeros((blk.shape[0], blocks[j].shape[1]), blk.dtype)
                                     for j in range(n)], axis=1))
    return jnp.concatenate(rows, axis=0)


def _key_features(S):
    pos = np.arange(S)
    f = np.zeros((4, S, LANES), np.float32)
    for e in range(2):
        base = NSA_DH * (1 - e)
        f[e, pos, base + FEAT_SEL + pos // SEL_BLOCK] = 1.0
        for br in range(2):
            f[br * 2 + e, :, base + FEAT_HI:base + FEAT_HI + FEAT_TERMS] = (pos // 64)[:, None]
            f[br * 2 + e, :, base + FEAT_LO:base + FEAT_LO + FEAT_TERMS] = (pos % 64)[:, None]
    return jnp.asarray(f, dtype=BF16)


def _cmp_features(S):
    c = np.arange(S // CMP_STRIDE)
    hi = (c * CMP_STRIDE) // 64
    lo = c * CMP_STRIDE + (CMP_BLOCK - 1) * 0.5 - 64 * hi
    f = np.zeros((2, c.size, LANES), np.float32)
    for e in range(2):
        base = NSA_DH * (1 - e)
        f[e, :, base + FEAT_HI:base + FEAT_HI + FEAT_TERMS] = hi[:, None]
        f[e, :, base + FEAT_LO:base + FEAT_LO + FEAT_TERMS] = lo[:, None]
    return jnp.asarray(f, dtype=BF16)


def _bf16_terms(x):
    terms = []
    for _ in range(FEAT_TERMS):
        terms.append(float(np.asarray(x - sum(terms), dtype=BF16)))
    return terms


def _nsa_prep(zc, zm, pe_k, pe_v, phi_k1, phi_k2, phi_v1, phi_v2, S):
    M = zc.shape[1]
    B = M // S
    n = S // CMP_STRIDE
    cw = 2 * NSA_KV
    assert S // SEL_BLOCK <= FEAT_HI and S <= 64 * 64
    pe = jnp.concatenate([pe_k, pe_k, pe_v, pe_v], axis=1)
    k1 = phi_k1.reshape(CMP_BLOCK, NSA_DH, NSA_DH)
    v1 = phi_v1.reshape(CMP_BLOCK, NSA_DH, NSA_DH)
    blocks = jnp.stack([k1, k1, v1, v1], axis=1)
    w1 = jnp.einsum('jaxy,ab->jaxby', blocks, jnp.eye(4, dtype=F32)).reshape(CMP_BLOCK, cw, cw).astype(BF16)
    k2 = jnp.concatenate([phi_k2, phi_k2], axis=1)
    w2 = _blockdiag([k2, k2, phi_v2, phi_v2]).astype(BF16)

    def full(shape):
        return pl.BlockSpec(shape, lambda b: (0,) * len(shape))

    return pl.pallas_call(
        _nsa_prep_kernel,
        grid=(B,),
        in_specs=[pl.BlockSpec((cw // LANES, S, LANES), lambda b: (0, b, 0)),
                  full((CMP_BLOCK, cw)), full((CMP_BLOCK, cw, cw)), full((cw, 3 * LANES)),
                  pl.BlockSpec((S, KK_W), lambda b: (b, ZM_KK // KK_W)),
                  pl.BlockSpec((S, VV_W), lambda b: (b, ZM_VV // VV_W)),
                  full((4, S, LANES)), full((2, n, LANES))],
        out_specs=[pl.BlockSpec((1, 8, S, LANES), lambda b: (b, 0, 0, 0)),
                   pl.BlockSpec((1, 2, NSA_G, VT_ROWS, S), lambda b: (b, 0, 0, 0, 0)),
                   pl.BlockSpec((1, 4, n, LANES), lambda b: (b, 0, 0, 0)),
                   pl.BlockSpec((1, LANES, n), lambda b: (b, 0, 0))],
        out_shape=[jax.ShapeDtypeStruct((B, 8, S, LANES), BF16),
                   jax.ShapeDtypeStruct((B, 2, NSA_G, VT_ROWS, S), BF16),
                   jax.ShapeDtypeStruct((B, 4, n, LANES), BF16),
                   jax.ShapeDtypeStruct((B, LANES, n), BF16)],
        compiler_params=_cparams(("arbitrary",)),
        name="nsa_prep",
    )(zc, pe, w1, w2, zm, zm, _key_features(S), _cmp_features(S))


def _nsa_mlstm_kernel(q_ref, kaug_ref, vt_ref, kcaug_ref, vct_ref, zs_ref, ovt_ref,
                      mq_ref, mk_ref, mv_ref, mo_ref, cw_ref, cb_ref, gb_ref, ng_ref, out_ref, yb_ref,
                      qat_ref, qaw_ref, ocmp_ref, m_ref, acc_ref, mc_ref, mm_ref):
    S = kaug_ref.shape[2]
    tq = q_ref.shape[0]
    ncmp = kcaug_ref.shape[2]
    nsel = S // SEL_BLOCK
    n_pairs = NSA_H // 2
    qi = pl.program_id(1)
    t0 = pl.multiple_of(qi * tq, tq)
    t_row = t0 + lax.broadcasted_iota(jnp.int32, (1, tq), 1)

    @pl.when(qi == 0)
    def _():
        mc_ref[...] = jnp.zeros_like(mc_ref)
        mm_ref[...] = jnp.zeros_like(mm_ref)

    def mlstm_chunk(c):
        return _mlstm_chunk_stages(mq_ref, mk_ref, mv_ref, mo_ref, zs_ref, cw_ref, cb_ref, gb_ref, ng_ref, yb_ref,
                                   mc_ref, mm_ref, row0=c * ML_CHUNK, t_abs=t0 + c * ML_CHUNK,
                                   has_history=(qi > 0) if c == 0 else True)

    for k in range(NSA_H // 2):
        q_t = (q_ref[:, k * LANES:(k + 1) * LANES].astype(F32) * (NSA_DH ** -0.5 * LOG2E)).T.astype(BF16)
        for e in range(2):
            for dst in (qat_ref, qaw_ref):
                dst[(k // 2) * 2 + e, e * NSA_DH:(e + 1) * NSA_DH, (k % 2) * tq:(k % 2 + 1) * tq] = (
                    q_t[e * NSA_DH:(e + 1) * NSA_DH])
    feat_row = lax.broadcasted_iota(jnp.int32, (NSA_DH, 2 * tq), 0)
    second_head = lax.broadcasted_iota(jnp.int32, (NSA_DH, 2 * tq), 1) >= tq
    for pair in range(n_pairs):
        g, e = pair // 2, pair % 2
        terms = [_bf16_terms(LOG2E * 2.0 ** (-8.0 * (g * NSA_R + 2 * s + e + 1.0) / NSA_H)) for s in range(2)]
        feat = jnp.zeros((NSA_DH, 2 * tq), F32)
        for i in range(FEAT_TERMS):
            term = jnp.where(second_head, terms[1][i], terms[0][i])
            feat = jnp.where(feat_row == FEAT_HI + i, term * 64.0, jnp.where(feat_row == FEAT_LO + i, term, feat))
        qat_ref[pair, NSA_DH * (1 - e):NSA_DH * (2 - e), :] = feat.astype(BF16)
        qaw_ref[pair, NSA_DH * (1 - e):NSA_DH * (2 - e), :] = feat.astype(BF16)

    ki = lax.broadcasted_iota(jnp.int32, (TK, 2 * tq), 0)
    qu = lax.broadcasted_iota(jnp.int32, (TK, 2 * tq), 1)
    qu = jnp.where(qu >= tq, qu - tq, qu)
    tile_valid = {MASK_CAUSAL: ki <= qu,
                  MASK_WINDOW_EDGE: ki > qu}

    c_col = lax.broadcasted_iota(jnp.int32, (ncmp, 1), 0)
    t_row2 = jnp.concatenate([t_row, t_row], axis=1)
    valid_c = c_col * CMP_STRIDE + (CMP_BLOCK - 1) <= t_row2
    j_col = lax.broadcasted_iota(jnp.int32, (nsel, 1), 0)
    j_f = j_col.astype(F32)
    jt = t_row >> 6
    forced = jnp.logical_or(j_col == 0, jnp.logical_or(j_col == jt, j_col == jt - 1))
    future = j_col > jt

    m_ref[...] = jnp.full(m_ref.shape, NEG, F32)
    acc_ref[...] = jnp.zeros(acc_ref.shape, F32)

    def compressed_scores():
        return [_dot(kcaug_ref[0, pair], qaw_ref[pair]) for pair in range(n_pairs)]

    def compressed_branch_and_selection(scores_c):
        for g in range(NSA_G):
            psum = jnp.zeros((ncmp, tq), F32)
            for e in range(2):
                pair = g * 2 + e
                sc = jnp.where(valid_c, scores_c[pair], NEG)
                ex = jnp.where(valid_c, jnp.exp2(sc - jnp.max(sc, axis=0, keepdims=True)), 0.0)
                den = jnp.sum(ex, axis=0, keepdims=True)
                p = ex * (1.0 / jnp.where(den > 0.0, den, 1.0))
                psum = psum + p[:, :tq] + p[:, tq:]
                ocmp_ref[pair] = _dot(vct_ref[0, g * NSA_DH:(g + 1) * NSA_DH, :], p.astype(BF16))

            p_hi = psum.astype(BF16)
            p_lo = (psum - p_hi.astype(F32)).astype(BF16)
            imp = _dot(ovt_ref[...], p_hi) + _dot(ovt_ref[...], p_lo)
            penalty = jnp.where(forced, 0.0, NEG)
            val = jnp.where(future, NEG, jnp.where(forced, TAKEN, imp))
            for _ in range(min(TOP_N, nsel) - N_FORCED):
                best = jnp.max(val, axis=0, keepdims=True)
                first = jnp.min(jnp.where(val == best, j_f, float(nsel)), axis=0, keepdims=True)
                hit = j_f == first
                penalty = jnp.where(hit, 0.0, penalty)
                val = jnp.where(hit, TAKEN, val)
            penalty = penalty.astype(BF16)
            for e in range(2):
                first = NSA_DH * (1 - e) + FEAT_SEL
                qat_ref[g * 2 + e, first:first + nsel, :] = jnp.concatenate([penalty, penalty], axis=1)

    def attend(br, tiles, interleave=()):
        stages = [(pl.multiple_of(kb * TK, TK), mask, pair) for kb, mask in tiles for pair in range(n_pairs)]

        def scores(stage):
            k0, _, pair = stage
            queries = qat_ref if br == BR_SEL else qaw_ref
            return _dot(kaug_ref[0, br * n_pairs + pair, pl.ds(k0, TK), :], queries[pair])

        s_queue = [scores(st) for st in stages[:SCORE_LOOKAHEAD]]
        pending = list(interleave)
        if pending:
            pending.pop(0)()
        for i, (k0, mask, pair) in enumerate(stages):
            s_t = s_queue.pop(0)
            if i + SCORE_LOOKAHEAD < len(stages):
                s_queue.append(scores(stages[i + SCORE_LOOKAHEAD]))
            slot = br * n_pairs + pair
            if mask is not None:
                s_t = jnp.where(tile_valid[mask], s_t, NEG)
            m_old = m_ref[slot]
            m_new = jnp.maximum(m_old, jnp.max(s_t, axis=0, keepdims=True))
            alpha = jnp.exp2(m_old - m_new)
            p = jnp.exp2(s_t - m_new).astype(BF16)
            acc_ref[slot] = alpha * acc_ref[slot] + _dot(vt_ref[0, br, pair // 2, :, pl.ds(k0, TK)], p)
            m_ref[slot] = m_new
            if pending and i % 2 == 1:
                pending.pop(0)()
        for emit in pending:
            emit()

    n_back = WINDOW // TK
    for n_behind in range(n_back + 1):
        tiles = [(qi, MASK_CAUSAL)] + [(qi - d, MASK_WINDOW_EDGE if d == n_back else None)
                                        for d in range(1, n_behind + 1)]

        @pl.when(qi >= n_back if n_behind == n_back else qi == n_behind)
        def _(tiles=tiles):
            scores_c = compressed_scores()
            project, *rest = mlstm_chunk(0)

            def selection_and_projections():
                compressed_branch_and_selection(scores_c)
                project()

            attend(BR_WIN, tiles, [selection_and_projections] + rest)

    def sel_body(i, carry):
        attend(BR_SEL, [(SEL_TILES * i + j, None) for j in range(SEL_TILES)])
        return carry

    lax.fori_loop(0, qi // SEL_TILES, sel_body, 0)
    for n_full in range(SEL_TILES):
        @pl.when(qi % SEL_TILES == n_full)
        def _(n_full=n_full):
            attend(BR_SEL, [(qi - n_full + j, None) for j in range(n_full)] + [(qi, MASK_CAUSAL)], mlstm_chunk(1))

    gates_t = jax.nn.sigmoid(zs_ref[...]).T
    for hp in range(NSA_H // 2):
        comb = []
        for e in range(2):
            h = 2 * hp + e
            c0 = ZS_GATE + NSA_NB * h
            pair = (h // NSA_R) * 2 + e
            cols = slice(((h % NSA_R) // 2) * tq, ((h % NSA_R) // 2 + 1) * tq)
            sel, win = BR_SEL * n_pairs + pair, BR_WIN * n_pairs + pair
            o_sel = acc_ref[sel, 0:NSA_DH, cols] * (1.0 / acc_ref[sel, NSA_DH:NSA_DH + 1, cols])
            o_win = acc_ref[win, 0:NSA_DH, cols] * (1.0 / acc_ref[win, NSA_DH:NSA_DH + 1, cols])
            comb.append(gates_t[c0:c0 + 1] * ocmp_ref[pair, :, cols] + gates_t[c0 + 1:c0 + 2] * o_sel
                        + gates_t[c0 + 2:c0 + 3] * o_win)
        out_ref[:, hp * LANES:(hp + 1) * LANES] = jnp.concatenate(comb, axis=0).T.astype(BF16)


def _overlap_t(S):
    c = np.arange(S // CMP_STRIDE)[None, :]
    j = np.arange(S // SEL_BLOCK)[:, None]
    ov = (c * CMP_STRIDE <= j * SEL_BLOCK + SEL_BLOCK - 1) & (c * CMP_STRIDE + CMP_BLOCK - 1 >= j * SEL_BLOCK)
    ov &= c < (S - CMP_BLOCK) // CMP_STRIDE + 1
    return jnp.asarray(ov.astype(np.float32), dtype=BF16)


def _nsa_mlstm(zm, zs, kaug, vt, kcaug, vct, conv_w, conv_b, gate_b, norm_g, S):
    M = zm.shape[0]
    B = M // S
    tq = min(TQ, S)
    assert tq == TK and WINDOW % TK == 0 and S % TK == 0 and tq == 2 * ML_CHUNK and ML_CHUNK == ML_DH
    nq = S // tq
    n = S // CMP_STRIDE
    nsel = S // SEL_BLOCK
    gb_row = jnp.zeros((1, ZS_W), F32).at[0, ZS_MI:ZS_MI + 2 * ML_H].set(gate_b)

    def tile(off):
        return pl.BlockSpec((tq, ML_W), lambda b, i: (b * nq + i, off // ML_W))

    def row(off):
        return pl.BlockSpec((S, ML_W), lambda b, i: (b, off // ML_W))

    def full(shape):
        return pl.BlockSpec(shape, lambda b, i: (0,) * len(shape))

    return pl.pallas_call(
        _nsa_mlstm_kernel,
        grid=(B, nq),
        in_specs=[tile(ZM_NQ),
                  pl.BlockSpec((1, 8, S, LANES), lambda b, i: (b, 0, 0, 0)),
                  pl.BlockSpec((1, 2, NSA_G, VT_ROWS, S), lambda b, i: (b, 0, 0, 0, 0)),
                  pl.BlockSpec((1, 4, n, LANES), lambda b, i: (b, 0, 0, 0)),
                  pl.BlockSpec((1, LANES, n), lambda b, i: (b, 0, 0)),
                  pl.BlockSpec((tq, ZS_W), lambda b, i: (b * nq + i, 0)),
                  full((nsel, n)),
                  row(ZM_MQ), row(ZM_MK), tile(ZM_MV), tile(ZM_MO),
                  full((CONV_K, 2 * ML_W)), full((1, 2 * ML_W)), full((1, ZS_W)), full((1, ML_W))],
        out_specs=[pl.BlockSpec((tq, NSA_W), lambda b, i: (b * nq + i, 0)),
                   pl.BlockSpec((tq, ML_W), lambda b, i: (b * nq + i, 0))],
        out_shape=[jax.ShapeDtypeStruct((M, NSA_W), BF16), jax.ShapeDtypeStruct((M, ML_W), BF16)],
        scratch_shapes=[pltpu.VMEM((NSA_H // 2, LANES, 2 * tq), BF16),
                        pltpu.VMEM((NSA_H // 2, LANES, 2 * tq), BF16),
                        pltpu.VMEM((NSA_H // 2, NSA_DH, 2 * tq), F32),
                        pltpu.VMEM((NSA_H, 1, 2 * tq), F32),
                        pltpu.VMEM((NSA_H, VT_ROWS, 2 * tq), F32),
                        pltpu.VMEM((ML_H, ML_DH, 2 * ML_DH), F32),
                        pltpu.VMEM((1, ZS_W), F32)],
        compiler_params=_cparams(("arbitrary", "arbitrary")),
        name="nsa_mlstm",
    )(zm, kaug, vt, kcaug, vct, zs, _overlap_t(S), zm, zm, zm, zm,
      conv_w, conv_b.reshape(1, 2 * ML_W), gb_row, norm_g.reshape(1, ML_W))


def _merge_kernel(gu0_ref, gv0_ref, gu_ref, gv_ref, lng_ref, lnb_ref, ws_ref, bst_ref, b_ref, c_ref,
                  ga_ref, gb_ref, gc_ref, x_ref, gt_ref, wa_ref, wb_ref, wc_ref, wo_ref, o_ref, a_ref):
    i = pl.program_id(0)

    @pl.when(i == 0)
    def _():
        _gmlp_rows(gu0_ref, gv0_ref, lng_ref, lnb_ref, ws_ref, bst_ref, a_ref.at[0])

    a = a_ref[i % 2]
    merged = (jax.nn.sigmoid(ga_ref[...].astype(F32)) * _dot(a, wa_ref[...])
              + jax.nn.sigmoid(gb_ref[...].astype(F32)) * _dot(b_ref[...], wb_ref[...])
              + jax.nn.sigmoid(gc_ref[...].astype(F32)) * _dot(c_ref[...], wc_ref[...]))
    o_ref[...] = x_ref[...] + gt_ref[0] * _dot(merged.astype(BF16), wo_ref[...])
    _gmlp_rows(gu_ref, gv_ref, lng_ref, lnb_ref, ws_ref, bst_ref, a_ref.at[(i + 1) % 2])


def _merge(yb, yc, zm, x2, gt, ln_g, ln_b, ws, bs, wa, wb, wc, wo, S):
    M, D = x2.shape
    tm = min(TM_MERGE, S)
    per_b = S // tm

    def rows(w, jcol=0):
        return pl.BlockSpec((tm, w), lambda i: (i, jcol))

    def next_rows(jcol):
        return pl.BlockSpec((tm, GM_W), lambda i: (jnp.minimum(i + 1, M // tm - 1), jcol))

    def full(shape):
        return pl.BlockSpec(shape, lambda i: (0,) * len(shape))

    return pl.pallas_call(
        _merge_kernel,
        grid=(M // tm,),
        in_specs=[rows(GM_W, ZM_GU // GM_W), rows(GM_W, ZM_GV // GM_W),
                  next_rows(ZM_GU // GM_W), next_rows(ZM_GV // GM_W), full((1, GM_W)), full((1, GM_W)),
                  full((GM_GROUPS, GM_CHUNK, GM_CHUNK)), full((GM_CHUNK, GM_GROUPS)),
                  rows(ML_W), rows(NSA_W),
                  rows(D, ZM_GA // D), rows(D, ZM_GBR // D), rows(D, ZM_GC // D),
                  rows(D), pl.BlockSpec((1, 1, D), lambda i: (i // per_b, 0, 0)),
                  full((GM_W, D)), full((ML_W, D)), full((NSA_W, D)), full((D, D))],
        out_specs=rows(D),
        out_shape=jax.ShapeDtypeStruct((M, D), F32),
        scratch_shapes=[pltpu.VMEM((2, tm, GM_W), BF16)],
        compiler_params=_cparams(("arbitrary",)),
        name="gmlp_merge_out",
    )(zm, zm, zm, zm, ln_g.reshape(1, GM_W), ln_b.reshape(1, GM_W), ws, bs.T, yb, yc, zm, zm, zm, x2, gt, wa, wb, wc, wo)


def _mlp_kernel(x_ref, g_ref, sc_ref, sh_ref, gt_ref, w1_ref, w2_ref, gf_ref, o_ref, acc_ref, *, final_norm):
    x = x_ref[...]
    h = _modulated_norm(x, g_ref[...], sc_ref[0], sh_ref[0]).astype(BF16)
    for c in range(D_FF // FF_CHUNK):
        mid = jnp.square(jnp.maximum(_dot(h, w1_ref[:, c * FF_CHUNK:(c + 1) * FF_CHUNK]), 0.0)).astype(BF16)
        upd = _dot(mid, w2_ref[c * FF_CHUNK:(c + 1) * FF_CHUNK, :])
        if c == 0:
            acc_ref[...] = upd
        else:
            acc_ref[...] += upd
    y = x + gt_ref[0] * acc_ref[...]
    if final_norm:
        y = (y * lax.rsqrt(jnp.mean(y * y, axis=-1, keepdims=True) + EPS)) * gf_ref[...]
    o_ref[...] = y


def _mlp(x2, g, sc, sh, gt, w1, w2, g_final, S, final_norm):
    M, D = x2.shape
    tm = min(TM_MLP, S)
    per_b = S // tm
    mod = pl.BlockSpec((1, 1, D), lambda i: (i // per_b, 0, 0))
    return pl.pallas_call(
        functools.partial(_mlp_kernel, final_norm=final_norm),
        grid=(M // tm,),
        in_specs=[pl.BlockSpec((tm, D), lambda i: (i, 0)),
                  pl.BlockSpec((1, D), lambda i: (0, 0)),
                  mod, mod, mod,
                  pl.BlockSpec((D, D_FF), lambda i: (0, 0)),
                  pl.BlockSpec((D_FF, D), lambda i: (0, 0)),
                  pl.BlockSpec((1, D), lambda i: (0, 0))],
        out_specs=pl.BlockSpec((tm, D), lambda i: (i, 0)),
        out_shape=jax.ShapeDtypeStruct((M, D), F32),
        scratch_shapes=[pltpu.VMEM((tm, D), F32)],
        compiler_params=_cparams(("arbitrary",)),
        name="relu2_mlp",
    )(x2, g, sc, sh, gt, w1, w2, g_final)


def _w_cols(w, name):
    return w[:, _OFFS[name]:_OFFS[name] + _SIZE[name]]


def _proj_weights(w):
    D = w.shape[0]
    wm = jnp.concatenate([_w_cols(w, n) for n in ('ga', 'gbr', 'gc', 'gu', 'gv', 'mq', 'mk', 'mv', 'mo', 'nq',
                                                  'nks', 'nkw', 'nvs', 'nvw')], axis=1)
    wc = jnp.concatenate([_w_cols(w, 'nkc'), _w_cols(w, 'nvc')], axis=1)
    ws = jnp.concatenate([_w_cols(w, 'ngate'), _w_cols(w, 'mi'), _w_cols(w, 'mf'),
                          jnp.zeros((D, ZS_W - NSA_H * NSA_NB - 2 * ML_H), w.dtype)], axis=1)
    return wm, wc, ws


def kernel(x, c, g_norm1, g_norm2, w_ada, b_ada, w_in, gm_ln_g, gm_ln_b, gm_ws, gm_bs, ml_conv_w, ml_conv_b,
           ml_gate_b, ml_norm_g, nsa_pe_k, nsa_pe_v, nsa_phi_k1, nsa_phi_k2, nsa_phi_v1, nsa_phi_v2,
           w_up_a, w_up_b, w_up_c, w_out, w_mlp1, w_mlp2, g_final):
    B, S, D = x.shape
    depth = w_in.shape[0]
    M = B * S
    mod = _ada(c, w_ada, b_ada)
    x2 = x.reshape(M, D)
    w_in, w_up_a, w_up_b, w_up_c, w_out, w_mlp1, w_mlp2 = [
        w.astype(BF16) for w in (w_in, w_up_a, w_up_b, w_up_c, w_out, w_mlp1, w_mlp2)]
    for l in range(depth):
        sh1, sc1, gt1, sh2, sc2, gt2 = [mod[l, :, i * D:(i + 1) * D].reshape(B, 1, D) for i in range(6)]
        wm, wc, ws = _proj_weights(w_in[l])
        zm, zc, zs = _inproj(x2, g_norm1[l].reshape(1, D), sc1, sh1, wm, wc, ws, S)
        kaug, vt, kcaug, vct = _nsa_prep(zc, zm, nsa_pe_k[l], nsa_pe_v[l], nsa_phi_k1[l], nsa_phi_k2[l],
                                         nsa_phi_v1[l], nsa_phi_v2[l], S)
        yc, yb = _nsa_mlstm(zm, zs, kaug, vt, kcaug, vct, ml_conv_w[l], ml_conv_b[l], ml_gate_b[l], ml_norm_g[l], S)
        x2 = _merge(yb, yc, zm, x2, gt1, gm_ln_g[l], gm_ln_b[l], gm_ws[l], gm_bs[l], w_up_a[l], w_up_b[l],
                    w_up_c[l], w_out[l], S)
        x2 = _mlp(x2, g_norm2[l].reshape(1, D), sc2, sh2, gt2, w_mlp1[l], w_mlp2[l], g_final.reshape(1, D), S,
                  final_norm=(l == depth - 1))
    return x2.reshape(B, S, D)
```

```python
import functools

import numpy as np
import jax
import jax.numpy as jnp
from jax import lax
from jax.experimental import pallas as pl
from jax.experimental.pallas import tpu as pltpu

F32 = jnp.float32
BF16 = jnp.bfloat16

D_MODEL = 1024
GM_W = 512
GM_GROUPS = 4
GM_CHUNK = 128
ML_H = 4
ML_DH = 128
ML_W = ML_H * ML_DH
CONV_K = 4
NSA_H = 8
NSA_G = 2
NSA_R = NSA_H // NSA_G
NSA_DH = 64
NSA_W = NSA_H * NSA_DH
NSA_KV = NSA_G * NSA_DH
NSA_NB = 3
CMP_BLOCK = 32
CMP_STRIDE = 16
SEL_BLOCK = 64
TOP_N = 8
WINDOW = 512
D_FF = 4 * D_MODEL
EPS = 1e-6
NEG = -1e30
TAKEN = -3e38
N_FORCED = 3
SPLIT_SIZES = (GM_W, GM_W, ML_W, ML_W, ML_W, ML_W, ML_H, ML_H, NSA_W, NSA_KV, NSA_KV, NSA_KV, NSA_KV,
               NSA_KV, NSA_KV, NSA_H * NSA_NB, D_MODEL, D_MODEL, D_MODEL)
SPLIT_NAMES = ('gu', 'gv', 'mq', 'mk', 'mv', 'mo', 'mi', 'mf', 'nq', 'nkc', 'nvc', 'nks', 'nvs', 'nkw', 'nvw',
               'ngate', 'ga', 'gbr', 'gc')
_OFFS = dict(zip(SPLIT_NAMES, np.concatenate([[0], np.cumsum(SPLIT_SIZES)[:-1]]).tolist()))
_SIZE = dict(zip(SPLIT_NAMES, SPLIT_SIZES))

LANES = 128
VMEM_LIMIT = 56 * 1024 * 1024

ZM_GA, ZM_GBR, ZM_GC = 0, 1024, 2048
ZM_GU, ZM_GV = 3072, 3584
ZM_MQ, ZM_MK, ZM_MV, ZM_MO = 4096, 4608, 5120, 5632
ZM_NQ = 6144
ZM_KK = 6656
ZM_VV = 6912
ZM_W = 7168
KK_W = 2 * LANES
VV_W = 2 * LANES
BR_SEL, BR_WIN = 0, 1
MASK_CAUSAL, MASK_WINDOW_EDGE = 0, 1
FEAT_SEL = 0
FEAT_TERMS = 3
FEAT_HI = 32
FEAT_LO = 35
LOG2E = 1.4426950408889634
ZS_GATE = 0
ZS_MI = 24
ZS_MF = 28
ZS_W = 128

TN_ADA = 1536
TM_PROJ = 1024
TN_PROJ = 3584
ML_CHUNK = 128
CONV_HALO = 16
TQ = 256
TK = 256
SCORE_LOOKAHEAD = 3
SEL_TILES = 4
VT_ROWS = NSA_DH + 16
TM_MERGE = 512
MERGE_SPLIT = 2
TM_MLP = 512
FF_CHUNK = 1024


def _dot(a, b):
    return jnp.dot(a, b, preferred_element_type=F32)


def _split3(x):
    x1 = x.astype(BF16)
    r1 = x - x1.astype(F32)
    x2 = r1.astype(BF16)
    x3 = (r1 - x2.astype(F32)).astype(BF16)
    return x1, x2, x3


def _cparams(sem):
    return pltpu.CompilerParams(dimension_semantics=sem, vmem_limit_bytes=VMEM_LIMIT)


def _ada_kernel(c_ref, w_ref, b_ref, o_ref):
    c = c_ref[...]
    cond = c * jax.nn.sigmoid(c)
    c1, c2, c3 = _split3(cond)
    w1, w2, w3 = _split3(w_ref[0])
    acc = _dot(c1, w1) + (_dot(c1, w2) + _dot(c2, w1)) + (_dot(c1, w3) + _dot(c2, w2) + _dot(c3, w1))
    o_ref[0] = acc + b_ref[0]


def _ada(c, w_ada, b_ada):
    L, D, N = w_ada.shape
    B = c.shape[0]
    tn = TN_ADA
    return pl.pallas_call(
        _ada_kernel,
        grid=(L, N // tn),
        in_specs=[pl.BlockSpec((B, D), lambda l, j: (0, 0)),
                  pl.BlockSpec((1, D, tn), lambda l, j: (l, 0, j)),
                  pl.BlockSpec((1, 1, tn), lambda l, j: (l, 0, j))],
        out_specs=pl.BlockSpec((1, B, tn), lambda l, j: (l, 0, j)),
        out_shape=jax.ShapeDtypeStruct((L, B, N), F32),
        compiler_params=_cparams(("arbitrary", "arbitrary")),
        name="ada_mod",
    )(c, w_ada, b_ada.reshape(L, 1, N))


def _modulated_norm(x, g, sc, sh):
    y = x * lax.rsqrt(jnp.mean(x * x, axis=-1, keepdims=True) + EPS)
    return (y * g) * (1.0 + sc) + sh


def _inproj_kernel(x_ref, g_ref, sc_ref, sh_ref, wm_ref, wc_ref, ws_ref, zm_ref, zc_ref, zs_ref, h_ref):
    @pl.when(pl.program_id(1) == 0)
    def _():
        h = _modulated_norm(x_ref[...], g_ref[...], sc_ref[0], sh_ref[0]).astype(BF16)
        h_ref[...] = h
        zc = _dot(h, wc_ref[...])
        for part in range(zc_ref.shape[0]):
            zc_ref[part] = zc[:, part * LANES:(part + 1) * LANES]
        zs_ref[...] = _dot(h, ws_ref[...])

    zm_ref[...] = _dot(h_ref[...], wm_ref[...]).astype(BF16)


def _inproj(x2, g, sc, sh, wm, wc, ws, S):
    M, D = x2.shape
    tm, tn = min(TM_PROJ, S), TN_PROJ
    per_b = S // tm
    return pl.pallas_call(
        _inproj_kernel,
        grid=(M // tm, ZM_W // tn),
        in_specs=[pl.BlockSpec((tm, D), lambda i, j: (i, 0)),
                  pl.BlockSpec((1, D), lambda i, j: (0, 0)),
                  pl.BlockSpec((1, 1, D), lambda i, j: (i // per_b, 0, 0)),
                  pl.BlockSpec((1, 1, D), lambda i, j: (i // per_b, 0, 0)),
                  pl.BlockSpec((D, tn), lambda i, j: (0, j)),
                  pl.BlockSpec((D, 2 * NSA_KV), lambda i, j: (0, 0)),
                  pl.BlockSpec((D, ZS_W), lambda i, j: (0, 0))],
        out_specs=[pl.BlockSpec((tm, tn), lambda i, j: (i, j)),
                   pl.BlockSpec((2 * NSA_KV // LANES, tm, LANES), lambda i, j: (0, i, 0)),
                   pl.BlockSpec((tm, ZS_W), lambda i, j: (i, 0))],
        out_shape=[jax.ShapeDtypeStruct((M, ZM_W), BF16),
                   jax.ShapeDtypeStruct((2 * NSA_KV // LANES, M, LANES), F32),
                   jax.ShapeDtypeStruct((M, ZS_W), F32)],
        scratch_shapes=[pltpu.VMEM((tm, D), BF16)],
        compiler_params=_cparams(("arbitrary", "arbitrary")),
        name="in_proj",
    )(x2, g, sc, sh, wm, wc, ws)


def _gmlp_rows(u_ref, v_ref, lng_ref, lnb_ref, ws_ref, bst_ref, o_ref):
    ts = u_ref.shape[0]
    dg = GM_W // GM_GROUPS
    row = lax.broadcasted_iota(jnp.int32, (GM_CHUNK, GM_CHUNK), 0)
    col = lax.broadcasted_iota(jnp.int32, (GM_CHUNK, GM_CHUNK), 1)
    ws = [jnp.where(row >= col, ws_ref[g], 0.0).astype(BF16) for g in range(GM_GROUPS)]
    lng = lng_ref[...]
    lnb = lnb_ref[...]
    for c in range(ts // GM_CHUNK):
        r0 = c * GM_CHUNK
        u = jax.nn.gelu(u_ref[r0:r0 + GM_CHUNK, :].astype(F32))
        v = jax.nn.gelu(v_ref[r0:r0 + GM_CHUNK, :].astype(F32))
        mu = jnp.mean(v, axis=-1, keepdims=True)
        var = jnp.mean(jnp.square(v - mu), axis=-1, keepdims=True)
        vb = ((v - mu) * lax.rsqrt(var + EPS) * lng + lnb).astype(BF16)
        for g in range(GM_GROUPS):
            mixed = _dot(ws[g], vb[:, g * dg:(g + 1) * dg]) + bst_ref[:, g:g + 1]
            o_ref[r0:r0 + GM_CHUNK, g * dg:(g + 1) * dg] = (u[:, g * dg:(g + 1) * dg] * mixed).astype(BF16)


def _log_sigmoid(x):
    return jnp.minimum(x, 0.0) - jnp.log1p(jnp.exp(-jnp.abs(x)))


def _conv_silu(x_ext, w, b):
    n = x_ext.shape[0] - CONV_HALO
    y = b
    for j in range(CONV_K):
        sh = CONV_K - 1 - j
        xs = x_ext if sh == 0 else pltpu.roll(x_ext, sh, axis=0)
        y = y + xs[CONV_HALO:CONV_HALO + n] * w[j:j + 1]
    return y * jax.nn.sigmoid(y)


def _mlstm_chunk_stages(q_ref, k_ref, v_ref, o_ref, zs_ref, cw_ref, cb_ref, gb_ref, ng_ref, out_ref, c_ref, m_ref,
                        row0, t_abs, has_history):
    L = ML_CHUNK
    rows = slice(row0, row0 + L)
    hs = [slice(h * ML_DH, (h + 1) * ML_DH) for h in range(ML_H)]
    heads = range(ML_H)
    st = {}

    def conv_rows(src, c0):
        cur = src[pl.ds(pl.multiple_of(t_abs, L), L), :].astype(F32)
        halo0 = pl.multiple_of(jnp.maximum(t_abs - CONV_HALO, 0), CONV_HALO)
        halo = jnp.where(has_history, src[pl.ds(halo0, CONV_HALO), :].astype(F32), 0.0)
        return _conv_silu(jnp.concatenate([halo, cur], axis=0), cw_ref[:, c0:c0 + ML_W], cb_ref[:, c0:c0 + ML_W])

    def projections():
        row = lax.broadcasted_iota(jnp.int32, (L, L), 0)
        col = lax.broadcasted_iota(jnp.int32, (L, L), 1)
        tril_b = jnp.where(row >= col, 1.0, 0.0).astype(BF16)
        triu_b = jnp.where(row <= col, 1.0, 0.0).astype(BF16)
        gi = zs_ref[rows, :] + gb_ref[...]
        gi_t = gi.T
        c1, c2, c3 = _split3(_log_sigmoid(gi) * LOG2E)
        b_cols = _dot(tril_b, c1) + _dot(tril_b, c2) + _dot(tril_b, c3)
        r1, r2, r3 = _split3(_log_sigmoid(gi_t[ZS_MI:ZS_MI + 2 * ML_H]) * LOG2E)
        b_rows = _dot(r1, triu_b) + _dot(r2, triu_b) + _dot(r3, triu_b)
        st['gi'] = gi * LOG2E
        st['r_rows'] = gi_t[ZS_MI:ZS_MI + ML_H] * LOG2E - b_rows[ML_H:2 * ML_H]
        st['b_at_i'] = pltpu.roll(b_cols, ZS_W - (ZS_MF - ZS_MI), axis=1)
        q_all = conv_rows(q_ref, 0).astype(BF16)
        k_all = conv_rows(k_ref, ML_W) * (ML_DH ** -0.5)
        st['q'] = [q_all[:, hs[h]] for h in heads]
        st['k_t'] = [k_all[:, hs[h]].T for h in heads]
        st['cmat'] = [c_ref[h] for h in heads]
        st['qk'] = [_dot(st['q'][h], st['k_t'][h].astype(BF16)) for h in heads]
        st['qc'] = [_dot(st['q'][h], st['cmat'][h].astype(BF16)) for h in heads]
        ones_blk = jnp.ones((L, ML_DH), BF16)
        st['v_aug'] = [jnp.concatenate([v_ref[rows, hs[h]], ones_blk], axis=1) for h in heads]

    def memory_update():
        tril = lax.broadcasted_iota(jnp.int32, (L, L), 0) >= lax.broadcasted_iota(jnp.int32, (L, L), 1)
        time_row = lax.broadcasted_iota(jnp.int32, (L, ZS_W), 0)
        m_row = m_ref[...]
        u = st['gi'] - st['b_at_i']
        shift = 1
        while shift < L:
            u = jnp.maximum(u, jnp.where(time_row >= shift, pltpu.roll(u, shift, axis=0), NEG))
            shift *= 2
        u = jnp.maximum(u, m_row)
        m_cols = st['b_at_i'] + u
        u_last = u[L - 1:L, :]
        w_prev_row = jnp.exp2(m_row - u_last)
        st['m_cols'], st['m_prev'], st['u_rep'], st['gate'] = m_cols, m_row, [], []
        for h in heads:
            lane_h = ZS_MI + h
            r_row = st['r_rows'][h:h + 1, :]
            u_rep = jnp.broadcast_to(u[:, lane_h:lane_h + 1], (L, ML_DH))
            st['u_rep'].append(u_rep)
            st['gate'].append(jnp.exp2(jnp.where(tril, r_row - u_rep, NEG)))
            w_s = jnp.exp2(r_row - u_last[:, lane_h:lane_h + 1])
            k_w = (st['k_t'][h] * w_s).astype(BF16)
            c_ref[h] = w_prev_row[:, lane_h:lane_h + 1] * st['cmat'][h] + _dot(k_w, st['v_aug'][h])
        m_ref[...] = m_cols[L - 1:L, :]

    def read_out():
        st['num_aug'] = []
        for h in heads:
            lane_h = ZS_MI + h
            w_inter = jnp.exp2(st['m_prev'][:, lane_h:lane_h + 1] - st['u_rep'][h])
            st['num_aug'].append(_dot((st['qk'][h] * st['gate'][h]).astype(BF16), st['v_aug'][h])
                                 + jnp.concatenate([w_inter, w_inter], axis=1) * st['qc'][h])

    def normalise_and_store():
        for h in heads:
            lane_h = ZS_MI + h
            num_aug = st['num_aug'][h]
            m_rep = jnp.broadcast_to(st['m_cols'][:, lane_h:lane_h + 1], (L, ML_DH))
            hval = num_aug[:, :ML_DH] / jnp.maximum(jnp.abs(num_aug[:, ML_DH:]), jnp.exp2(-m_rep))
            mu = jnp.mean(hval, axis=-1, keepdims=True)
            var = jnp.mean(jnp.square(hval - mu), axis=-1, keepdims=True)
            hn = (hval - mu) * lax.rsqrt(var + EPS) * ng_ref[:, hs[h]]
            og = jax.nn.sigmoid(o_ref[rows, hs[h]].astype(F32))
            out_ref[rows, hs[h]] = (og * hn).astype(BF16)

    return [projections, memory_update, read_out, normalise_and_store]


def _nsa_prep_kernel(x_ref, pe_ref, w1_ref, w2_ref, kk_ref, vv_ref, feat_ref, featc_ref,
                     kaug_ref, vt_ref, kcaug_ref, vct_ref):
    S = kk_ref.shape[0]
    n = S // CMP_STRIDE
    cw = 2 * NSA_KV
    half = CMP_BLOCK // 2
    acc_a = jnp.zeros((n, cw), F32)
    acc_b = jnp.zeros((n, cw), F32)
    for j in range(half):
        xj = jnp.concatenate([x_ref[part, pl.ds(j, n, stride=CMP_STRIDE), :] for part in range(cw // LANES)],
                             axis=1)
        acc_a = acc_a + _dot((xj + pe_ref[j:j + 1, :]).astype(BF16), w1_ref[j])
        acc_b = acc_b + _dot((xj + pe_ref[half + j:half + j + 1, :]).astype(BF16), w1_ref[half + j])
    pre = acc_a + pltpu.roll(acc_b, n - 1, axis=0)
    cmp = _dot(jax.nn.gelu(pre).astype(BF16), w2_ref[...])

    lane = lax.broadcasted_iota(jnp.int32, (1, LANES), 1)
    keep = [jnp.where(lane < NSA_DH, 1.0, 0.0).astype(BF16), jnp.where(lane >= NSA_DH, 1.0, 0.0).astype(BF16)]
    for g in range(NSA_G):
        kd = cmp[:, g * LANES:(g + 1) * LANES].astype(BF16)
        for e in range(2):
            kcaug_ref[0, g * 2 + e] = kd * keep[e] + featc_ref[e]
    vct_ref[0] = cmp[:, NSA_G * LANES:(NSA_G + 1) * LANES].T.astype(BF16)
    tb = min(TK, S)
    for br in range(2):
        for c in range(S // tb):
            rows = slice(c * tb, (c + 1) * tb)
            k = kk_ref[rows, br * LANES:(br + 1) * LANES]
            k_swapped = pltpu.roll(k.astype(F32), NSA_DH, axis=1).astype(BF16)
            for g in range(NSA_G):
                for e in range(2):
                    kaug_ref[0, (br * NSA_G + g) * 2 + e, rows, :] = (
                        (k if g == e else k_swapped) * keep[e] + feat_ref[br * 2 + e, rows, :])
        for c in range(S // tb):
            v_t = vv_ref[c * tb:(c + 1) * tb, br * LANES:(br + 1) * LANES].astype(F32).T.astype(BF16)
            for g in range(NSA_G):
                vt_ref[0, br, g, 0:NSA_DH, c * tb:(c + 1) * tb] = v_t[g * NSA_DH:(g + 1) * NSA_DH]
                vt_ref[0, br, g, NSA_DH:VT_ROWS, c * tb:(c + 1) * tb] = jnp.ones((VT_ROWS - NSA_DH, tb), BF16)


def _blockdiag(blocks):
    n = len(blocks)
    rows = []
    for i, blk in enumerate(blocks):
        rows.append(jnp.concatenate([blk if j == i else jnp.zeros((blk.shape[0], blocks[j].shape[1]), blk.dtype)
                                     for j in range(n)], axis=1))
    return jnp.concatenate(rows, axis=0)


def _key_features(S):
    pos = np.arange(S)
    f = np.zeros((4, S, LANES), np.float32)
    for e in range(2):
        base = NSA_DH * (1 - e)
        f[e, pos, base + FEAT_SEL + pos // SEL_BLOCK] = 1.0
        for br in range(2):
            f[br * 2 + e, :, base + FEAT_HI:base + FEAT_HI + FEAT_TERMS] = (pos // 64)[:, None]
            f[br * 2 + e, :, base + FEAT_LO:base + FEAT_LO + FEAT_TERMS] = (pos % 64)[:, None]
    return jnp.asarray(f, dtype=BF16)


def _cmp_features(S):
    c = np.arange(S // CMP_STRIDE)
    hi = (c * CMP_STRIDE) // 64
    lo = c * CMP_STRIDE + (CMP_BLOCK - 1) * 0.5 - 64 * hi
    f = np.zeros((2, c.size, LANES), np.float32)
    for e in range(2):
        base = NSA_DH * (1 - e)
        f[e, :, base + FEAT_HI:base + FEAT_HI + FEAT_TERMS] = hi[:, None]
        f[e, :, base + FEAT_LO:base + FEAT_LO + FEAT_TERMS] = lo[:, None]
    return jnp.asarray(f, dtype=BF16)


def _bf16_terms(x):
    terms = []
    for _ in range(FEAT_TERMS):
        terms.append(float(np.asarray(x - sum(terms), dtype=BF16)))
    return terms


def _nsa_prep(zc, zm, pe_k, pe_v, phi_k1, phi_k2, phi_v1, phi_v2, S):
    M = zc.shape[1]
    B = M // S
    n = S // CMP_STRIDE
    cw = 2 * NSA_KV
    assert S // SEL_BLOCK <= FEAT_HI and S <= 64 * 64
    pe = jnp.concatenate([pe_k, pe_k, pe_v, pe_v], axis=1)
    k1 = phi_k1.reshape(CMP_BLOCK, NSA_DH, NSA_DH)
    v1 = phi_v1.reshape(CMP_BLOCK, NSA_DH, NSA_DH)
    blocks = jnp.stack([k1, k1, v1, v1], axis=1)
    w1 = jnp.einsum('jaxy,ab->jaxby', blocks, jnp.eye(4, dtype=F32)).reshape(CMP_BLOCK, cw, cw).astype(BF16)
    k2 = jnp.concatenate([phi_k2, phi_k2], axis=1)
    w2 = _blockdiag([k2, k2, phi_v2, phi_v2]).astype(BF16)

    def full(shape):
        return pl.BlockSpec(shape, lambda b: (0,) * len(shape))

    return pl.pallas_call(
        _nsa_prep_kernel,
        grid=(B,),
        in_specs=[pl.BlockSpec((cw // LANES, S, LANES), lambda b: (0, b, 0)),
                  full((CMP_BLOCK, cw)), full((CMP_BLOCK, cw, cw)), full((cw, 3 * LANES)),
                  pl.BlockSpec((S, KK_W), lambda b: (b, ZM_KK // KK_W)),
                  pl.BlockSpec((S, VV_W), lambda b: (b, ZM_VV // VV_W)),
                  full((4, S, LANES)), full((2, n, LANES))],
        out_specs=[pl.BlockSpec((1, 8, S, LANES), lambda b: (b, 0, 0, 0)),
                   pl.BlockSpec((1, 2, NSA_G, VT_ROWS, S), lambda b: (b, 0, 0, 0, 0)),
                   pl.BlockSpec((1, 4, n, LANES), lambda b: (b, 0, 0, 0)),
                   pl.BlockSpec((1, LANES, n), lambda b: (b, 0, 0))],
        out_shape=[jax.ShapeDtypeStruct((B, 8, S, LANES), BF16),
                   jax.ShapeDtypeStruct((B, 2, NSA_G, VT_ROWS, S), BF16),
                   jax.ShapeDtypeStruct((B, 4, n, LANES), BF16),
                   jax.ShapeDtypeStruct((B, LANES, n), BF16)],
        compiler_params=_cparams(("arbitrary",)),
        name="nsa_prep",
    )(zc, pe, w1, w2, zm, zm, _key_features(S), _cmp_features(S))


def _nsa_mlstm_kernel(q_ref, kaug_ref, vt_ref, kcaug_ref, vct_ref, zs_ref, ovt_ref,
                      mq_ref, mk_ref, mv_ref, mo_ref, cw_ref, cb_ref, gb_ref, ng_ref, out_ref, yb_ref,
                      qat_ref, qaw_ref, ocmp_ref, m_ref, acc_ref, mc_ref, mm_ref):
    S = kaug_ref.shape[2]
    tq = q_ref.shape[0]
    ncmp = kcaug_ref.shape[2]
    nsel = S // SEL_BLOCK
    n_pairs = NSA_H // 2
    qi = pl.program_id(1)
    t0 = pl.multiple_of(qi * tq, tq)
    t_row = t0 + lax.broadcasted_iota(jnp.int32, (1, tq), 1)

    @pl.when(qi == 0)
    def _():
        mc_ref[...] = jnp.zeros_like(mc_ref)
        mm_ref[...] = jnp.zeros_like(mm_ref)

    def mlstm_chunk(c):
        return _mlstm_chunk_stages(mq_ref, mk_ref, mv_ref, mo_ref, zs_ref, cw_ref, cb_ref, gb_ref, ng_ref, yb_ref,
                                   mc_ref, mm_ref, row0=c * ML_CHUNK, t_abs=t0 + c * ML_CHUNK,
                                   has_history=(qi > 0) if c == 0 else True)

    for k in range(NSA_H // 2):
        q_t = (q_ref[:, k * LANES:(k + 1) * LANES].astype(F32) * (NSA_DH ** -0.5 * LOG2E)).T.astype(BF16)
        for e in range(2):
            for dst in (qat_ref, qaw_ref):
                dst[(k // 2) * 2 + e, e * NSA_DH:(e + 1) * NSA_DH, (k % 2) * tq:(k % 2 + 1) * tq] = (
                    q_t[e * NSA_DH:(e + 1) * NSA_DH])
    feat_row = lax.broadcasted_iota(jnp.int32, (NSA_DH, 2 * tq), 0)
    second_head = lax.broadcasted_iota(jnp.int32, (NSA_DH, 2 * tq), 1) >= tq
    for pair in range(n_pairs):
        g, e = pair // 2, pair % 2
        terms = [_bf16_terms(LOG2E * 2.0 ** (-8.0 * (g * NSA_R + 2 * s + e + 1.0) / NSA_H)) for s in range(2)]
        feat = jnp.zeros((NSA_DH, 2 * tq), F32)
        for i in range(FEAT_TERMS):
            term = jnp.where(second_head, terms[1][i], terms[0][i])
            feat = jnp.where(feat_row == FEAT_HI + i, term * 64.0, jnp.where(feat_row == FEAT_LO + i, term, feat))
        qat_ref[pair, NSA_DH * (1 - e):NSA_DH * (2 - e), :] = feat.astype(BF16)
        qaw_ref[pair, NSA_DH * (1 - e):NSA_DH * (2 - e), :] = feat.astype(BF16)

    ki = lax.broadcasted_iota(jnp.int32, (TK, 2 * tq), 0)
    qu = lax.broadcasted_iota(jnp.int32, (TK, 2 * tq), 1)
    qu = jnp.where(qu >= tq, qu - tq, qu)
    tile_valid = {MASK_CAUSAL: ki <= qu,
                  MASK_WINDOW_EDGE: ki > qu}

    c_col = lax.broadcasted_iota(jnp.int32, (ncmp, 1), 0)
    t_row2 = jnp.concatenate([t_row, t_row], axis=1)
    valid_c = c_col * CMP_STRIDE + (CMP_BLOCK - 1) <= t_row2
    j_col = lax.broadcasted_iota(jnp.int32, (nsel, 1), 0)
    j_f = j_col.astype(F32)
    jt = t_row >> 6
    forced = jnp.logical_or(j_col == 0, jnp.logical_or(j_col == jt, j_col == jt - 1))
    future = j_col > jt

    m_ref[...] = jnp.full(m_ref.shape, NEG, F32)
    acc_ref[...] = jnp.zeros(acc_ref.shape, F32)

    def compressed_scores():
        return [_dot(kcaug_ref[0, pair], qaw_ref[pair]) for pair in range(n_pairs)]

    def compressed_branch_and_selection(scores_c):
        for g in range(NSA_G):
            psum = jnp.zeros((ncmp, tq), F32)
            for e in range(2):
                pair = g * 2 + e
                sc = jnp.where(valid_c, scores_c[pair], NEG)
                ex = jnp.where(valid_c, jnp.exp2(sc - jnp.max(sc, axis=0, keepdims=True)), 0.0)
                den = jnp.sum(ex, axis=0, keepdims=True)
                p = ex * (1.0 / jnp.where(den > 0.0, den, 1.0))
                psum = psum + p[:, :tq] + p[:, tq:]
                ocmp_ref[pair] = _dot(vct_ref[0, g * NSA_DH:(g + 1) * NSA_DH, :], p.astype(BF16))

            p_hi = psum.astype(BF16)
            p_lo = (psum - p_hi.astype(F32)).astype(BF16)
            imp = _dot(ovt_ref[...], p_hi) + _dot(ovt_ref[...], p_lo)
            penalty = jnp.where(forced, 0.0, NEG)
            val = jnp.where(future, NEG, jnp.where(forced, TAKEN, imp))
            for _ in range(min(TOP_N, nsel) - N_FORCED):
                best = jnp.max(val, axis=0, keepdims=True)
                first = jnp.min(jnp.where(val == best, j_f, float(nsel)), axis=0, keepdims=True)
                hit = j_f == first
                penalty = jnp.where(hit, 0.0, penalty)
                val = jnp.where(hit, TAKEN, val)
            penalty = penalty.astype(BF16)
            for e in range(2):
                first = NSA_DH * (1 - e) + FEAT_SEL
                qat_ref[g * 2 + e, first:first + nsel, :] = jnp.concatenate([penalty, penalty], axis=1)

    def attend(br, tiles, interleave=()):
        stages = [(pl.multiple_of(kb * TK, TK), mask, pair) for kb, mask in tiles for pair in range(n_pairs)]

        def scores(stage):
            k0, _, pair = stage
            queries = qat_ref if br == BR_SEL else qaw_ref
            return _dot(kaug_ref[0, br * n_pairs + pair, pl.ds(k0, TK), :], queries[pair])

        s_queue = [scores(st) for st in stages[:SCORE_LOOKAHEAD]]
        pending = list(interleave)
        if pending:
            pending.pop(0)()
        for i, (k0, mask, pair) in enumerate(stages):
            s_t = s_queue.pop(0)
            if i + SCORE_LOOKAHEAD < len(stages):
                s_queue.append(scores(stages[i + SCORE_LOOKAHEAD]))
            slot = br * n_pairs + pair
            if mask is not None:
                s_t = jnp.where(tile_valid[mask], s_t, NEG)
            m_old = m_ref[slot]
            m_new = jnp.maximum(m_old, jnp.max(s_t, axis=0, keepdims=True))
            alpha = jnp.exp2(m_old - m_new)
            p = jnp.exp2(s_t - m_new).astype(BF16)
            acc_ref[slot] = alpha * acc_ref[slot] + _dot(vt_ref[0, br, pair // 2, :, pl.ds(k0, TK)], p)
            m_ref[slot] = m_new
            if pending and i % 2 == 1:
                pending.pop(0)()
        for emit in pending:
            emit()

    n_back = WINDOW // TK
    for n_behind in range(n_back + 1):
        tiles = [(qi, MASK_CAUSAL)] + [(qi - d, MASK_WINDOW_EDGE if d == n_back else None)
                                        for d in range(1, n_behind + 1)]

        @pl.when(qi >= n_back if n_behind == n_back else qi == n_behind)
        def _(tiles=tiles):
            scores_c = compressed_scores()
            project, *rest = mlstm_chunk(0)

            def selection_and_projections():
                compressed_branch_and_selection(scores_c)
                project()

            attend(BR_WIN, tiles, [selection_and_projections] + rest)

    def sel_body(i, carry):
        attend(BR_SEL, [(SEL_TILES * i + j, None) for j in range(SEL_TILES)])
        return carry

    lax.fori_loop(0, qi // SEL_TILES, sel_body, 0)
    for n_full in range(SEL_TILES):
        @pl.when(qi % SEL_TILES == n_full)
        def _(n_full=n_full):
            attend(BR_SEL, [(qi - n_full + j, None) for j in range(n_full)] + [(qi, MASK_CAUSAL)], mlstm_chunk(1))

    gates_t = jax.nn.sigmoid(zs_ref[...]).T
    for hp in range(NSA_H // 2):
        comb = []
        for e in range(2):
            h = 2 * hp + e
            c0 = ZS_GATE + NSA_NB * h
            pair = (h // NSA_R) * 2 + e
            cols = slice(((h % NSA_R) // 2) * tq, ((h % NSA_R) // 2 + 1) * tq)
            sel, win = BR_SEL * n_pairs + pair, BR_WIN * n_pairs + pair
            o_sel = acc_ref[sel, 0:NSA_DH, cols] * (1.0 / acc_ref[sel, NSA_DH:NSA_DH + 1, cols])
            o_win = acc_ref[win, 0:NSA_DH, cols] * (1.0 / acc_ref[win, NSA_DH:NSA_DH + 1, cols])
            comb.append(gates_t[c0:c0 + 1] * ocmp_ref[pair, :, cols] + gates_t[c0 + 1:c0 + 2] * o_sel
                        + gates_t[c0 + 2:c0 + 3] * o_win)
        out_ref[:, hp * LANES:(hp + 1) * LANES] = jnp.concatenate(comb, axis=0).T.astype(BF16)


def _overlap_t(S):
    c = np.arange(S // CMP_STRIDE)[None, :]
    j = np.arange(S // SEL_BLOCK)[:, None]
    ov = (c * CMP_STRIDE <= j * SEL_BLOCK + SEL_BLOCK - 1) & (c * CMP_STRIDE + CMP_BLOCK - 1 >= j * SEL_BLOCK)
    ov &= c < (S - CMP_BLOCK) // CMP_STRIDE + 1
    return jnp.asarray(ov.astype(np.float32), dtype=BF16)


def _nsa_mlstm(zm, zs, kaug, vt, kcaug, vct, conv_w, conv_b, gate_b, norm_g, S):
    M = zm.shape[0]
    B = M // S
    tq = min(TQ, S)
    assert tq == TK and WINDOW % TK == 0 and S % TK == 0 and tq == 2 * ML_CHUNK and ML_CHUNK == ML_DH
    nq = S // tq
    n = S // CMP_STRIDE
    nsel = S // SEL_BLOCK
    gb_row = jnp.zeros((1, ZS_W), F32).at[0, ZS_MI:ZS_MI + 2 * ML_H].set(gate_b)

    def tile(off):
        return pl.BlockSpec((tq, ML_W), lambda b, i: (b * nq + i, off // ML_W))

    def row(off):
        return pl.BlockSpec((S, ML_W), lambda b, i: (b, off // ML_W))

    def full(shape):
        return pl.BlockSpec(shape, lambda b, i: (0,) * len(shape))

    return pl.pallas_call(
        _nsa_mlstm_kernel,
        grid=(B, nq),
        in_specs=[tile(ZM_NQ),
                  pl.BlockSpec((1, 8, S, LANES), lambda b, i: (b, 0, 0, 0)),
                  pl.BlockSpec((1, 2, NSA_G, VT_ROWS, S), lambda b, i: (b, 0, 0, 0, 0)),
                  pl.BlockSpec((1, 4, n, LANES), lambda b, i: (b, 0, 0, 0)),
                  pl.BlockSpec((1, LANES, n), lambda b, i: (b, 0, 0)),
                  pl.BlockSpec((tq, ZS_W), lambda b, i: (b * nq + i, 0)),
                  full((nsel, n)),
                  row(ZM_MQ), row(ZM_MK), tile(ZM_MV), tile(ZM_MO),
                  full((CONV_K, 2 * ML_W)), full((1, 2 * ML_W)), full((1, ZS_W)), full((1, ML_W))],
        out_specs=[pl.BlockSpec((tq, NSA_W), lambda b, i: (b * nq + i, 0)),
                   pl.BlockSpec((tq, ML_W), lambda b, i: (b * nq + i, 0))],
        out_shape=[jax.ShapeDtypeStruct((M, NSA_W), BF16), jax.ShapeDtypeStruct((M, ML_W), BF16)],
        scratch_shapes=[pltpu.VMEM((NSA_H // 2, LANES, 2 * tq), BF16),
                        pltpu.VMEM((NSA_H // 2, LANES, 2 * tq), BF16),
                        pltpu.VMEM((NSA_H // 2, NSA_DH, 2 * tq), F32),
                        pltpu.VMEM((NSA_H, 1, 2 * tq), F32),
                        pltpu.VMEM((NSA_H, VT_ROWS, 2 * tq), F32),
                        pltpu.VMEM((ML_H, ML_DH, 2 * ML_DH), F32),
                        pltpu.VMEM((1, ZS_W), F32)],
        compiler_params=_cparams(("arbitrary", "arbitrary")),
        name="nsa_mlstm",
    )(zm, kaug, vt, kcaug, vct, zs, _overlap_t(S), zm, zm, zm, zm,
      conv_w, conv_b.reshape(1, 2 * ML_W), gb_row, norm_g.reshape(1, ML_W))


def _merge_kernel(gu0_ref, gv0_ref, gu_ref, gv_ref, lng_ref, lnb_ref, ws_ref, bst_ref, b_ref, c_ref,
                  ga_ref, gb_ref, gc_ref, x_ref, gt_ref, wa_ref, wb_ref, wc_ref, wo_ref, o_ref, a_ref):
    i = pl.program_id(0)

    @pl.when(i == 0)
    def _():
        _gmlp_rows(gu0_ref, gv0_ref, lng_ref, lnb_ref, ws_ref, bst_ref, a_ref.at[0])

    rows_per = o_ref.shape[0] // MERGE_SPLIT
    for part in range(MERGE_SPLIT):
        rows = slice(part * rows_per, (part + 1) * rows_per)
        merged = (jax.nn.sigmoid(ga_ref[rows, :].astype(F32)) * _dot(a_ref[i % 2, rows, :], wa_ref[...])
                  + jax.nn.sigmoid(gb_ref[rows, :].astype(F32)) * _dot(b_ref[rows, :], wb_ref[...])
                  + jax.nn.sigmoid(gc_ref[rows, :].astype(F32)) * _dot(c_ref[rows, :], wc_ref[...]))
        o_ref[rows, :] = x_ref[rows, :] + gt_ref[0] * _dot(merged.astype(BF16), wo_ref[...])
    _gmlp_rows(gu_ref, gv_ref, lng_ref, lnb_ref, ws_ref, bst_ref, a_ref.at[(i + 1) % 2])


def _merge(yb, yc, zm, x2, gt, ln_g, ln_b, ws, bs, wa, wb, wc, wo, S):
    M, D = x2.shape
    tm = min(TM_MERGE, S)
    per_b = S // tm

    def rows(w, jcol=0):
        return pl.BlockSpec((tm, w), lambda i: (i, jcol))

    def next_rows(jcol):
        return pl.BlockSpec((tm, GM_W), lambda i: (jnp.minimum(i + 1, M // tm - 1), jcol))

    def full(shape):
        return pl.BlockSpec(shape, lambda i: (0,) * len(shape))

    return pl.pallas_call(
        _merge_kernel,
        grid=(M // tm,),
        in_specs=[rows(GM_W, ZM_GU // GM_W), rows(GM_W, ZM_GV // GM_W),
                  next_rows(ZM_GU // GM_W), next_rows(ZM_GV // GM_W), full((1, GM_W)), full((1, GM_W)),
                  full((GM_GROUPS, GM_CHUNK, GM_CHUNK)), full((GM_CHUNK, GM_GROUPS)),
                  rows(ML_W), rows(NSA_W),
                  rows(D, ZM_GA // D), rows(D, ZM_GBR // D), rows(D, ZM_GC // D),
                  rows(D), pl.BlockSpec((1, 1, D), lambda i: (i // per_b, 0, 0)),
                  full((GM_W, D)), full((ML_W, D)), full((NSA_W, D)), full((D, D))],
        out_specs=rows(D),
        out_shape=jax.ShapeDtypeStruct((M, D), F32),
        scratch_shapes=[pltpu.VMEM((2, tm, GM_W), BF16)],
        compiler_params=_cparams(("arbitrary",)),
        name="gmlp_merge_out",
    )(zm, zm, zm, zm, ln_g.reshape(1, GM_W), ln_b.reshape(1, GM_W), ws, bs.T, yb, yc, zm, zm, zm, x2, gt, wa, wb, wc, wo)


def _mlp_kernel(x_ref, g_ref, sc_ref, sh_ref, gt_ref, w1_ref, w2_ref, gf_ref, o_ref, acc_ref, *, final_norm):
    x = x_ref[...]
    h = _modulated_norm(x, g_ref[...], sc_ref[0], sh_ref[0]).astype(BF16)
    for c in range(D_FF // FF_CHUNK):
        mid = jnp.square(jnp.maximum(_dot(h, w1_ref[:, c * FF_CHUNK:(c + 1) * FF_CHUNK]), 0.0)).astype(BF16)
        upd = _dot(mid, w2_ref[c * FF_CHUNK:(c + 1) * FF_CHUNK, :])
        if c == 0:
            acc_ref[...] = upd
        else:
            acc_ref[...] += upd
    y = x + gt_ref[0] * acc_ref[...]
    if final_norm:
        y = (y * lax.rsqrt(jnp.mean(y * y, axis=-1, keepdims=True) + EPS)) * gf_ref[...]
    o_ref[...] = y


def _mlp(x2, g, sc, sh, gt, w1, w2, g_final, S, final_norm):
    M, D = x2.shape
    tm = min(TM_MLP, S)
    per_b = S // tm
    mod = pl.BlockSpec((1, 1, D), lambda i: (i // per_b, 0, 0))
    return pl.pallas_call(
        functools.partial(_mlp_kernel, final_norm=final_norm),
        grid=(M // tm,),
        in_specs=[pl.BlockSpec((tm, D), lambda i: (i, 0)),
                  pl.BlockSpec((1, D), lambda i: (0, 0)),
                  mod, mod, mod,
                  pl.BlockSpec((D, D_FF), lambda i: (0, 0)),
                  pl.BlockSpec((D_FF, D), lambda i: (0, 0)),
                  pl.BlockSpec((1, D), lambda i: (0, 0))],
        out_specs=pl.BlockSpec((tm, D), lambda i: (i, 0)),
        out_shape=jax.ShapeDtypeStruct((M, D), F32),
        scratch_shapes=[pltpu.VMEM((tm, D), F32)],
        compiler_params=_cparams(("arbitrary",)),
        name="relu2_mlp",
    )(x2, g, sc, sh, gt, w1, w2, g_final)


def _w_cols(w, name):
    return w[:, _OFFS[name]:_OFFS[name] + _SIZE[name]]


def _proj_weights(w):
    D = w.shape[0]
    wm = jnp.concatenate([_w_cols(w, n) for n in ('ga', 'gbr', 'gc', 'gu', 'gv', 'mq', 'mk', 'mv', 'mo', 'nq',
                                                  'nks', 'nkw', 'nvs', 'nvw')], axis=1)
    wc = jnp.concatenate([_w_cols(w, 'nkc'), _w_cols(w, 'nvc')], axis=1)
    ws = jnp.concatenate([_w_cols(w, 'ngate'), _w_cols(w, 'mi'), _w_cols(w, 'mf'),
                          jnp.zeros((D, ZS_W - NSA_H * NSA_NB - 2 * ML_H), w.dtype)], axis=1)
    return wm, wc, ws


def kernel(x, c, g_norm1, g_norm2, w_ada, b_ada, w_in, gm_ln_g, gm_ln_b, gm_ws, gm_bs, ml_conv_w, ml_conv_b,
           ml_gate_b, ml_norm_g, nsa_pe_k, nsa_pe_v, nsa_phi_k1, nsa_phi_k2, nsa_phi_v1, nsa_phi_v2,
           w_up_a, w_up_b, w_up_c, w_out, w_mlp1, w_mlp2, g_final):
    B, S, D = x.shape
    depth = w_in.shape[0]
    M = B * S
    mod = _ada(c, w_ada, b_ada)
    x2 = x.reshape(M, D)
    w_in, w_up_a, w_up_b, w_up_c, w_out, w_mlp1, w_mlp2 = [
        w.astype(BF16) for w in (w_in, w_up_a, w_up_b, w_up_c, w_out, w_mlp1, w_mlp2)]
    for l in range(depth):
        sh1, sc1, gt1, sh2, sc2, gt2 = [mod[l, :, i * D:(i + 1) * D].reshape(B, 1, D) for i in range(6)]
        wm, wc, ws = _proj_weights(w_in[l])
        zm, zc, zs = _inproj(x2, g_norm1[l].reshape(1, D), sc1, sh1, wm, wc, ws, S)
        kaug, vt, kcaug, vct = _nsa_prep(zc, zm, nsa_pe_k[l], nsa_pe_v[l], nsa_phi_k1[l], nsa_phi_k2[l],
                                         nsa_phi_v1[l], nsa_phi_v2[l], S)
        yc, yb = _nsa_mlstm(zm, zs, kaug, vt, kcaug, vct, ml_conv_w[l], ml_conv_b[l], ml_gate_b[l], ml_norm_g[l], S)
        x2 = _merge(yb, yc, zm, x2, gt1, gm_ln_g[l], gm_ln_b[l], gm_ws[l], gm_bs[l], w_up_a[l], w_up_b[l],
                    w_up_c[l], w_out[l], S)
        x2 = _mlp(x2, g_norm2[l].reshape(1, D), sc2, sh2, gt2, w_mlp1[l], w_mlp2[l], g_final.reshape(1, D), S,
                  final_norm=(l == depth - 1))
    return x2.reshape(B, S, D)
```
